```python
import math
import jax
import jax.numpy as jnp
from jax import lax
import numpy as np

D_MODEL = 1024
BATCH = 2
SEQ = 8192
DEPTH = 2
DEC_BATCH = 32
DEC_SEQ = 4
PAST_LEN = 8192
PAGE_SIZE = 128

HEAD_DIM = 64
MOBA_HEADS = D_MODEL // HEAD_DIM
NSA_HEADS = MOBA_HEADS // 2
NSA_KV_HEADS = NSA_HEADS // 4
FOX_HEADS = MOBA_HEADS - NSA_HEADS
N_BIAS_SLOTS = MOBA_HEADS
CMP_LEN = 32
CMP_STRIDE = 16
CMP_HID = 2 * HEAD_DIM
SEL_BLOCK = 64
CMP_PER_SEL = SEL_BLOCK // CMP_STRIDE
NSA_TOPN = 16
NSA_WINDOW = 512
MOBA_BLOCK = 256
MOBA_TOPK = 3
N_BUCKETS = 32
T5_MAX_DISTANCE = 128
N_GROUPS = 4
EXPERTS_PER_GROUP = 4
N_EXPERTS = N_GROUPS * EXPERTS_PER_GROUP
D_EXPERT = 256
MOE_TOPK = 2
N_EVEN = (DEPTH + 1) // 2
N_ODD = DEPTH // 2
ALPHA = (2 * DEPTH) ** 0.25
BETA = (8 * DEPTH) ** -0.25
SCALE = HEAD_DIM ** -0.5
GATHER_QBLK = 16
DENSE_QBLK = 128
NEG_INF = -1e30
FORCE_SCORE = 1e4
LN_EPS = 1e-5
FORGET_BIAS_INIT = 3.0
EVEN_SIZES = (NSA_HEADS * HEAD_DIM, 2 * NSA_KV_HEADS * HEAD_DIM, 2 * NSA_KV_HEADS * HEAD_DIM,
              2 * NSA_KV_HEADS * HEAD_DIM, 3 * NSA_HEADS, FOX_HEADS * HEAD_DIM,
              2 * FOX_HEADS * HEAD_DIM, FOX_HEADS)
EVEN_IN = sum(EVEN_SIZES)
EVEN_CUTS = tuple(sum(EVEN_SIZES[:i + 1]) for i in range(len(EVEN_SIZES) - 1))
MIX_WIDTH = (NSA_HEADS + FOX_HEADS) * HEAD_DIM
ODD_IN = 3 * MOBA_HEADS * HEAD_DIM

kernel_name = 'nsa_fox_moba_hmoe_decode_step'


def _t5_bucket(dist):
    n = jnp.maximum(dist, 0)
    exact = N_BUCKETS // 2
    nf = jnp.maximum(n, exact).astype(jnp.float32)
    far = exact + (jnp.log(nf / exact) / math.log(T5_MAX_DISTANCE / exact)
                   * (N_BUCKETS - exact)).astype(jnp.int32)
    return jnp.where(n < exact, n, jnp.minimum(far, N_BUCKETS - 1))


def _masked_softmax(s, mask):
    s = jnp.where(mask, s.astype(jnp.float32), NEG_INF)
    m = jnp.max(s, axis=-1, keepdims=True)
    e = jnp.where(mask, jnp.exp(s - m), 0.0)
    return e / jnp.maximum(jnp.sum(e, axis=-1, keepdims=True), 1e-30)


def _layer_norm(x, g, b):
    xf = x.astype(jnp.float32)
    mu = jnp.mean(xf, axis=-1, keepdims=True)
    var = jnp.mean(jnp.square(xf - mu), axis=-1, keepdims=True)
    return ((xf - mu) * lax.rsqrt(var + LN_EPS) * g + b).astype(x.dtype)


def _concat_rows(parts, mult):
    total = sum(p.shape[1] for p in parts)
    pad = (-total) % mult
    dt = parts[0].dtype
    z = jnp.zeros((parts[0].shape[0], pad) + parts[0].shape[2:], dt)
    return jnp.concatenate([p.astype(dt) for p in parts] + [z], axis=1)


def _gather_pages(pool, li, page_table):
    g = pool[li, page_table]
    return g.reshape((g.shape[0], g.shape[1] * g.shape[2]) + g.shape[3:])


def _map_query_blocks(fn, blk, qpos, *xs):
    T = qpos.shape[0]
    nb = T // blk
    xs_b = tuple(jnp.swapaxes(x.reshape((x.shape[0], nb, blk) + x.shape[2:]), 0, 1) for x in xs)
    out = lax.map(lambda a: fn(a[0], *a[1:]), (qpos.reshape(nb, blk),) + xs_b)
    out = jnp.swapaxes(out, 0, 1)
    return out.reshape((out.shape[0], T) + out.shape[3:])


def _nsa_compress(kv, pe, w1, b1, w2, b2):
    B, L = kv.shape[:2]
    ch = kv.reshape(B, L // CMP_STRIDE, CMP_STRIDE, 2, NSA_KV_HEADS, HEAD_DIM)
    h_first = jnp.einsum('bncsgd,csdh->bnsgh', ch + pe[None, None, :CMP_STRIDE, :, None, :], w1[:CMP_STRIDE])
    h_second = jnp.einsum('bncsgd,csdh->bnsgh', ch + pe[None, None, CMP_STRIDE:, :, None, :], w1[CMP_STRIDE:])
    h = jax.nn.gelu(h_first[:, :-1] + h_second[:, 1:] + b1[None, None, :, None, :])
    return jnp.einsum('bnsgh,shd->bnsgd', h, w2) + b2[None, None, :, None, :]


def _nsa_context(kv_c, kv_s, pe, w1, b1, w2, b2):
    kv_cmp = _nsa_compress(kv_c, pe, w1, b1, w2, b2)
    cmp_end = jnp.arange(kv_cmp.shape[1], dtype=jnp.int32) * CMP_STRIDE + (CMP_LEN - 1)
    B, L = kv_s.shape[:2]
    kv_sel_blk = kv_s.reshape(B, L // SEL_BLOCK, SEL_BLOCK, 2, NSA_KV_HEADS, HEAD_DIM)
    return kv_cmp, cmp_end, kv_sel_blk


def _nsa_chunk(q, qpos, gates, kv_cmp, cmp_end, kv_sel_blk, kv_win, win_pos, rel_bias):
    B, Tq = q.shape[:2]
    G = NSA_KV_HEADS
    R = NSA_HEADS // G
    qg = (q * SCALE).reshape(B, Tq, G, R, HEAD_DIM)
    tab = rel_bias[:, :NSA_HEADS].reshape(N_BUCKETS, G, R)
    d_c = qpos[:, None] - cmp_end[None, :]
    s_c = jnp.einsum('btgrd,bngd->bgrtn', qg, kv_cmp[:, :, 0]) + tab[_t5_bucket(d_c)].transpose(2, 3, 0, 1)
    p_c = _masked_softmax(s_c, d_c >= 0)
    o_c = jnp.einsum('bgrtn,bngd->btgrd', p_c, kv_cmp[:, :, 1])
    NS = kv_sel_blk.shape[1]
    imp = jnp.pad(jnp.sum(p_c, axis=2), ((0, 0), (0, 0), (0, 0), (1, 1)))
    p_s = (jnp.sum(imp[..., :CMP_PER_SEL * NS].reshape(B, G, Tq, NS, CMP_PER_SEL), axis=-1)
           + imp[..., CMP_PER_SEL::CMP_PER_SEL])
    qb = qpos // SEL_BLOCK
    j = jnp.arange(NS, dtype=jnp.int32)[None, :]
    valid = j <= qb[:, None]
    forced = (j == 0) | (j == qb[:, None]) | (j == qb[:, None] - 1)
    score = jnp.where(valid, jnp.where(forced, FORCE_SCORE, p_s), NEG_INF)
    top_v, top_i = lax.top_k(score, min(NSA_TOPN, NS))
    ok = top_v > 0.5 * NEG_INF
    n = top_i.shape[-1]
    b_ix = jnp.arange(B)[:, None, None, None]
    g_ix = jnp.arange(G)[None, :, None, None]
    kv_s = kv_sel_blk[b_ix, top_i, :, :, g_ix].reshape(B, G, Tq, n * SEL_BLOCK, 2, HEAD_DIM)
    kpos = (top_i[..., None] * SEL_BLOCK + jnp.arange(SEL_BLOCK, dtype=jnp.int32)).reshape(B, G, Tq, n * SEL_BLOCK)
    d_s = qpos[None, None, :, None] - kpos
    m_s = jnp.repeat(ok, SEL_BLOCK, axis=-1) & (d_s >= 0)
    b_s = tab.transpose(1, 0, 2)[g_ix, _t5_bucket(d_s)]
    s_s = jnp.einsum('btgrd,bgtkd->bgrtk', qg, kv_s[..., 0, :]) + b_s.transpose(0, 1, 4, 2, 3)
    p_sel = _masked_softmax(s_s, m_s[:, :, None])
    o_s = jnp.einsum('bgrtk,bgtkd->btgrd', p_sel, kv_s[..., 1, :])
    d_w = qpos[:, None] - win_pos[None, :]
    m_w = (d_w >= 0) & (d_w < NSA_WINDOW) & (win_pos[None, :] >= 0)
    s_w = jnp.einsum('btgrd,blgd->bgrtl', qg, kv_win[:, :, 0]) + tab[_t5_bucket(d_w)].transpose(2, 3, 0, 1)
    p_w = _masked_softmax(s_w, m_w)
    o_w = jnp.einsum('bgrtl,blgd->btgrd', p_w, kv_win[:, :, 1])
    g = jax.nn.sigmoid(gates.astype(jnp.float32)).reshape(B, Tq, G, R, 3)
    o = g[..., 0:1] * o_c + g[..., 1:2] * o_s + g[..., 2:3] * o_w
    return o.reshape(B, Tq, NSA_HEADS * HEAD_DIM).astype(q.dtype)


def _fox_chunk(q, qpos, cq, kv, ck, kpos):
    B, Tq = q.shape[:2]
    s = jnp.einsum('bthd,blhd->bhtl', q * SCALE, kv[:, :, 0]).astype(jnp.float32)
    s = s + jnp.transpose(cq, (0, 2, 1))[..., None] - jnp.transpose(ck, (0, 2, 1))[:, :, None, :]
    p = _masked_softmax(s, kpos[None, :] <= qpos[:, None])
    o = jnp.einsum('bhtl,blhd->bthd', p, kv[:, :, 1])
    return o.reshape(B, Tq, FOX_HEADS * HEAD_DIM).astype(q.dtype)


def _moba_context(kv_seq):
    B, L = kv_seq.shape[:2]
    kv_blk = kv_seq.reshape(B, L // MOBA_BLOCK, MOBA_BLOCK, 2, MOBA_HEADS, HEAD_DIM)
    k_mean = jnp.mean(kv_blk[:, :, :, 0].astype(jnp.float32), axis=2)
    return kv_blk, k_mean


def _moba_chunk(q, qpos, kv_blk, k_mean, rel_bias):
    B, Tq = q.shape[:2]
    NB = kv_blk.shape[1]
    qb = qpos // MOBA_BLOCK
    gate = jnp.einsum('bthd,bnhd->bhtn', q, k_mean).astype(jnp.float32)
    past = jnp.arange(NB, dtype=jnp.int32)[None, :] < qb[:, None]
    top_v, top_i = lax.top_k(jnp.where(past, gate, NEG_INF), min(MOBA_TOPK, NB))
    own = jnp.broadcast_to(qb[None, None, :, None], (B, MOBA_HEADS, Tq, 1))
    blk = jnp.concatenate([top_i, own], axis=-1)
    okb = jnp.concatenate([top_v > 0.5 * NEG_INF, jnp.ones(own.shape, bool)], axis=-1)
    nb = blk.shape[-1]
    b_ix = jnp.arange(B)[:, None, None, None]
    h_ix = jnp.arange(MOBA_HEADS)[None, :, None, None]
    kv_g = kv_blk[b_ix, blk, :, :, h_ix].reshape(B, MOBA_HEADS, Tq, nb * MOBA_BLOCK, 2, HEAD_DIM)
    kpos = (blk[..., None] * MOBA_BLOCK + jnp.arange(MOBA_BLOCK, dtype=jnp.int32)).reshape(B, MOBA_HEADS, Tq, nb * MOBA_BLOCK)
    d = qpos[None, None, :, None] - kpos
    mask = jnp.repeat(okb, MOBA_BLOCK, axis=-1) & (d >= 0)
    s = jnp.einsum('bthd,bhtkd->bhtk', q * SCALE, kv_g[..., 0, :]).astype(jnp.float32) + rel_bias[_t5_bucket(d), h_ix]
    p = _masked_softmax(s, mask)
    o = jnp.einsum('bhtk,bhtkd->bthd', p, kv_g[..., 1, :])
    return o.reshape(B, Tq, MOBA_HEADS * HEAD_DIM).astype(q.dtype)


def _even_project(x, w_in, b_in):
    B, T, _ = x.shape
    z = x @ w_in + b_in
    q_a, kv_c, kv_s, kv_w, g_a, q_f, kv_f, f_logit = jnp.split(z, EVEN_CUTS, axis=-1)
    kvshape = (B, T, 2, NSA_KV_HEADS, HEAD_DIM)
    return (q_a.reshape(B, T, NSA_HEADS, HEAD_DIM), kv_c.reshape(kvshape), kv_s.reshape(kvshape),
            kv_w.reshape(kvshape), g_a.reshape(B, T, NSA_HEADS, 3),
            q_f.reshape(B, T, FOX_HEADS, HEAD_DIM), kv_f.reshape(B, T, 2, FOX_HEADS, HEAD_DIM),
            jax.nn.log_sigmoid(f_logit.astype(jnp.float32)))


def _even_prompt(x, rel_bias, w_in, b_in, cmp, w_out):
    B, S, _ = x.shape
    q_a, kv_c, kv_s, kv_w, g_a, q_f, kv_f, logf = _even_project(x, w_in, b_in)
    pos = jnp.arange(S, dtype=jnp.int32)
    kv_cmp, cmp_end, kv_sel_blk = _nsa_context(_concat_rows([kv_c], SEL_BLOCK), _concat_rows([kv_s], SEL_BLOCK), *cmp)
    kv_w_pad = jnp.pad(kv_w, ((0, 0), (NSA_WINDOW, 0), (0, 0), (0, 0), (0, 0)))
    band = NSA_WINDOW + GATHER_QBLK

    def nsa_block(qp, qb, gb):
        kvw = lax.dynamic_slice_in_dim(kv_w_pad, qp[0], band, axis=1)
        wpos = qp[0] - NSA_WINDOW + jnp.arange(band, dtype=jnp.int32)
        return _nsa_chunk(qb, qp, gb, kv_cmp, cmp_end, kv_sel_blk, kvw, wpos, rel_bias)

    o_a = _map_query_blocks(nsa_block, GATHER_QBLK, pos, q_a, g_a)
    c = jnp.cumsum(logf, axis=1)

    def fox_block(qp, qb, cb):
        return _fox_chunk(qb, qp, cb, kv_f, c, pos)

    o_f = _map_query_blocks(fox_block, DENSE_QBLK, pos, q_f, c)
    y = (jnp.concatenate([o_a, o_f.astype(o_a.dtype)], axis=-1) @ w_out).astype(x.dtype)
    return y, (kv_c, kv_s, kv_w[:, max(S - NSA_WINDOW, 0):], kv_f, logf)


def _even_sample(x, li, cache_cmp, cache_sel, win_state, cache_fkv, cache_flogf, page_table,
                 rel_bias, w_in, b_in, cmp, w_out):
    B, T, _ = x.shape
    past = page_table.shape[1] * PAGE_SIZE
    q_a, kv_c, kv_s, kv_w, g_a, q_f, kv_f, logf = _even_project(x, w_in, b_in)
    pos = past + jnp.arange(T, dtype=jnp.int32)
    kv_cmp, cmp_end, kv_sel_blk = _nsa_context(
        _concat_rows([_gather_pages(cache_cmp, li, page_table), kv_c], SEL_BLOCK),
        _concat_rows([_gather_pages(cache_sel, li, page_table), kv_s], SEL_BLOCK), *cmp)
    win_buf = win_state[li]
    wb = win_buf.shape[1]
    kvw = _concat_rows([win_buf, kv_w], 1)
    wpos = past - wb + jnp.arange(wb + T, dtype=jnp.int32)
    o_a = _nsa_chunk(q_a, pos, g_a, kv_cmp, cmp_end, kv_sel_blk, kvw, wpos, rel_bias)
    c_all = jnp.cumsum(_concat_rows([_gather_pages(cache_flogf, li, page_table).astype(jnp.float32), logf], 1), axis=1)
    kv_f_all = _concat_rows([_gather_pages(cache_fkv, li, page_table), kv_f], 1)
    o_f = _fox_chunk(q_f, pos, c_all[:, past:], kv_f_all, c_all, jnp.arange(past + T, dtype=jnp.int32))
    y = (jnp.concatenate([o_a, o_f.astype(o_a.dtype)], axis=-1) @ w_out).astype(x.dtype)
    return y, (kv_c, kv_s, kvw[:, T:], kv_f, logf)


def _odd_project(x, w_in):
    B, T, _ = x.shape
    q, kv = jnp.split(x @ w_in, [MOBA_HEADS * HEAD_DIM], axis=-1)
    return q.reshape(B, T, MOBA_HEADS, HEAD_DIM), kv.reshape(B, T, 2, MOBA_HEADS, HEAD_DIM)


def _odd_prompt(x, rel_bias, w_in, w_out):
    B, S, _ = x.shape
    q, kv = _odd_project(x, w_in)
    kv_blk, k_mean = _moba_context(_concat_rows([kv], MOBA_BLOCK))
    pos = jnp.arange(S, dtype=jnp.int32)
    o = _map_query_blocks(lambda qp, qb: _moba_chunk(qb, qp, kv_blk, k_mean, rel_bias), GATHER_QBLK, pos, q)
    return (o @ w_out).astype(x.dtype), kv


def _odd_sample(x, li, cache_kv, page_table, rel_bias, w_in, w_out):
    B, T, _ = x.shape
    past = page_table.shape[1] * PAGE_SIZE
    q, kv = _odd_project(x, w_in)
    kv_blk, k_mean = _moba_context(_concat_rows([_gather_pages(cache_kv, li, page_table), kv], MOBA_BLOCK))
    pos = past + jnp.arange(T, dtype=jnp.int32)
    o = _moba_chunk(q, pos, kv_blk, k_mean, rel_bias)
    return (o @ w_out).astype(x.dtype), kv


def _hier_moe(x, wg, bg, we, be, w1, w3, w2):
    B, T, D = x.shape
    xf = x.reshape(B * T, D)
    N = xf.shape[0]
    p_grp = jax.nn.softmax((xf @ wg + bg).astype(jnp.float32), axis=-1)
    p_top, g_top = lax.top_k(p_grp, 1)
    le = (xf @ we + be).astype(jnp.float32).reshape(N, N_GROUPS, EXPERTS_PER_GROUP)
    p_in = jax.nn.softmax(le[jnp.arange(N), g_top[:, 0]], axis=-1)
    w_top, e_top = lax.top_k(p_in, MOE_TOPK)
    gate_k = p_top * w_top / jnp.sum(w_top, axis=-1, keepdims=True)
    eid = g_top * EXPERTS_PER_GROUP + e_top
    gate = jnp.sum(jax.nn.one_hot(eid, N_EXPERTS, dtype=jnp.float32) * gate_k[..., None], axis=1)
    h = jax.nn.silu(jnp.einsum('nd,edf->nef', xf, w1)) * jnp.einsum('nd,edf->nef', xf, w3)
    y = jnp.einsum('nef,efd->nd', h * gate[:, :, None].astype(h.dtype), w2)
    return y.reshape(B, T, D).astype(x.dtype)


def setup_inputs(seed: int = 0) -> dict:
    key = jax.random.key(seed)
    ks = iter(jax.random.split(key, 40))

    def nrm(shape, scale):
        return scale * jax.random.normal(next(ks), shape, jnp.float32)

    n_pages = PAST_LEN // PAGE_SIZE
    n_used = DEC_BATCH * n_pages
    n_pool = n_used + max(n_used // 4, 1)
    wb = min(NSA_WINDOW, PAST_LEN)
    kv_nsa = (N_EVEN, n_pool, PAGE_SIZE, 2, NSA_KV_HEADS, HEAD_DIM)
    page_table = jax.random.permutation(next(ks), n_pool)[:n_used].reshape(DEC_BATCH, n_pages).astype(jnp.int32)
    b_in_even = nrm((N_EVEN, EVEN_IN), 0.02).at[:, EVEN_IN - FOX_HEADS:].add(FORGET_BIAS_INIT)
    return {
        'x_prompt': nrm((BATCH, SEQ, D_MODEL), 1.0),
        'x_sample': nrm((DEC_BATCH, DEC_SEQ, D_MODEL), 1.0),
        'cache_nsa_cmp': nrm(kv_nsa, 1.0),
        'cache_nsa_sel': nrm(kv_nsa, 1.0),
        'state_nsa_win': nrm((N_EVEN, DEC_BATCH, wb, 2, NSA_KV_HEADS, HEAD_DIM), 1.0),
        'cache_fox_kv': nrm((N_EVEN, n_pool, PAGE_SIZE, 2, FOX_HEADS, HEAD_DIM), 1.0),
        'cache_fox_logf': jax.nn.log_sigmoid(FORGET_BIAS_INIT + nrm((N_EVEN, n_pool, PAGE_SIZE, FOX_HEADS), 1.0)),
        'cache_moba_kv': nrm((N_ODD, n_pool, PAGE_SIZE, 2, MOBA_HEADS, HEAD_DIM), 1.0),
        'page_table': page_table,
        'rel_bias': nrm((N_BUCKETS, N_BIAS_SLOTS), 0.5),
        'ln_g': 1.0 + nrm((DEPTH, 2, D_MODEL), 0.02),
        'ln_b': nrm((DEPTH, 2, D_MODEL), 0.02),
        'w_in_even': nrm((N_EVEN, D_MODEL, EVEN_IN), D_MODEL ** -0.5),
        'b_in_even': b_in_even,
        'nsa_cmp_pe': nrm((N_EVEN, CMP_LEN, 2, HEAD_DIM), 0.2),
        'nsa_cmp_w1': nrm((N_EVEN, CMP_LEN, 2, HEAD_DIM, CMP_HID), (CMP_LEN * HEAD_DIM) ** -0.5),
        'nsa_cmp_b1': nrm((N_EVEN, 2, CMP_HID), 0.02),
        'nsa_cmp_w2': nrm((N_EVEN, 2, CMP_HID, HEAD_DIM), CMP_HID ** -0.5),
        'nsa_cmp_b2': nrm((N_EVEN, 2, HEAD_DIM), 0.02),
        'w_out_even': nrm((N_EVEN, MIX_WIDTH, D_MODEL), BETA * MIX_WIDTH ** -0.5),
        'w_in_odd': nrm((N_ODD, D_MODEL, ODD_IN), D_MODEL ** -0.5),
        'w_out_odd': nrm((N_ODD, MOBA_HEADS * HEAD_DIM, D_MODEL), BETA * (MOBA_HEADS * HEAD_DIM) ** -0.5),
        'moe_wg': nrm((DEPTH, D_MODEL, N_GROUPS), D_MODEL ** -0.5),
        'moe_bg': nrm((DEPTH, N_GROUPS), 0.01),
        'moe_we': nrm((DEPTH, D_MODEL, N_EXPERTS), D_MODEL ** -0.5),
        'moe_be': nrm((DEPTH, N_EXPERTS), 0.01),
        'moe_w1': nrm((DEPTH, N_EXPERTS, D_MODEL, D_EXPERT), D_MODEL ** -0.5),
        'moe_w3': nrm((DEPTH, N_EXPERTS, D_MODEL, D_EXPERT), D_MODEL ** -0.5),
        'moe_w2': nrm((DEPTH, N_EXPERTS, D_EXPERT, D_MODEL), BETA * D_EXPERT ** -0.5),
    }


def reference(x_prompt, x_sample, cache_nsa_cmp, cache_nsa_sel, state_nsa_win, cache_fox_kv,
              cache_fox_logf, cache_moba_kv, page_table, rel_bias, ln_g, ln_b, w_in_even, b_in_even,
              nsa_cmp_pe, nsa_cmp_w1, nsa_cmp_b1, nsa_cmp_w2, nsa_cmp_b2, w_out_even, w_in_odd,
              w_out_odd, moe_wg, moe_bg, moe_we, moe_be, moe_w1, moe_w3, moe_w2):
    xp, xs = x_prompt, x_sample
    cmp_p, cmp_s, sel_p, sel_s, win_p, win_s, fkv_p, fkv_s, flf_p, flf_s, mkv_p, mkv_s = ([] for _ in range(12))
    for layer in range(DEPTH):
        li = layer // 2
        if layer % 2 == 0:
            cmp = (nsa_cmp_pe[li], nsa_cmp_w1[li], nsa_cmp_b1[li], nsa_cmp_w2[li], nsa_cmp_b2[li])
            mp, (pc, ps, pw, pkv, plf) = _even_prompt(xp, rel_bias, w_in_even[li], b_in_even[li], cmp, w_out_even[li])
            ms, (sc, ss, sw, skv, slf) = _even_sample(xs, li, cache_nsa_cmp, cache_nsa_sel, state_nsa_win,
                                                      cache_fox_kv, cache_fox_logf, page_table, rel_bias,
                                                      w_in_even[li], b_in_even[li], cmp, w_out_even[li])
            cmp_p.append(pc)
            cmp_s.append(sc)
            sel_p.append(ps)
            sel_s.append(ss)
            win_p.append(pw)
            win_s.append(sw)
            fkv_p.append(pkv)
            fkv_s.append(skv)
            flf_p.append(plf)
            flf_s.append(slf)
        else:
            mp, pm = _odd_prompt(xp, rel_bias, w_in_odd[li], w_out_odd[li])
            ms, sm = _odd_sample(xs, li, cache_moba_kv, page_table, rel_bias, w_in_odd[li], w_out_odd[li])
            mkv_p.append(pm)
            mkv_s.append(sm)
        xp = _layer_norm(ALPHA * xp + mp, ln_g[layer, 0], ln_b[layer, 0])
        xs = _layer_norm(ALPHA * xs + ms, ln_g[layer, 0], ln_b[layer, 0])
        moe = (moe_wg[layer], moe_bg[layer], moe_we[layer], moe_be[layer], moe_w1[layer], moe_w3[layer], moe_w2[layer])
        xp = _layer_norm(ALPHA * xp + _hier_moe(xp, *moe), ln_g[layer, 1], ln_b[layer, 1])
        xs = _layer_norm(ALPHA * xs + _hier_moe(xs, *moe), ln_g[layer, 1], ln_b[layer, 1])
    new_nsa_cmp_prompt = jnp.stack(cmp_p)
    new_nsa_cmp_sample = jnp.stack(cmp_s)
    new_nsa_sel_prompt = jnp.stack(sel_p)
    new_nsa_sel_sample = jnp.stack(sel_s)
    new_nsa_win_prompt = jnp.stack(win_p)
    new_nsa_win_sample = jnp.stack(win_s)
    new_fox_kv_prompt = jnp.stack(fkv_p)
    new_fox_kv_sample = jnp.stack(fkv_s)
    new_fox_logf_prompt = jnp.stack(flf_p)
    new_fox_logf_sample = jnp.stack(flf_s)
    new_moba_kv_prompt = jnp.stack(mkv_p)
    new_moba_kv_sample = jnp.stack(mkv_s)
    return (xp, xs, new_nsa_cmp_prompt, new_nsa_cmp_sample, new_nsa_sel_prompt, new_nsa_sel_sample,
            new_nsa_win_prompt, new_nsa_win_sample, new_fox_kv_prompt, new_fox_kv_sample,
            new_fox_logf_prompt, new_fox_logf_sample, new_moba_kv_prompt, new_moba_kv_sample)
```

```python
import functools
import math

import numpy as np
import jax
import jax.numpy as jnp
from jax import lax
from jax.experimental import pallas as pl
from jax.experimental.pallas import tpu as pltpu

_BF = jnp.bfloat16
_F32 = jnp.float32

HEAD_DIM = 64
NSA_KV_HEADS = 2
NSA_GROUP = 4
NSA_HEADS = NSA_KV_HEADS * NSA_GROUP
FOX_HEADS = 8
MOBA_HEADS = 16
CMP_LEN = 32
CMP_STRIDE = 16
CMP_HID = 128
SEL_BLOCK = 64
CMP_PER_SEL = SEL_BLOCK // CMP_STRIDE
NSA_TOPN = 16
NSA_WINDOW = 512
MOBA_BLOCK = 256
MOBA_TOPK = 3
N_BUCKETS = 32
T5_MAX_DISTANCE = 128
N_GROUPS = 4
EXPERTS_PER_GROUP = 4
N_EXPERTS = N_GROUPS * EXPERTS_PER_GROUP
PAGE_SIZE = 128
SCALE = HEAD_DIM ** -0.5
NEG_INF = -1e30
FORCE_SCORE = 1e4
LN_EPS = 1e-5
LANES = 128
VMEM_LIMIT = 48 * 1024 * 1024
PAGES_PER_STEP = 8


def _cparams(*sem):
    return pltpu.CompilerParams(dimension_semantics=sem, vmem_limit_bytes=VMEM_LIMIT)


def _round_up(n, m):
    return (n + m - 1) // m * m


def _split3(x):
    hi = x.astype(_BF)
    r1 = x - hi.astype(_F32)
    mid = r1.astype(_BF)
    lo = (r1 - mid.astype(_F32)).astype(_BF)
    return hi, mid, lo


def _dot3(x, m01):
    hi, mid, lo = _split3(x)
    acc = jnp.dot(hi, m01, preferred_element_type=_F32)
    acc += jnp.dot(mid, m01, preferred_element_type=_F32)
    acc += jnp.dot(lo, m01, preferred_element_type=_F32)
    return acc


def _t5_bucket_np(dist):
    n = np.maximum(dist, 0)
    exact = N_BUCKETS // 2
    nf = np.maximum(n, exact).astype(np.float32)
    far = exact + (np.log(nf / np.float32(exact)) / np.float32(math.log(T5_MAX_DISTANCE / exact))
                   * np.float32(N_BUCKETS - exact)).astype(np.int32)
    return np.where(n < exact, n, np.minimum(far, N_BUCKETS - 1)).astype(np.int32)


def _bucket_thresholds():
    d = np.arange(0, 4 * T5_MAX_DISTANCE)
    b = _t5_bucket_np(d)
    return [int(d[b >= k][0]) for k in range(1, N_BUCKETS)]


def _linear_kernel(x_ref, w_ref, b_ref, o_ref):
    o_ref[...] = jnp.dot(x_ref[...].astype(_BF), w_ref[...], preferred_element_type=_F32) + b_ref[...]


def _linear(x, w_bf, b, tm):
    m, k = x.shape
    n = w_bf.shape[1]
    return pl.pallas_call(
        _linear_kernel,
        grid=(m // tm,),
        in_specs=[pl.BlockSpec((tm, k), lambda i: (i, 0)),
                  pl.BlockSpec((k, n), lambda i: (0, 0)),
                  pl.BlockSpec((1, n), lambda i: (0, 0))],
        out_specs=pl.BlockSpec((tm, n), lambda i: (i, 0)),
        out_shape=jax.ShapeDtypeStruct((m, n), _F32),
        compiler_params=_cparams("parallel"),
        name="linear",
    )(x, w_bf, b.reshape(1, n))


def _layer_norm_rows(y, g, b):
    mu = jnp.mean(y, axis=-1, keepdims=True)
    yc = y - mu
    var = jnp.mean(yc * yc, axis=-1, keepdims=True)
    return yc * lax.rsqrt(var + LN_EPS) * g + b


def _even_out_kernel(alpha, oc_ref, os_ref, ow_ref, gl_ref, of_ref, ex_ref, wa_ref, wf_ref, x_ref,
                     g_ref, b_ref, y_ref):
    wa = NSA_HEADS * HEAD_DIM
    gexp = _dot3(jax.nn.sigmoid(gl_ref[...]), ex_ref[...])
    o_a = (gexp[:, 0:wa] * oc_ref[...] + gexp[:, wa:2 * wa] * os_ref[...] + gexp[:, 2 * wa:3 * wa] * ow_ref[...])
    m = jnp.dot(o_a.astype(_BF), wa_ref[...], preferred_element_type=_F32)
    m += jnp.dot(of_ref[...].astype(_BF), wf_ref[...], preferred_element_type=_F32)
    y_ref[...] = _layer_norm_rows(alpha * x_ref[...] + m, g_ref[...], b_ref[...])


def _even_out(alpha, o_c, o_s, o_w, gl, o_f, w_out_bf, x, g, b, tm):
    n, d = x.shape
    wa = NSA_HEADS * HEAD_DIM
    wf = FOX_HEADS * HEAD_DIM
    ex_np = np.zeros((LANES, 3 * wa), np.float32)
    for j in range(3):
        for h in range(NSA_HEADS):
            ex_np[j * NSA_HEADS + h, j * wa + h * HEAD_DIM:j * wa + (h + 1) * HEAD_DIM] = 1.0
    ex = jnp.asarray(ex_np, _BF)
    gl = _pad_last(gl, LANES)
    row = lambda w: pl.BlockSpec((tm, w), lambda i: (i, 0))
    full = lambda a: pl.BlockSpec(a.shape, lambda i: (0,) * a.ndim)
    args = (o_c, o_s, o_w, gl, o_f, ex, w_out_bf[:wa], w_out_bf[wa:], x, g.reshape(1, d), b.reshape(1, d))
    specs = [row(wa), row(wa), row(wa), row(LANES), row(wf), full(ex), full(args[6]), full(args[7]),
             row(d), full(args[9]), full(args[10])]
    return pl.pallas_call(
        functools.partial(_even_out_kernel, alpha),
        grid=(n // tm,), in_specs=specs, out_specs=row(d),
        out_shape=jax.ShapeDtypeStruct((n, d), _F32),
        compiler_params=_cparams("parallel"), name="even_out_ln",
    )(*args)


def _odd_out_kernel(alpha, o_ref, w_ref, x_ref, g_ref, b_ref, y_ref):
    m = jnp.dot(o_ref[...].astype(_BF), w_ref[...], preferred_element_type=_F32)
    y_ref[...] = _layer_norm_rows(alpha * x_ref[...] + m, g_ref[...], b_ref[...])


def _odd_out(alpha, o, w_out_bf, x, g, b, tm):
    n, d = x.shape
    row = lambda w: pl.BlockSpec((tm, w), lambda i: (i, 0))
    full = lambda shp: pl.BlockSpec(shp, lambda i: (0,) * len(shp))
    return pl.pallas_call(
        functools.partial(_odd_out_kernel, alpha),
        grid=(n // tm,),
        in_specs=[row(o.shape[1]), full(w_out_bf.shape), row(d), full((1, d)), full((1, d))],
        out_specs=row(d), out_shape=jax.ShapeDtypeStruct((n, d), _F32),
        compiler_params=_cparams("parallel"), name="odd_out_ln",
    )(o, w_out_bf, x, g.reshape(1, d), b.reshape(1, d))


def _moe_kernel(alpha, x_ref, wr_ref, br_ref, w1_ref, w3_ref, w2_ref, g_ref, b_ref, y_ref,
                gate_ref, acc_ref):
    e = pl.program_id(1)
    x = x_ref[...]

    @pl.when(e == 0)
    def _route():
        logits = jnp.dot(x.astype(_BF), wr_ref[...], preferred_element_type=_F32) + br_ref[...]
        lane = lax.broadcasted_iota(jnp.int32, logits.shape, 1)
        big = jnp.int32(1 << 20)
        is_g = lane < N_GROUPS
        lg = jnp.where(is_g, logits, NEG_INF)
        mg = jnp.max(lg, axis=-1, keepdims=True)
        sg = jnp.sum(jnp.where(is_g, jnp.exp(lg - mg), 0.0), axis=-1, keepdims=True)
        p_top = 1.0 / sg
        g_top = jnp.min(jnp.where(lg == mg, lane, big), axis=-1, keepdims=True)
        lo = N_GROUPS + EXPERTS_PER_GROUP * g_top
        in_grp = (lane >= lo) & (lane < lo + EXPERTS_PER_GROUP)
        le = jnp.where(in_grp, logits, NEG_INF)
        me = jnp.max(le, axis=-1, keepdims=True)
        se = jnp.sum(jnp.where(in_grp, jnp.exp(le - me), 0.0), axis=-1, keepdims=True)
        i1 = jnp.min(jnp.where(le == me, lane, big), axis=-1, keepdims=True)
        le2 = jnp.where(lane == i1, NEG_INF, le)
        m2 = jnp.max(le2, axis=-1, keepdims=True)
        i2 = jnp.min(jnp.where(le2 == m2, lane, big), axis=-1, keepdims=True)
        w1 = 1.0 / se
        w2 = jnp.exp(m2 - me) / se
        tot = w1 + w2
        gate_ref[...] = jnp.where(lane == i1, p_top * w1 / tot,
                                  jnp.where(lane == i2, p_top * w2 / tot, 0.0))
        acc_ref[...] = jnp.zeros_like(acc_ref)

    xb = x.astype(_BF)
    h1 = jnp.dot(xb, w1_ref[0], preferred_element_type=_F32)
    h3 = jnp.dot(xb, w3_ref[0], preferred_element_type=_F32)
    gate = gate_ref[...]
    lane = lax.broadcasted_iota(jnp.int32, gate.shape, 1)
    ge = jnp.sum(jnp.where(lane == e + N_GROUPS, gate, 0.0), axis=-1, keepdims=True)
    h = (jax.nn.silu(h1) * h3) * ge
    acc_ref[...] += jnp.dot(h.astype(_BF), w2_ref[0], preferred_element_type=_F32)

    @pl.when(e == N_EXPERTS - 1)
    def _finish():
        y_ref[...] = _layer_norm_rows(alpha * x + acc_ref[...], g_ref[...], b_ref[...])


def _moe_ln(alpha, x, wg, bg, we, be, w1_bf, w3_bf, w2_bf, g, b, tm):
    n, d = x.shape
    f = w1_bf.shape[2]
    wr = jnp.zeros((d, LANES), _F32).at[:, :N_GROUPS].set(wg).at[:, N_GROUPS:N_GROUPS + N_EXPERTS].set(we)
    wr = wr.astype(_BF)
    br = jnp.zeros((1, LANES), _F32).at[0, :N_GROUPS].set(bg).at[0, N_GROUPS:N_GROUPS + N_EXPERTS].set(be)
    return pl.pallas_call(
        functools.partial(_moe_kernel, alpha),
        grid=(n // tm, N_EXPERTS),
        in_specs=[pl.BlockSpec((tm, d), lambda i, e: (i, 0)),
                  pl.BlockSpec((d, LANES), lambda i, e: (0, 0)),
                  pl.BlockSpec((1, LANES), lambda i, e: (0, 0)),
                  pl.BlockSpec((1, d, f), lambda i, e: (e, 0, 0)),
                  pl.BlockSpec((1, d, f), lambda i, e: (e, 0, 0)),
                  pl.BlockSpec((1, f, d), lambda i, e: (e, 0, 0)),
                  pl.BlockSpec((1, d), lambda i, e: (0, 0)),
                  pl.BlockSpec((1, d), lambda i, e: (0, 0))],
        out_specs=pl.BlockSpec((tm, d), lambda i, e: (i, 0)),
        out_shape=jax.ShapeDtypeStruct((n, d), _F32),
        scratch_shapes=[pltpu.VMEM((tm, LANES), _F32), pltpu.VMEM((tm, d), _F32)],
        compiler_params=_cparams("parallel", "arbitrary"), name="moe_ln",
    )(x, wr, br, w1_bf, w3_bf, w2_bf, g.reshape(1, d), b.reshape(1, d))


def _flash_kernel(shared, qi_ref, ki_ref, bi_ref, first_ref, last_ref, *refs):
    if shared:
        qa_ref, ka_ref, v_ref, bias_ref, qb_ref, kb_ref, o_ref, m_ref, l_ref, acc_ref = refs
    else:
        qa_ref, ka_ref, v_ref, bias_ref, o_ref, m_ref, l_ref, acc_ref = refs
    step = pl.program_id(1)

    @pl.when(first_ref[step] == 1)
    def _init():
        m_ref[...] = jnp.full_like(m_ref, NEG_INF)
        l_ref[...] = jnp.zeros_like(l_ref)
        acc_ref[...] = jnp.zeros_like(acc_ref)

    nt = (((1,), (1,)), ((), ()))
    s = lax.dot_general(qa_ref[0], ka_ref[0], nt, preferred_element_type=_F32)
    if shared:
        s = s + lax.dot_general(qb_ref[0], kb_ref[...], nt, preferred_element_type=_F32)
    s = s + bias_ref[0, 0]
    m_old = m_ref[...]
    m_new = jnp.maximum(m_old, jnp.max(s, axis=-1, keepdims=True))
    alpha = jnp.exp(m_old - m_new)
    p = jnp.exp(s - m_new)
    l_ref[...] = alpha * l_ref[...] + jnp.sum(p, axis=-1, keepdims=True)
    acc_ref[...] = alpha * acc_ref[...] + jnp.dot(p.astype(_BF), v_ref[0], preferred_element_type=_F32)
    m_ref[...] = m_new

    @pl.when(last_ref[step] == 1)
    def _done():
        o_ref[0] = acc_ref[...] / jnp.maximum(l_ref[...], 1e-30)


def _flash(qa, ka, v, bias, pairs, tqr, tk, qb=None, kb=None):
    bh, rq, kdim = qa.shape
    dv = v.shape[2]
    hb = bias.shape[0]
    shared = qb is not None
    pairs = np.asarray(pairs, np.int32)
    qi, ki, bi = pairs[:, 0], pairs[:, 1], pairs[:, 2]
    first = np.concatenate([[1], (qi[1:] != qi[:-1]).astype(np.int32)]).astype(np.int32)
    last = np.concatenate([(qi[1:] != qi[:-1]).astype(np.int32), [1]]).astype(np.int32)
    in_specs = [pl.BlockSpec((1, tqr, kdim), lambda b, s, qi, ki, bi, f, l: (b, qi[s], 0)),
                pl.BlockSpec((1, tk, kdim), lambda b, s, qi, ki, bi, f, l: (b, ki[s], 0)),
                pl.BlockSpec((1, tk, dv), lambda b, s, qi, ki, bi, f, l: (b, ki[s], 0)),
                pl.BlockSpec((1, 1, tqr, tk), lambda b, s, qi, ki, bi, f, l: (b % hb, bi[s], 0, 0))]
    args = [qa, ka, v, bias]
    if shared:
        kbd = qb.shape[2]
        in_specs += [pl.BlockSpec((1, tqr, kbd), lambda b, s, qi, ki, bi, f, l: (b, qi[s], 0)),
                     pl.BlockSpec((tk, kbd), lambda b, s, qi, ki, bi, f, l: (ki[s], 0))]
        args += [qb, kb]
    grid_spec = pltpu.PrefetchScalarGridSpec(
        num_scalar_prefetch=5,
        grid=(bh, len(qi)),
        in_specs=in_specs,
        out_specs=pl.BlockSpec((1, tqr, dv), lambda b, s, qi, ki, bi, f, l: (b, qi[s], 0)),
        scratch_shapes=[pltpu.VMEM((tqr, 1), _F32), pltpu.VMEM((tqr, 1), _F32), pltpu.VMEM((tqr, dv), _F32)],
    )
    return pl.pallas_call(
        functools.partial(_flash_kernel, shared), grid_spec=grid_spec,
        out_shape=jax.ShapeDtypeStruct((bh, rq, dv), _F32),
        compiler_params=_cparams("parallel", "arbitrary"), name="flash",
    )(jnp.asarray(qi), jnp.asarray(ki), jnp.asarray(bi), jnp.asarray(first), jnp.asarray(last), *args)


def _causal_pairs(nq, far_tile, lookback=None):
    out = []
    for q in range(nq):
        lo = 0 if lookback is None else max(0, q - lookback)
        for k in range(lo, q + 1):
            out.append((q, k, min(q - k, far_tile)))
    return out


def _compress_kernel(x_ref, pe_ref, w1a_ref, w1b_ref, b1_ref, w2_ref, b2_ref, o_ref):
    x = x_ref[0, 0]
    pe = pe_ref[0]
    hf = jnp.dot((x + pe[0:1]).astype(_BF), w1a_ref[0], preferred_element_type=_F32)
    hs = jnp.dot((x + pe[1:2]).astype(_BF), w1b_ref[0], preferred_element_type=_F32)
    hs_next = pltpu.roll(hs, hs.shape[0] - 1, 0)
    h = jax.nn.gelu(hf + hs_next + b1_ref[0], approximate=True)
    o_ref[0, 0] = jnp.dot(h.astype(_BF), w2_ref[0], preferred_element_type=_F32) + b2_ref[0]


def _nsa_compress(kv_c, pe, w1, b1, w2, b2):
    bsz, length = kv_c.shape[:2]
    nch = length // CMP_STRIDE
    sg = 2 * NSA_KV_HEADS
    flat = CMP_STRIDE * HEAD_DIM
    x = kv_c.reshape(bsz, nch, CMP_STRIDE, sg, HEAD_DIM).transpose(3, 0, 1, 2, 4).reshape(sg, bsz, nch, flat)
    pe2 = pe.reshape(2, CMP_STRIDE, 2, HEAD_DIM).transpose(2, 0, 1, 3).reshape(2, 2, flat)
    w1r = w1.reshape(2, CMP_STRIDE, 2, HEAD_DIM, CMP_HID).transpose(2, 0, 1, 3, 4).reshape(2, 2, flat, CMP_HID)
    w1r = w1r.astype(_BF)
    smap = lambda s, b: (s // NSA_KV_HEADS, 0, 0)
    return pl.pallas_call(
        _compress_kernel,
        grid=(sg, bsz),
        in_specs=[pl.BlockSpec((1, 1, nch, flat), lambda s, b: (s, b, 0, 0)),
                  pl.BlockSpec((1, 2, flat), smap),
                  pl.BlockSpec((1, flat, CMP_HID), smap),
                  pl.BlockSpec((1, flat, CMP_HID), smap),
                  pl.BlockSpec((1, 1, CMP_HID), smap),
                  pl.BlockSpec((1, CMP_HID, HEAD_DIM), smap),
                  pl.BlockSpec((1, 1, HEAD_DIM), smap)],
        out_specs=pl.BlockSpec((1, 1, nch, HEAD_DIM), lambda s, b: (s, b, 0, 0)),
        out_shape=jax.ShapeDtypeStruct((sg, bsz, nch, HEAD_DIM), _F32),
        compiler_params=_cparams("parallel", "parallel"), name="nsa_compress",
    )(x, pe2, w1r[:, 0], w1r[:, 1], b1.reshape(2, 1, CMP_HID), w2.astype(_BF), b2.reshape(2, 1, HEAD_DIM))


def _cmp_select_kernel(theta, tq, qpos0, n_sel, tab_ref, qa_ref, kc_ref, vc_ref, msel_ref, o_ref, selb_ref):
    g = pl.program_id(0) % NSA_KV_HEADS
    i = pl.program_id(1)
    ncp = kc_ref.shape[1]
    nsl = selb_ref.shape[2]
    t = qpos0 + i * tq + lax.broadcasted_iota(jnp.int32, (tq, ncp), 0)
    n = lax.broadcasted_iota(jnp.int32, (tq, ncp), 1)
    d = t - (n * CMP_STRIDE + (CMP_LEN - 1))
    ok = d >= 0
    ind = [d >= th for th in theta]
    kc = kc_ref[0]
    vc = vc_ref[0]
    imp = jnp.zeros((tq, ncp), _F32)
    for r in range(NSA_GROUP):
        base = (g * NSA_GROUP + r) * N_BUCKETS
        bias = jnp.full((tq, ncp), tab_ref[base], _F32)
        for k in range(1, N_BUCKETS):
            bias = bias + jnp.where(ind[k - 1], tab_ref[base + k] - tab_ref[base + k - 1], 0.0)
        s = lax.dot_general(qa_ref[0, r * tq:(r + 1) * tq, :], kc, (((1,), (1,)), ((), ())),
                            preferred_element_type=_F32) + bias
        s = jnp.where(ok, s, NEG_INF)
        m = jnp.max(s, axis=-1, keepdims=True)
        e = jnp.where(ok, jnp.exp(s - m), 0.0)
        p = e / jnp.maximum(jnp.sum(e, axis=-1, keepdims=True), 1e-30)
        o_ref[0, r * tq:(r + 1) * tq, :] = jnp.dot(p.astype(_BF), vc, preferred_element_type=_F32)
        imp = imp + p
    p_s = _dot3(imp, msel_ref[...])
    j = lax.broadcasted_iota(jnp.int32, (tq, nsl), 1)
    qb = (qpos0 + i * tq + lax.broadcasted_iota(jnp.int32, (tq, nsl), 0)) >> int(math.log2(SEL_BLOCK))
    valid = j <= qb
    forced = (j == 0) | (j == qb) | (j == qb - 1)
    score = jnp.where(valid, jnp.where(forced, FORCE_SCORE, p_s), NEG_INF)
    sel = jnp.zeros((tq, nsl), jnp.bool_)
    big = jnp.int32(1 << 20)
    for _ in range(n_sel):
        mx = jnp.max(score, axis=-1, keepdims=True)
        firsti = jnp.min(jnp.where(score == mx, j, big), axis=-1, keepdims=True)
        hit = j == firsti
        sel = sel | (hit & (mx > 0.5 * NEG_INF))
        score = jnp.where(hit, -3e38, score)
    selb_ref[0] = jnp.where(sel, 0.0, NEG_INF).astype(_BF)


def _cmp_select(qa, kc, vc, tab, tq, qpos0, n_blocks):
    bg, rows, _ = qa.shape
    ncp = kc.shape[1]
    nq = rows // (NSA_GROUP * tq)
    nsl = _round_up(n_blocks, LANES)
    nn = np.arange(ncp)[:, None]
    jj = np.arange(nsl)[None, :]
    msel = ((nn >= CMP_PER_SEL * jj - 1) & (nn <= CMP_PER_SEL * jj + CMP_PER_SEL - 1) & (jj < n_blocks))
    msel = jnp.asarray(msel.astype(np.float32), _BF)
    tabf = tab[:, :NSA_HEADS].T.reshape(-1)
    kern = functools.partial(_cmp_select_kernel, _bucket_thresholds(), tq, qpos0, min(NSA_TOPN, n_blocks))
    return pl.pallas_call(
        kern,
        grid=(bg, nq),
        in_specs=[pl.BlockSpec(memory_space=pltpu.SMEM),
                  pl.BlockSpec((1, NSA_GROUP * tq, HEAD_DIM), lambda b, i: (b, i, 0)),
                  pl.BlockSpec((1, ncp, HEAD_DIM), lambda b, i: (b, 0, 0)),
                  pl.BlockSpec((1, ncp, HEAD_DIM), lambda b, i: (b, 0, 0)),
                  pl.BlockSpec((ncp, nsl), lambda b, i: (0, 0))],
        out_specs=[pl.BlockSpec((1, NSA_GROUP * tq, HEAD_DIM), lambda b, i: (b, i, 0)),
                   pl.BlockSpec((1, tq, nsl), lambda b, i: (b, i, 0))],
        out_shape=[jax.ShapeDtypeStruct((bg, rows, HEAD_DIM), _F32),
                   jax.ShapeDtypeStruct((bg, nq * tq, nsl), _BF)],
        compiler_params=_cparams("parallel", "parallel"), name="nsa_cmp_select",
    )(tabf, qa, kc, vc, msel)


def _moba_select_kernel(tq, qpos0, n_blocks, q_ref, km_ref, selb_ref):
    i = pl.program_id(1)
    nbl = selb_ref.shape[2]
    gate = lax.dot_general(q_ref[0], km_ref[0], (((1,), (1,)), ((), ())), preferred_element_type=_F32)
    j = lax.broadcasted_iota(jnp.int32, (tq, nbl), 1)
    qb = (qpos0 + i * tq + lax.broadcasted_iota(jnp.int32, (tq, nbl), 0)) >> int(math.log2(MOBA_BLOCK))
    score = jnp.where(j < qb, gate, NEG_INF)
    sel = j == qb
    big = jnp.int32(1 << 20)
    for _ in range(min(MOBA_TOPK, n_blocks)):
        mx = jnp.max(score, axis=-1, keepdims=True)
        firsti = jnp.min(jnp.where(score == mx, j, big), axis=-1, keepdims=True)
        hit = j == firsti
        sel = sel | (hit & (mx > 0.5 * NEG_INF))
        score = jnp.where(hit, -3e38, score)
    selb_ref[0] = jnp.where(sel, 0.0, NEG_INF).astype(_BF)


def _moba_select(q_bf, kmean_bf, tq, qpos0, n_blocks):
    bh, tqs, _ = q_bf.shape
    nbl = kmean_bf.shape[1]
    return pl.pallas_call(
        functools.partial(_moba_select_kernel, tq, qpos0, n_blocks),
        grid=(bh, tqs // tq),
        in_specs=[pl.BlockSpec((1, tq, HEAD_DIM), lambda b, i: (b, i, 0)),
                  pl.BlockSpec((1, nbl, HEAD_DIM), lambda b, i: (b, 0, 0))],
        out_specs=pl.BlockSpec((1, tq, nbl), lambda b, i: (b, i, 0)),
        out_shape=jax.ShapeDtypeStruct((bh, tqs, nbl), _BF),
        compiler_params=_cparams("parallel", "parallel"), name="moba_select",
    )(q_bf, kmean_bf)


def _block_mean_kernel(k_ref, o_ref):
    o_ref[0, 0] = jnp.sum(k_ref[0], axis=0, keepdims=True) * (1.0 / MOBA_BLOCK)


def _block_mean(k):
    bsz, length, w = k.shape
    nb = length // MOBA_BLOCK
    out = pl.pallas_call(
        _block_mean_kernel, grid=(bsz, nb),
        in_specs=[pl.BlockSpec((1, MOBA_BLOCK, w), lambda b, j: (b, j, 0))],
        out_specs=pl.BlockSpec((1, 1, 1, w), lambda b, j: (b, j, 0, 0)),
        out_shape=jax.ShapeDtypeStruct((bsz, nb, 1, w), _F32),
        compiler_params=_cparams("parallel", "parallel"), name="moba_block_mean",
    )(k)
    return out.reshape(bsz, nb, w)


def _logf_cumsum_kernel(n_new, x_ref, u_ref, lf_ref, hi_ref, mid_ref, lo_ref):
    length = x_ref.shape[1]
    x = x_ref[...]
    col = lax.broadcasted_iota(jnp.int32, x.shape, 1)
    ls = jnp.minimum(x, 0.0) - jnp.log1p(jnp.exp(-jnp.abs(x)))
    lf = jnp.where(col >= length - n_new, ls, x)
    lf_ref[...] = lf
    u = u_ref[...]
    carry = jnp.zeros((x.shape[0], 1), _F32)
    for k in range(length // LANES):
        blk = _dot3(lf[:, k * LANES:(k + 1) * LANES], u) + carry
        hi, mid, lo = _split3(blk)
        hi_ref[:, k * LANES:(k + 1) * LANES] = hi
        mid_ref[:, k * LANES:(k + 1) * LANES] = mid
        lo_ref[:, k * LANES:(k + 1) * LANES] = lo
        carry = blk[:, LANES - 1:LANES]


def _logf_cumsum(x, n_new, rows_per_step):
    rows, length = x.shape
    u = jnp.asarray(np.triu(np.ones((LANES, LANES), np.float32)), _BF)
    spec = pl.BlockSpec((rows_per_step, length), lambda i: (i, 0))
    lf, hi, mid, lo = pl.pallas_call(
        functools.partial(_logf_cumsum_kernel, n_new),
        grid=(rows // rows_per_step,),
        in_specs=[spec, pl.BlockSpec((LANES, LANES), lambda i: (0, 0))],
        out_specs=[spec] * 4,
        out_shape=[jax.ShapeDtypeStruct((rows, length), _F32)] + [jax.ShapeDtypeStruct((rows, length), _BF)] * 3,
        compiler_params=_cparams("parallel"), name="fox_logf_cumsum",
    )(x, u)
    return lf, (hi, mid, lo)


def _gather_kernel(n_pg, pt_ref, *refs):
    ins, out = refs[:n_pg], refs[n_pg]
    for k in range(n_pg):
        out[0, k * PAGE_SIZE:(k + 1) * PAGE_SIZE, :] = ins[k][0].astype(out.dtype)


def _gather_pages(pool, page_table, col_block, n_col_blocks, out_dtype):
    bsz, n_pages = page_table.shape
    w = pool.shape[2] // n_col_blocks
    n_pg = PAGES_PER_STEP if n_pages % PAGES_PER_STEP == 0 else 1

    def in_map(k):
        return lambda b, p, pt: (pt[b * n_pages + p * n_pg + k], 0, col_block)

    grid_spec = pltpu.PrefetchScalarGridSpec(
        num_scalar_prefetch=1,
        grid=(bsz, n_pages // n_pg),
        in_specs=[pl.BlockSpec((1, PAGE_SIZE, w), in_map(k)) for k in range(n_pg)],
        out_specs=pl.BlockSpec((1, n_pg * PAGE_SIZE, w), lambda b, p, pt: (b, p, 0)),
    )
    return pl.pallas_call(
        functools.partial(_gather_kernel, n_pg), grid_spec=grid_spec,
        out_shape=jax.ShapeDtypeStruct((bsz, n_pages * PAGE_SIZE, w), out_dtype),
        compiler_params=_cparams("parallel", "arbitrary"), name="page_gather",
    )(page_table.reshape(-1), *([pool] * n_pg))


def _rel_bias_tiles(tab, qpos, kpos, valid):
    d = qpos[:, None] - kpos[None, :]
    idx = _t5_bucket_np(d)
    vals = jnp.transpose(tab[jnp.asarray(idx)], (2, 0, 1))
    return jnp.where(jnp.asarray(valid)[None], vals, NEG_INF).astype(_F32)


def _group_rows(b):
    h, r, c = b.shape
    return b.reshape(h // NSA_GROUP, NSA_GROUP * r, c)


def _pad_last(x, width):
    return jnp.pad(x, [(0, 0)] * (x.ndim - 1) + [(0, width - x.shape[-1])])


def _pad_axis(x, axis, size):
    pads = [(0, 0)] * x.ndim
    pads[axis] = (0, size - x.shape[axis])
    return jnp.pad(x, pads)


def _nsa_q_rows(q, tq):
    b, t, _, dh = q.shape
    nq = t // tq
    x = q.reshape(b, nq, tq, NSA_KV_HEADS, NSA_GROUP, dh).transpose(0, 3, 1, 4, 2, 5)
    return x.reshape(b * NSA_KV_HEADS, nq * NSA_GROUP * tq, dh)


def _nsa_rows_back(o, b, t, tq):
    nq = t // tq
    x = o.reshape(b, NSA_KV_HEADS, nq, NSA_GROUP, tq, HEAD_DIM).transpose(0, 2, 4, 1, 3, 5)
    return x.reshape(b, t, NSA_HEADS * HEAD_DIM)


def _heads_major(x):
    b, t, h, dh = x.shape
    return x.transpose(0, 2, 1, 3).reshape(b * h, t, dh)


def _heads_back(o, b, h):
    t, dh = o.shape[1:]
    return o.reshape(b, h, t, dh).transpose(0, 2, 1, 3).reshape(b, t, h * dh)


def _nsa_attention(q_a, kv_c_rows, kv_s_rows, kv_w_rows, w_start, qpos0, tq, tk, tab, cmp_w, prompt):
    b, t_real = q_a.shape[:2]
    t = _round_up(t_real, tq)
    nq = t // tq
    rows = NSA_GROUP * tq
    g = NSA_KV_HEADS
    tabn = tab[:, :NSA_HEADS]
    q_rows = _nsa_q_rows(_pad_axis(q_a * SCALE, 1, t), tq).astype(_BF)
    qpos = qpos0 + np.arange(t)

    length = kv_c_rows.shape[1]
    ncp = _round_up(length // CMP_STRIDE, LANES)
    cmp_tok = _nsa_compress(kv_c_rows, *cmp_w)
    cmp_tok = _pad_axis(cmp_tok, 2, ncp).astype(_BF)
    kc = cmp_tok[:g].transpose(1, 0, 2, 3).reshape(b * g, ncp, HEAD_DIM)
    vc = cmp_tok[g:].transpose(1, 0, 2, 3).reshape(b * g, ncp, HEAD_DIM)
    n_blocks = kv_s_rows.shape[1] // SEL_BLOCK
    o_c, selb = _cmp_select(q_rows, kc, vc, tab, tq, qpos0, n_blocks)
    nsl = selb.shape[2]

    ls = kv_s_rows.shape[1]
    lsp = _round_up(ls, tk)
    ka_dim = _round_up(HEAD_DIM + nsl, LANES)
    sel_rows = jnp.broadcast_to(selb.reshape(b * g, nq, 1, tq, nsl), (b * g, nq, NSA_GROUP, tq, nsl))
    sel_rows = sel_rows.reshape(b * g, nq * rows, nsl)
    ks = _pad_axis(_heads_major(kv_s_rows[:, :, 0]), 1, lsp).astype(_BF)
    vs = _pad_axis(_heads_major(kv_s_rows[:, :, 1]), 1, lsp).astype(_BF)
    onehot = jnp.asarray((np.arange(lsp)[:, None] // SEL_BLOCK == np.arange(nsl)[None, :]).astype(np.float32), _BF)
    if prompt:
        assert tq == tk
        qa = _pad_last(jnp.concatenate([q_rows, sel_rows], axis=-1), ka_dim)
        ka = _pad_last(jnp.concatenate([ks, jnp.broadcast_to(onehot[None], (b * g, lsp, nsl))], axis=-1), ka_dim)
        far = 2
        tiles = [_group_rows(_rel_bias_tiles(tabn, np.arange(tq) + k * tq, np.arange(tk),
                                             (np.arange(tq)[:, None] + k * tq) >= np.arange(tk)[None, :]))
                 for k in range(far + 1)]
        bias = jnp.stack(tiles, axis=1)
        o_s = _flash(qa, ka, vs, bias, _causal_pairs(nq, far), rows, tk)
    else:
        assert nq == 1
        kpos = np.arange(lsp)
        bias = _group_rows(_rel_bias_tiles(tabn, qpos, kpos, qpos[:, None] >= kpos[None, :]))
        bias = bias.reshape(g, rows, lsp // tk, tk).transpose(0, 2, 1, 3)
        pairs = [(0, k, k) for k in range(lsp // tk)]
        o_s = _flash(q_rows, ks, vs, bias, pairs, rows, tk, qb=sel_rows, kb=onehot)

    lw = kv_w_rows.shape[1]
    kw = _heads_major(kv_w_rows[:, :, 0])
    vw = _heads_major(kv_w_rows[:, :, 1])
    if prompt:
        assert w_start == 0 and NSA_WINDOW % tk == 0 and lw % tk == 0
        look = NSA_WINDOW // tk
        tiles = []
        for k in range(look + 1):
            dd = (np.arange(tq)[:, None] + k * tq) - np.arange(tk)[None, :]
            tiles.append(_group_rows(_rel_bias_tiles(tabn, np.arange(tq) + k * tq, np.arange(tk),
                                                     (dd >= 0) & (dd < NSA_WINDOW))))
        bias_w = jnp.stack(tiles, axis=1)
        pairs_w = _causal_pairs(nq, look, lookback=look)
        lwp = _round_up(lw, tk)
        tkw = tk
    else:
        lwp = _round_up(lw, LANES)
        tkw = lwp
        wpos = w_start + np.arange(lwp)
        dd = qpos[:, None] - wpos[None, :]
        okw = (dd >= 0) & (dd < NSA_WINDOW) & (wpos[None, :] >= 0) & (np.arange(lwp)[None, :] < lw)
        bias_w = _group_rows(_rel_bias_tiles(tabn, qpos, wpos, okw))[:, None]
        pairs_w = [(0, 0, 0)]
    kw = _pad_axis(kw, 1, lwp).astype(_BF)
    vw = _pad_axis(vw, 1, lwp).astype(_BF)
    o_w = _flash(q_rows, kw, vw, bias_w, pairs_w, rows, tkw)

    back = lambda o: _nsa_rows_back(o, b, t, tq)[:, :t_real]
    return back(o_c), back(o_s), back(o_w)


def _fox_attention(q_f, cq, kv_f_rows, ck, qpos0, tq, tk, prompt):
    b, t_real, h, _ = q_f.shape
    t = _round_up(t_real, tq)
    length = kv_f_rows.shape[1]
    lp = _round_up(length, tk)
    qh = _pad_axis(_heads_major(q_f * SCALE), 1, t).astype(_BF)
    ones = jnp.ones((b * h, t, 3), _BF)
    qa = _pad_last(jnp.concatenate([qh] + [_pad_axis(c, 1, t)[..., None] for c in cq] + [ones], axis=-1), LANES)
    kh = _pad_axis(_heads_major(kv_f_rows[:, :, 0]), 1, lp).astype(_BF)
    vh = _pad_axis(_heads_major(kv_f_rows[:, :, 1]), 1, lp).astype(_BF)
    onesk = jnp.ones((b * h, lp, 3), _BF)
    ka = _pad_last(jnp.concatenate([kh, onesk] + [-_pad_axis(c, 1, lp)[..., None] for c in ck], axis=-1), LANES)
    if prompt:
        assert tq == tk
        diag = np.where(np.arange(tq)[:, None] >= np.arange(tk)[None, :], 0.0, NEG_INF).astype(np.float32)
        bias = jnp.asarray(np.stack([diag, np.zeros_like(diag)])[None])
        pairs = _causal_pairs(t // tq, 1)
    else:
        qpos = qpos0 + np.arange(t)
        kpos = np.arange(lp)
        okm = (kpos[None, :] <= qpos[:, None]) & (kpos[None, :] < length)
        m = np.where(okm, 0.0, NEG_INF).astype(np.float32)
        bias = jnp.asarray(m.reshape(1, t, lp // tk, tk).transpose(0, 2, 1, 3))
        pairs = [(0, k, k) for k in range(lp // tk)]
    o = _flash(qa, ka, vh, bias, pairs, tq, tk)
    return _heads_back(o, b, h)[:, :t_real]


def _moba_attention(q, kv_rows, kmean, qpos0, tq, tk, tab, prompt):
    b, t_real, h, _ = q.shape
    t = _round_up(t_real, tq)
    length = kv_rows.shape[1]
    n_blocks = length // MOBA_BLOCK
    nbl = _round_up(n_blocks, LANES)
    lp = _round_up(length, tk)
    qpad = _pad_axis(q, 1, t)
    km = _pad_axis(kmean.reshape(b, kmean.shape[1], h, HEAD_DIM).transpose(0, 2, 1, 3), 2, nbl)
    km = km.reshape(b * h, nbl, HEAD_DIM).astype(_BF)
    tqs = min(t, 1024)
    selb = _moba_select(_heads_major(qpad).astype(_BF), km, tqs, qpos0, n_blocks)
    ka_dim = _round_up(HEAD_DIM + nbl, LANES)
    qs = _heads_major(qpad * SCALE).astype(_BF)
    kh = _pad_axis(_heads_major(kv_rows[:, :, 0]), 1, lp).astype(_BF)
    vh = _pad_axis(_heads_major(kv_rows[:, :, 1]), 1, lp).astype(_BF)
    onehot = jnp.asarray((np.arange(lp)[:, None] // MOBA_BLOCK == np.arange(nbl)[None, :]).astype(np.float32), _BF)
    tabm = tab[:, :MOBA_HEADS]
    if prompt:
        assert tq == tk and tq > T5_MAX_DISTANCE
        qa = _pad_last(jnp.concatenate([qs, selb], axis=-1), ka_dim)
        ka = _pad_last(jnp.concatenate([kh, jnp.broadcast_to(onehot[None], (b * h, lp, nbl))], axis=-1), ka_dim)
        far = 2
        tiles = [_rel_bias_tiles(tabm, np.arange(tq) + k * tq, np.arange(tk),
                                 (np.arange(tq)[:, None] + k * tq) >= np.arange(tk)[None, :])
                 for k in range(far + 1)]
        bias = jnp.stack(tiles, axis=1)
        o = _flash(qa, ka, vh, bias, _causal_pairs(t // tq, far), tq, tk)
    else:
        assert t == tq
        qpos = qpos0 + np.arange(t)
        kpos = np.arange(lp)
        bias = _rel_bias_tiles(tabm, qpos, kpos, qpos[:, None] >= kpos[None, :])
        bias = bias.reshape(h, t, lp // tk, tk).transpose(0, 2, 1, 3)
        pairs = [(0, k, k) for k in range(lp // tk)]
        o = _flash(qs, kh, vh, bias, pairs, tq, tk, qb=selb, kb=onehot)
    return _heads_back(o, b, h)[:, :t_real]


_EVEN_SIZES = (NSA_HEADS * HEAD_DIM, 2 * NSA_KV_HEADS * HEAD_DIM, 2 * NSA_KV_HEADS * HEAD_DIM,
               2 * NSA_KV_HEADS * HEAD_DIM, 3 * NSA_HEADS, FOX_HEADS * HEAD_DIM,
               2 * FOX_HEADS * HEAD_DIM, FOX_HEADS)
_EVEN_CUTS = tuple(int(c) for c in np.cumsum(_EVEN_SIZES)[:-1])


def _even_split(z, b, t):
    q_a, kv_c, kv_s, kv_w, g_a, q_f, kv_f, f_logit = jnp.split(z.reshape(b, t, -1), _EVEN_CUTS, axis=-1)
    kvshape = (b, t, 2, NSA_KV_HEADS, HEAD_DIM)
    g_bm = g_a.reshape(b, t, NSA_HEADS, 3).transpose(0, 1, 3, 2).reshape(b, t, 3 * NSA_HEADS)
    return (q_a.reshape(b, t, NSA_HEADS, HEAD_DIM), kv_c.reshape(kvshape), kv_s.reshape(kvshape),
            kv_w.reshape(kvshape), g_bm, q_f.reshape(b, t, FOX_HEADS, HEAD_DIM),
            kv_f.reshape(b, t, 2, FOX_HEADS, HEAD_DIM), f_logit)


def _kernel_impl(x_prompt, x_sample, cache_nsa_cmp, cache_nsa_sel, state_nsa_win, cache_fox_kv,
                 cache_fox_logf, cache_moba_kv, page_table, rel_bias, ln_g, ln_b, w_in_even, b_in_even,
                 nsa_cmp_pe, nsa_cmp_w1, nsa_cmp_b1, nsa_cmp_w2, nsa_cmp_b2, w_out_even, w_in_odd,
                 w_out_odd, moe_wg, moe_bg, moe_we, moe_be, moe_w1, moe_w3, moe_w2):
    bp, sp, d = x_prompt.shape
    bs, ts, _ = x_sample.shape
    n_pages = page_table.shape[1]
    past = n_pages * PAGE_SIZE
    depth = ln_g.shape[0]
    alpha = (2 * depth) ** 0.25
    np_tok, ns_tok = bp * sp, bs * ts
    tm_p = 512 if np_tok % 512 == 0 else 256
    tm_s = ns_tok
    xp = x_prompt.reshape(np_tok, d)
    xs = x_sample.reshape(ns_tok, d)
    outs = {k: [] for k in ("cmp_p", "cmp_s", "sel_p", "sel_s", "win_p", "win_s", "fkv_p", "fkv_s",
                            "flf_p", "flf_s", "mkv_p", "mkv_s")}
    tq_nsa = 256 if sp % 256 == 0 else sp
    tq_big = 512 if sp % 512 == 0 else tq_nsa

    for layer in range(depth):
        li = layer // 2
        if layer % 2 == 0:
            w_in = w_in_even[li].astype(_BF)
            cmp_w = (nsa_cmp_pe[li], nsa_cmp_w1[li], nsa_cmp_b1[li], nsa_cmp_w2[li], nsa_cmp_b2[li])
            w_out = w_out_even[li].astype(_BF)
            z = _linear(xp, w_in, b_in_even[li], tm_p)
            q_a, kv_c, kv_s, kv_w, g_bm, q_f, kv_f, f_logit = _even_split(z, bp, sp)
            lpad = _round_up(sp, SEL_BLOCK)
            o_c, o_s, o_w = _nsa_attention(q_a, _pad_axis(kv_c, 1, lpad), _pad_axis(kv_s, 1, lpad), kv_w, 0, 0,
                                           tq_nsa, tq_nsa, rel_bias, cmp_w, True)
            fl = f_logit.transpose(0, 2, 1).reshape(bp * FOX_HEADS, sp)
            logf_t, c3 = _logf_cumsum(fl, sp, bp * FOX_HEADS)
            o_f = _fox_attention(q_f, c3, kv_f, c3, 0, tq_big, tq_big, True)
            logf = logf_t.reshape(bp, FOX_HEADS, sp).transpose(0, 2, 1)
            xp = _even_out(alpha, o_c.reshape(np_tok, -1), o_s.reshape(np_tok, -1), o_w.reshape(np_tok, -1),
                           g_bm.reshape(np_tok, -1), o_f.reshape(np_tok, -1), w_out, xp,
                           ln_g[layer, 0], ln_b[layer, 0], tm_p)
            outs["cmp_p"].append(kv_c)
            outs["sel_p"].append(kv_s)
            outs["win_p"].append(kv_w[:, max(sp - NSA_WINDOW, 0):])
            outs["fkv_p"].append(kv_f)
            outs["flf_p"].append(logf)
            z = _linear(xs, w_in, b_in_even[li], tm_s)
            q_a, kv_c, kv_s, kv_w, g_bm, q_f, kv_f, f_logit = _even_split(z, bs, ts)
            wcols = 2 * NSA_KV_HEADS * HEAD_DIM
            pool_c = cache_nsa_cmp[li].reshape(-1, PAGE_SIZE, wcols)
            pool_s = cache_nsa_sel[li].reshape(-1, PAGE_SIZE, wcols)
            kvshape = (bs, past, 2, NSA_KV_HEADS, HEAD_DIM)
            past_c = _gather_pages(pool_c, page_table, 0, 1, _F32).reshape(kvshape)
            past_s = _gather_pages(pool_s, page_table, 0, 1, _F32).reshape(kvshape)
            lpad = _round_up(past + ts, SEL_BLOCK)
            assert past % CMP_STRIDE == 0 and ts <= CMP_STRIDE
            rows_s = _pad_axis(jnp.concatenate([past_s, kv_s], axis=1), 1, lpad)
            win_buf = state_nsa_win[li]
            wb = win_buf.shape[1]
            kvw_all = jnp.concatenate([win_buf, kv_w], axis=1)
            tk_s = _round_up(lpad, LANES)
            o_c, o_s, o_w = _nsa_attention(q_a, past_c, _pad_axis(rows_s, 1, tk_s), kvw_all, past - wb, past,
                                           16, tk_s, rel_bias, cmp_w, False)
            fw = FOX_HEADS * HEAD_DIM
            pool_f = cache_fox_kv[li].reshape(-1, PAGE_SIZE, 2 * fw)
            past_f = jnp.concatenate([_gather_pages(pool_f, page_table, 0, 2, _BF),
                                      _gather_pages(pool_f, page_table, 1, 2, _BF)], axis=-1)
            past_f = past_f.reshape(bs, past, 2, FOX_HEADS, HEAD_DIM)
            pool_l = cache_fox_logf[li].reshape(-1, PAGE_SIZE, FOX_HEADS)
            past_l = _gather_pages(pool_l, page_table, 0, 1, _F32)
            lf_len = _round_up(past + ts, LANES)
            fl_all = jnp.concatenate([past_l, f_logit], axis=1).transpose(0, 2, 1)
            fl_all = jnp.concatenate([jnp.zeros((bs, FOX_HEADS, lf_len - past - ts), _F32), fl_all], axis=-1)
            logf_t, c3 = _logf_cumsum(fl_all.reshape(bs * FOX_HEADS, lf_len), ts, min(bs * FOX_HEADS, 64))
            ck3 = tuple(c[:, lf_len - past - ts:] for c in c3)
            cq3 = tuple(c[:, past:] for c in ck3)
            logf = logf_t[:, lf_len - ts:].reshape(bs, FOX_HEADS, ts).transpose(0, 2, 1)
            kv_f_all = jnp.concatenate([past_f, kv_f.astype(_BF)], axis=1)
            tk_f = _round_up(past + ts, LANES)
            o_f = _fox_attention(q_f, cq3, kv_f_all, ck3, past, 16, tk_f, False)
            xs = _even_out(alpha, o_c.reshape(ns_tok, -1), o_s.reshape(ns_tok, -1), o_w.reshape(ns_tok, -1),
                           g_bm.reshape(ns_tok, -1), o_f.reshape(ns_tok, -1), w_out, xs,
                           ln_g[layer, 0], ln_b[layer, 0], tm_s)
            outs["cmp_s"].append(kv_c)
            outs["sel_s"].append(kv_s)
            outs["win_s"].append(kvw_all[:, ts:])
            outs["fkv_s"].append(kv_f)
            outs["flf_s"].append(logf)
        else:
            w_in = w_in_odd[li].astype(_BF)
            w_out = w_out_odd[li].astype(_BF)
            hw = MOBA_HEADS * HEAD_DIM
            zero_b = jnp.zeros((3 * hw,), _F32)
            z = _linear(xp, w_in, zero_b, tm_p).reshape(bp, sp, 3 * hw)
            q = z[..., :hw].reshape(bp, sp, MOBA_HEADS, HEAD_DIM)
            kv = z[..., hw:].reshape(bp, sp, 2, MOBA_HEADS, HEAD_DIM)
            kv_pad = _pad_axis(kv, 1, _round_up(sp, MOBA_BLOCK))
            kmean = _block_mean(kv_pad[:, :, 0].reshape(bp, -1, hw))
            o = _moba_attention(q, kv_pad, kmean, 0, tq_big, tq_big, rel_bias, True)
            xp = _odd_out(alpha, o.reshape(np_tok, hw), w_out, xp, ln_g[layer, 0], ln_b[layer, 0], tm_p)
            outs["mkv_p"].append(kv)
            z = _linear(xs, w_in, zero_b, tm_s).reshape(bs, ts, 3 * hw)
            q = z[..., :hw].reshape(bs, ts, MOBA_HEADS, HEAD_DIM)
            kv = z[..., hw:].reshape(bs, ts, 2, MOBA_HEADS, HEAD_DIM)
            pool_m = cache_moba_kv[li].reshape(-1, PAGE_SIZE, 2 * hw)
            past_k = _gather_pages(pool_m, page_table, 0, 2, _F32)
            past_v = _gather_pages(pool_m, page_table, 1, 2, _BF)
            assert past % MOBA_BLOCK == 0
            kmean = _block_mean(past_k)
            lpad = _round_up(past + ts, MOBA_BLOCK)
            k_all = _pad_axis(jnp.concatenate([past_k.astype(_BF), kv[:, :, 0].reshape(bs, ts, hw).astype(_BF)], 1), 1, lpad)
            v_all = _pad_axis(jnp.concatenate([past_v, kv[:, :, 1].reshape(bs, ts, hw).astype(_BF)], 1), 1, lpad)
            kv_all = jnp.stack([k_all, v_all], axis=2).reshape(bs, lpad, 2, MOBA_HEADS, HEAD_DIM)
            o = _moba_attention(q, kv_all, kmean, past, 16, lpad, rel_bias, False)
            xs = _odd_out(alpha, o.reshape(ns_tok, hw), w_out, xs, ln_g[layer, 0], ln_b[layer, 0], tm_s)
            outs["mkv_s"].append(kv)
        moe = (moe_wg[layer], moe_bg[layer], moe_we[layer], moe_be[layer], moe_w1[layer].astype(_BF),
               moe_w3[layer].astype(_BF), moe_w2[layer].astype(_BF), ln_g[layer, 1], ln_b[layer, 1])
        xp = _moe_ln(alpha, xp, *moe, tm_p)
        xs = _moe_ln(alpha, xs, *moe, tm_s)

    st = lambda k: jnp.stack(outs[k])
    return (xp.reshape(bp, sp, d), xs.reshape(bs, ts, d), st("cmp_p"), st("cmp_s"), st("sel_p"), st("sel_s"),
            st("win_p"), st("win_s"), st("fkv_p"), st("fkv_s"), st("flf_p"), st("flf_s"), st("mkv_p"), st("mkv_s"))


def kernel(x_prompt, x_sample, cache_nsa_cmp, cache_nsa_sel, state_nsa_win, cache_fox_kv, cache_fox_logf, cache_moba_kv, page_table, rel_bias, ln_g, ln_b, w_in_even, b_in_even, nsa_cmp_pe, nsa_cmp_w1, nsa_cmp_b1, nsa_cmp_w2, nsa_cmp_b2, w_out_even, w_in_odd, w_out_odd, moe_wg, moe_bg, moe_we, moe_be, moe_w1, moe_w3, moe_w2):
    return _kernel_impl(x_prompt, x_sample, cache_nsa_cmp, cache_nsa_sel, state_nsa_win, cache_fox_kv,
                        cache_fox_logf, cache_moba_kv, page_table, rel_bias, ln_g, ln_b, w_in_even, b_in_even,
                        nsa_cmp_pe, nsa_cmp_w1, nsa_cmp_b1, nsa_cmp_w2, nsa_cmp_b2, w_out_even, w_in_odd,
                        w_out_odd, moe_wg, moe_bg, moe_we, moe_be, moe_w1, moe_w3, moe_w2)
```

```python
import functools
import math

import numpy as np
import jax
import jax.numpy as jnp
from jax import lax
from jax.experimental import pallas as pl
from jax.experimental.pallas import tpu as pltpu

_BF = jnp.bfloat16
_F32 = jnp.float32

HEAD_DIM = 64
NSA_KV_HEADS = 2
NSA_GROUP = 4
NSA_HEADS = NSA_KV_HEADS * NSA_GROUP
FOX_HEADS = 8
MOBA_HEADS = 16
CMP_LEN = 32
CMP_STRIDE = 16
CMP_HID = 128
SEL_BLOCK = 64
CMP_PER_SEL = SEL_BLOCK // CMP_STRIDE
NSA_TOPN = 16
NSA_WINDOW = 512
MOBA_BLOCK = 256
MOBA_TOPK = 3
N_BUCKETS = 32
T5_MAX_DISTANCE = 128
N_GROUPS = 4
EXPERTS_PER_GROUP = 4
N_EXPERTS = N_GROUPS * EXPERTS_PER_GROUP
PAGE_SIZE = 128
SCALE = HEAD_DIM ** -0.5
NEG_INF = -1e30
FORCE_SCORE = 1e4
LN_EPS = 1e-5
LANES = 128
SUBLANES = 8
BF16_ROWS = 16
VMEM_LIMIT = 48 * 1024 * 1024
PAGES_PER_STEP = 8

_QA, _KVC, _KVS, _KVW, _QF, _KVF, _GA, _FL = 0, 512, 768, 1024, 1280, 1792, 2816, 2840
_EVEN_OUT = 2848
_EVEN_SIZES = (NSA_HEADS * HEAD_DIM, 2 * NSA_KV_HEADS * HEAD_DIM, 2 * NSA_KV_HEADS * HEAD_DIM,
               2 * NSA_KV_HEADS * HEAD_DIM, 3 * NSA_HEADS, FOX_HEADS * HEAD_DIM,
               2 * FOX_HEADS * HEAD_DIM, FOX_HEADS)
_EVEN_CUTS = tuple(int(c) for c in np.cumsum(_EVEN_SIZES)[:-1])


def _cparams(*sem):
    return pltpu.CompilerParams(dimension_semantics=sem, vmem_limit_bytes=VMEM_LIMIT)


def _round_up(n, m):
    return (n + m - 1) // m * m


def _split3(x):
    hi = x.astype(_BF)
    r1 = x - hi.astype(_F32)
    mid = r1.astype(_BF)
    lo = (r1 - mid.astype(_F32)).astype(_BF)
    return hi, mid, lo


def _dot3(x, m01):
    hi, mid, lo = _split3(x)
    acc = jnp.dot(hi, m01, preferred_element_type=_F32)
    acc += jnp.dot(mid, m01, preferred_element_type=_F32)
    acc += jnp.dot(lo, m01, preferred_element_type=_F32)
    return acc


def _t5_bucket_np(dist):
    n = np.maximum(dist, 0)
    exact = N_BUCKETS // 2
    nf = np.maximum(n, exact).astype(np.float32)
    far = exact + (np.log(nf / np.float32(exact)) / np.float32(math.log(T5_MAX_DISTANCE / exact))
                   * np.float32(N_BUCKETS - exact)).astype(np.int32)
    return np.where(n < exact, n, np.minimum(far, N_BUCKETS - 1)).astype(np.int32)


def _bucket_thresholds():
    d = np.arange(0, 4 * T5_MAX_DISTANCE)
    b = _t5_bucket_np(d)
    return [int(d[b >= k][0]) for k in range(1, N_BUCKETS)]


_FAR_DISTANCE = _bucket_thresholds()[-1]


def _pad_last(x, width):
    return jnp.pad(x, [(0, 0)] * (x.ndim - 1) + [(0, width - x.shape[-1])])


def _pad_axis(x, axis, size):
    pads = [(0, 0)] * x.ndim
    pads[axis] = (0, size - x.shape[axis])
    return jnp.pad(x, pads)


def _toeplitz(g, rows, cols):
    n = g.shape[-1]
    lead = g.shape[:-1]
    x = jnp.broadcast_to(g[..., None, :], lead + (rows, n)).reshape(lead + (rows * n,))
    return x[..., :rows * (n - 1)].reshape(lead + (rows, n - 1))[..., :cols]


def _distance_values(tabh, d, window):
    valid = (d >= 0) if window is None else ((d >= 0) & (d < window))
    if tabh is None:
        vals = jnp.zeros((1, d.shape[0]), _F32)
    else:
        vals = tabh[jnp.asarray(_t5_bucket_np(d))].T
    return jnp.where(jnp.asarray(valid)[None], vals, NEG_INF).astype(_F32)


def _bias_tile(tabh, d0, rows, cols, window=None):
    n = rows + cols
    m = np.arange(n)
    d = np.where(m < cols, d0 - m, d0 + n - m)
    return _toeplitz(_distance_values(tabh, d, window), rows, cols)


def _bias_tile_t(tabh, delta, tk, tq, window=None):
    n = tk + tq
    m = np.arange(n)
    d = np.where(m < tq, delta + m, delta + m - n)
    return _toeplitz(_distance_values(tabh, d, window), tk, tq)


def _plan_tiles(nq, tq, tk, window, has_table):
    deltas, pairs = [], []
    for qi in range(nq):
        q0 = qi * tq
        k_hi = (q0 + tq - 1) // tk
        k_lo = 0 if window is None else max(0, (q0 - (window - 1)) // tk)
        for ki in range(k_lo, k_hi + 1):
            delta = q0 - ki * tk
            dmin, dmax = delta - (tk - 1), delta + tq - 1
            plain = dmin >= (_FAR_DISTANCE if has_table else 0) and (window is None or dmax < window)
            if plain:
                pairs.append((qi, ki, -1))
            else:
                if delta not in deltas:
                    deltas.append(delta)
                pairs.append((qi, ki, deltas.index(delta)))
    pairs = [(q, k, b if b >= 0 else len(deltas)) for q, k, b in pairs]
    return pairs, deltas


def _bias_tiles_t(tabh, deltas, tk, tq, window, group):
    far = tk + tq + _FAR_DISTANCE
    tiles = [_bias_tile_t(tabh, dl, tk, tq, window) for dl in deltas]
    tiles.append(_bias_tile_t(tabh, far, tk, tq, None))
    t = jnp.stack(tiles, axis=1)
    h, nb = t.shape[:2]
    t = t.reshape(h // group, group, nb, tk, tq).transpose(0, 2, 3, 1, 4)
    return t.reshape(h // group, nb, tk, group * tq)


def _layer_norm_cols(y, g, b):
    mu = jnp.mean(y, axis=0, keepdims=True)
    yc = y - mu
    var = jnp.mean(yc * yc, axis=0, keepdims=True)
    return yc * lax.rsqrt(var + LN_EPS) * g + b


def _proj_fm_kernel(row_major_in, x_ref, w_ref, b_ref, sc_ref, zf_ref, zb_ref):
    x = x_ref[0].astype(_BF)
    if row_major_in:
        z = lax.dot_general(w_ref[...], x, (((1,), (1,)), ((), ())), preferred_element_type=_F32)
    else:
        z = jnp.dot(w_ref[...], x, preferred_element_type=_F32)
    z = z + b_ref[...]
    zf_ref[0] = z
    zb_ref[0] = (z * sc_ref[...]).astype(_BF)


def _proj_fm(x, w_t_bf, b_col, scale_col, tm, row_major_in):
    bsz = x.shape[0]
    s = x.shape[1] if row_major_in else x.shape[2]
    d = x.shape[2] if row_major_in else x.shape[1]
    n = w_t_bf.shape[0]
    x_spec = (pl.BlockSpec((1, tm, d), lambda b, i: (b, i, 0)) if row_major_in
              else pl.BlockSpec((1, d, tm), lambda b, i: (b, 0, i)))
    col = pl.BlockSpec((n, 1), lambda b, i: (0, 0))
    out = pl.BlockSpec((1, n, tm), lambda b, i: (b, 0, i))
    return pl.pallas_call(
        functools.partial(_proj_fm_kernel, row_major_in),
        grid=(bsz, s // tm),
        in_specs=[x_spec, pl.BlockSpec((n, d), lambda b, i: (0, 0)), col, col],
        out_specs=[out, out],
        out_shape=[jax.ShapeDtypeStruct((bsz, n, s), _F32), jax.ShapeDtypeStruct((bsz, n, s), _BF)],
        compiler_params=_cparams("parallel", "parallel"), name="proj_fm",
    )(x, w_t_bf, b_col, scale_col)


def _even_out_fm_kernel(alpha, oc_ref, os_ref, ow_ref, gz_ref, of_ref, wo_ref, x_ref, g_ref, b_ref, y_ref):
    wa = NSA_HEADS * HEAD_DIM
    sg = jax.nn.sigmoid(gz_ref[0])
    parts = []
    for h in range(NSA_HEADS):
        sl = slice(h * HEAD_DIM, (h + 1) * HEAD_DIM)
        parts.append(sg[h:h + 1] * oc_ref[0, sl, :] + sg[NSA_HEADS + h:NSA_HEADS + h + 1] * os_ref[0, sl, :]
                     + sg[2 * NSA_HEADS + h:2 * NSA_HEADS + h + 1] * ow_ref[0, sl, :])
    o_a = jnp.concatenate(parts, axis=0)
    m = jnp.dot(wo_ref[:, 0:wa], o_a.astype(_BF), preferred_element_type=_F32)
    m += jnp.dot(wo_ref[:, wa:], of_ref[0].astype(_BF), preferred_element_type=_F32)
    y_ref[0] = _layer_norm_cols(alpha * x_ref[0].T + m, g_ref[...], b_ref[...])


def _even_out_fm(alpha, o_c, o_s, o_w, zf, o_f, w_out_t_bf, x_rows, g, b, tm):
    bsz, s, d = x_rows.shape
    wa = NSA_HEADS * HEAD_DIM
    gate_rows = _EVEN_OUT - _GA
    blk = lambda rows: pl.BlockSpec((1, rows, tm), lambda b_, i: (b_, 0, i))
    col = pl.BlockSpec((d, 1), lambda b_, i: (0, 0))
    return pl.pallas_call(
        functools.partial(_even_out_fm_kernel, alpha),
        grid=(bsz, s // tm),
        in_specs=[blk(wa), blk(wa), blk(wa),
                  pl.BlockSpec((1, gate_rows, tm), lambda b_, i: (b_, _GA // gate_rows, i)),
                  blk(FOX_HEADS * HEAD_DIM),
                  pl.BlockSpec(w_out_t_bf.shape, lambda b_, i: (0, 0)),
                  pl.BlockSpec((1, tm, d), lambda b_, i: (b_, i, 0)), col, col],
        out_specs=blk(d),
        out_shape=jax.ShapeDtypeStruct((bsz, d, s), _F32),
        compiler_params=_cparams("parallel", "parallel"), name="even_out_ln_fm",
    )(o_c, o_s, o_w, zf, o_f, w_out_t_bf, x_rows, g.reshape(d, 1), b.reshape(d, 1))


def _odd_out_fm_kernel(alpha, o_ref, wo_ref, x_ref, g_ref, b_ref, y_ref):
    m = jnp.dot(wo_ref[...], o_ref[0].astype(_BF), preferred_element_type=_F32)
    y_ref[0] = _layer_norm_cols(alpha * x_ref[0] + m, g_ref[...], b_ref[...])


def _odd_out_fm(alpha, o, w_out_t_bf, x_t, g, b, tm):
    bsz, d, s = x_t.shape
    blk = lambda rows: pl.BlockSpec((1, rows, tm), lambda b_, i: (b_, 0, i))
    col = pl.BlockSpec((d, 1), lambda b_, i: (0, 0))
    return pl.pallas_call(
        functools.partial(_odd_out_fm_kernel, alpha),
        grid=(bsz, s // tm),
        in_specs=[blk(o.shape[1]), pl.BlockSpec(w_out_t_bf.shape, lambda b_, i: (0, 0)), blk(d), col, col],
        out_specs=blk(d), out_shape=jax.ShapeDtypeStruct((bsz, d, s), _F32),
        compiler_params=_cparams("parallel", "parallel"), name="odd_out_ln_fm",
    )(o, w_out_t_bf, x_t, g.reshape(d, 1), b.reshape(d, 1))


def _route_gates(logits, axis):
    idx = lax.broadcasted_iota(jnp.int32, logits.shape, axis)
    big = jnp.int32(1 << 20)
    red = lambda f, v: f(v, axis=axis, keepdims=True)
    is_g = idx < N_GROUPS
    lg = jnp.where(is_g, logits, NEG_INF)
    mg = red(jnp.max, lg)
    sg = red(jnp.sum, jnp.where(is_g, jnp.exp(lg - mg), 0.0))
    p_top = 1.0 / sg
    g_top = red(jnp.min, jnp.where(lg == mg, idx, big))
    lo = N_GROUPS + EXPERTS_PER_GROUP * g_top
    in_grp = (idx >= lo) & (idx < lo + EXPERTS_PER_GROUP)
    le = jnp.where(in_grp, logits, NEG_INF)
    me = red(jnp.max, le)
    se = red(jnp.sum, jnp.where(in_grp, jnp.exp(le - me), 0.0))
    i1 = red(jnp.min, jnp.where(le == me, idx, big))
    le2 = jnp.where(idx == i1, NEG_INF, le)
    m2 = red(jnp.max, le2)
    i2 = red(jnp.min, jnp.where(le2 == m2, idx, big))
    w1 = 1.0 / se
    w2 = jnp.exp(m2 - me) / se
    tot = w1 + w2
    return jnp.where(idx == i1, p_top * w1 / tot, jnp.where(idx == i2, p_top * w2 / tot, 0.0))


def _moe_fm_kernel(alpha, rows_out, x_ref, wr_ref, br_ref, w13_ref, w2_ref, g_ref, b_ref, y_ref, gate_ref, acc_ref):
    e = pl.program_id(2)
    x = x_ref[0]
    xb = x.astype(_BF)
    f = w13_ref.shape[1] // 2

    @pl.when(e == 0)
    def _route():
        logits = jnp.dot(wr_ref[...], xb, preferred_element_type=_F32) + br_ref[...]
        gate_ref[...] = _route_gates(logits, 0)
        acc_ref[...] = jnp.zeros_like(acc_ref)

    h13 = jnp.dot(w13_ref[0], xb, preferred_element_type=_F32)
    ge = gate_ref[pl.ds(e + N_GROUPS, 1), :]
    h = (jax.nn.silu(h13[0:f]) * h13[f:2 * f]) * ge
    acc_ref[...] += jnp.dot(w2_ref[0], h.astype(_BF), preferred_element_type=_F32)

    @pl.when(e == N_EXPERTS - 1)
    def _finish():
        y = _layer_norm_cols(alpha * x + acc_ref[...], g_ref[...], b_ref[...])
        y_ref[0] = y.T if rows_out else y


def _moe_ln_fm(alpha, x_t, wg, bg, we, be, w13_t_bf, w2_t_bf, g, b, tm, rows_out):
    bsz, d, s = x_t.shape
    rr = 2 * BF16_ROWS
    wr = jnp.zeros((rr, d), _F32).at[:N_GROUPS].set(wg.T).at[N_GROUPS:N_GROUPS + N_EXPERTS].set(we.T).astype(_BF)
    br = jnp.zeros((rr, 1), _F32).at[:N_GROUPS, 0].set(bg).at[N_GROUPS:N_GROUPS + N_EXPERTS, 0].set(be)
    f2 = w13_t_bf.shape[1]
    col = pl.BlockSpec((d, 1), lambda b_, i, e: (0, 0))
    out_spec = (pl.BlockSpec((1, tm, d), lambda b_, i, e: (b_, i, 0)) if rows_out
                else pl.BlockSpec((1, d, tm), lambda b_, i, e: (b_, 0, i)))
    out_shape = jax.ShapeDtypeStruct((bsz, s, d) if rows_out else (bsz, d, s), _F32)
    return pl.pallas_call(
        functools.partial(_moe_fm_kernel, alpha, rows_out),
        grid=(bsz, s // tm, N_EXPERTS),
        in_specs=[pl.BlockSpec((1, d, tm), lambda b_, i, e: (b_, 0, i)),
                  pl.BlockSpec((rr, d), lambda b_, i, e: (0, 0)),
                  pl.BlockSpec((rr, 1), lambda b_, i, e: (0, 0)),
                  pl.BlockSpec((1, f2, d), lambda b_, i, e: (e, 0, 0)),
                  pl.BlockSpec((1, d, f2 // 2), lambda b_, i, e: (e, 0, 0)),
                  col, col],
        out_specs=out_spec, out_shape=out_shape,
        scratch_shapes=[pltpu.VMEM((rr, tm), _F32), pltpu.VMEM((d, tm), _F32)],
        compiler_params=_cparams("parallel", "parallel", "arbitrary"), name="moe_ln_fm",
    )(x_t, wr, br, w13_t_bf, w2_t_bf, g.reshape(d, 1), b.reshape(d, 1))


def _flash_t_kernel(group, aug, sel_rows, sel_block, qi_ref, ki_ref, bi_ref, first_ref, last_ref, *refs):
    refs = list(refs)
    q_ref, k_ref, v_ref, bias_ref = refs[:4]
    pos = 4
    if aug:
        qaug_ref, kaug_ref = refs[pos:pos + 2]
        pos += 2
    if sel_rows:
        sel_ref = refs[pos]
        pos += 1
    o_ref, m_ref, l_ref, acc_ref = refs[pos:pos + 4]
    step = pl.program_id(1)
    tq = q_ref.shape[2]

    @pl.when(first_ref[step] == 1)
    def _init():
        m_ref[...] = jnp.full_like(m_ref, NEG_INF)
        l_ref[...] = jnp.zeros_like(l_ref)
        acc_ref[...] = jnp.zeros_like(acc_ref)

    k_t = k_ref[0]
    if aug:
        k_t = jnp.concatenate([k_t, kaug_ref[0]], axis=0)
    v_t = v_ref[0]
    if sel_rows:
        off = (ki_ref[step] * sel_rows) % SUBLANES
        sel_tile = jnp.concatenate(
            [jnp.broadcast_to(sel_ref[0, pl.ds(off + j, 1), :], (sel_block, tq)) for j in range(sel_rows)], axis=0)
    for r in range(group):
        rows = slice(r * HEAD_DIM, (r + 1) * HEAD_DIM)
        q_t = q_ref[0, rows, :]
        if aug:
            q_t = jnp.concatenate([q_t, qaug_ref[0]], axis=0)
        s = lax.dot_general(k_t, q_t, (((0,), (0,)), ((), ())), preferred_element_type=_F32)
        s = s + bias_ref[0, 0, :, r * tq:(r + 1) * tq]
        if sel_rows:
            s = s + sel_tile
        m_old = m_ref[r:r + 1, :]
        m_new = jnp.maximum(m_old, jnp.max(s, axis=0, keepdims=True))
        a = jnp.exp(m_old - m_new)
        p = jnp.exp(s - m_new)
        l_ref[r:r + 1, :] = a * l_ref[r:r + 1, :] + jnp.sum(p, axis=0, keepdims=True)
        acc_ref[rows, :] = a * acc_ref[rows, :] + jnp.dot(v_t, p.astype(_BF), preferred_element_type=_F32)
        m_ref[r:r + 1, :] = m_new

    @pl.when(last_ref[step] == 1)
    def _done():
        for r in range(group):
            rows = slice(r * HEAD_DIM, (r + 1) * HEAD_DIM)
            o_ref[0, rows, :] = acc_ref[rows, :] / jnp.maximum(l_ref[r:r + 1, :], 1e-30)


def _flash_t(zb, q_blk, k_blk, v_blk, heads, group, bias, pairs, tq, tk, qaug=None, kaug=None, sel=None, sel_block=0):
    bsz, _, s = zb.shape
    bh = bsz * heads
    hb = bias.shape[0]
    aug = qaug is not None
    sel_rows = 0 if sel is None else tk // sel_block
    pairs = np.asarray(pairs, np.int32)
    qi, ki, bi = pairs[:, 0], pairs[:, 1], pairs[:, 2]
    first = np.concatenate([[1], (qi[1:] != qi[:-1]).astype(np.int32)]).astype(np.int32)
    last = np.concatenate([(qi[1:] != qi[:-1]).astype(np.int32), [1]]).astype(np.int32)
    gq = group * HEAD_DIM
    in_specs = [pl.BlockSpec((1, gq, tq), lambda b, t, qi, ki, bi, f, l: (b // heads, q_blk + b % heads, qi[t])),
                pl.BlockSpec((1, HEAD_DIM, tk), lambda b, t, qi, ki, bi, f, l: (b // heads, k_blk + b % heads, ki[t])),
                pl.BlockSpec((1, HEAD_DIM, tk), lambda b, t, qi, ki, bi, f, l: (b // heads, v_blk + b % heads, ki[t])),
                pl.BlockSpec((1, 1, tk, group * tq), lambda b, t, qi, ki, bi, f, l: (b % hb, bi[t], 0, 0))]
    args = [zb, zb, zb, bias]
    if aug:
        in_specs += [pl.BlockSpec((1, BF16_ROWS, tq), lambda b, t, qi, ki, bi, f, l: (b, 0, qi[t])),
                     pl.BlockSpec((1, BF16_ROWS, tk), lambda b, t, qi, ki, bi, f, l: (b, 0, ki[t]))]
        args += [qaug, kaug]
    if sel_rows:
        assert sel_rows in (1, 2, 4, 8)
        in_specs += [pl.BlockSpec((1, SUBLANES, tq),
                                  lambda b, t, qi, ki, bi, f, l: (b, (ki[t] * sel_rows) // SUBLANES, qi[t]))]
        args += [sel]
    grid_spec = pltpu.PrefetchScalarGridSpec(
        num_scalar_prefetch=5, grid=(bh, len(qi)), in_specs=in_specs,
        out_specs=pl.BlockSpec((1, gq, tq), lambda b, t, qi, ki, bi, f, l: (b, 0, qi[t])),
        scratch_shapes=[pltpu.VMEM((SUBLANES, tq), _F32), pltpu.VMEM((SUBLANES, tq), _F32), pltpu.VMEM((gq, tq), _F32)],
    )
    return pl.pallas_call(
        functools.partial(_flash_t_kernel, group, aug, sel_rows, sel_block), grid_spec=grid_spec,
        out_shape=jax.ShapeDtypeStruct((bh, gq, s), _F32),
        compiler_params=_cparams("parallel", "arbitrary"), name="flash_t",
    )(jnp.asarray(qi), jnp.asarray(ki), jnp.asarray(bi), jnp.asarray(first), jnp.asarray(last), *args)


def _topk_rows(score, n_sel, keep):
    j = lax.broadcasted_iota(jnp.int32, score.shape, 0)
    big = jnp.int32(1 << 20)
    for _ in range(n_sel):
        mx = jnp.max(score, axis=0, keepdims=True)
        firsti = jnp.min(jnp.where(score == mx, j, big), axis=0, keepdims=True)
        hit = j == firsti
        keep = keep | (hit & (mx > 0.5 * NEG_INF))
        score = jnp.where(hit, -3e38, score)
    return keep


def _cmp_select_t_kernel(theta, tq, n_sel, tab_ref, q_ref, kc_ref, vc_ref, msel_ref, o_ref, sel_ref):
    g = pl.program_id(0) % NSA_KV_HEADS
    i = pl.program_id(1)
    ncp = kc_ref.shape[1]
    nsl = sel_ref.shape[1]
    n = lax.broadcasted_iota(jnp.int32, (ncp, tq), 0)
    t = i * tq + lax.broadcasted_iota(jnp.int32, (ncp, tq), 1)
    d = t - (n * CMP_STRIDE + (CMP_LEN - 1))
    ok = d >= 0
    ind = [d >= th for th in theta]
    kc = kc_ref[0]
    vc_t = vc_ref[0]
    imp = jnp.zeros((ncp, tq), _F32)
    for r in range(NSA_GROUP):
        base = (g * NSA_GROUP + r) * N_BUCKETS
        bias = jnp.full((ncp, tq), tab_ref[base], _F32)
        for k in range(1, N_BUCKETS):
            bias = bias + jnp.where(ind[k - 1], tab_ref[base + k] - tab_ref[base + k - 1], 0.0)
        rows = slice(r * HEAD_DIM, (r + 1) * HEAD_DIM)
        s = jnp.dot(kc, q_ref[0, rows, :], preferred_element_type=_F32) + bias
        s = jnp.where(ok, s, NEG_INF)
        m = jnp.max(s, axis=0, keepdims=True)
        e = jnp.where(ok, jnp.exp(s - m), 0.0)
        p = e / jnp.maximum(jnp.sum(e, axis=0, keepdims=True), 1e-30)
        o_ref[0, rows, :] = jnp.dot(vc_t, p.astype(_BF), preferred_element_type=_F32)
        imp = imp + p
    hi, mid, lo = _split3(imp)
    msel = msel_ref[...]
    p_s = (jnp.dot(msel, hi, preferred_element_type=_F32) + jnp.dot(msel, mid, preferred_element_type=_F32)
           + jnp.dot(msel, lo, preferred_element_type=_F32))
    j = lax.broadcasted_iota(jnp.int32, (nsl, tq), 0)
    qb = (i * tq + lax.broadcasted_iota(jnp.int32, (nsl, tq), 1)) >> int(math.log2(SEL_BLOCK))
    valid = j <= qb
    forced = (j == 0) | (j == qb) | (j == qb - 1)
    score = jnp.where(valid, jnp.where(forced, FORCE_SCORE, p_s), NEG_INF)
    chosen = _topk_rows(score, n_sel, jnp.zeros((nsl, tq), jnp.bool_))
    sel_ref[0] = jnp.where(chosen, 0.0, NEG_INF)


def _cmp_select_t(zb, kc, vc_t, tab, tq, n_blocks):
    bsz, _, s = zb.shape
    bg = bsz * NSA_KV_HEADS
    ncp = kc.shape[1]
    nsl = _round_up(n_blocks, LANES)
    gq = NSA_GROUP * HEAD_DIM
    jj = np.arange(nsl)[:, None]
    nn = np.arange(ncp)[None, :]
    msel = ((nn >= CMP_PER_SEL * jj - 1) & (nn <= CMP_PER_SEL * jj + CMP_PER_SEL - 1) & (jj < n_blocks))
    msel = jnp.asarray(msel.astype(np.float32), _BF)
    tabf = tab[:, :NSA_HEADS].T.reshape(-1)
    kern = functools.partial(_cmp_select_t_kernel, _bucket_thresholds(), tq, min(NSA_TOPN, n_blocks))
    return pl.pallas_call(
        kern,
        grid=(bg, s // tq),
        in_specs=[pl.BlockSpec(memory_space=pltpu.SMEM),
                  pl.BlockSpec((1, gq, tq), lambda b, i: (b // NSA_KV_HEADS, b % NSA_KV_HEADS, i)),
                  pl.BlockSpec((1, ncp, HEAD_DIM), lambda b, i: (b, 0, 0)),
                  pl.BlockSpec((1, HEAD_DIM, ncp), lambda b, i: (b, 0, 0)),
                  pl.BlockSpec((nsl, ncp), lambda b, i: (0, 0))],
        out_specs=[pl.BlockSpec((1, gq, tq), lambda b, i: (b, 0, i)),
                   pl.BlockSpec((1, nsl, tq), lambda b, i: (b, 0, i))],
        out_shape=[jax.ShapeDtypeStruct((bg, gq, s), _F32), jax.ShapeDtypeStruct((bg, nsl, s), _F32)],
        compiler_params=_cparams("parallel", "parallel"), name="nsa_cmp_select_t",
    )(tabf, zb, kc, vc_t, msel)


def _moba_select_t_kernel(tq, n_blocks, kf_ref, avg_ref, q_ref, sel_ref, km_ref):
    i = pl.program_id(1)

    @pl.when(i == 0)
    def _means():
        km_ref[...] = _dot3(kf_ref[0], avg_ref[...])

    gate = lax.dot_general(km_ref[...].astype(_BF), q_ref[0], (((0,), (0,)), ((), ())),
                           preferred_element_type=_F32)
    j = lax.broadcasted_iota(jnp.int32, gate.shape, 0)
    qb = (i * tq + lax.broadcasted_iota(jnp.int32, gate.shape, 1)) >> int(math.log2(MOBA_BLOCK))
    score = jnp.where(j < qb, gate, NEG_INF)
    chosen = _topk_rows(score, min(MOBA_TOPK, n_blocks), j == qb)
    sel_ref[0] = jnp.where(chosen, 0.0, NEG_INF)[0:sel_ref.shape[1]]


def _moba_select_t(zf, zb, k_blk, tq):
    bsz, _, s = zb.shape
    n_blocks = s // MOBA_BLOCK
    assert n_blocks <= LANES
    rows = _round_up(n_blocks, SUBLANES)
    avg = (np.arange(s)[:, None] // MOBA_BLOCK == np.arange(LANES)[None, :]).astype(np.float32) / MOBA_BLOCK
    bh = bsz * MOBA_HEADS
    return pl.pallas_call(
        functools.partial(_moba_select_t_kernel, tq, n_blocks),
        grid=(bh, s // tq),
        in_specs=[pl.BlockSpec((1, HEAD_DIM, s), lambda b, i: (b // MOBA_HEADS, k_blk + b % MOBA_HEADS, 0)),
                  pl.BlockSpec((s, LANES), lambda b, i: (0, 0)),
                  pl.BlockSpec((1, HEAD_DIM, tq), lambda b, i: (b // MOBA_HEADS, b % MOBA_HEADS, i))],
        out_specs=pl.BlockSpec((1, rows, tq), lambda b, i: (b, 0, i)),
        out_shape=jax.ShapeDtypeStruct((bh, rows, s), _F32),
        scratch_shapes=[pltpu.VMEM((HEAD_DIM, LANES), _F32)],
        compiler_params=_cparams("parallel", "arbitrary"), name="moba_select_t",
    )(zf, jnp.asarray(avg, _BF), zb)


def _linear_kernel(x_ref, w_ref, b_ref, o_ref):
    o_ref[...] = jnp.dot(x_ref[...].astype(_BF), w_ref[...], preferred_element_type=_F32) + b_ref[...]


def _linear(x, w_bf, b, tm):
    m, k = x.shape
    n = w_bf.shape[1]
    return pl.pallas_call(
        _linear_kernel,
        grid=(m // tm,),
        in_specs=[pl.BlockSpec((tm, k), lambda i: (i, 0)),
                  pl.BlockSpec((k, n), lambda i: (0, 0)),
                  pl.BlockSpec((1, n), lambda i: (0, 0))],
        out_specs=pl.BlockSpec((tm, n), lambda i: (i, 0)),
        out_shape=jax.ShapeDtypeStruct((m, n), _F32),
        compiler_params=_cparams("parallel"),
        name="linear",
    )(x, w_bf, b.reshape(1, n))


def _layer_norm_rows(y, g, b):
    mu = jnp.mean(y, axis=-1, keepdims=True)
    yc = y - mu
    var = jnp.mean(yc * yc, axis=-1, keepdims=True)
    return yc * lax.rsqrt(var + LN_EPS) * g + b


def _even_out_kernel(alpha, oc_ref, os_ref, ow_ref, gl_ref, of_ref, ex_ref, wa_ref, wf_ref, x_ref,
                     g_ref, b_ref, y_ref):
    wa = NSA_HEADS * HEAD_DIM
    gexp = _dot3(jax.nn.sigmoid(gl_ref[...]), ex_ref[...])
    o_a = (gexp[:, 0:wa] * oc_ref[...] + gexp[:, wa:2 * wa] * os_ref[...] + gexp[:, 2 * wa:3 * wa] * ow_ref[...])
    m = jnp.dot(o_a.astype(_BF), wa_ref[...], preferred_element_type=_F32)
    m += jnp.dot(of_ref[...].astype(_BF), wf_ref[...], preferred_element_type=_F32)
    y_ref[...] = _layer_norm_rows(alpha * x_ref[...] + m, g_ref[...], b_ref[...])


def _even_out(alpha, o_c, o_s, o_w, gl, o_f, w_out_bf, x, g, b, tm):
    n, d = x.shape
    wa = NSA_HEADS * HEAD_DIM
    wf = FOX_HEADS * HEAD_DIM
    ex_np = np.zeros((LANES, 3 * wa), np.float32)
    for j in range(3):
        for h in range(NSA_HEADS):
            ex_np[j * NSA_HEADS + h, j * wa + h * HEAD_DIM:j * wa + (h + 1) * HEAD_DIM] = 1.0
    ex = jnp.asarray(ex_np, _BF)
    gl = _pad_last(gl, LANES)
    row = lambda w: pl.BlockSpec((tm, w), lambda i: (i, 0))
    full = lambda a: pl.BlockSpec(a.shape, lambda i: (0,) * a.ndim)
    args = (o_c, o_s, o_w, gl, o_f, ex, w_out_bf[:wa], w_out_bf[wa:], x, g.reshape(1, d), b.reshape(1, d))
    specs = [row(wa), row(wa), row(wa), row(LANES), row(wf), full(ex), full(args[6]), full(args[7]),
             row(d), full(args[9]), full(args[10])]
    return pl.pallas_call(
        functools.partial(_even_out_kernel, alpha),
        grid=(n // tm,), in_specs=specs, out_specs=row(d),
        out_shape=jax.ShapeDtypeStruct((n, d), _F32),
        compiler_params=_cparams("parallel"), name="even_out_ln",
    )(*args)


def _odd_out_kernel(alpha, o_ref, w_ref, x_ref, g_ref, b_ref, y_ref):
    m = jnp.dot(o_ref[...].astype(_BF), w_ref[...], preferred_element_type=_F32)
    y_ref[...] = _layer_norm_rows(alpha * x_ref[...] + m, g_ref[...], b_ref[...])


def _odd_out(alpha, o, w_out_bf, x, g, b, tm):
    n, d = x.shape
    row = lambda w: pl.BlockSpec((tm, w), lambda i: (i, 0))
    full = lambda shp: pl.BlockSpec(shp, lambda i: (0,) * len(shp))
    return pl.pallas_call(
        functools.partial(_odd_out_kernel, alpha),
        grid=(n // tm,),
        in_specs=[row(o.shape[1]), full(w_out_bf.shape), row(d), full((1, d)), full((1, d))],
        out_specs=row(d), out_shape=jax.ShapeDtypeStruct((n, d), _F32),
        compiler_params=_cparams("parallel"), name="odd_out_ln",
    )(o, w_out_bf, x, g.reshape(1, d), b.reshape(1, d))


def _moe_kernel(alpha, x_ref, wr_ref, br_ref, w1_ref, w3_ref, w2_ref, g_ref, b_ref, y_ref,
                gate_ref, acc_ref):
    e = pl.program_id(1)
    x = x_ref[...]

    @pl.when(e == 0)
    def _route():
        logits = jnp.dot(x.astype(_BF), wr_ref[...], preferred_element_type=_F32) + br_ref[...]
        gate_ref[...] = _route_gates(logits, 1)
        acc_ref[...] = jnp.zeros_like(acc_ref)

    xb = x.astype(_BF)
    h1 = jnp.dot(xb, w1_ref[0], preferred_element_type=_F32)
    h3 = jnp.dot(xb, w3_ref[0], preferred_element_type=_F32)
    gate = gate_ref[...]
    lane = lax.broadcasted_iota(jnp.int32, gate.shape, 1)
    ge = jnp.sum(jnp.where(lane == e + N_GROUPS, gate, 0.0), axis=-1, keepdims=True)
    h = (jax.nn.silu(h1) * h3) * ge
    acc_ref[...] += jnp.dot(h.astype(_BF), w2_ref[0], preferred_element_type=_F32)

    @pl.when(e == N_EXPERTS - 1)
    def _finish():
        y_ref[...] = _layer_norm_rows(alpha * x + acc_ref[...], g_ref[...], b_ref[...])


def _moe_ln(alpha, x, wg, bg, we, be, w1_bf, w3_bf, w2_bf, g, b, tm):
    n, d = x.shape
    f = w1_bf.shape[2]
    wr = jnp.zeros((d, LANES), _F32).at[:, :N_GROUPS].set(wg).at[:, N_GROUPS:N_GROUPS + N_EXPERTS].set(we)
    wr = wr.astype(_BF)
    br = jnp.zeros((1, LANES), _F32).at[0, :N_GROUPS].set(bg).at[0, N_GROUPS:N_GROUPS + N_EXPERTS].set(be)
    return pl.pallas_call(
        functools.partial(_moe_kernel, alpha),
        grid=(n // tm, N_EXPERTS),
        in_specs=[pl.BlockSpec((tm, d), lambda i, e: (i, 0)),
                  pl.BlockSpec((d, LANES), lambda i, e: (0, 0)),
                  pl.BlockSpec((1, LANES), lambda i, e: (0, 0)),
                  pl.BlockSpec((1, d, f), lambda i, e: (e, 0, 0)),
                  pl.BlockSpec((1, d, f), lambda i, e: (e, 0, 0)),
                  pl.BlockSpec((1, f, d), lambda i, e: (e, 0, 0)),
                  pl.BlockSpec((1, d), lambda i, e: (0, 0)),
                  pl.BlockSpec((1, d), lambda i, e: (0, 0))],
        out_specs=pl.BlockSpec((tm, d), lambda i, e: (i, 0)),
        out_shape=jax.ShapeDtypeStruct((n, d), _F32),
        scratch_shapes=[pltpu.VMEM((tm, LANES), _F32), pltpu.VMEM((tm, d), _F32)],
        compiler_params=_cparams("parallel", "arbitrary"), name="moe_ln",
    )(x, wr, br, w1_bf, w3_bf, w2_bf, g.reshape(1, d), b.reshape(1, d))


def _flash_kernel(shared, qi_ref, ki_ref, bi_ref, first_ref, last_ref, *refs):
    if shared:
        qa_ref, ka_ref, v_ref, bias_ref, qb_ref, kb_ref, o_ref, m_ref, l_ref, acc_ref = refs
    else:
        qa_ref, ka_ref, v_ref, bias_ref, o_ref, m_ref, l_ref, acc_ref = refs
    step = pl.program_id(1)

    @pl.when(first_ref[step] == 1)
    def _init():
        m_ref[...] = jnp.full_like(m_ref, NEG_INF)
        l_ref[...] = jnp.zeros_like(l_ref)
        acc_ref[...] = jnp.zeros_like(acc_ref)

    nt = (((1,), (1,)), ((), ()))
    s = lax.dot_general(qa_ref[0], ka_ref[0], nt, preferred_element_type=_F32)
    if shared:
        s = s + lax.dot_general(qb_ref[0], kb_ref[...], nt, preferred_element_type=_F32)
    s = s + bias_ref[0, 0]
    m_old = m_ref[...]
    m_new = jnp.maximum(m_old, jnp.max(s, axis=-1, keepdims=True))
    alpha = jnp.exp(m_old - m_new)
    p = jnp.exp(s - m_new)
    l_ref[...] = alpha * l_ref[...] + jnp.sum(p, axis=-1, keepdims=True)
    acc_ref[...] = alpha * acc_ref[...] + jnp.dot(p.astype(_BF), v_ref[0], preferred_element_type=_F32)
    m_ref[...] = m_new

    @pl.when(last_ref[step] == 1)
    def _done():
        o_ref[0] = acc_ref[...] / jnp.maximum(l_ref[...], 1e-30)


def _flash(qa, ka, v, bias, pairs, tqr, tk, qb=None, kb=None):
    bh, rq, kdim = qa.shape
    dv = v.shape[2]
    hb = bias.shape[0]
    shared = qb is not None
    pairs = np.asarray(pairs, np.int32)
    qi, ki, bi = pairs[:, 0], pairs[:, 1], pairs[:, 2]
    first = np.concatenate([[1], (qi[1:] != qi[:-1]).astype(np.int32)]).astype(np.int32)
    last = np.concatenate([(qi[1:] != qi[:-1]).astype(np.int32), [1]]).astype(np.int32)
    in_specs = [pl.BlockSpec((1, tqr, kdim), lambda b, s, qi, ki, bi, f, l: (b, qi[s], 0)),
                pl.BlockSpec((1, tk, kdim), lambda b, s, qi, ki, bi, f, l: (b, ki[s], 0)),
                pl.BlockSpec((1, tk, dv), lambda b, s, qi, ki, bi, f, l: (b, ki[s], 0)),
                pl.BlockSpec((1, 1, tqr, tk), lambda b, s, qi, ki, bi, f, l: (b % hb, bi[s], 0, 0))]
    args = [qa, ka, v, bias]
    if shared:
        kbd = qb.shape[2]
        in_specs += [pl.BlockSpec((1, tqr, kbd), lambda b, s, qi, ki, bi, f, l: (b, qi[s], 0)),
                     pl.BlockSpec((tk, kbd), lambda b, s, qi, ki, bi, f, l: (ki[s], 0))]
        args += [qb, kb]
    grid_spec = pltpu.PrefetchScalarGridSpec(
        num_scalar_prefetch=5,
        grid=(bh, len(qi)),
        in_specs=in_specs,
        out_specs=pl.BlockSpec((1, tqr, dv), lambda b, s, qi, ki, bi, f, l: (b, qi[s], 0)),
        scratch_shapes=[pltpu.VMEM((tqr, 1), _F32), pltpu.VMEM((tqr, 1), _F32), pltpu.VMEM((tqr, dv), _F32)],
    )
    return pl.pallas_call(
        functools.partial(_flash_kernel, shared), grid_spec=grid_spec,
        out_shape=jax.ShapeDtypeStruct((bh, rq, dv), _F32),
        compiler_params=_cparams("parallel", "arbitrary"), name="flash",
    )(jnp.asarray(qi), jnp.asarray(ki), jnp.asarray(bi), jnp.asarray(first), jnp.asarray(last), *args)


def _compress_kernel(x_ref, pe_ref, w1a_ref, w1b_ref, b1_ref, w2_ref, b2_ref, o_ref):
    x = x_ref[0, 0]
    pe = pe_ref[0]
    hf = jnp.dot((x + pe[0:1]).astype(_BF), w1a_ref[0], preferred_element_type=_F32)
    hs = jnp.dot((x + pe[1:2]).astype(_BF), w1b_ref[0], preferred_element_type=_F32)
    hs_next = pltpu.roll(hs, hs.shape[0] - 1, 0)
    h = jax.nn.gelu(hf + hs_next + b1_ref[0], approximate=True)
    o_ref[0, 0] = jnp.dot(h.astype(_BF), w2_ref[0], preferred_element_type=_F32) + b2_ref[0]


def _nsa_compress(x, pe, w1, b1, w2, b2):
    sg, bsz, nch, flat = x.shape
    pe2 = pe.reshape(2, CMP_STRIDE, 2, HEAD_DIM).transpose(2, 0, 1, 3).reshape(2, 2, flat)
    w1r = w1.reshape(2, CMP_STRIDE, 2, HEAD_DIM, CMP_HID).transpose(2, 0, 1, 3, 4).reshape(2, 2, flat, CMP_HID)
    w1r = w1r.astype(_BF)
    smap = lambda s, b: (s // NSA_KV_HEADS, 0, 0)
    return pl.pallas_call(
        _compress_kernel,
        grid=(sg, bsz),
        in_specs=[pl.BlockSpec((1, 1, nch, flat), lambda s, b: (s, b, 0, 0)),
                  pl.BlockSpec((1, 2, flat), smap),
                  pl.BlockSpec((1, flat, CMP_HID), smap),
                  pl.BlockSpec((1, flat, CMP_HID), smap),
                  pl.BlockSpec((1, 1, CMP_HID), smap),
                  pl.BlockSpec((1, CMP_HID, HEAD_DIM), smap),
                  pl.BlockSpec((1, 1, HEAD_DIM), smap)],
        out_specs=pl.BlockSpec((1, 1, nch, HEAD_DIM), lambda s, b: (s, b, 0, 0)),
        out_shape=jax.ShapeDtypeStruct((sg, bsz, nch, HEAD_DIM), _F32),
        compiler_params=_cparams("parallel", "parallel"), name="nsa_compress",
    )(x, pe2, w1r[:, 0], w1r[:, 1], b1.reshape(2, 1, CMP_HID), w2.astype(_BF), b2.reshape(2, 1, HEAD_DIM))


def _chunks_from_rows(kv_c):
    bsz, length = kv_c.shape[:2]
    nch = length // CMP_STRIDE
    sg = 2 * NSA_KV_HEADS
    x = kv_c.reshape(bsz, nch, CMP_STRIDE, sg, HEAD_DIM).transpose(3, 0, 1, 2, 4)
    return x.reshape(sg, bsz, nch, CMP_STRIDE * HEAD_DIM)


def _chunks_from_fm(kv_t):
    bsz, _, length = kv_t.shape
    nch = length // CMP_STRIDE
    sg = 2 * NSA_KV_HEADS
    x = kv_t.reshape(bsz, sg, HEAD_DIM, nch, CMP_STRIDE).transpose(1, 0, 3, 4, 2)
    return x.reshape(sg, bsz, nch, CMP_STRIDE * HEAD_DIM)


def _cmp_select_kernel(theta, tq, qpos0, n_sel, tab_ref, qa_ref, kc_ref, vc_ref, msel_ref, o_ref, selb_ref):
    g = pl.program_id(0) % NSA_KV_HEADS
    i = pl.program_id(1)
    ncp = kc_ref.shape[1]
    nsl = selb_ref.shape[2]
    t = qpos0 + i * tq + lax.broadcasted_iota(jnp.int32, (tq, ncp), 0)
    n = lax.broadcasted_iota(jnp.int32, (tq, ncp), 1)
    d = t - (n * CMP_STRIDE + (CMP_LEN - 1))
    ok = d >= 0
    ind = [d >= th for th in theta]
    kc = kc_ref[0]
    vc = vc_ref[0]
    imp = jnp.zeros((tq, ncp), _F32)
    for r in range(NSA_GROUP):
        base = (g * NSA_GROUP + r) * N_BUCKETS
        bias = jnp.full((tq, ncp), tab_ref[base], _F32)
        for k in range(1, N_BUCKETS):
            bias = bias + jnp.where(ind[k - 1], tab_ref[base + k] - tab_ref[base + k - 1], 0.0)
        s = lax.dot_general(qa_ref[0, r * tq:(r + 1) * tq, :], kc, (((1,), (1,)), ((), ())),
                            preferred_element_type=_F32) + bias
        s = jnp.where(ok, s, NEG_INF)
        m = jnp.max(s, axis=-1, keepdims=True)
        e = jnp.where(ok, jnp.exp(s - m), 0.0)
        p = e / jnp.maximum(jnp.sum(e, axis=-1, keepdims=True), 1e-30)
        o_ref[0, r * tq:(r + 1) * tq, :] = jnp.dot(p.astype(_BF), vc, preferred_element_type=_F32)
        imp = imp + p
    p_s = _dot3(imp, msel_ref[...])
    j = lax.broadcasted_iota(jnp.int32, (tq, nsl), 1)
    qb = (qpos0 + i * tq + lax.broadcasted_iota(jnp.int32, (tq, nsl), 0)) >> int(math.log2(SEL_BLOCK))
    valid = j <= qb
    forced = (j == 0) | (j == qb) | (j == qb - 1)
    score = jnp.where(valid, jnp.where(forced, FORCE_SCORE, p_s), NEG_INF)
    sel = jnp.zeros((tq, nsl), jnp.bool_)
    big = jnp.int32(1 << 20)
    for _ in range(n_sel):
        mx = jnp.max(score, axis=-1, keepdims=True)
        firsti = jnp.min(jnp.where(score == mx, j, big), axis=-1, keepdims=True)
        hit = j == firsti
        sel = sel | (hit & (mx > 0.5 * NEG_INF))
        score = jnp.where(hit, -3e38, score)
    selb_ref[0] = jnp.where(sel, 0.0, NEG_INF).astype(_BF)


def _cmp_select(qa, kc, vc, tab, tq, qpos0, n_blocks):
    bg, rows, _ = qa.shape
    ncp = kc.shape[1]
    nq = rows // (NSA_GROUP * tq)
    nsl = _round_up(n_blocks, LANES)
    nn = np.arange(ncp)[:, None]
    jj = np.arange(nsl)[None, :]
    msel = ((nn >= CMP_PER_SEL * jj - 1) & (nn <= CMP_PER_SEL * jj + CMP_PER_SEL - 1) & (jj < n_blocks))
    msel = jnp.asarray(msel.astype(np.float32), _BF)
    tabf = tab[:, :NSA_HEADS].T.reshape(-1)
    kern = functools.partial(_cmp_select_kernel, _bucket_thresholds(), tq, qpos0, min(NSA_TOPN, n_blocks))
    return pl.pallas_call(
        kern,
        grid=(bg, nq),
        in_specs=[pl.BlockSpec(memory_space=pltpu.SMEM),
                  pl.BlockSpec((1, NSA_GROUP * tq, HEAD_DIM), lambda b, i: (b, i, 0)),
                  pl.BlockSpec((1, ncp, HEAD_DIM), lambda b, i: (b, 0, 0)),
                  pl.BlockSpec((1, ncp, HEAD_DIM), lambda b, i: (b, 0, 0)),
                  pl.BlockSpec((ncp, nsl), lambda b, i: (0, 0))],
        out_specs=[pl.BlockSpec((1, NSA_GROUP * tq, HEAD_DIM), lambda b, i: (b, i, 0)),
                   pl.BlockSpec((1, tq, nsl), lambda b, i: (b, i, 0))],
        out_shape=[jax.ShapeDtypeStruct((bg, rows, HEAD_DIM), _F32),
                   jax.ShapeDtypeStruct((bg, nq * tq, nsl), _BF)],
        compiler_params=_cparams("parallel", "parallel"), name="nsa_cmp_select",
    )(tabf, qa, kc, vc, msel)


def _moba_select_kernel(tq, qpos0, n_blocks, q_ref, km_ref, selb_ref):
    i = pl.program_id(1)
    nbl = selb_ref.shape[2]
    gate = lax.dot_general(q_ref[0], km_ref[0], (((1,), (1,)), ((), ())), preferred_element_type=_F32)
    j = lax.broadcasted_iota(jnp.int32, (tq, nbl), 1)
    qb = (qpos0 + i * tq + lax.broadcasted_iota(jnp.int32, (tq, nbl), 0)) >> int(math.log2(MOBA_BLOCK))
    score = jnp.where(j < qb, gate, NEG_INF)
    sel = j == qb
    big = jnp.int32(1 << 20)
    for _ in range(min(MOBA_TOPK, n_blocks)):
        mx = jnp.max(score, axis=-1, keepdims=True)
        firsti = jnp.min(jnp.where(score == mx, j, big), axis=-1, keepdims=True)
        hit = j == firsti
        sel = sel | (hit & (mx > 0.5 * NEG_INF))
        score = jnp.where(hit, -3e38, score)
    selb_ref[0] = jnp.where(sel, 0.0, NEG_INF).astype(_BF)


def _moba_select(q_bf, kmean_bf, tq, qpos0, n_blocks):
    bh, tqs, _ = q_bf.shape
    nbl = kmean_bf.shape[1]
    return pl.pallas_call(
        functools.partial(_moba_select_kernel, tq, qpos0, n_blocks),
        grid=(bh, tqs // tq),
        in_specs=[pl.BlockSpec((1, tq, HEAD_DIM), lambda b, i: (b, i, 0)),
                  pl.BlockSpec((1, nbl, HEAD_DIM), lambda b, i: (b, 0, 0))],
        out_specs=pl.BlockSpec((1, tq, nbl), lambda b, i: (b, i, 0)),
        out_shape=jax.ShapeDtypeStruct((bh, tqs, nbl), _BF),
        compiler_params=_cparams("parallel", "parallel"), name="moba_select",
    )(q_bf, kmean_bf)


def _block_mean_kernel(k_ref, o_ref):
    o_ref[0, 0] = jnp.sum(k_ref[0], axis=0, keepdims=True) * (1.0 / MOBA_BLOCK)


def _block_mean(k):
    bsz, length, w = k.shape
    nb = length // MOBA_BLOCK
    out = pl.pallas_call(
        _block_mean_kernel, grid=(bsz, nb),
        in_specs=[pl.BlockSpec((1, MOBA_BLOCK, w), lambda b, j: (b, j, 0))],
        out_specs=pl.BlockSpec((1, 1, 1, w), lambda b, j: (b, j, 0, 0)),
        out_shape=jax.ShapeDtypeStruct((bsz, nb, 1, w), _F32),
        compiler_params=_cparams("parallel", "parallel"), name="moba_block_mean",
    )(k)
    return out.reshape(bsz, nb, w)


def _logf_cumsum_kernel(n_new, x_ref, u_ref, lf_ref, hi_ref, mid_ref, lo_ref):
    length = x_ref.shape[1]
    x = x_ref[...]
    col = lax.broadcasted_iota(jnp.int32, x.shape, 1)
    ls = jnp.minimum(x, 0.0) - jnp.log1p(jnp.exp(-jnp.abs(x)))
    lf = jnp.where(col >= length - n_new, ls, x)
    lf_ref[...] = lf
    u = u_ref[...]
    carry = jnp.zeros((x.shape[0], 1), _F32)
    for k in range(length // LANES):
        blk = _dot3(lf[:, k * LANES:(k + 1) * LANES], u) + carry
        hi, mid, lo = _split3(blk)
        hi_ref[:, k * LANES:(k + 1) * LANES] = hi
        mid_ref[:, k * LANES:(k + 1) * LANES] = mid
        lo_ref[:, k * LANES:(k + 1) * LANES] = lo
        carry = blk[:, LANES - 1:LANES]


def _logf_cumsum(x, n_new, rows_per_step):
    rows, length = x.shape
    u = jnp.asarray(np.triu(np.ones((LANES, LANES), np.float32)), _BF)
    spec = pl.BlockSpec((rows_per_step, length), lambda i: (i, 0))
    lf, hi, mid, lo = pl.pallas_call(
        functools.partial(_logf_cumsum_kernel, n_new),
        grid=(rows // rows_per_step,),
        in_specs=[spec, pl.BlockSpec((LANES, LANES), lambda i: (0, 0))],
        out_specs=[spec] * 4,
        out_shape=[jax.ShapeDtypeStruct((rows, length), _F32)] + [jax.ShapeDtypeStruct((rows, length), _BF)] * 3,
        compiler_params=_cparams("parallel"), name="fox_logf_cumsum",
    )(x, u)
    return lf, (hi, mid, lo)


def _fox_aug_rows(c3):
    one = jnp.ones_like(c3[0])
    zero = jnp.zeros_like(c3[0])
    pad = [zero] * (BF16_ROWS - 6)
    qaug = jnp.stack(list(c3) + [one, one, one] + pad, axis=1)
    kaug = jnp.stack([one, one, one] + [-c for c in c3] + pad, axis=1)
    return qaug, kaug


def _gather_kernel(n_pg, pt_ref, *refs):
    ins, out = refs[:n_pg], refs[n_pg]
    for k in range(n_pg):
        out[0, k * PAGE_SIZE:(k + 1) * PAGE_SIZE, :] = ins[k][0].astype(out.dtype)


def _gather_pages(pool, page_table, col_block, n_col_blocks, out_dtype):
    bsz, n_pages = page_table.shape
    w = pool.shape[2] // n_col_blocks
    n_pg = PAGES_PER_STEP if n_pages % PAGES_PER_STEP == 0 else 1

    def in_map(k):
        return lambda b, p, pt: (pt[b * n_pages + p * n_pg + k], 0, col_block)

    grid_spec = pltpu.PrefetchScalarGridSpec(
        num_scalar_prefetch=1,
        grid=(bsz, n_pages // n_pg),
        in_specs=[pl.BlockSpec((1, PAGE_SIZE, w), in_map(k)) for k in range(n_pg)],
        out_specs=pl.BlockSpec((1, n_pg * PAGE_SIZE, w), lambda b, p, pt: (b, p, 0)),
    )
    return pl.pallas_call(
        functools.partial(_gather_kernel, n_pg), grid_spec=grid_spec,
        out_shape=jax.ShapeDtypeStruct((bsz, n_pages * PAGE_SIZE, w), out_dtype),
        compiler_params=_cparams("parallel", "arbitrary"), name="page_gather",
    )(page_table.reshape(-1), *([pool] * n_pg))


def _group_rows(b):
    h, r, c = b.shape
    return b.reshape(h // NSA_GROUP, NSA_GROUP * r, c)


def _nsa_q_rows(q, tq):
    b, t, _, dh = q.shape
    nq = t // tq
    x = q.reshape(b, nq, tq, NSA_KV_HEADS, NSA_GROUP, dh).transpose(0, 3, 1, 4, 2, 5)
    return x.reshape(b * NSA_KV_HEADS, nq * NSA_GROUP * tq, dh)


def _nsa_rows_back(o, b, t, tq):
    nq = t // tq
    x = o.reshape(b, NSA_KV_HEADS, nq, NSA_GROUP, tq, HEAD_DIM).transpose(0, 2, 4, 1, 3, 5)
    return x.reshape(b, t, NSA_HEADS * HEAD_DIM)


def _heads_major(x):
    b, t, h, dh = x.shape
    return x.transpose(0, 2, 1, 3).reshape(b * h, t, dh)


def _heads_back(o, b, h):
    t, dh = o.shape[1:]
    return o.reshape(b, h, t, dh).transpose(0, 2, 1, 3).reshape(b, t, h * dh)


def _nsa_sample(q_a, kv_c_rows, kv_s_rows, kv_w_rows, w_start, qpos0, tq, tab, cmp_w):
    b, t_real = q_a.shape[:2]
    assert t_real <= tq
    rows = NSA_GROUP * tq
    g = NSA_KV_HEADS
    tabn = tab[:, :NSA_HEADS]
    q_rows = _nsa_q_rows(_pad_axis(q_a * SCALE, 1, tq), tq).astype(_BF)

    ncp = _round_up(kv_c_rows.shape[1] // CMP_STRIDE, LANES)
    cmp_tok = _nsa_compress(_chunks_from_rows(kv_c_rows), *cmp_w)
    cmp_tok = _pad_axis(cmp_tok, 2, ncp).astype(_BF)
    kc = cmp_tok[:g].transpose(1, 0, 2, 3).reshape(b * g, ncp, HEAD_DIM)
    vc = cmp_tok[g:].transpose(1, 0, 2, 3).reshape(b * g, ncp, HEAD_DIM)
    ls = kv_s_rows.shape[1]
    n_blocks = ls // SEL_BLOCK
    o_c, selb = _cmp_select(q_rows, kc, vc, tab, tq, qpos0, n_blocks)
    nsl = selb.shape[2]

    sel_rows = jnp.broadcast_to(selb.reshape(b * g, 1, tq, nsl), (b * g, NSA_GROUP, tq, nsl)).reshape(b * g, rows, nsl)
    ks = _heads_major(kv_s_rows[:, :, 0]).astype(_BF)
    vs = _heads_major(kv_s_rows[:, :, 1]).astype(_BF)
    onehot = jnp.asarray((np.arange(ls)[:, None] // SEL_BLOCK == np.arange(nsl)[None, :]).astype(np.float32), _BF)
    bias = _group_rows(_bias_tile(tabn, qpos0, tq, ls))[:, None]
    o_s = _flash(q_rows, ks, vs, bias, [(0, 0, 0)], rows, ls, qb=sel_rows, kb=onehot)

    lw = kv_w_rows.shape[1]
    lwp = _round_up(lw, LANES)
    kw = _pad_axis(_heads_major(kv_w_rows[:, :, 0]), 1, lwp).astype(_BF)
    vw = _pad_axis(_heads_major(kv_w_rows[:, :, 1]), 1, lwp).astype(_BF)
    assert w_start >= 0
    col_ok = jnp.asarray(np.arange(lwp) < lw)[None, None, :]
    bias_w = jnp.where(col_ok, _bias_tile(tabn, qpos0 - w_start, tq, lwp, NSA_WINDOW), NEG_INF)
    o_w = _flash(q_rows, kw, vw, _group_rows(bias_w)[:, None], [(0, 0, 0)], rows, lwp)

    back = lambda o: _nsa_rows_back(o, b, tq, tq)[:, :t_real]
    return back(o_c), back(o_s), back(o_w)


def _fox_sample(q_f, cq, kv_f_rows, ck, qpos0, tq, tk):
    b, t_real, h, _ = q_f.shape
    assert t_real <= tq
    length = kv_f_rows.shape[1]
    lp = _round_up(length, tk)
    qh = _pad_axis(_heads_major(q_f * SCALE), 1, tq).astype(_BF)
    ones = jnp.ones((b * h, tq, 3), _BF)
    qa = _pad_last(jnp.concatenate([qh] + [_pad_axis(c, 1, tq)[..., None] for c in cq] + [ones], axis=-1), LANES)
    kh = _pad_axis(_heads_major(kv_f_rows[:, :, 0]), 1, lp).astype(_BF)
    vh = _pad_axis(_heads_major(kv_f_rows[:, :, 1]), 1, lp).astype(_BF)
    onesk = jnp.ones((b * h, lp, 3), _BF)
    ka = _pad_last(jnp.concatenate([kh, onesk] + [-_pad_axis(c, 1, lp)[..., None] for c in ck], axis=-1), LANES)
    qpos = qpos0 + np.arange(tq)
    kpos = np.arange(lp)
    okm = (kpos[None, :] <= qpos[:, None]) & (kpos[None, :] < length)
    m = np.where(okm, 0.0, NEG_INF).astype(np.float32)
    bias = jnp.asarray(m.reshape(1, tq, lp // tk, tk).transpose(0, 2, 1, 3))
    pairs = [(0, k, k) for k in range(lp // tk)]
    o = _flash(qa, ka, vh, bias, pairs, tq, tk)
    return _heads_back(o, b, h)[:, :t_real]


def _moba_sample(q, kv_rows, kmean, qpos0, tq, tab):
    b, t_real, h, _ = q.shape
    assert t_real <= tq
    lp = kv_rows.shape[1]
    n_blocks = lp // MOBA_BLOCK
    nbl = _round_up(n_blocks, LANES)
    qpad = _pad_axis(q, 1, tq)
    km = _pad_axis(kmean.reshape(b, kmean.shape[1], h, HEAD_DIM).transpose(0, 2, 1, 3), 2, nbl)
    km = km.reshape(b * h, nbl, HEAD_DIM).astype(_BF)
    selb = _moba_select(_heads_major(qpad).astype(_BF), km, tq, qpos0, n_blocks)
    qs = _heads_major(qpad * SCALE).astype(_BF)
    kh = _heads_major(kv_rows[:, :, 0]).astype(_BF)
    vh = _heads_major(kv_rows[:, :, 1]).astype(_BF)
    onehot = jnp.asarray((np.arange(lp)[:, None] // MOBA_BLOCK == np.arange(nbl)[None, :]).astype(np.float32), _BF)
    bias = _bias_tile(tab[:, :MOBA_HEADS], qpos0, tq, lp)[:, None]
    o = _flash(qs, kh, vh, bias, [(0, 0, 0)], tq, lp, qb=selb, kb=onehot)
    return _heads_back(o, b, h)[:, :t_real]


def _even_split(z, b, t):
    q_a, kv_c, kv_s, kv_w, g_a, q_f, kv_f, f_logit = jnp.split(z.reshape(b, t, -1), _EVEN_CUTS, axis=-1)
    kvshape = (b, t, 2, NSA_KV_HEADS, HEAD_DIM)
    g_bm = g_a.reshape(b, t, NSA_HEADS, 3).transpose(0, 1, 3, 2).reshape(b, t, 3 * NSA_HEADS)
    return (q_a.reshape(b, t, NSA_HEADS, HEAD_DIM), kv_c.reshape(kvshape), kv_s.reshape(kvshape),
            kv_w.reshape(kvshape), g_bm, q_f.reshape(b, t, FOX_HEADS, HEAD_DIM),
            kv_f.reshape(b, t, 2, FOX_HEADS, HEAD_DIM), f_logit)


def _even_row_perm():
    cuts = (0,) + _EVEN_CUTS + (sum(_EVEN_SIZES),)
    seg = lambda k: np.arange(cuts[k], cuts[k + 1])
    gates = cuts[4] + (np.arange(NSA_HEADS)[None, :] * 3 + np.arange(3)[:, None]).reshape(-1)
    return np.concatenate([seg(0), seg(1), seg(2), seg(3), seg(5), seg(6), gates, seg(7)])


def _kv_leaf(zf, row0, heads):
    bsz, _, s = zf.shape
    blk = zf[:, row0:row0 + 2 * heads * HEAD_DIM, :].reshape(bsz, 2, heads, HEAD_DIM, s)
    return blk.transpose(0, 4, 1, 2, 3)


def _even_prompt(x_rows, tab, w_in, b_in, cmp_w, w_out, ln_g, ln_b, alpha, tm):
    bsz, s, d = x_rows.shape
    perm = _even_row_perm()
    w_t = w_in.T[perm].astype(_BF)
    scale = np.ones((_EVEN_OUT, 1), np.float32)
    scale[_QA:_QA + NSA_HEADS * HEAD_DIM] = SCALE
    scale[_QF:_QF + FOX_HEADS * HEAD_DIM] = SCALE
    zf, zb = _proj_fm(x_rows, w_t, b_in[perm].reshape(-1, 1), jnp.asarray(scale), tm, True)
    tabn = tab[:, :NSA_HEADS]
    g = NSA_KV_HEADS
    hb = HEAD_DIM

    assert s % SEL_BLOCK == 0
    cmp_tok = _nsa_compress(_chunks_from_fm(zf[:, _KVC:_KVS, :]), *cmp_w)
    ncp = _round_up(cmp_tok.shape[2], LANES)
    cmp_tok = _pad_axis(cmp_tok, 2, ncp).astype(_BF)
    kc = cmp_tok[:g].transpose(1, 0, 2, 3).reshape(bsz * g, ncp, HEAD_DIM)
    vc_t = cmp_tok[g:].transpose(1, 0, 3, 2).reshape(bsz * g, HEAD_DIM, ncp)
    tq, tk = 256, 512
    o_c, sel = _cmp_select_t(zb, kc, vc_t, tab, tq, s // SEL_BLOCK)

    pairs, deltas = _plan_tiles(s // tq, tq, tk, None, True)
    bias = _bias_tiles_t(tabn, deltas, tk, tq, None, NSA_GROUP)
    o_s = _flash_t(zb, _QA // (NSA_GROUP * hb), _KVS // hb, _KVS // hb + g, g, NSA_GROUP, bias, pairs, tq, tk,
                   sel=sel, sel_block=SEL_BLOCK)
    pairs, deltas = _plan_tiles(s // tq, tq, tk, NSA_WINDOW, True)
    bias = _bias_tiles_t(tabn, deltas, tk, tq, NSA_WINDOW, NSA_GROUP)
    o_w = _flash_t(zb, _QA // (NSA_GROUP * hb), _KVW // hb, _KVW // hb + g, g, NSA_GROUP, bias, pairs, tq, tk)

    logf_t, c3 = _logf_cumsum(zf[:, _FL:_FL + FOX_HEADS, :].reshape(bsz * FOX_HEADS, s), s, bsz * FOX_HEADS)
    qaug, kaug = _fox_aug_rows(c3)
    tqf = tkf = 512
    pairs, deltas = _plan_tiles(s // tqf, tqf, tkf, None, False)
    bias = _bias_tiles_t(None, deltas, tkf, tqf, None, 1)
    o_f = _flash_t(zb, _QF // hb, _KVF // hb, _KVF // hb + FOX_HEADS, FOX_HEADS, 1, bias, pairs, tqf, tkf,
                   qaug=qaug, kaug=kaug)

    wide = lambda o: o.reshape(bsz, -1, s)
    x_t = _even_out_fm(alpha, wide(o_c), wide(o_s), wide(o_w), zf, wide(o_f), w_out.T.astype(_BF), x_rows,
                       ln_g, ln_b, tm)
    kv_c = _kv_leaf(zf, _KVC, g)
    kv_s = _kv_leaf(zf, _KVS, g)
    kv_w = _kv_leaf(zf, _KVW, g)
    kv_f = _kv_leaf(zf, _KVF, FOX_HEADS)
    logf = logf_t.reshape(bsz, FOX_HEADS, s).transpose(0, 2, 1)
    return x_t, (kv_c, kv_s, kv_w[:, max(s - NSA_WINDOW, 0):], kv_f, logf)


def _odd_prompt(x_t, tab, w_in, w_out, ln_g, ln_b, alpha, tm):
    bsz, d, s = x_t.shape
    hw = MOBA_HEADS * HEAD_DIM
    scale = np.ones((3 * hw, 1), np.float32)
    scale[:hw] = SCALE
    zf, zb = _proj_fm(x_t, w_in.T.astype(_BF), jnp.zeros((3 * hw, 1), _F32), jnp.asarray(scale), tm, False)
    assert s % MOBA_BLOCK == 0
    sel = _moba_select_t(zf, zb, MOBA_HEADS, 1024 if s % 1024 == 0 else 256)
    tq = tk = 512
    pairs, deltas = _plan_tiles(s // tq, tq, tk, None, True)
    bias = _bias_tiles_t(tab[:, :MOBA_HEADS], deltas, tk, tq, None, 1)
    o = _flash_t(zb, 0, MOBA_HEADS, 2 * MOBA_HEADS, MOBA_HEADS, 1, bias, pairs, tq, tk, sel=sel, sel_block=MOBA_BLOCK)
    x_t = _odd_out_fm(alpha, o.reshape(bsz, hw, s), w_out.T.astype(_BF), x_t, ln_g, ln_b, tm)
    return x_t, _kv_leaf(zf, hw, MOBA_HEADS)


def _kernel_impl(x_prompt, x_sample, cache_nsa_cmp, cache_nsa_sel, state_nsa_win, cache_fox_kv,
                 cache_fox_logf, cache_moba_kv, page_table, rel_bias, ln_g, ln_b, w_in_even, b_in_even,
                 nsa_cmp_pe, nsa_cmp_w1, nsa_cmp_b1, nsa_cmp_w2, nsa_cmp_b2, w_out_even, w_in_odd,
                 w_out_odd, moe_wg, moe_bg, moe_we, moe_be, moe_w1, moe_w3, moe_w2):
    bp, sp, d = x_prompt.shape
    bs, ts, _ = x_sample.shape
    n_pages = page_table.shape[1]
    past = n_pages * PAGE_SIZE
    depth = ln_g.shape[0]
    alpha = (2 * depth) ** 0.25
    ns_tok = bs * ts
    tm_p = 512
    tm_s = ns_tok
    assert sp % 1024 == 0 and depth % 2 == 0
    xp = x_prompt
    xs = x_sample.reshape(ns_tok, d)
    outs = {k: [] for k in ("cmp_p", "cmp_s", "sel_p", "sel_s", "win_p", "win_s", "fkv_p", "fkv_s",
                            "flf_p", "flf_s", "mkv_p", "mkv_s")}
    tq_s = BF16_ROWS

    for layer in range(depth):
        li = layer // 2
        if layer % 2 == 0:
            assert layer == 0
            cmp_w = (nsa_cmp_pe[li], nsa_cmp_w1[li], nsa_cmp_b1[li], nsa_cmp_w2[li], nsa_cmp_b2[li])
            xp, (kv_c, kv_s, kv_w, kv_f, logf) = _even_prompt(
                xp, rel_bias, w_in_even[li], b_in_even[li], cmp_w, w_out_even[li], ln_g[layer, 0], ln_b[layer, 0],
                alpha, tm_p)
            outs["cmp_p"].append(kv_c)
            outs["sel_p"].append(kv_s)
            outs["win_p"].append(kv_w)
            outs["fkv_p"].append(kv_f)
            outs["flf_p"].append(logf)
            w_in = w_in_even[li].astype(_BF)
            w_out = w_out_even[li].astype(_BF)
            z = _linear(xs, w_in, b_in_even[li], tm_s)
            q_a, kv_c, kv_s, kv_w, g_bm, q_f, kv_f, f_logit = _even_split(z, bs, ts)
            wcols = 2 * NSA_KV_HEADS * HEAD_DIM
            pool_c = cache_nsa_cmp[li].reshape(-1, PAGE_SIZE, wcols)
            pool_s = cache_nsa_sel[li].reshape(-1, PAGE_SIZE, wcols)
            kvshape = (bs, past, 2, NSA_KV_HEADS, HEAD_DIM)
            past_c = _gather_pages(pool_c, page_table, 0, 1, _F32).reshape(kvshape)
            past_s = _gather_pages(pool_s, page_table, 0, 1, _F32).reshape(kvshape)
            assert past % CMP_STRIDE == 0 and ts <= CMP_STRIDE
            ls = _round_up(_round_up(past + ts, SEL_BLOCK), LANES)
            rows_s = _pad_axis(jnp.concatenate([past_s, kv_s], axis=1), 1, ls)
            win_buf = state_nsa_win[li]
            wb = win_buf.shape[1]
            kvw_all = jnp.concatenate([win_buf, kv_w], axis=1)
            o_c, o_s, o_w = _nsa_sample(q_a, past_c, rows_s, kvw_all, past - wb, past, tq_s, rel_bias, cmp_w)
            fw = FOX_HEADS * HEAD_DIM
            pool_f = cache_fox_kv[li].reshape(-1, PAGE_SIZE, 2 * fw)
            past_f = jnp.concatenate([_gather_pages(pool_f, page_table, 0, 2, _BF),
                                      _gather_pages(pool_f, page_table, 1, 2, _BF)], axis=-1)
            past_f = past_f.reshape(bs, past, 2, FOX_HEADS, HEAD_DIM)
            pool_l = cache_fox_logf[li].reshape(-1, PAGE_SIZE, FOX_HEADS)
            past_l = _gather_pages(pool_l, page_table, 0, 1, _F32)
            lf_len = _round_up(past + ts, LANES)
            fl_all = jnp.concatenate([past_l, f_logit], axis=1).transpose(0, 2, 1)
            fl_all = jnp.concatenate([jnp.zeros((bs, FOX_HEADS, lf_len - past - ts), _F32), fl_all], axis=-1)
            logf_t, c3 = _logf_cumsum(fl_all.reshape(bs * FOX_HEADS, lf_len), ts, min(bs * FOX_HEADS, 64))
            ck3 = tuple(c[:, lf_len - past - ts:] for c in c3)
            cq3 = tuple(c[:, past:] for c in ck3)
            logf = logf_t[:, lf_len - ts:].reshape(bs, FOX_HEADS, ts).transpose(0, 2, 1)
            kv_f_all = jnp.concatenate([past_f, kv_f.astype(_BF)], axis=1)
            o_f = _fox_sample(q_f, cq3, kv_f_all, ck3, past, tq_s, _round_up(past + ts, LANES))
            xs = _even_out(alpha, o_c.reshape(ns_tok, -1), o_s.reshape(ns_tok, -1), o_w.reshape(ns_tok, -1),
                           g_bm.reshape(ns_tok, -1), o_f.reshape(ns_tok, -1), w_out, xs,
                           ln_g[layer, 0], ln_b[layer, 0], tm_s)
            outs["cmp_s"].append(kv_c)
            outs["sel_s"].append(kv_s)
            outs["win_s"].append(kvw_all[:, ts:])
            outs["fkv_s"].append(kv_f)
            outs["flf_s"].append(logf)
        else:
            hw = MOBA_HEADS * HEAD_DIM
            xp, kv = _odd_prompt(xp, rel_bias, w_in_odd[li], w_out_odd[li], ln_g[layer, 0], ln_b[layer, 0], alpha, tm_p)
            outs["mkv_p"].append(kv)
            w_in = w_in_odd[li].astype(_BF)
            w_out = w_out_odd[li].astype(_BF)
            z = _linear(xs, w_in, jnp.zeros((3 * hw,), _F32), tm_s).reshape(bs, ts, 3 * hw)
            q = z[..., :hw].reshape(bs, ts, MOBA_HEADS, HEAD_DIM)
            kv = z[..., hw:].reshape(bs, ts, 2, MOBA_HEADS, HEAD_DIM)
            pool_m = cache_moba_kv[li].reshape(-1, PAGE_SIZE, 2 * hw)
            past_k = _gather_pages(pool_m, page_table, 0, 2, _F32)
            past_v = _gather_pages(pool_m, page_table, 1, 2, _BF)
            assert past % MOBA_BLOCK == 0
            kmean = _block_mean(past_k)
            lpad = _round_up(past + ts, MOBA_BLOCK)
            k_all = _pad_axis(jnp.concatenate([past_k.astype(_BF), kv[:, :, 0].reshape(bs, ts, hw).astype(_BF)], 1), 1, lpad)
            v_all = _pad_axis(jnp.concatenate([past_v, kv[:, :, 1].reshape(bs, ts, hw).astype(_BF)], 1), 1, lpad)
            kv_all = jnp.stack([k_all, v_all], axis=2).reshape(bs, lpad, 2, MOBA_HEADS, HEAD_DIM)
            o = _moba_sample(q, kv_all, kmean, past, tq_s, rel_bias)
            xs = _odd_out(alpha, o.reshape(ns_tok, hw), w_out, xs, ln_g[layer, 0], ln_b[layer, 0], tm_s)
            outs["mkv_s"].append(kv)
        w1b, w3b, w2b = moe_w1[layer].astype(_BF), moe_w3[layer].astype(_BF), moe_w2[layer].astype(_BF)
        w13_t = jnp.concatenate([w1b.transpose(0, 2, 1), w3b.transpose(0, 2, 1)], axis=1)
        router = (moe_wg[layer], moe_bg[layer], moe_we[layer], moe_be[layer])
        xp = _moe_ln_fm(alpha, xp, *router, w13_t, w2b.transpose(0, 2, 1), ln_g[layer, 1], ln_b[layer, 1], tm_p,
                        layer == depth - 1)
        xs = _moe_ln(alpha, xs, *router, w1b, w3b, w2b, ln_g[layer, 1], ln_b[layer, 1], tm_s)

    st = lambda k: jnp.stack(outs[k])
    return (xp, xs.reshape(bs, ts, d), st("cmp_p"), st("cmp_s"), st("sel_p"), st("sel_s"),
            st("win_p"), st("win_s"), st("fkv_p"), st("fkv_s"), st("flf_p"), st("flf_s"), st("mkv_p"), st("mkv_s"))


def kernel(x_prompt, x_sample, cache_nsa_cmp, cache_nsa_sel, state_nsa_win, cache_fox_kv, cache_fox_logf, cache_moba_kv, page_table, rel_bias, ln_g, ln_b, w_in_even, b_in_even, nsa_cmp_pe, nsa_cmp_w1, nsa_cmp_b1, nsa_cmp_w2, nsa_cmp_b2, w_out_even, w_in_odd, w_out_odd, moe_wg, moe_bg, moe_we, moe_be, moe_w1, moe_w3, moe_w2):
    return _kernel_impl(x_prompt, x_sample, cache_nsa_cmp, cache_nsa_sel, state_nsa_win, cache_fox_kv,
                        cache_fox_logf, cache_moba_kv, page_table, rel_bias, ln_g, ln_b, w_in_even, b_in_even,
                        nsa_cmp_pe, nsa_cmp_w1, nsa_cmp_b1, nsa_cmp_w2, nsa_cmp_b2, w_out_even, w_in_odd,
                        w_out_odd, moe_wg, moe_bg, moe_we, moe_be, moe_w1, moe_w3, moe_w2)
```

```python
import functools
import math

import numpy as np
import jax
import jax.numpy as jnp
from jax import lax
from jax.experimental import pallas as pl
from jax.experimental.pallas import tpu as pltpu

_BF = jnp.bfloat16
_F32 = jnp.float32

HEAD_DIM = 64
NSA_KV_HEADS = 2
NSA_GROUP = 4
NSA_HEADS = NSA_KV_HEADS * NSA_GROUP
FOX_HEADS = 8
MOBA_HEADS = 16
CMP_LEN = 32
CMP_STRIDE = 16
CMP_HID = 128
SEL_BLOCK = 64
CMP_PER_SEL = SEL_BLOCK // CMP_STRIDE
NSA_TOPN = 16
NSA_WINDOW = 512
MOBA_BLOCK = 256
MOBA_TOPK = 3
N_BUCKETS = 32
T5_MAX_DISTANCE = 128
N_GROUPS = 4
EXPERTS_PER_GROUP = 4
N_EXPERTS = N_GROUPS * EXPERTS_PER_GROUP
PAGE_SIZE = 128
SCALE = HEAD_DIM ** -0.5
NEG_INF = -1e30
FORCE_SCORE = 1e4
LN_EPS = 1e-5
LANES = 128
SUBLANES = 8
BF16_ROWS = 16
VMEM_LIMIT = 48 * 1024 * 1024
PAGES_PER_STEP = 8

_QA, _KVC, _KVS, _KVW, _QF, _KVF, _GA, _FL = 0, 512, 768, 1024, 1280, 1792, 2816, 2840
_EVEN_OUT = 2848
_EVEN_SIZES = (NSA_HEADS * HEAD_DIM, 2 * NSA_KV_HEADS * HEAD_DIM, 2 * NSA_KV_HEADS * HEAD_DIM,
               2 * NSA_KV_HEADS * HEAD_DIM, 3 * NSA_HEADS, FOX_HEADS * HEAD_DIM,
               2 * FOX_HEADS * HEAD_DIM, FOX_HEADS)
_EVEN_CUTS = tuple(int(c) for c in np.cumsum(_EVEN_SIZES)[:-1])


def _cparams(*sem):
    return pltpu.CompilerParams(dimension_semantics=sem, vmem_limit_bytes=VMEM_LIMIT)


def _round_up(n, m):
    return (n + m - 1) // m * m


def _split3(x):
    hi = x.astype(_BF)
    r1 = x - hi.astype(_F32)
    mid = r1.astype(_BF)
    lo = (r1 - mid.astype(_F32)).astype(_BF)
    return hi, mid, lo


def _dot3(x, m01):
    hi, mid, lo = _split3(x)
    acc = jnp.dot(hi, m01, preferred_element_type=_F32)
    acc += jnp.dot(mid, m01, preferred_element_type=_F32)
    acc += jnp.dot(lo, m01, preferred_element_type=_F32)
    return acc


def _t5_bucket_np(dist):
    n = np.maximum(dist, 0)
    exact = N_BUCKETS // 2
    nf = np.maximum(n, exact).astype(np.float32)
    far = exact + (np.log(nf / np.float32(exact)) / np.float32(math.log(T5_MAX_DISTANCE / exact))
                   * np.float32(N_BUCKETS - exact)).astype(np.int32)
    return np.where(n < exact, n, np.minimum(far, N_BUCKETS - 1)).astype(np.int32)


def _bucket_thresholds():
    d = np.arange(0, 4 * T5_MAX_DISTANCE)
    b = _t5_bucket_np(d)
    return [int(d[b >= k][0]) for k in range(1, N_BUCKETS)]


_FAR_DISTANCE = _bucket_thresholds()[-1]


def _pad_last(x, width):
    return jnp.pad(x, [(0, 0)] * (x.ndim - 1) + [(0, width - x.shape[-1])])


def _pad_axis(x, axis, size):
    pads = [(0, 0)] * x.ndim
    pads[axis] = (0, size - x.shape[axis])
    return jnp.pad(x, pads)


def _toeplitz(g, rows, cols):
    n = g.shape[-1]
    lead = g.shape[:-1]
    x = jnp.broadcast_to(g[..., None, :], lead + (rows, n)).reshape(lead + (rows * n,))
    return x[..., :rows * (n - 1)].reshape(lead + (rows, n - 1))[..., :cols]


def _distance_values(tabh, d, window):
    valid = (d >= 0) if window is None else ((d >= 0) & (d < window))
    if tabh is None:
        vals = jnp.zeros((1, d.shape[0]), _F32)
    else:
        vals = tabh[jnp.asarray(_t5_bucket_np(d))].T
    return jnp.where(jnp.asarray(valid)[None], vals, NEG_INF).astype(_F32)


def _bias_tile(tabh, d0, rows, cols, window=None):
    n = rows + cols
    m = np.arange(n)
    d = np.where(m < cols, d0 - m, d0 + n - m)
    return _toeplitz(_distance_values(tabh, d, window), rows, cols)


def _bias_tile_t(tabh, delta, tk, tq, window=None):
    n = tk + tq
    m = np.arange(n)
    d = np.where(m < tq, delta + m, delta + m - n)
    return _toeplitz(_distance_values(tabh, d, window), tk, tq)


def _plan_tiles(nq, tq, tk, window, has_table):
    deltas, pairs = [], []
    for qi in range(nq):
        q0 = qi * tq
        k_hi = (q0 + tq - 1) // tk
        k_lo = 0 if window is None else max(0, (q0 - (window - 1)) // tk)
        for ki in range(k_lo, k_hi + 1):
            delta = q0 - ki * tk
            dmin, dmax = delta - (tk - 1), delta + tq - 1
            plain = dmin >= (_FAR_DISTANCE if has_table else 0) and (window is None or dmax < window)
            if plain:
                pairs.append((qi, ki, -1))
            else:
                if delta not in deltas:
                    deltas.append(delta)
                pairs.append((qi, ki, deltas.index(delta)))
    pairs = [(q, k, b if b >= 0 else len(deltas)) for q, k, b in pairs]
    return pairs, deltas


def _bias_tiles_t(tabh, deltas, tk, tq, window, group):
    far = tk + tq + _FAR_DISTANCE
    tiles = [_bias_tile_t(tabh, dl, tk, tq, window) for dl in deltas]
    tiles.append(_bias_tile_t(tabh, far, tk, tq, None))
    t = jnp.stack(tiles, axis=1)
    h, nb = t.shape[:2]
    t = t.reshape(h // group, group, nb, tk, tq).transpose(0, 2, 3, 1, 4)
    return t.reshape(h // group, nb, tk, group * tq)


def _layer_norm_cols(y, g, b):
    mu = jnp.mean(y, axis=0, keepdims=True)
    yc = y - mu
    var = jnp.mean(yc * yc, axis=0, keepdims=True)
    return yc * lax.rsqrt(var + LN_EPS) * g + b


def _proj_fm_kernel(row_major_in, x_ref, w_ref, b_ref, sc_ref, zf_ref, zb_ref):
    x = x_ref[0].astype(_BF)
    if row_major_in:
        z = lax.dot_general(w_ref[...], x, (((1,), (1,)), ((), ())), preferred_element_type=_F32)
    else:
        z = jnp.dot(w_ref[...], x, preferred_element_type=_F32)
    z = z + b_ref[...]
    zf_ref[0] = z
    zb_ref[0] = (z * sc_ref[...]).astype(_BF)


def _proj_fm(x, w_t_bf, b_col, scale_col, tm, row_major_in):
    bsz = x.shape[0]
    s = x.shape[1] if row_major_in else x.shape[2]
    d = x.shape[2] if row_major_in else x.shape[1]
    n = w_t_bf.shape[0]
    x_spec = (pl.BlockSpec((1, tm, d), lambda b, i: (b, i, 0)) if row_major_in
              else pl.BlockSpec((1, d, tm), lambda b, i: (b, 0, i)))
    col = pl.BlockSpec((n, 1), lambda b, i: (0, 0))
    out = pl.BlockSpec((1, n, tm), lambda b, i: (b, 0, i))
    return pl.pallas_call(
        functools.partial(_proj_fm_kernel, row_major_in),
        grid=(bsz, s // tm),
        in_specs=[x_spec, pl.BlockSpec((n, d), lambda b, i: (0, 0)), col, col],
        out_specs=[out, out],
        out_shape=[jax.ShapeDtypeStruct((bsz, n, s), _F32), jax.ShapeDtypeStruct((bsz, n, s), _BF)],
        compiler_params=_cparams("parallel", "parallel"), name="proj_fm",
    )(x, w_t_bf, b_col, scale_col)


def _even_out_fm_kernel(alpha, oc_ref, os_ref, ow_ref, gz_ref, of_ref, wo_ref, x_ref, g_ref, b_ref, y_ref):
    wa = NSA_HEADS * HEAD_DIM
    sg = jax.nn.sigmoid(gz_ref[0])
    parts = []
    for h in range(NSA_HEADS):
        sl = slice(h * HEAD_DIM, (h + 1) * HEAD_DIM)
        parts.append(sg[h:h + 1] * oc_ref[0, sl, :] + sg[NSA_HEADS + h:NSA_HEADS + h + 1] * os_ref[0, sl, :]
                     + sg[2 * NSA_HEADS + h:2 * NSA_HEADS + h + 1] * ow_ref[0, sl, :])
    o_a = jnp.concatenate(parts, axis=0)
    m = jnp.dot(wo_ref[:, 0:wa], o_a.astype(_BF), preferred_element_type=_F32)
    m += jnp.dot(wo_ref[:, wa:], of_ref[0].astype(_BF), preferred_element_type=_F32)
    y_ref[0] = _layer_norm_cols(alpha * x_ref[0].T + m, g_ref[...], b_ref[...])


def _even_out_fm(alpha, o_c, o_s, o_w, zf, o_f, w_out_t_bf, x_rows, g, b, tm):
    bsz, s, d = x_rows.shape
    wa = NSA_HEADS * HEAD_DIM
    gate_rows = _EVEN_OUT - _GA
    blk = lambda rows: pl.BlockSpec((1, rows, tm), lambda b_, i: (b_, 0, i))
    col = pl.BlockSpec((d, 1), lambda b_, i: (0, 0))
    return pl.pallas_call(
        functools.partial(_even_out_fm_kernel, alpha),
        grid=(bsz, s // tm),
        in_specs=[blk(wa), blk(wa), blk(wa),
                  pl.BlockSpec((1, gate_rows, tm), lambda b_, i: (b_, _GA // gate_rows, i)),
                  blk(FOX_HEADS * HEAD_DIM),
                  pl.BlockSpec(w_out_t_bf.shape, lambda b_, i: (0, 0)),
                  pl.BlockSpec((1, tm, d), lambda b_, i: (b_, i, 0)), col, col],
        out_specs=blk(d),
        out_shape=jax.ShapeDtypeStruct((bsz, d, s), _F32),
        compiler_params=_cparams("parallel", "parallel"), name="even_out_ln_fm",
    )(o_c, o_s, o_w, zf, o_f, w_out_t_bf, x_rows, g.reshape(d, 1), b.reshape(d, 1))


def _odd_out_fm_kernel(alpha, o_ref, wo_ref, x_ref, g_ref, b_ref, y_ref):
    m = jnp.dot(wo_ref[...], o_ref[0].astype(_BF), preferred_element_type=_F32)
    y_ref[0] = _layer_norm_cols(alpha * x_ref[0] + m, g_ref[...], b_ref[...])


def _odd_out_fm(alpha, o, w_out_t_bf, x_t, g, b, tm):
    bsz, d, s = x_t.shape
    blk = lambda rows: pl.BlockSpec((1, rows, tm), lambda b_, i: (b_, 0, i))
    col = pl.BlockSpec((d, 1), lambda b_, i: (0, 0))
    return pl.pallas_call(
        functools.partial(_odd_out_fm_kernel, alpha),
        grid=(bsz, s // tm),
        in_specs=[blk(o.shape[1]), pl.BlockSpec(w_out_t_bf.shape, lambda b_, i: (0, 0)), blk(d), col, col],
        out_specs=blk(d), out_shape=jax.ShapeDtypeStruct((bsz, d, s), _F32),
        compiler_params=_cparams("parallel", "parallel"), name="odd_out_ln_fm",
    )(o, w_out_t_bf, x_t, g.reshape(d, 1), b.reshape(d, 1))


def _route_gates(logits, axis):
    idx = lax.broadcasted_iota(jnp.int32, logits.shape, axis)
    big = jnp.int32(1 << 20)
    red = lambda f, v: f(v, axis=axis, keepdims=True)
    is_g = idx < N_GROUPS
    lg = jnp.where(is_g, logits, NEG_INF)
    mg = red(jnp.max, lg)
    sg = red(jnp.sum, jnp.where(is_g, jnp.exp(lg - mg), 0.0))
    p_top = 1.0 / sg
    g_top = red(jnp.min, jnp.where(lg == mg, idx, big))
    lo = N_GROUPS + EXPERTS_PER_GROUP * g_top
    in_grp = (idx >= lo) & (idx < lo + EXPERTS_PER_GROUP)
    le = jnp.where(in_grp, logits, NEG_INF)
    me = red(jnp.max, le)
    se = red(jnp.sum, jnp.where(in_grp, jnp.exp(le - me), 0.0))
    i1 = red(jnp.min, jnp.where(le == me, idx, big))
    le2 = jnp.where(idx == i1, NEG_INF, le)
    m2 = red(jnp.max, le2)
    i2 = red(jnp.min, jnp.where(le2 == m2, idx, big))
    w1 = 1.0 / se
    w2 = jnp.exp(m2 - me) / se
    tot = w1 + w2
    return jnp.where(idx == i1, p_top * w1 / tot, jnp.where(idx == i2, p_top * w2 / tot, 0.0))


def _moe_fm_kernel(alpha, rows_out, x_ref, wr_ref, br_ref, w13_ref, w2_ref, g_ref, b_ref, y_ref, gate_ref, acc_ref):
    e = pl.program_id(2)
    x = x_ref[0]
    xb = x.astype(_BF)
    f = w13_ref.shape[1] // 2

    @pl.when(e == 0)
    def _route():
        logits = jnp.dot(wr_ref[...], xb, preferred_element_type=_F32) + br_ref[...]
        gate_ref[...] = _route_gates(logits, 0)
        acc_ref[...] = jnp.zeros_like(acc_ref)

    h13 = jnp.dot(w13_ref[0], xb, preferred_element_type=_F32)
    ge = gate_ref[pl.ds(e + N_GROUPS, 1), :]
    h = (jax.nn.silu(h13[0:f]) * h13[f:2 * f]) * ge
    acc_ref[...] += jnp.dot(w2_ref[0], h.astype(_BF), preferred_element_type=_F32)

    @pl.when(e == N_EXPERTS - 1)
    def _finish():
        y = _layer_norm_cols(alpha * x + acc_ref[...], g_ref[...], b_ref[...])
        y_ref[0] = y.T if rows_out else y


def _moe_ln_fm(alpha, x_t, wg, bg, we, be, w13_t_bf, w2_t_bf, g, b, tm, rows_out):
    bsz, d, s = x_t.shape
    rr = 2 * BF16_ROWS
    wr = jnp.zeros((rr, d), _F32).at[:N_GROUPS].set(wg.T).at[N_GROUPS:N_GROUPS + N_EXPERTS].set(we.T).astype(_BF)
    br = jnp.zeros((rr, 1), _F32).at[:N_GROUPS, 0].set(bg).at[N_GROUPS:N_GROUPS + N_EXPERTS, 0].set(be)
    f2 = w13_t_bf.shape[1]
    col = pl.BlockSpec((d, 1), lambda b_, i, e: (0, 0))
    out_spec = (pl.BlockSpec((1, tm, d), lambda b_, i, e: (b_, i, 0)) if rows_out
                else pl.BlockSpec((1, d, tm), lambda b_, i, e: (b_, 0, i)))
    out_shape = jax.ShapeDtypeStruct((bsz, s, d) if rows_out else (bsz, d, s), _F32)
    return pl.pallas_call(
        functools.partial(_moe_fm_kernel, alpha, rows_out),
        grid=(bsz, s // tm, N_EXPERTS),
        in_specs=[pl.BlockSpec((1, d, tm), lambda b_, i, e: (b_, 0, i)),
                  pl.BlockSpec((rr, d), lambda b_, i, e: (0, 0)),
                  pl.BlockSpec((rr, 1), lambda b_, i, e: (0, 0)),
                  pl.BlockSpec((1, f2, d), lambda b_, i, e: (e, 0, 0)),
                  pl.BlockSpec((1, d, f2 // 2), lambda b_, i, e: (e, 0, 0)),
                  col, col],
        out_specs=out_spec, out_shape=out_shape,
        scratch_shapes=[pltpu.VMEM((rr, tm), _F32), pltpu.VMEM((d, tm), _F32)],
        compiler_params=_cparams("parallel", "parallel", "arbitrary"), name="moe_ln_fm",
    )(x_t, wr, br, w13_t_bf, w2_t_bf, g.reshape(d, 1), b.reshape(d, 1))


def _flash_t_kernel(group, aug, sel_rows, sel_block, qi_ref, ki_ref, bi_ref, first_ref, last_ref, *refs):
    refs = list(refs)
    q_ref, k_ref, v_ref, bias_ref = refs[:4]
    pos = 4
    if aug:
        qaug_ref, kaug_ref = refs[pos:pos + 2]
        pos += 2
    if sel_rows:
        sel_ref = refs[pos]
        pos += 1
    o_ref, m_ref, l_ref, acc_ref = refs[pos:pos + 4]
    step = pl.program_id(1)
    tq = q_ref.shape[2]

    @pl.when(first_ref[step] == 1)
    def _init():
        m_ref[...] = jnp.full_like(m_ref, NEG_INF)
        l_ref[...] = jnp.zeros_like(l_ref)
        acc_ref[...] = jnp.zeros_like(acc_ref)

    k_t = k_ref[0]
    if aug:
        k_t = jnp.concatenate([k_t, kaug_ref[0]], axis=0)
    v_t = v_ref[0]
    if sel_rows:
        off = (ki_ref[step] * sel_rows) % SUBLANES
        sel_tile = jnp.concatenate(
            [jnp.broadcast_to(sel_ref[0, pl.ds(off + j, 1), :], (sel_block, tq)) for j in range(sel_rows)], axis=0)
    for r in range(group):
        rows = slice(r * HEAD_DIM, (r + 1) * HEAD_DIM)
        q_t = q_ref[0, rows, :]
        if aug:
            q_t = jnp.concatenate([q_t, qaug_ref[0]], axis=0)
        s = lax.dot_general(k_t, q_t, (((0,), (0,)), ((), ())), preferred_element_type=_F32)
        s = s + bias_ref[0, 0, :, r * tq:(r + 1) * tq]
        if sel_rows:
            s = s + sel_tile
        m_old = m_ref[r:r + 1, :]
        m_new = jnp.maximum(m_old, jnp.max(s, axis=0, keepdims=True))
        a = jnp.exp(m_old - m_new)
        p = jnp.exp(s - m_new)
        l_ref[r:r + 1, :] = a * l_ref[r:r + 1, :] + jnp.sum(p, axis=0, keepdims=True)
        acc_ref[rows, :] = a * acc_ref[rows, :] + jnp.dot(v_t, p.astype(_BF), preferred_element_type=_F32)
        m_ref[r:r + 1, :] = m_new

    @pl.when(last_ref[step] == 1)
    def _done():
        for r in range(group):
            rows = slice(r * HEAD_DIM, (r + 1) * HEAD_DIM)
            o_ref[0, rows, :] = acc_ref[rows, :] / jnp.maximum(l_ref[r:r + 1, :], 1e-30)


def _flash_t(zb, q_blk, k_blk, v_blk, heads, group, bias, pairs, tq, tk, qaug=None, kaug=None, sel=None, sel_block=0):
    bsz, _, s = zb.shape
    bh = bsz * heads
    hb = bias.shape[0]
    aug = qaug is not None
    sel_rows = 0 if sel is None else tk // sel_block
    pairs = np.asarray(pairs, np.int32)
    qi, ki, bi = pairs[:, 0], pairs[:, 1], pairs[:, 2]
    first = np.concatenate([[1], (qi[1:] != qi[:-1]).astype(np.int32)]).astype(np.int32)
    last = np.concatenate([(qi[1:] != qi[:-1]).astype(np.int32), [1]]).astype(np.int32)
    gq = group * HEAD_DIM
    in_specs = [pl.BlockSpec((1, gq, tq), lambda b, t, qi, ki, bi, f, l: (b // heads, q_blk + b % heads, qi[t])),
                pl.BlockSpec((1, HEAD_DIM, tk), lambda b, t, qi, ki, bi, f, l: (b // heads, k_blk + b % heads, ki[t])),
                pl.BlockSpec((1, HEAD_DIM, tk), lambda b, t, qi, ki, bi, f, l: (b // heads, v_blk + b % heads, ki[t])),
                pl.BlockSpec((1, 1, tk, group * tq), lambda b, t, qi, ki, bi, f, l: (b % hb, bi[t], 0, 0))]
    args = [zb, zb, zb, bias]
    if aug:
        in_specs += [pl.BlockSpec((1, BF16_ROWS, tq), lambda b, t, qi, ki, bi, f, l: (b, 0, qi[t])),
                     pl.BlockSpec((1, BF16_ROWS, tk), lambda b, t, qi, ki, bi, f, l: (b, 0, ki[t]))]
        args += [qaug, kaug]
    if sel_rows:
        assert sel_rows in (1, 2, 4, 8)
        in_specs += [pl.BlockSpec((1, SUBLANES, tq),
                                  lambda b, t, qi, ki, bi, f, l: (b, (ki[t] * sel_rows) // SUBLANES, qi[t]))]
        args += [sel]
    grid_spec = pltpu.PrefetchScalarGridSpec(
        num_scalar_prefetch=5, grid=(bh, len(qi)), in_specs=in_specs,
        out_specs=pl.BlockSpec((1, gq, tq), lambda b, t, qi, ki, bi, f, l: (b, 0, qi[t])),
        scratch_shapes=[pltpu.VMEM((SUBLANES, tq), _F32), pltpu.VMEM((SUBLANES, tq), _F32), pltpu.VMEM((gq, tq), _F32)],
    )
    return pl.pallas_call(
        functools.partial(_flash_t_kernel, group, aug, sel_rows, sel_block), grid_spec=grid_spec,
        out_shape=jax.ShapeDtypeStruct((bh, gq, s), _F32),
        compiler_params=_cparams("parallel", "arbitrary"), name="flash_t",
    )(jnp.asarray(qi), jnp.asarray(ki), jnp.asarray(bi), jnp.asarray(first), jnp.asarray(last), *args)


def _topk_axis(score, n_sel, keep, axis):
    j = lax.broadcasted_iota(jnp.int32, score.shape, axis)
    big = jnp.int32(1 << 20)
    for _ in range(n_sel):
        mx = jnp.max(score, axis=axis, keepdims=True)
        firsti = jnp.min(jnp.where(score == mx, j, big), axis=axis, keepdims=True)
        hit = j == firsti
        keep = keep | (hit & (mx > 0.5 * NEG_INF))
        score = jnp.where(hit, -3e38, score)
    return keep


def _topk_rows(score, n_sel, keep):
    return _topk_axis(score, n_sel, keep, 0)


def _cmp_select_t_kernel(theta, tq, n_sel, tab_ref, q_ref, kc_ref, vc_ref, msel_ref, o_ref, sel_ref):
    g = pl.program_id(0) % NSA_KV_HEADS
    i = pl.program_id(1)
    ncp = kc_ref.shape[1]
    nsl = sel_ref.shape[1]
    n = lax.broadcasted_iota(jnp.int32, (ncp, tq), 0)
    t = i * tq + lax.broadcasted_iota(jnp.int32, (ncp, tq), 1)
    d = t - (n * CMP_STRIDE + (CMP_LEN - 1))
    ok = d >= 0
    ind = [d >= th for th in theta]
    kc = kc_ref[0]
    vc_t = vc_ref[0]
    imp = jnp.zeros((ncp, tq), _F32)
    for r in range(NSA_GROUP):
        base = (g * NSA_GROUP + r) * N_BUCKETS
        bias = jnp.full((ncp, tq), tab_ref[base], _F32)
        for k in range(1, N_BUCKETS):
            bias = bias + jnp.where(ind[k - 1], tab_ref[base + k] - tab_ref[base + k - 1], 0.0)
        rows = slice(r * HEAD_DIM, (r + 1) * HEAD_DIM)
        s = jnp.dot(kc, q_ref[0, rows, :], preferred_element_type=_F32) + bias
        s = jnp.where(ok, s, NEG_INF)
        m = jnp.max(s, axis=0, keepdims=True)
        e = jnp.where(ok, jnp.exp(s - m), 0.0)
        p = e / jnp.maximum(jnp.sum(e, axis=0, keepdims=True), 1e-30)
        o_ref[0, rows, :] = jnp.dot(vc_t, p.astype(_BF), preferred_element_type=_F32)
        imp = imp + p
    hi, mid, lo = _split3(imp)
    msel = msel_ref[...]
    p_s = (jnp.dot(msel, hi, preferred_element_type=_F32) + jnp.dot(msel, mid, preferred_element_type=_F32)
           + jnp.dot(msel, lo, preferred_element_type=_F32))
    j = lax.broadcasted_iota(jnp.int32, (nsl, tq), 0)
    qb = (i * tq + lax.broadcasted_iota(jnp.int32, (nsl, tq), 1)) >> int(math.log2(SEL_BLOCK))
    valid = j <= qb
    forced = (j == 0) | (j == qb) | (j == qb - 1)
    score = jnp.where(valid, jnp.where(forced, FORCE_SCORE, p_s), NEG_INF)
    chosen = _topk_rows(score, n_sel, jnp.zeros((nsl, tq), jnp.bool_))
    sel_ref[0] = jnp.where(chosen, 0.0, NEG_INF)


def _cmp_select_t(zb, kc, vc_t, tab, tq, n_blocks):
    bsz, _, s = zb.shape
    bg = bsz * NSA_KV_HEADS
    ncp = kc.shape[1]
    nsl = _round_up(n_blocks, LANES)
    gq = NSA_GROUP * HEAD_DIM
    jj = np.arange(nsl)[:, None]
    nn = np.arange(ncp)[None, :]
    msel = ((nn >= CMP_PER_SEL * jj - 1) & (nn <= CMP_PER_SEL * jj + CMP_PER_SEL - 1) & (jj < n_blocks))
    msel = jnp.asarray(msel.astype(np.float32), _BF)
    tabf = tab[:, :NSA_HEADS].T.reshape(-1)
    kern = functools.partial(_cmp_select_t_kernel, _bucket_thresholds(), tq, min(NSA_TOPN, n_blocks))
    return pl.pallas_call(
        kern,
        grid=(bg, s // tq),
        in_specs=[pl.BlockSpec(memory_space=pltpu.SMEM),
                  pl.BlockSpec((1, gq, tq), lambda b, i: (b // NSA_KV_HEADS, b % NSA_KV_HEADS, i)),
                  pl.BlockSpec((1, ncp, HEAD_DIM), lambda b, i: (b, 0, 0)),
                  pl.BlockSpec((1, HEAD_DIM, ncp), lambda b, i: (b, 0, 0)),
                  pl.BlockSpec((nsl, ncp), lambda b, i: (0, 0))],
        out_specs=[pl.BlockSpec((1, gq, tq), lambda b, i: (b, 0, i)),
                   pl.BlockSpec((1, nsl, tq), lambda b, i: (b, 0, i))],
        out_shape=[jax.ShapeDtypeStruct((bg, gq, s), _F32), jax.ShapeDtypeStruct((bg, nsl, s), _F32)],
        compiler_params=_cparams("parallel", "parallel"), name="nsa_cmp_select_t",
    )(tabf, zb, kc, vc_t, msel)


def _moba_select_t_kernel(tq, n_blocks, kf_ref, avg_ref, q_ref, sel_ref, km_ref):
    i = pl.program_id(1)

    @pl.when(i == 0)
    def _means():
        km_ref[...] = _dot3(kf_ref[0], avg_ref[...])

    gate = lax.dot_general(km_ref[...].astype(_BF), q_ref[0], (((0,), (0,)), ((), ())),
                           preferred_element_type=_F32)
    j = lax.broadcasted_iota(jnp.int32, gate.shape, 0)
    qb = (i * tq + lax.broadcasted_iota(jnp.int32, gate.shape, 1)) >> int(math.log2(MOBA_BLOCK))
    score = jnp.where(j < qb, gate, NEG_INF)
    chosen = _topk_rows(score, min(MOBA_TOPK, n_blocks), j == qb)
    sel_ref[0] = jnp.where(chosen, 0.0, NEG_INF)[0:sel_ref.shape[1]]


def _moba_select_t(zf, zb, k_blk, tq):
    bsz, _, s = zb.shape
    n_blocks = s // MOBA_BLOCK
    assert n_blocks <= LANES
    rows = _round_up(n_blocks, SUBLANES)
    avg = (np.arange(s)[:, None] // MOBA_BLOCK == np.arange(LANES)[None, :]).astype(np.float32) / MOBA_BLOCK
    bh = bsz * MOBA_HEADS
    return pl.pallas_call(
        functools.partial(_moba_select_t_kernel, tq, n_blocks),
        grid=(bh, s // tq),
        in_specs=[pl.BlockSpec((1, HEAD_DIM, s), lambda b, i: (b // MOBA_HEADS, k_blk + b % MOBA_HEADS, 0)),
                  pl.BlockSpec((s, LANES), lambda b, i: (0, 0)),
                  pl.BlockSpec((1, HEAD_DIM, tq), lambda b, i: (b // MOBA_HEADS, b % MOBA_HEADS, i))],
        out_specs=pl.BlockSpec((1, rows, tq), lambda b, i: (b, 0, i)),
        out_shape=jax.ShapeDtypeStruct((bh, rows, s), _F32),
        scratch_shapes=[pltpu.VMEM((HEAD_DIM, LANES), _F32)],
        compiler_params=_cparams("parallel", "arbitrary"), name="moba_select_t",
    )(zf, jnp.asarray(avg, _BF), zb)


def _linear_kernel(x_ref, w_ref, b_ref, o_ref):
    o_ref[...] = jnp.dot(x_ref[...].astype(_BF), w_ref[...], preferred_element_type=_F32) + b_ref[...]


def _linear(x, w_bf, b, tm):
    m, k = x.shape
    n = w_bf.shape[1]
    return pl.pallas_call(
        _linear_kernel,
        grid=(m // tm,),
        in_specs=[pl.BlockSpec((tm, k), lambda i: (i, 0)),
                  pl.BlockSpec((k, n), lambda i: (0, 0)),
                  pl.BlockSpec((1, n), lambda i: (0, 0))],
        out_specs=pl.BlockSpec((tm, n), lambda i: (i, 0)),
        out_shape=jax.ShapeDtypeStruct((m, n), _F32),
        compiler_params=_cparams("parallel"),
        name="linear",
    )(x, w_bf, b.reshape(1, n))


def _layer_norm_rows(y, g, b):
    mu = jnp.mean(y, axis=-1, keepdims=True)
    yc = y - mu
    var = jnp.mean(yc * yc, axis=-1, keepdims=True)
    return yc * lax.rsqrt(var + LN_EPS) * g + b


def _even_out_kernel(alpha, oc_ref, os_ref, ow_ref, gl_ref, of_ref, ex_ref, wa_ref, wf_ref, x_ref,
                     g_ref, b_ref, y_ref):
    wa = NSA_HEADS * HEAD_DIM
    gexp = _dot3(jax.nn.sigmoid(gl_ref[...]), ex_ref[...])
    o_a = (gexp[:, 0:wa] * oc_ref[...] + gexp[:, wa:2 * wa] * os_ref[...] + gexp[:, 2 * wa:3 * wa] * ow_ref[...])
    m = jnp.dot(o_a.astype(_BF), wa_ref[...], preferred_element_type=_F32)
    m += jnp.dot(of_ref[...].astype(_BF), wf_ref[...], preferred_element_type=_F32)
    y_ref[...] = _layer_norm_rows(alpha * x_ref[...] + m, g_ref[...], b_ref[...])


def _even_out(alpha, o_c, o_s, o_w, gl, o_f, w_out_bf, x, g, b, tm):
    n, d = x.shape
    wa = NSA_HEADS * HEAD_DIM
    wf = FOX_HEADS * HEAD_DIM
    ex_np = np.zeros((LANES, 3 * wa), np.float32)
    for j in range(3):
        for h in range(NSA_HEADS):
            ex_np[j * NSA_HEADS + h, j * wa + h * HEAD_DIM:j * wa + (h + 1) * HEAD_DIM] = 1.0
    ex = jnp.asarray(ex_np, _BF)
    gl = _pad_last(gl, LANES)
    row = lambda w: pl.BlockSpec((tm, w), lambda i: (i, 0))
    full = lambda a: pl.BlockSpec(a.shape, lambda i: (0,) * a.ndim)
    args = (o_c, o_s, o_w, gl, o_f, ex, w_out_bf[:wa], w_out_bf[wa:], x, g.reshape(1, d), b.reshape(1, d))
    specs = [row(wa), row(wa), row(wa), row(LANES), row(wf), full(ex), full(args[6]), full(args[7]),
             row(d), full(args[9]), full(args[10])]
    return pl.pallas_call(
        functools.partial(_even_out_kernel, alpha),
        grid=(n // tm,), in_specs=specs, out_specs=row(d),
        out_shape=jax.ShapeDtypeStruct((n, d), _F32),
        compiler_params=_cparams("parallel"), name="even_out_ln",
    )(*args)


def _odd_out_kernel(alpha, o_ref, w_ref, x_ref, g_ref, b_ref, y_ref):
    m = jnp.dot(o_ref[...].astype(_BF), w_ref[...], preferred_element_type=_F32)
    y_ref[...] = _layer_norm_rows(alpha * x_ref[...] + m, g_ref[...], b_ref[...])


def _odd_out(alpha, o, w_out_bf, x, g, b, tm):
    n, d = x.shape
    row = lambda w: pl.BlockSpec((tm, w), lambda i: (i, 0))
    full = lambda shp: pl.BlockSpec(shp, lambda i: (0,) * len(shp))
    return pl.pallas_call(
        functools.partial(_odd_out_kernel, alpha),
        grid=(n // tm,),
        in_specs=[row(o.shape[1]), full(w_out_bf.shape), row(d), full((1, d)), full((1, d))],
        out_specs=row(d), out_shape=jax.ShapeDtypeStruct((n, d), _F32),
        compiler_params=_cparams("parallel"), name="odd_out_ln",
    )(o, w_out_bf, x, g.reshape(1, d), b.reshape(1, d))


def _moe_kernel(alpha, x_ref, wr_ref, br_ref, w1_ref, w3_ref, w2_ref, g_ref, b_ref, y_ref,
                gate_ref, acc_ref):
    e = pl.program_id(1)
    x = x_ref[...]

    @pl.when(e == 0)
    def _route():
        logits = jnp.dot(x.astype(_BF), wr_ref[...], preferred_element_type=_F32) + br_ref[...]
        gate_ref[...] = _route_gates(logits, 1)
        acc_ref[...] = jnp.zeros_like(acc_ref)

    xb = x.astype(_BF)
    h1 = jnp.dot(xb, w1_ref[0], preferred_element_type=_F32)
    h3 = jnp.dot(xb, w3_ref[0], preferred_element_type=_F32)
    gate = gate_ref[...]
    lane = lax.broadcasted_iota(jnp.int32, gate.shape, 1)
    ge = jnp.sum(jnp.where(lane == e + N_GROUPS, gate, 0.0), axis=-1, keepdims=True)
    h = (jax.nn.silu(h1) * h3) * ge
    acc_ref[...] += jnp.dot(h.astype(_BF), w2_ref[0], preferred_element_type=_F32)

    @pl.when(e == N_EXPERTS - 1)
    def _finish():
        y_ref[...] = _layer_norm_rows(alpha * x + acc_ref[...], g_ref[...], b_ref[...])


def _moe_ln(alpha, x, wg, bg, we, be, w1_bf, w3_bf, w2_bf, g, b, tm):
    n, d = x.shape
    f = w1_bf.shape[2]
    wr = jnp.zeros((d, LANES), _F32).at[:, :N_GROUPS].set(wg).at[:, N_GROUPS:N_GROUPS + N_EXPERTS].set(we)
    wr = wr.astype(_BF)
    br = jnp.zeros((1, LANES), _F32).at[0, :N_GROUPS].set(bg).at[0, N_GROUPS:N_GROUPS + N_EXPERTS].set(be)
    return pl.pallas_call(
        functools.partial(_moe_kernel, alpha),
        grid=(n // tm, N_EXPERTS),
        in_specs=[pl.BlockSpec((tm, d), lambda i, e: (i, 0)),
                  pl.BlockSpec((d, LANES), lambda i, e: (0, 0)),
                  pl.BlockSpec((1, LANES), lambda i, e: (0, 0)),
                  pl.BlockSpec((1, d, f), lambda i, e: (e, 0, 0)),
                  pl.BlockSpec((1, d, f), lambda i, e: (e, 0, 0)),
                  pl.BlockSpec((1, f, d), lambda i, e: (e, 0, 0)),
                  pl.BlockSpec((1, d), lambda i, e: (0, 0)),
                  pl.BlockSpec((1, d), lambda i, e: (0, 0))],
        out_specs=pl.BlockSpec((tm, d), lambda i, e: (i, 0)),
        out_shape=jax.ShapeDtypeStruct((n, d), _F32),
        scratch_shapes=[pltpu.VMEM((tm, LANES), _F32), pltpu.VMEM((tm, d), _F32)],
        compiler_params=_cparams("parallel", "arbitrary"), name="moe_ln",
    )(x, wr, br, w1_bf, w3_bf, w2_bf, g.reshape(1, d), b.reshape(1, d))


def _pages_per_step(n_pages):
    return next(n for n in (PAGES_PER_STEP, 4, 2, 1) if n_pages % n == 0)


def _paged_attn_kernel(n_pg, n_steps, kvh, sel, fox, pt_ref, *refs):
    refs = list(refs)
    qbd_ref = refs[0]
    pages = refs[1:1 + n_pg]
    pos = 1 + n_pg
    bias_ref, bias_new_ref, knew_ref, vnew_ref = refs[pos:pos + 4]
    pos += 4
    if sel:
        selb_ref, e_ref, e_new_ref = refs[pos:pos + 3]
        pos += 3
    if fox:
        cq_ref, ck_ref, ck_new_ref = refs[pos:pos + 3]
        pos += 3
    o_ref, m_ref, l_ref, acc_ref = refs[pos:pos + 4]
    step = pl.program_id(1)
    qbd = qbd_ref[0]
    r, f = qbd.shape
    nt = (((1,), (1,)), ((), ()))

    @pl.when(step == 0)
    def _init():
        m_ref[...] = jnp.full_like(m_ref, NEG_INF)
        l_ref[...] = jnp.zeros_like(l_ref)
        acc_ref[...] = jnp.zeros_like(acc_ref)

    def extra(bias, e, ck):
        add = bias
        if sel:
            add = add + jnp.dot(selb_ref[0], e, preferred_element_type=_F32)
        if fox:
            add = add + cq_ref[0][:, 0:1] - jnp.concatenate([ck] * (r // kvh), axis=0)
        return add

    def absorb(k_list, v_list, add):
        s = jnp.concatenate([jnp.dot(qbd, k, preferred_element_type=_F32) for k in k_list], axis=1) + add
        m_old = m_ref[...]
        m_new = jnp.maximum(m_old, jnp.max(s, axis=1, keepdims=True))
        a = jnp.exp(m_old - m_new)
        p = jnp.exp(s - m_new)
        l_ref[...] = a * l_ref[...] + jnp.sum(p, axis=1, keepdims=True)
        pv = None
        for j, v in enumerate(v_list):
            t = lax.dot_general(p[:, j * PAGE_SIZE:(j + 1) * PAGE_SIZE].astype(_BF), v, nt,
                                preferred_element_type=_F32)
            pv = t if pv is None else pv + t
        acc_ref[...] = a * acc_ref[...] + pv
        m_ref[...] = m_new

    absorb([pg[0, 0].reshape(f, PAGE_SIZE).astype(_BF) for pg in pages],
           [pg[0, 1].reshape(f, PAGE_SIZE).astype(_BF) for pg in pages],
           extra(bias_ref[...], e_ref[...] if sel else None, ck_ref[0] if fox else None))

    @pl.when(step == n_steps - 1)
    def _done():
        absorb([knew_ref[0]], [vnew_ref[0]],
               extra(bias_new_ref[...], e_new_ref[...] if sel else None, ck_new_ref[0] if fox else None))
        o_ref[0] = acc_ref[...] / jnp.maximum(l_ref[...], 1e-30)


def _paged_attn(qbd, pages_t, page_table, bias, knew_t, vnew_t, selb=None, e=None, cq=None, ck=None, ck_new=None):
    bsz, r, f = qbd.shape
    n_pages = page_table.shape[1]
    kvh = pages_t.shape[2]
    length = n_pages * PAGE_SIZE
    n_pg = _pages_per_step(n_pages)
    n_steps = n_pages // n_pg
    sel = selb is not None
    fox = cq is not None
    w = n_pg * PAGE_SIZE

    def page_map(k):
        return lambda b, p, pt: (pt[b * n_pages + p * n_pg + k], 0, 0, 0, 0)

    per_b = lambda shape: pl.BlockSpec((1,) + shape, lambda b, p, pt: (b, 0, 0))
    whole = lambda shape: pl.BlockSpec(shape, lambda b, p, pt: (0, 0))
    in_specs = [per_b((r, f))]
    in_specs += [pl.BlockSpec((1, 2, kvh, HEAD_DIM, PAGE_SIZE), page_map(k)) for k in range(n_pg)]
    in_specs += [pl.BlockSpec((r, w), lambda b, p, pt: (0, p)), whole((r, PAGE_SIZE)),
                 per_b((f, PAGE_SIZE)), per_b((f, PAGE_SIZE))]
    args = [qbd] + [pages_t] * n_pg + [bias[:, :length], bias[:, length:], knew_t, vnew_t]
    if sel:
        nbl = selb.shape[2]
        in_specs += [per_b((r, nbl)), pl.BlockSpec((nbl, w), lambda b, p, pt: (0, p)), whole((nbl, PAGE_SIZE))]
        args += [selb, e[:, :length], e[:, length:]]
    if fox:
        in_specs += [per_b((r, LANES)), pl.BlockSpec((1, kvh, w), lambda b, p, pt: (b, 0, p)), per_b((kvh, PAGE_SIZE))]
        args += [cq, ck, ck_new]
    grid_spec = pltpu.PrefetchScalarGridSpec(
        num_scalar_prefetch=1, grid=(bsz, n_steps), in_specs=in_specs,
        out_specs=per_b((r, f)),
        scratch_shapes=[pltpu.VMEM((r, 1), _F32), pltpu.VMEM((r, 1), _F32), pltpu.VMEM((r, f), _F32)],
    )
    return pl.pallas_call(
        functools.partial(_paged_attn_kernel, n_pg, n_steps, kvh, sel, fox), grid_spec=grid_spec,
        out_shape=jax.ShapeDtypeStruct((bsz, r, f), _F32),
        compiler_params=_cparams("parallel", "arbitrary"), name="paged_attn",
    )(page_table.reshape(-1), *args)


def _moba_pick_kernel(n_pg, n_steps, n_blocks, n_top, pt_ref, qbd_ref, *refs):
    pages = refs[:n_pg]
    selb_ref, km_ref = refs[n_pg], refs[n_pg + 1]
    step = pl.program_id(1)
    f = km_ref.shape[0]

    @pl.when(step == 0)
    def _init():
        km_ref[...] = jnp.zeros_like(km_ref)

    lane = lax.broadcasted_iota(jnp.int32, km_ref.shape, 1)
    km = km_ref[...]
    for k in range(n_pg):
        col = jnp.sum(pages[k][0, 0].reshape(f, PAGE_SIZE), axis=1, keepdims=True)
        blk = (step * n_pg + k) // (MOBA_BLOCK // PAGE_SIZE)
        km = jnp.where(lane == blk, km + col, km)
    km_ref[...] = km

    @pl.when(step == n_steps - 1)
    def _pick():
        means = (km * (1.0 / MOBA_BLOCK)).astype(_BF)
        gate = jnp.dot(qbd_ref[0], means, preferred_element_type=_F32)
        j = lax.broadcasted_iota(jnp.int32, gate.shape, 1)
        score = jnp.where(j < n_blocks, gate, NEG_INF)
        chosen = _topk_axis(score, n_top, j == n_blocks, 1)
        selb_ref[0] = jnp.where(chosen, 0.0, NEG_INF).astype(_BF)


def _moba_pick(qbd, pages_t, page_table):
    bsz, r, f = qbd.shape
    n_pages = page_table.shape[1]
    kvh = pages_t.shape[2]
    n_blocks = n_pages * PAGE_SIZE // MOBA_BLOCK
    assert n_blocks < LANES and (n_pages * PAGE_SIZE) % MOBA_BLOCK == 0
    n_pg = _pages_per_step(n_pages)
    n_steps = n_pages // n_pg

    def page_map(k):
        return lambda b, p, pt: (pt[b * n_pages + p * n_pg + k], 0, 0, 0, 0)

    grid_spec = pltpu.PrefetchScalarGridSpec(
        num_scalar_prefetch=1, grid=(bsz, n_steps),
        in_specs=[pl.BlockSpec((1, r, f), lambda b, p, pt: (b, 0, 0))]
        + [pl.BlockSpec((1, 1, kvh, HEAD_DIM, PAGE_SIZE), page_map(k)) for k in range(n_pg)],
        out_specs=pl.BlockSpec((1, r, LANES), lambda b, p, pt: (b, 0, 0)),
        scratch_shapes=[pltpu.VMEM((f, LANES), _F32)],
    )
    return pl.pallas_call(
        functools.partial(_moba_pick_kernel, n_pg, n_steps, n_blocks, min(MOBA_TOPK, n_blocks)), grid_spec=grid_spec,
        out_shape=jax.ShapeDtypeStruct((bsz, r, LANES), _BF),
        compiler_params=_cparams("parallel", "arbitrary"), name="moba_pick",
    )(page_table.reshape(-1), qbd, *([pages_t] * n_pg))


def _compress_kernel(x_ref, pe_ref, w1a_ref, w1b_ref, b1_ref, w2_ref, b2_ref, o_ref):
    x = x_ref[0, 0]
    pe = pe_ref[0]
    hf = jnp.dot((x + pe[0:1]).astype(_BF), w1a_ref[0], preferred_element_type=_F32)
    hs = jnp.dot((x + pe[1:2]).astype(_BF), w1b_ref[0], preferred_element_type=_F32)
    hs_next = pltpu.roll(hs, hs.shape[0] - 1, 0)
    h = jax.nn.gelu(hf + hs_next + b1_ref[0], approximate=True)
    o_ref[0, 0] = jnp.dot(h.astype(_BF), w2_ref[0], preferred_element_type=_F32) + b2_ref[0]


def _nsa_compress(x, pe, w1, b1, w2, b2):
    sg, bsz, nch, flat = x.shape
    pe2 = pe.reshape(2, CMP_STRIDE, 2, HEAD_DIM).transpose(2, 0, 1, 3).reshape(2, 2, flat)
    w1r = w1.reshape(2, CMP_STRIDE, 2, HEAD_DIM, CMP_HID).transpose(2, 0, 1, 3, 4).reshape(2, 2, flat, CMP_HID)
    w1r = w1r.astype(_BF)
    smap = lambda s, b: (s // NSA_KV_HEADS, 0, 0)
    return pl.pallas_call(
        _compress_kernel,
        grid=(sg, bsz),
        in_specs=[pl.BlockSpec((1, 1, nch, flat), lambda s, b: (s, b, 0, 0)),
                  pl.BlockSpec((1, 2, flat), smap),
                  pl.BlockSpec((1, flat, CMP_HID), smap),
                  pl.BlockSpec((1, flat, CMP_HID), smap),
                  pl.BlockSpec((1, 1, CMP_HID), smap),
                  pl.BlockSpec((1, CMP_HID, HEAD_DIM), smap),
                  pl.BlockSpec((1, 1, HEAD_DIM), smap)],
        out_specs=pl.BlockSpec((1, 1, nch, HEAD_DIM), lambda s, b: (s, b, 0, 0)),
        out_shape=jax.ShapeDtypeStruct((sg, bsz, nch, HEAD_DIM), _F32),
        compiler_params=_cparams("parallel", "parallel"), name="nsa_compress",
    )(x, pe2, w1r[:, 0], w1r[:, 1], b1.reshape(2, 1, CMP_HID), w2.astype(_BF), b2.reshape(2, 1, HEAD_DIM))


def _chunks_from_fm(kv_t):
    bsz, _, length = kv_t.shape
    nch = length // CMP_STRIDE
    sg = 2 * NSA_KV_HEADS
    x = kv_t.reshape(bsz, sg, HEAD_DIM, nch, CMP_STRIDE).transpose(1, 0, 3, 4, 2)
    return x.reshape(sg, bsz, nch, CMP_STRIDE * HEAD_DIM)


def _cmp_select_kernel(theta, tq, qpos0, n_sel, tab_ref, qa_ref, kc_ref, vc_ref, msel_ref, o_ref, selb_ref):
    g = pl.program_id(0) % NSA_KV_HEADS
    i = pl.program_id(1)
    ncp = kc_ref.shape[1]
    nsl = selb_ref.shape[2]
    t = qpos0 + i * tq + lax.broadcasted_iota(jnp.int32, (tq, ncp), 0)
    n = lax.broadcasted_iota(jnp.int32, (tq, ncp), 1)
    d = t - (n * CMP_STRIDE + (CMP_LEN - 1))
    ok = d >= 0
    ind = [d >= th for th in theta]
    kc = kc_ref[0]
    vc = vc_ref[0]
    imp = jnp.zeros((tq, ncp), _F32)
    for r in range(NSA_GROUP):
        base = (g * NSA_GROUP + r) * N_BUCKETS
        bias = jnp.full((tq, ncp), tab_ref[base], _F32)
        for k in range(1, N_BUCKETS):
            bias = bias + jnp.where(ind[k - 1], tab_ref[base + k] - tab_ref[base + k - 1], 0.0)
        s = lax.dot_general(qa_ref[0, r * tq:(r + 1) * tq, :], kc, (((1,), (1,)), ((), ())),
                            preferred_element_type=_F32) + bias
        s = jnp.where(ok, s, NEG_INF)
        m = jnp.max(s, axis=-1, keepdims=True)
        e = jnp.where(ok, jnp.exp(s - m), 0.0)
        p = e / jnp.maximum(jnp.sum(e, axis=-1, keepdims=True), 1e-30)
        o_ref[0, r * tq:(r + 1) * tq, :] = jnp.dot(p.astype(_BF), vc, preferred_element_type=_F32)
        imp = imp + p
    p_s = _dot3(imp, msel_ref[...])
    j = lax.broadcasted_iota(jnp.int32, (tq, nsl), 1)
    qb = (qpos0 + i * tq + lax.broadcasted_iota(jnp.int32, (tq, nsl), 0)) >> int(math.log2(SEL_BLOCK))
    valid = j <= qb
    forced = (j == 0) | (j == qb) | (j == qb - 1)
    score = jnp.where(valid, jnp.where(forced, FORCE_SCORE, p_s), NEG_INF)
    sel = jnp.zeros((tq, nsl), jnp.bool_)
    big = jnp.int32(1 << 20)
    for _ in range(n_sel):
        mx = jnp.max(score, axis=-1, keepdims=True)
        firsti = jnp.min(jnp.where(score == mx, j, big), axis=-1, keepdims=True)
        hit = j == firsti
        sel = sel | (hit & (mx > 0.5 * NEG_INF))
        score = jnp.where(hit, -3e38, score)
    selb_ref[0] = jnp.where(sel, 0.0, NEG_INF).astype(_BF)


def _cmp_select(qa, kc, vc, tab, tq, qpos0, n_blocks):
    bg, rows, _ = qa.shape
    ncp = kc.shape[1]
    nq = rows // (NSA_GROUP * tq)
    nsl = _round_up(n_blocks, LANES)
    nn = np.arange(ncp)[:, None]
    jj = np.arange(nsl)[None, :]
    msel = ((nn >= CMP_PER_SEL * jj - 1) & (nn <= CMP_PER_SEL * jj + CMP_PER_SEL - 1) & (jj < n_blocks))
    msel = jnp.asarray(msel.astype(np.float32), _BF)
    tabf = tab[:, :NSA_HEADS].T.reshape(-1)
    kern = functools.partial(_cmp_select_kernel, _bucket_thresholds(), tq, qpos0, min(NSA_TOPN, n_blocks))
    return pl.pallas_call(
        kern,
        grid=(bg, nq),
        in_specs=[pl.BlockSpec(memory_space=pltpu.SMEM),
                  pl.BlockSpec((1, NSA_GROUP * tq, HEAD_DIM), lambda b, i: (b, i, 0)),
                  pl.BlockSpec((1, ncp, HEAD_DIM), lambda b, i: (b, 0, 0)),
                  pl.BlockSpec((1, ncp, HEAD_DIM), lambda b, i: (b, 0, 0)),
                  pl.BlockSpec((ncp, nsl), lambda b, i: (0, 0))],
        out_specs=[pl.BlockSpec((1, NSA_GROUP * tq, HEAD_DIM), lambda b, i: (b, i, 0)),
                   pl.BlockSpec((1, tq, nsl), lambda b, i: (b, i, 0))],
        out_shape=[jax.ShapeDtypeStruct((bg, rows, HEAD_DIM), _F32),
                   jax.ShapeDtypeStruct((bg, nq * tq, nsl), _BF)],
        compiler_params=_cparams("parallel", "parallel"), name="nsa_cmp_select",
    )(tabf, qa, kc, vc, msel)


def _logf_cumsum_kernel(n_new, x_ref, u_ref, lf_ref, c_ref, hi_ref, mid_ref, lo_ref):
    length = x_ref.shape[1]
    x = x_ref[...]
    col = lax.broadcasted_iota(jnp.int32, x.shape, 1)
    ls = jnp.minimum(x, 0.0) - jnp.log1p(jnp.exp(-jnp.abs(x)))
    lf = jnp.where(col >= length - n_new, ls, x)
    lf_ref[...] = lf
    u = u_ref[...]
    carry = jnp.zeros((x.shape[0], 1), _F32)
    for k in range(length // LANES):
        blk = _dot3(lf[:, k * LANES:(k + 1) * LANES], u) + carry
        c_ref[:, k * LANES:(k + 1) * LANES] = blk
        hi, mid, lo = _split3(blk)
        hi_ref[:, k * LANES:(k + 1) * LANES] = hi
        mid_ref[:, k * LANES:(k + 1) * LANES] = mid
        lo_ref[:, k * LANES:(k + 1) * LANES] = lo
        carry = blk[:, LANES - 1:LANES]


def _logf_cumsum(x, n_new, rows_per_step):
    rows, length = x.shape
    u = jnp.asarray(np.triu(np.ones((LANES, LANES), np.float32)), _BF)
    spec = pl.BlockSpec((rows_per_step, length), lambda i: (i, 0))
    lf, c, hi, mid, lo = pl.pallas_call(
        functools.partial(_logf_cumsum_kernel, n_new),
        grid=(rows // rows_per_step,),
        in_specs=[spec, pl.BlockSpec((LANES, LANES), lambda i: (0, 0))],
        out_specs=[spec] * 5,
        out_shape=[jax.ShapeDtypeStruct((rows, length), _F32)] * 2 + [jax.ShapeDtypeStruct((rows, length), _BF)] * 3,
        compiler_params=_cparams("parallel"), name="fox_logf_cumsum",
    )(x, u)
    return lf, c, (hi, mid, lo)


def _fox_aug_rows(c3):
    one = jnp.ones_like(c3[0])
    zero = jnp.zeros_like(c3[0])
    pad = [zero] * (BF16_ROWS - 6)
    qaug = jnp.stack(list(c3) + [one, one, one] + pad, axis=1)
    kaug = jnp.stack([one, one, one] + [-c for c in c3] + pad, axis=1)
    return qaug, kaug


def _gather_kernel(n_pg, pt_ref, *refs):
    ins, out = refs[:n_pg], refs[n_pg]
    for k in range(n_pg):
        out[0, :, k * PAGE_SIZE:(k + 1) * PAGE_SIZE] = ins[k][0]


def _gather_fm(pool_t, page_table):
    bsz, n_pages = page_table.shape
    f = pool_t.shape[1]
    n_pg = _pages_per_step(n_pages)

    def in_map(k):
        return lambda b, p, pt: (pt[b * n_pages + p * n_pg + k], 0, 0)

    grid_spec = pltpu.PrefetchScalarGridSpec(
        num_scalar_prefetch=1,
        grid=(bsz, n_pages // n_pg),
        in_specs=[pl.BlockSpec((1, f, PAGE_SIZE), in_map(k)) for k in range(n_pg)],
        out_specs=pl.BlockSpec((1, f, n_pg * PAGE_SIZE), lambda b, p, pt: (b, 0, p)),
    )
    return pl.pallas_call(
        functools.partial(_gather_kernel, n_pg), grid_spec=grid_spec,
        out_shape=jax.ShapeDtypeStruct((bsz, f, n_pages * PAGE_SIZE), pool_t.dtype),
        compiler_params=_cparams("parallel", "arbitrary"), name="page_gather",
    )(page_table.reshape(-1), *([pool_t] * n_pg))


def _pages_fm(cache_l):
    return cache_l.transpose(0, 2, 3, 4, 1)


def _kv_group_onehot(heads, group):
    return (np.arange(heads)[:, None] // group == np.arange(heads // group)[None, :]).astype(np.float32)


def _block_diag_queries(q, group):
    b, t, h, dh = q.shape
    oh = jnp.asarray(_kv_group_onehot(h, group))
    x = q[:, :, :, None, :] * oh[None, None, :, :, None]
    return x.reshape(b, t * h, (h // group) * dh).astype(_BF)


def _own_head_columns(o, t, heads, group):
    b = o.shape[0]
    oh = jnp.asarray(_kv_group_onehot(heads, group))
    x = o.reshape(b, t, heads, heads // group, HEAD_DIM) * oh[None, None, :, :, None]
    return jnp.sum(x, axis=3).reshape(b, t, heads * HEAD_DIM)


def _sample_bias(tabh, heads, d0, t, cols, window=None):
    tile = _bias_tile(tabh, d0, t, cols, window)
    tile = jnp.broadcast_to(tile, (heads, t, cols))
    return tile.transpose(1, 0, 2).reshape(t * heads, cols)


def _new_rows_fm(x):
    return _pad_last(x.transpose(0, 2, 1), PAGE_SIZE).astype(_BF)


def _nsa_q_rows(q, tq):
    b, t, _, dh = q.shape
    nq = t // tq
    x = q.reshape(b, nq, tq, NSA_KV_HEADS, NSA_GROUP, dh).transpose(0, 3, 1, 4, 2, 5)
    return x.reshape(b * NSA_KV_HEADS, nq * NSA_GROUP * tq, dh)


def _nsa_rows_back(o, b, t, tq):
    nq = t // tq
    x = o.reshape(b, NSA_KV_HEADS, nq, NSA_GROUP, tq, HEAD_DIM).transpose(0, 2, 4, 1, 3, 5)
    return x.reshape(b, t, NSA_HEADS * HEAD_DIM)


def _even_sample(z, li, cache_cmp, cache_sel, win_state, cache_fkv, cache_flogf, page_table, tab, cmp_w, tq):
    bs, n_pages = page_table.shape
    past = n_pages * PAGE_SIZE
    ts = z.shape[0] // bs
    q_a, kv_c, kv_s, kv_w, g_bm, q_f, kv_f, f_logit = _even_split(z, bs, ts)
    g = NSA_KV_HEADS
    tabn = tab[:, :NSA_HEADS]
    wide = g * HEAD_DIM
    total = past + PAGE_SIZE
    assert past % CMP_STRIDE == 0 and past % SEL_BLOCK == 0 and ts <= CMP_STRIDE

    pool_c = _pages_fm(cache_cmp[li]).reshape(-1, 2 * wide, PAGE_SIZE)
    cmp_tok = _nsa_compress(_chunks_from_fm(_gather_fm(pool_c, page_table)), *cmp_w)
    ncp = _round_up(cmp_tok.shape[2], LANES)
    cmp_tok = _pad_axis(cmp_tok, 2, ncp).astype(_BF)
    kc = cmp_tok[:g].transpose(1, 0, 2, 3).reshape(bs * g, ncp, HEAD_DIM)
    vc = cmp_tok[g:].transpose(1, 0, 2, 3).reshape(bs * g, ncp, HEAD_DIM)
    q_rows = _nsa_q_rows(_pad_axis(q_a * SCALE, 1, tq), tq).astype(_BF)
    o_c, selb = _cmp_select(q_rows, kc, vc, tab, tq, past, total // SEL_BLOCK)
    o_c = _nsa_rows_back(o_c, bs, tq, tq)[:, :ts]
    nsl = selb.shape[2]

    qbd = _block_diag_queries(q_a * SCALE, NSA_GROUP)
    sel_rows = jnp.repeat(selb.reshape(bs, g, tq, nsl)[:, :, :ts].transpose(0, 2, 1, 3), NSA_GROUP, axis=2)
    sel_rows = sel_rows.reshape(bs, ts * NSA_HEADS, nsl)
    e_sel = jnp.asarray((np.arange(nsl)[:, None] == np.arange(total)[None, :] // SEL_BLOCK).astype(np.float32), _BF)
    flat = lambda kv, s: kv[:, :, s].reshape(bs, ts, -1)
    o_s = _paged_attn(qbd, _pages_fm(cache_sel[li]), page_table, _sample_bias(tabn, NSA_HEADS, past, ts, total),
                      _new_rows_fm(flat(kv_s, 0)), _new_rows_fm(flat(kv_s, 1)), selb=sel_rows, e=e_sel)
    win_buf = win_state[li]
    wb = win_buf.shape[1]
    assert wb % PAGE_SIZE == 0
    wpages = _pages_fm(win_buf).reshape(bs, 2, g, HEAD_DIM, wb // PAGE_SIZE, PAGE_SIZE)
    wpages = wpages.transpose(0, 4, 1, 2, 3, 5).reshape(bs * (wb // PAGE_SIZE), 2, g, HEAD_DIM, PAGE_SIZE)
    wtable = jnp.arange(bs * (wb // PAGE_SIZE), dtype=jnp.int32).reshape(bs, wb // PAGE_SIZE)
    o_w = _paged_attn(qbd, wpages, wtable, _sample_bias(tabn, NSA_HEADS, wb, ts, wb + PAGE_SIZE, NSA_WINDOW),
                      _new_rows_fm(flat(kv_w, 0)), _new_rows_fm(flat(kv_w, 1)))
    o_s = _own_head_columns(o_s, ts, NSA_HEADS, NSA_GROUP)
    o_w = _own_head_columns(o_w, ts, NSA_HEADS, NSA_GROUP)

    past_l = _gather_fm(cache_flogf[li].transpose(0, 2, 1), page_table)
    lf_len = _round_up(past + ts, LANES)
    front = lf_len - past - ts
    fl_all = jnp.concatenate([jnp.zeros((bs, FOX_HEADS, front), _F32), past_l, f_logit.transpose(0, 2, 1)], axis=-1)
    logf_t, c, _ = _logf_cumsum(fl_all.reshape(bs * FOX_HEADS, lf_len), ts, min(bs * FOX_HEADS, 64))
    c = c.reshape(bs, FOX_HEADS, lf_len)
    logf = logf_t[:, lf_len - ts:].reshape(bs, FOX_HEADS, ts).transpose(0, 2, 1)
    c_new = c[:, :, front + past:]
    cq = jnp.broadcast_to(c_new.transpose(0, 2, 1).reshape(bs, ts * FOX_HEADS, 1), (bs, ts * FOX_HEADS, LANES))
    o_f = _paged_attn(_block_diag_queries(q_f * SCALE, 1), _pages_fm(cache_fkv[li]), page_table,
                      _sample_bias(None, FOX_HEADS, past, ts, total),
                      _new_rows_fm(flat(kv_f, 0)), _new_rows_fm(flat(kv_f, 1)),
                      cq=cq, ck=c[:, :, front:front + past], ck_new=_pad_last(c_new, PAGE_SIZE))
    o_f = _own_head_columns(o_f, ts, FOX_HEADS, 1)
    kvw_all = jnp.concatenate([win_buf, kv_w], axis=1)
    n_tok = bs * ts
    outs = (o_c.reshape(n_tok, -1), o_s.reshape(n_tok, -1), o_w.reshape(n_tok, -1), o_f.reshape(n_tok, -1),
            g_bm.reshape(n_tok, -1))
    return outs, (kv_c, kv_s, kvw_all[:, ts:], kv_f, logf)


def _odd_sample(z, li, cache_kv, page_table, tab):
    bs, n_pages = page_table.shape
    past = n_pages * PAGE_SIZE
    ts = z.shape[0] // bs
    hw = MOBA_HEADS * HEAD_DIM
    z = z.reshape(bs, ts, 3 * hw)
    q = z[..., :hw].reshape(bs, ts, MOBA_HEADS, HEAD_DIM)
    total = past + PAGE_SIZE
    assert ts <= MOBA_BLOCK and past % MOBA_BLOCK == 0
    pages = _pages_fm(cache_kv[li])
    qbd = _block_diag_queries(q * SCALE, 1)
    selb = _moba_pick(qbd, pages, page_table)
    e_blk = jnp.asarray((np.arange(LANES)[:, None] == np.arange(total)[None, :] // MOBA_BLOCK).astype(np.float32), _BF)
    o = _paged_attn(qbd, pages, page_table, _sample_bias(tab[:, :MOBA_HEADS], MOBA_HEADS, past, ts, total),
                    _new_rows_fm(z[..., hw:2 * hw]), _new_rows_fm(z[..., 2 * hw:]), selb=selb, e=e_blk)
    o = _own_head_columns(o, ts, MOBA_HEADS, 1)
    return o.reshape(bs * ts, hw), z[..., hw:].reshape(bs, ts, 2, MOBA_HEADS, HEAD_DIM)


def _even_split(z, b, t):
    q_a, kv_c, kv_s, kv_w, g_a, q_f, kv_f, f_logit = jnp.split(z.reshape(b, t, -1), _EVEN_CUTS, axis=-1)
    kvshape = (b, t, 2, NSA_KV_HEADS, HEAD_DIM)
    g_bm = g_a.reshape(b, t, NSA_HEADS, 3).transpose(0, 1, 3, 2).reshape(b, t, 3 * NSA_HEADS)
    return (q_a.reshape(b, t, NSA_HEADS, HEAD_DIM), kv_c.reshape(kvshape), kv_s.reshape(kvshape),
            kv_w.reshape(kvshape), g_bm, q_f.reshape(b, t, FOX_HEADS, HEAD_DIM),
            kv_f.reshape(b, t, 2, FOX_HEADS, HEAD_DIM), f_logit)


def _even_row_perm():
    cuts = (0,) + _EVEN_CUTS + (sum(_EVEN_SIZES),)
    seg = lambda k: np.arange(cuts[k], cuts[k + 1])
    gates = cuts[4] + (np.arange(NSA_HEADS)[None, :] * 3 + np.arange(3)[:, None]).reshape(-1)
    return np.concatenate([seg(0), seg(1), seg(2), seg(3), seg(5), seg(6), gates, seg(7)])


def _kv_leaf(zf, row0, heads):
    bsz, _, s = zf.shape
    blk = zf[:, row0:row0 + 2 * heads * HEAD_DIM, :].reshape(bsz, 2, heads, HEAD_DIM, s)
    return blk.transpose(0, 4, 1, 2, 3)


def _even_prompt(x_rows, tab, w_in, b_in, cmp_w, w_out, ln_g, ln_b, alpha, tm):
    bsz, s, d = x_rows.shape
    perm = _even_row_perm()
    w_t = w_in.T[perm].astype(_BF)
    scale = np.ones((_EVEN_OUT, 1), np.float32)
    scale[_QA:_QA + NSA_HEADS * HEAD_DIM] = SCALE
    scale[_QF:_QF + FOX_HEADS * HEAD_DIM] = SCALE
    zf, zb = _proj_fm(x_rows, w_t, b_in[perm].reshape(-1, 1), jnp.asarray(scale), tm, True)
    tabn = tab[:, :NSA_HEADS]
    g = NSA_KV_HEADS
    hb = HEAD_DIM

    assert s % SEL_BLOCK == 0
    cmp_tok = _nsa_compress(_chunks_from_fm(zf[:, _KVC:_KVS, :]), *cmp_w)
    ncp = _round_up(cmp_tok.shape[2], LANES)
    cmp_tok = _pad_axis(cmp_tok, 2, ncp).astype(_BF)
    kc = cmp_tok[:g].transpose(1, 0, 2, 3).reshape(bsz * g, ncp, HEAD_DIM)
    vc_t = cmp_tok[g:].transpose(1, 0, 3, 2).reshape(bsz * g, HEAD_DIM, ncp)
    tq, tk = 256, 512
    o_c, sel = _cmp_select_t(zb, kc, vc_t, tab, tq, s // SEL_BLOCK)

    pairs, deltas = _plan_tiles(s // tq, tq, tk, None, True)
    bias = _bias_tiles_t(tabn, deltas, tk, tq, None, NSA_GROUP)
    o_s = _flash_t(zb, _QA // (NSA_GROUP * hb), _KVS // hb, _KVS // hb + g, g, NSA_GROUP, bias, pairs, tq, tk,
                   sel=sel, sel_block=SEL_BLOCK)
    pairs, deltas = _plan_tiles(s // tq, tq, tk, NSA_WINDOW, True)
    bias = _bias_tiles_t(tabn, deltas, tk, tq, NSA_WINDOW, NSA_GROUP)
    o_w = _flash_t(zb, _QA // (NSA_GROUP * hb), _KVW // hb, _KVW // hb + g, g, NSA_GROUP, bias, pairs, tq, tk)

    logf_t, _, c3 = _logf_cumsum(zf[:, _FL:_FL + FOX_HEADS, :].reshape(bsz * FOX_HEADS, s), s, bsz * FOX_HEADS)
    qaug, kaug = _fox_aug_rows(c3)
    tqf = tkf = 512
    pairs, deltas = _plan_tiles(s // tqf, tqf, tkf, None, False)
    bias = _bias_tiles_t(None, deltas, tkf, tqf, None, 1)
    o_f = _flash_t(zb, _QF // hb, _KVF // hb, _KVF // hb + FOX_HEADS, FOX_HEADS, 1, bias, pairs, tqf, tkf,
                   qaug=qaug, kaug=kaug)

    wide = lambda o: o.reshape(bsz, -1, s)
    x_t = _even_out_fm(alpha, wide(o_c), wide(o_s), wide(o_w), zf, wide(o_f), w_out.T.astype(_BF), x_rows,
                       ln_g, ln_b, tm)
    kv_c = _kv_leaf(zf, _KVC, g)
    kv_s = _kv_leaf(zf, _KVS, g)
    kv_w = _kv_leaf(zf, _KVW, g)
    kv_f = _kv_leaf(zf, _KVF, FOX_HEADS)
    logf = logf_t.reshape(bsz, FOX_HEADS, s).transpose(0, 2, 1)
    return x_t, (kv_c, kv_s, kv_w[:, max(s - NSA_WINDOW, 0):], kv_f, logf)


def _odd_prompt(x_t, tab, w_in, w_out, ln_g, ln_b, alpha, tm):
    bsz, d, s = x_t.shape
    hw = MOBA_HEADS * HEAD_DIM
    scale = np.ones((3 * hw, 1), np.float32)
    scale[:hw] = SCALE
    zf, zb = _proj_fm(x_t, w_in.T.astype(_BF), jnp.zeros((3 * hw, 1), _F32), jnp.asarray(scale), tm, False)
    assert s % MOBA_BLOCK == 0
    sel = _moba_select_t(zf, zb, MOBA_HEADS, 1024 if s % 1024 == 0 else 256)
    tq = tk = 512
    pairs, deltas = _plan_tiles(s // tq, tq, tk, None, True)
    bias = _bias_tiles_t(tab[:, :MOBA_HEADS], deltas, tk, tq, None, 1)
    o = _flash_t(zb, 0, MOBA_HEADS, 2 * MOBA_HEADS, MOBA_HEADS, 1, bias, pairs, tq, tk, sel=sel, sel_block=MOBA_BLOCK)
    x_t = _odd_out_fm(alpha, o.reshape(bsz, hw, s), w_out.T.astype(_BF), x_t, ln_g, ln_b, tm)
    return x_t, _kv_leaf(zf, hw, MOBA_HEADS)


def _kernel_impl(x_prompt, x_sample, cache_nsa_cmp, cache_nsa_sel, state_nsa_win, cache_fox_kv,
                 cache_fox_logf, cache_moba_kv, page_table, rel_bias, ln_g, ln_b, w_in_even, b_in_even,
                 nsa_cmp_pe, nsa_cmp_w1, nsa_cmp_b1, nsa_cmp_w2, nsa_cmp_b2, w_out_even, w_in_odd,
                 w_out_odd, moe_wg, moe_bg, moe_we, moe_be, moe_w1, moe_w3, moe_w2):
    bp, sp, d = x_prompt.shape
    bs, ts, _ = x_sample.shape
    n_pages = page_table.shape[1]
    past = n_pages * PAGE_SIZE
    depth = ln_g.shape[0]
    alpha = (2 * depth) ** 0.25
    ns_tok = bs * ts
    tm_p = 512
    tm_s = ns_tok
    assert sp % 1024 == 0 and depth % 2 == 0
    xp = x_prompt
    xs = x_sample.reshape(ns_tok, d)
    outs = {k: [] for k in ("cmp_p", "cmp_s", "sel_p", "sel_s", "win_p", "win_s", "fkv_p", "fkv_s",
                            "flf_p", "flf_s", "mkv_p", "mkv_s")}
    tq_s = BF16_ROWS

    for layer in range(depth):
        li = layer // 2
        if layer % 2 == 0:
            assert layer == 0
            cmp_w = (nsa_cmp_pe[li], nsa_cmp_w1[li], nsa_cmp_b1[li], nsa_cmp_w2[li], nsa_cmp_b2[li])
            xp, (kv_c, kv_s, kv_w, kv_f, logf) = _even_prompt(
                xp, rel_bias, w_in_even[li], b_in_even[li], cmp_w, w_out_even[li], ln_g[layer, 0], ln_b[layer, 0],
                alpha, tm_p)
            outs["cmp_p"].append(kv_c)
            outs["sel_p"].append(kv_s)
            outs["win_p"].append(kv_w)
            outs["fkv_p"].append(kv_f)
            outs["flf_p"].append(logf)
            w_in = w_in_even[li].astype(_BF)
            w_out = w_out_even[li].astype(_BF)
            z = _linear(xs, w_in, b_in_even[li], tm_s)
            (o_c, o_s, o_w, o_f, g_bm), (kv_c, kv_s, kv_w, kv_f, logf) = _even_sample(
                z, li, cache_nsa_cmp, cache_nsa_sel, state_nsa_win, cache_fox_kv, cache_fox_logf, page_table,
                rel_bias, cmp_w, tq_s)
            xs = _even_out(alpha, o_c, o_s, o_w, g_bm, o_f, w_out, xs, ln_g[layer, 0], ln_b[layer, 0], tm_s)
            outs["cmp_s"].append(kv_c)
            outs["sel_s"].append(kv_s)
            outs["win_s"].append(kv_w)
            outs["fkv_s"].append(kv_f)
            outs["flf_s"].append(logf)
        else:
            hw = MOBA_HEADS * HEAD_DIM
            xp, kv = _odd_prompt(xp, rel_bias, w_in_odd[li], w_out_odd[li], ln_g[layer, 0], ln_b[layer, 0], alpha, tm_p)
            outs["mkv_p"].append(kv)
            w_in = w_in_odd[li].astype(_BF)
            w_out = w_out_odd[li].astype(_BF)
            z = _linear(xs, w_in, jnp.zeros((3 * hw,), _F32), tm_s)
            o, kv = _odd_sample(z, li, cache_moba_kv, page_table, rel_bias)
            xs = _odd_out(alpha, o, w_out, xs, ln_g[layer, 0], ln_b[layer, 0], tm_s)
            outs["mkv_s"].append(kv)
        w1b, w3b, w2b = moe_w1[layer].astype(_BF), moe_w3[layer].astype(_BF), moe_w2[layer].astype(_BF)
        w13_t = jnp.concatenate([w1b.transpose(0, 2, 1), w3b.transpose(0, 2, 1)], axis=1)
        router = (moe_wg[layer], moe_bg[layer], moe_we[layer], moe_be[layer])
        xp = _moe_ln_fm(alpha, xp, *router, w13_t, w2b.transpose(0, 2, 1), ln_g[layer, 1], ln_b[layer, 1], tm_p,
                        layer == depth - 1)
        xs = _moe_ln(alpha, xs, *router, w1b, w3b, w2b, ln_g[layer, 1], ln_b[layer, 1], tm_s)

    st = lambda k: jnp.stack(outs[k])
    return (xp, xs.reshape(bs, ts, d), st("cmp_p"), st("cmp_s"), st("sel_p"), st("sel_s"),
            st("win_p"), st("win_s"), st("fkv_p"), st("fkv_s"), st("flf_p"), st("flf_s"), st("mkv_p"), st("mkv_s"))


def kernel(x_prompt, x_sample, cache_nsa_cmp, cache_nsa_sel, state_nsa_win, cache_fox_kv, cache_fox_logf, cache_moba_kv, page_table, rel_bias, ln_g, ln_b, w_in_even, b_in_even, nsa_cmp_pe, nsa_cmp_w1, nsa_cmp_b1, nsa_cmp_w2, nsa_cmp_b2, w_out_even, w_in_odd, w_out_odd, moe_wg, moe_bg, moe_we, moe_be, moe_w1, moe_w3, moe_w2):
    return _kernel_impl(x_prompt, x_sample, cache_nsa_cmp, cache_nsa_sel, state_nsa_win, cache_fox_kv,
                        cache_fox_logf, cache_moba_kv, page_table, rel_bias, ln_g, ln_b, w_in_even, b_in_even,
                        nsa_cmp_pe, nsa_cmp_w1, nsa_cmp_b1, nsa_cmp_w2, nsa_cmp_b2, w_out_even, w_in_odd,
                        w_out_odd, moe_wg, moe_bg, moe_we, moe_be, moe_w1, moe_w3, moe_w2)
```

```python
import functools
import math

import numpy as np
import jax
import jax.numpy as jnp
from jax import lax
from jax.experimental import pallas as pl
from jax.experimental.pallas import tpu as pltpu

_BF = jnp.bfloat16
_F32 = jnp.float32

HEAD_DIM = 64
NSA_KV_HEADS = 2
NSA_GROUP = 4
NSA_HEADS = NSA_KV_HEADS * NSA_GROUP
FOX_HEADS = 8
MOBA_HEADS = 16
CMP_LEN = 32
CMP_STRIDE = 16
CMP_HID = 128
SEL_BLOCK = 64
CMP_PER_SEL = SEL_BLOCK // CMP_STRIDE
NSA_TOPN = 16
NSA_WINDOW = 512
MOBA_BLOCK = 256
MOBA_TOPK = 3
N_BUCKETS = 32
T5_MAX_DISTANCE = 128
N_GROUPS = 4
EXPERTS_PER_GROUP = 4
N_EXPERTS = N_GROUPS * EXPERTS_PER_GROUP
PAGE_SIZE = 128
SCALE = HEAD_DIM ** -0.5
NEG_INF = -1e30
FORCE_SCORE = 1e4
LN_EPS = 1e-5
LANES = 128
SUBLANES = 8
BF16_ROWS = 16
VMEM_LIMIT = 48 * 1024 * 1024
PAGES_PER_STEP = 8
HEADS_PER_STEP = 4

_QA, _KVC, _KVS, _KVW, _QF, _KVF, _GA, _FL = 0, 512, 768, 1024, 1280, 1792, 2816, 2840
_EVEN_OUT = 2848
_EVEN_SIZES = (NSA_HEADS * HEAD_DIM, 2 * NSA_KV_HEADS * HEAD_DIM, 2 * NSA_KV_HEADS * HEAD_DIM,
               2 * NSA_KV_HEADS * HEAD_DIM, 3 * NSA_HEADS, FOX_HEADS * HEAD_DIM,
               2 * FOX_HEADS * HEAD_DIM, FOX_HEADS)
_EVEN_CUTS = tuple(int(c) for c in np.cumsum(_EVEN_SIZES)[:-1])


def _cparams(*sem):
    return pltpu.CompilerParams(dimension_semantics=sem, vmem_limit_bytes=VMEM_LIMIT)


def _round_up(n, m):
    return (n + m - 1) // m * m


def _split3(x):
    hi = x.astype(_BF)
    r1 = x - hi.astype(_F32)
    mid = r1.astype(_BF)
    lo = (r1 - mid.astype(_F32)).astype(_BF)
    return hi, mid, lo


def _dot3(x, m01):
    hi, mid, lo = _split3(x)
    acc = jnp.dot(hi, m01, preferred_element_type=_F32)
    acc += jnp.dot(mid, m01, preferred_element_type=_F32)
    acc += jnp.dot(lo, m01, preferred_element_type=_F32)
    return acc


def _t5_bucket_np(dist):
    n = np.maximum(dist, 0)
    exact = N_BUCKETS // 2
    nf = np.maximum(n, exact).astype(np.float32)
    far = exact + (np.log(nf / np.float32(exact)) / np.float32(math.log(T5_MAX_DISTANCE / exact))
                   * np.float32(N_BUCKETS - exact)).astype(np.int32)
    return np.where(n < exact, n, np.minimum(far, N_BUCKETS - 1)).astype(np.int32)


def _bucket_thresholds():
    d = np.arange(0, 4 * T5_MAX_DISTANCE)
    b = _t5_bucket_np(d)
    return [int(d[b >= k][0]) for k in range(1, N_BUCKETS)]


_FAR_DISTANCE = _bucket_thresholds()[-1]


def _pad_last(x, width):
    return jnp.pad(x, [(0, 0)] * (x.ndim - 1) + [(0, width - x.shape[-1])])


def _pad_axis(x, axis, size):
    pads = [(0, 0)] * x.ndim
    pads[axis] = (0, size - x.shape[axis])
    return jnp.pad(x, pads)


def _toeplitz(g, rows, cols):
    n = g.shape[-1]
    lead = g.shape[:-1]
    x = jnp.broadcast_to(g[..., None, :], lead + (rows, n)).reshape(lead + (rows * n,))
    return x[..., :rows * (n - 1)].reshape(lead + (rows, n - 1))[..., :cols]


def _distance_values(tabh, d, window):
    valid = (d >= 0) if window is None else ((d >= 0) & (d < window))
    if tabh is None:
        vals = jnp.zeros((1, d.shape[0]), _F32)
    else:
        vals = tabh[jnp.asarray(_t5_bucket_np(d))].T
    return jnp.where(jnp.asarray(valid)[None], vals, NEG_INF).astype(_F32)


def _bias_tile(tabh, d0, rows, cols, window=None):
    n = rows + cols
    m = np.arange(n)
    d = np.where(m < cols, d0 - m, d0 + n - m)
    return _toeplitz(_distance_values(tabh, d, window), rows, cols)


def _bias_tile_t(tabh, delta, tk, tq, window=None):
    n = tk + tq
    m = np.arange(n)
    d = np.where(m < tq, delta + m, delta + m - n)
    return _toeplitz(_distance_values(tabh, d, window), tk, tq)


def _plan_tiles(nq, tq, tk, window, has_table):
    deltas, pairs = [], []
    for qi in range(nq):
        q0 = qi * tq
        k_hi = (q0 + tq - 1) // tk
        k_lo = 0 if window is None else max(0, (q0 - (window - 1)) // tk)
        for ki in range(k_lo, k_hi + 1):
            delta = q0 - ki * tk
            dmin, dmax = delta - (tk - 1), delta + tq - 1
            plain = dmin >= (_FAR_DISTANCE if has_table else 0) and (window is None or dmax < window)
            if plain:
                pairs.append((qi, ki, -1))
            else:
                if delta not in deltas:
                    deltas.append(delta)
                pairs.append((qi, ki, deltas.index(delta)))
    pairs = [(q, k, b if b >= 0 else len(deltas)) for q, k, b in pairs]
    return pairs, deltas


def _bias_tiles_t(tabh, deltas, tk, tq, window, group):
    far = tk + tq + _FAR_DISTANCE
    tiles = [_bias_tile_t(tabh, dl, tk, tq, window) for dl in deltas]
    tiles.append(_bias_tile_t(tabh, far, tk, tq, None))
    t = jnp.stack(tiles, axis=1)
    if tabh is None:
        t = jnp.broadcast_to(t, (group,) + t.shape[1:])
    h, nb = t.shape[:2]
    t = t.reshape(h // group, group, nb, tk, tq).transpose(0, 2, 3, 1, 4)
    return t.reshape(h // group, nb, tk, group * tq)


def _layer_norm_cols(y, g, b):
    mu = jnp.mean(y, axis=0, keepdims=True)
    yc = y - mu
    var = jnp.mean(yc * yc, axis=0, keepdims=True)
    return yc * lax.rsqrt(var + LN_EPS) * g + b


def _proj_fm_kernel(row_major_in, x_ref, w_ref, b_ref, sc_ref, zf_ref, zb_ref):
    x = x_ref[0].astype(_BF)
    if row_major_in:
        z = lax.dot_general(w_ref[...], x, (((1,), (1,)), ((), ())), preferred_element_type=_F32)
    else:
        z = jnp.dot(w_ref[...], x, preferred_element_type=_F32)
    z = z + b_ref[...]
    zf_ref[0] = z
    zb_ref[0] = (z * sc_ref[...]).astype(_BF)


def _proj_fm(x, w_t_bf, b_col, scale_col, tm, row_major_in):
    bsz = x.shape[0]
    s = x.shape[1] if row_major_in else x.shape[2]
    d = x.shape[2] if row_major_in else x.shape[1]
    n = w_t_bf.shape[0]
    x_spec = (pl.BlockSpec((1, tm, d), lambda b, i: (b, i, 0)) if row_major_in
              else pl.BlockSpec((1, d, tm), lambda b, i: (b, 0, i)))
    col = pl.BlockSpec((n, 1), lambda b, i: (0, 0))
    out = pl.BlockSpec((1, n, tm), lambda b, i: (b, 0, i))
    return pl.pallas_call(
        functools.partial(_proj_fm_kernel, row_major_in),
        grid=(bsz, s // tm),
        in_specs=[x_spec, pl.BlockSpec((n, d), lambda b, i: (0, 0)), col, col],
        out_specs=[out, out],
        out_shape=[jax.ShapeDtypeStruct((bsz, n, s), _F32), jax.ShapeDtypeStruct((bsz, n, s), _BF)],
        compiler_params=_cparams("parallel", "parallel"), name="proj_fm",
    )(x, w_t_bf, b_col, scale_col)


def _even_out_fm_kernel(alpha, oc_ref, os_ref, ow_ref, gz_ref, of_ref, wo_ref, x_ref, g_ref, b_ref, y_ref):
    wa = NSA_HEADS * HEAD_DIM
    sg = jax.nn.sigmoid(gz_ref[0])
    parts = []
    for h in range(NSA_HEADS):
        sl = slice(h * HEAD_DIM, (h + 1) * HEAD_DIM)
        parts.append(sg[h:h + 1] * oc_ref[0, sl, :] + sg[NSA_HEADS + h:NSA_HEADS + h + 1] * os_ref[0, sl, :]
                     + sg[2 * NSA_HEADS + h:2 * NSA_HEADS + h + 1] * ow_ref[0, sl, :])
    o_a = jnp.concatenate(parts, axis=0)
    m = jnp.dot(wo_ref[:, 0:wa], o_a.astype(_BF), preferred_element_type=_F32)
    m += jnp.dot(wo_ref[:, wa:], of_ref[0].astype(_BF), preferred_element_type=_F32)
    y_ref[0] = _layer_norm_cols(alpha * x_ref[0].T + m, g_ref[...], b_ref[...])


def _even_out_fm(alpha, o_c, o_s, o_w, zf, o_f, w_out_t_bf, x_rows, g, b, tm):
    bsz, s, d = x_rows.shape
    wa = NSA_HEADS * HEAD_DIM
    gate_rows = _EVEN_OUT - _GA
    blk = lambda rows: pl.BlockSpec((1, rows, tm), lambda b_, i: (b_, 0, i))
    col = pl.BlockSpec((d, 1), lambda b_, i: (0, 0))
    return pl.pallas_call(
        functools.partial(_even_out_fm_kernel, alpha),
        grid=(bsz, s // tm),
        in_specs=[blk(wa), blk(wa), blk(wa),
                  pl.BlockSpec((1, gate_rows, tm), lambda b_, i: (b_, _GA // gate_rows, i)),
                  blk(FOX_HEADS * HEAD_DIM),
                  pl.BlockSpec(w_out_t_bf.shape, lambda b_, i: (0, 0)),
                  pl.BlockSpec((1, tm, d), lambda b_, i: (b_, i, 0)), col, col],
        out_specs=blk(d),
        out_shape=jax.ShapeDtypeStruct((bsz, d, s), _F32),
        compiler_params=_cparams("parallel", "parallel"), name="even_out_ln_fm",
    )(o_c, o_s, o_w, zf, o_f, w_out_t_bf, x_rows, g.reshape(d, 1), b.reshape(d, 1))


def _odd_out_fm_kernel(alpha, o_ref, wo_ref, x_ref, g_ref, b_ref, y_ref):
    m = jnp.dot(wo_ref[...], o_ref[0].astype(_BF), preferred_element_type=_F32)
    y_ref[0] = _layer_norm_cols(alpha * x_ref[0] + m, g_ref[...], b_ref[...])


def _odd_out_fm(alpha, o, w_out_t_bf, x_t, g, b, tm):
    bsz, d, s = x_t.shape
    blk = lambda rows: pl.BlockSpec((1, rows, tm), lambda b_, i: (b_, 0, i))
    col = pl.BlockSpec((d, 1), lambda b_, i: (0, 0))
    return pl.pallas_call(
        functools.partial(_odd_out_fm_kernel, alpha),
        grid=(bsz, s // tm),
        in_specs=[blk(o.shape[1]), pl.BlockSpec(w_out_t_bf.shape, lambda b_, i: (0, 0)), blk(d), col, col],
        out_specs=blk(d), out_shape=jax.ShapeDtypeStruct((bsz, d, s), _F32),
        compiler_params=_cparams("parallel", "parallel"), name="odd_out_ln_fm",
    )(o, w_out_t_bf, x_t, g.reshape(d, 1), b.reshape(d, 1))


def _route_gates(logits, axis):
    idx = lax.broadcasted_iota(jnp.int32, logits.shape, axis)
    big = jnp.int32(1 << 20)
    red = lambda f, v: f(v, axis=axis, keepdims=True)
    is_g = idx < N_GROUPS
    lg = jnp.where(is_g, logits, NEG_INF)
    mg = red(jnp.max, lg)
    sg = red(jnp.sum, jnp.where(is_g, jnp.exp(lg - mg), 0.0))
    p_top = 1.0 / sg
    g_top = red(jnp.min, jnp.where(lg == mg, idx, big))
    lo = N_GROUPS + EXPERTS_PER_GROUP * g_top
    in_grp = (idx >= lo) & (idx < lo + EXPERTS_PER_GROUP)
    le = jnp.where(in_grp, logits, NEG_INF)
    me = red(jnp.max, le)
    se = red(jnp.sum, jnp.where(in_grp, jnp.exp(le - me), 0.0))
    i1 = red(jnp.min, jnp.where(le == me, idx, big))
    le2 = jnp.where(idx == i1, NEG_INF, le)
    m2 = red(jnp.max, le2)
    i2 = red(jnp.min, jnp.where(le2 == m2, idx, big))
    w1 = 1.0 / se
    w2 = jnp.exp(m2 - me) / se
    tot = w1 + w2
    return jnp.where(idx == i1, p_top * w1 / tot, jnp.where(idx == i2, p_top * w2 / tot, 0.0))


def _moe_fm_kernel(alpha, rows_out, x_ref, wr_ref, br_ref, w13_ref, w2_ref, g_ref, b_ref, y_ref, gate_ref, acc_ref):
    e = pl.program_id(2)
    x = x_ref[0]
    xb = x.astype(_BF)
    f = w13_ref.shape[1] // 2

    @pl.when(e == 0)
    def _route():
        logits = jnp.dot(wr_ref[...], xb, preferred_element_type=_F32) + br_ref[...]
        gate_ref[...] = _route_gates(logits, 0)
        acc_ref[...] = jnp.zeros_like(acc_ref)

    h13 = jnp.dot(w13_ref[0], xb, preferred_element_type=_F32)
    ge = gate_ref[pl.ds(e + N_GROUPS, 1), :]
    h = (jax.nn.silu(h13[0:f]) * h13[f:2 * f]) * ge
    acc_ref[...] += jnp.dot(w2_ref[0], h.astype(_BF), preferred_element_type=_F32)

    @pl.when(e == N_EXPERTS - 1)
    def _finish():
        y = _layer_norm_cols(alpha * x + acc_ref[...], g_ref[...], b_ref[...])
        y_ref[0] = y.T if rows_out else y


def _moe_ln_fm(alpha, x_t, wg, bg, we, be, w13_t_bf, w2_t_bf, g, b, tm, rows_out):
    bsz, d, s = x_t.shape
    rr = 2 * BF16_ROWS
    wr = jnp.zeros((rr, d), _F32).at[:N_GROUPS].set(wg.T).at[N_GROUPS:N_GROUPS + N_EXPERTS].set(we.T).astype(_BF)
    br = jnp.zeros((rr, 1), _F32).at[:N_GROUPS, 0].set(bg).at[N_GROUPS:N_GROUPS + N_EXPERTS, 0].set(be)
    f2 = w13_t_bf.shape[1]
    col = pl.BlockSpec((d, 1), lambda b_, i, e: (0, 0))
    out_spec = (pl.BlockSpec((1, tm, d), lambda b_, i, e: (b_, i, 0)) if rows_out
                else pl.BlockSpec((1, d, tm), lambda b_, i, e: (b_, 0, i)))
    out_shape = jax.ShapeDtypeStruct((bsz, s, d) if rows_out else (bsz, d, s), _F32)
    return pl.pallas_call(
        functools.partial(_moe_fm_kernel, alpha, rows_out),
        grid=(bsz, s // tm, N_EXPERTS),
        in_specs=[pl.BlockSpec((1, d, tm), lambda b_, i, e: (b_, 0, i)),
                  pl.BlockSpec((rr, d), lambda b_, i, e: (0, 0)),
                  pl.BlockSpec((rr, 1), lambda b_, i, e: (0, 0)),
                  pl.BlockSpec((1, f2, d), lambda b_, i, e: (e, 0, 0)),
                  pl.BlockSpec((1, d, f2 // 2), lambda b_, i, e: (e, 0, 0)),
                  col, col],
        out_specs=out_spec, out_shape=out_shape,
        scratch_shapes=[pltpu.VMEM((rr, tm), _F32), pltpu.VMEM((d, tm), _F32)],
        compiler_params=_cparams("parallel", "parallel", "arbitrary"), name="moe_ln_fm",
    )(x_t, wr, br, w13_t_bf, w2_t_bf, g.reshape(d, 1), b.reshape(d, 1))


def _flash_t_kernel(group, kv_shared, aug, sel_rows, sel_block, qi_ref, ki_ref, bi_ref, first_ref, last_ref, *refs):
    refs = list(refs)
    q_ref, k_ref, v_ref, bias_ref = refs[:4]
    pos = 4
    if aug:
        qaug_ref, kaug_ref = refs[pos:pos + 2]
        pos += 2
    if sel_rows:
        sel_ref = refs[pos]
        pos += 1
    o_ref, m_ref, l_ref, acc_ref, s_ref = refs[pos:pos + 5]
    step = pl.program_id(1)
    tq = q_ref.shape[2]

    @pl.when(first_ref[step] == 1)
    def _init():
        m_ref[...] = jnp.full_like(m_ref, NEG_INF)
        l_ref[...] = jnp.zeros_like(l_ref)
        acc_ref[...] = jnp.zeros_like(acc_ref)

    m_all = m_ref[...]
    l_all = l_ref[...]
    acc_all = acc_ref[...]
    if sel_rows:
        off = (ki_ref[step] * sel_rows) % SUBLANES
    def scores(r):
        rows = slice(r * HEAD_DIM, (r + 1) * HEAD_DIM)
        q_t = q_ref[0, rows, :]
        k_t = k_ref[0, slice(0, HEAD_DIM) if kv_shared else rows, :]
        if aug:
            arows = slice(r * BF16_ROWS, (r + 1) * BF16_ROWS)
            q_t = jnp.concatenate([q_t, qaug_ref[0, arows, :]], axis=0)
            k_t = jnp.concatenate([k_t, kaug_ref[0, arows, :]], axis=0)
        return lax.dot_general(k_t, q_t, (((0,), (0,)), ((), ())), preferred_element_type=_F32)

    for r in range(group):
        s_ref[r] = scores(r)
    m_out, l_out, acc_out = [], [], []
    for r in range(group):
        rows = slice(r * HEAD_DIM, (r + 1) * HEAD_DIM)
        kv_rows = slice(0, HEAD_DIM) if kv_shared else rows
        s = s_ref[r] + bias_ref[0, 0, :, r * tq:(r + 1) * tq]
        if sel_rows:
            sr = 0 if kv_shared else r
            s = s + jnp.concatenate(
                [jnp.broadcast_to(sel_ref[0, sr, pl.ds(off + j, 1), :], (sel_block, tq)) for j in range(sel_rows)],
                axis=0)
        m_old = m_all[r:r + 1, :]
        m_new = jnp.maximum(m_old, jnp.max(s, axis=0, keepdims=True))
        a = jnp.exp(m_old - m_new)
        p = jnp.exp(s - m_new)
        l_out.append(a * l_all[r:r + 1, :] + jnp.sum(p, axis=0, keepdims=True))
        acc_out.append(a * acc_all[rows, :]
                       + jnp.dot(v_ref[0, kv_rows, :], p.astype(_BF), preferred_element_type=_F32))
        m_out.append(m_new)
    pad = [jnp.zeros((SUBLANES - group, tq), _F32)] if group < SUBLANES else []
    m_ref[...] = jnp.concatenate(m_out + pad, axis=0)
    l_ref[...] = jnp.concatenate(l_out + pad, axis=0)
    acc_ref[...] = jnp.concatenate(acc_out, axis=0)

    @pl.when(last_ref[step] == 1)
    def _done():
        for r in range(group):
            rows = slice(r * HEAD_DIM, (r + 1) * HEAD_DIM)
            o_ref[0, rows, :] = acc_out[r] / jnp.maximum(l_out[r], 1e-30)


def _flash_t(zb, q_blk, k_blk, v_blk, units, group, kv_shared, bias, pairs, tq, tk,
             qaug=None, kaug=None, sel=None, sel_block=0):
    bsz, _, s = zb.shape
    bh = bsz * units
    hb = bias.shape[0]
    aug = qaug is not None
    sel_rows = 0 if sel is None else tk // sel_block
    pairs = np.asarray(pairs, np.int32)
    qi, ki, bi = pairs[:, 0], pairs[:, 1], pairs[:, 2]
    first = np.concatenate([[1], (qi[1:] != qi[:-1]).astype(np.int32)]).astype(np.int32)
    last = np.concatenate([(qi[1:] != qi[:-1]).astype(np.int32), [1]]).astype(np.int32)
    gq = group * HEAD_DIM
    kvr = HEAD_DIM if kv_shared else gq
    in_specs = [pl.BlockSpec((1, gq, tq), lambda b, t, qi, ki, bi, f, l: (b // units, q_blk + b % units, qi[t])),
                pl.BlockSpec((1, kvr, tk), lambda b, t, qi, ki, bi, f, l: (b // units, k_blk + b % units, ki[t])),
                pl.BlockSpec((1, kvr, tk), lambda b, t, qi, ki, bi, f, l: (b // units, v_blk + b % units, ki[t])),
                pl.BlockSpec((1, 1, tk, group * tq), lambda b, t, qi, ki, bi, f, l: (b % hb, bi[t], 0, 0))]
    args = [zb, zb, zb, bias]
    if aug:
        in_specs += [pl.BlockSpec((1, group * BF16_ROWS, tq), lambda b, t, qi, ki, bi, f, l: (b, 0, qi[t])),
                     pl.BlockSpec((1, group * BF16_ROWS, tk), lambda b, t, qi, ki, bi, f, l: (b, 0, ki[t]))]
        args += [qaug, kaug]
    if sel_rows:
        assert sel_rows in (1, 2, 4, 8)
        in_specs += [pl.BlockSpec((1, sel.shape[1], SUBLANES, tq),
                                  lambda b, t, qi, ki, bi, f, l: (b, 0, (ki[t] * sel_rows) // SUBLANES, qi[t]))]
        args += [sel]
    grid_spec = pltpu.PrefetchScalarGridSpec(
        num_scalar_prefetch=5, grid=(bh, len(qi)), in_specs=in_specs,
        out_specs=pl.BlockSpec((1, gq, tq), lambda b, t, qi, ki, bi, f, l: (b, 0, qi[t])),
        scratch_shapes=[pltpu.VMEM((SUBLANES, tq), _F32), pltpu.VMEM((SUBLANES, tq), _F32), pltpu.VMEM((gq, tq), _F32),
                        pltpu.VMEM((group, tk, tq), _F32)],
    )
    return pl.pallas_call(
        functools.partial(_flash_t_kernel, group, kv_shared, aug, sel_rows, sel_block), grid_spec=grid_spec,
        out_shape=jax.ShapeDtypeStruct((bh, gq, s), _F32),
        compiler_params=_cparams("parallel", "arbitrary"), name="flash_t",
    )(jnp.asarray(qi), jnp.asarray(ki), jnp.asarray(bi), jnp.asarray(first), jnp.asarray(last), *args)


def _topk_axis(score, n_sel, keep, axis):
    j = lax.broadcasted_iota(jnp.int32, score.shape, axis)
    big = jnp.int32(1 << 20)
    for _ in range(n_sel):
        mx = jnp.max(score, axis=axis, keepdims=True)
        firsti = jnp.min(jnp.where(score == mx, j, big), axis=axis, keepdims=True)
        hit = j == firsti
        keep = keep | (hit & (mx > 0.5 * NEG_INF))
        score = jnp.where(hit, -3e38, score)
    return keep


def _topk_rows(score, n_sel, keep):
    return _topk_axis(score, n_sel, keep, 0)


def _cmp_select_t_kernel(theta, tq, n_sel, tab_ref, q_ref, kc_ref, vc_ref, msel_ref, o_ref, sel_ref, s_ref):
    g = pl.program_id(0) % NSA_KV_HEADS
    i = pl.program_id(1)
    ncp = kc_ref.shape[1]
    nsl = sel_ref.shape[1]
    n = lax.broadcasted_iota(jnp.int32, (ncp, tq), 0)
    t = i * tq + lax.broadcasted_iota(jnp.int32, (ncp, tq), 1)
    ok = t - (n * CMP_STRIDE + (CMP_LEN - 1)) >= 0
    band = tq // CMP_STRIDE + CMP_STRIDE
    assert (CMP_STRIDE + 1) * CMP_STRIDE - (CMP_LEN - 1) >= _FAR_DISTANCE and band <= ncp
    w0 = pl.multiple_of(jnp.maximum(i * (tq // CMP_STRIDE) - CMP_STRIDE, 0), SUBLANES)
    nw = w0 + lax.broadcasted_iota(jnp.int32, (band, tq), 0)
    tw = i * tq + lax.broadcasted_iota(jnp.int32, (band, tq), 1)
    dw = tw - (nw * CMP_STRIDE + (CMP_LEN - 1))
    ind = [dw >= th for th in theta]
    kc = kc_ref[0]
    vc_t = vc_ref[0]
    imp = jnp.zeros((ncp, tq), _F32)
    for r in range(NSA_GROUP):
        base = (g * NSA_GROUP + r) * N_BUCKETS
        far = tab_ref[base + N_BUCKETS - 1]
        corr = jnp.full((band, tq), tab_ref[base] - far, _F32)
        for k in range(1, N_BUCKETS):
            corr = corr + jnp.where(ind[k - 1], tab_ref[base + k] - tab_ref[base + k - 1], 0.0)
        rows = slice(r * HEAD_DIM, (r + 1) * HEAD_DIM)
        s_ref[...] = jnp.dot(kc, q_ref[0, rows, :], preferred_element_type=_F32) + far
        s_ref[pl.ds(w0, band), :] = s_ref[pl.ds(w0, band), :] + corr
        s = jnp.where(ok, s_ref[...], NEG_INF)
        m = jnp.max(s, axis=0, keepdims=True)
        e = jnp.where(ok, jnp.exp(s - m), 0.0)
        p = e / jnp.maximum(jnp.sum(e, axis=0, keepdims=True), 1e-30)
        o_ref[0, rows, :] = jnp.dot(vc_t, p.astype(_BF), preferred_element_type=_F32)
        imp = imp + p
    hi, mid, lo = _split3(imp)
    msel = msel_ref[...]
    p_s = (jnp.dot(msel, hi, preferred_element_type=_F32) + jnp.dot(msel, mid, preferred_element_type=_F32)
           + jnp.dot(msel, lo, preferred_element_type=_F32))
    j = lax.broadcasted_iota(jnp.int32, (nsl, tq), 0)
    qb = (i * tq + lax.broadcasted_iota(jnp.int32, (nsl, tq), 1)) >> int(math.log2(SEL_BLOCK))
    valid = j <= qb
    forced = (j == 0) | (j == qb) | (j == qb - 1)
    score = jnp.where(valid, jnp.where(forced, FORCE_SCORE, p_s), NEG_INF)
    chosen = _topk_rows(score, n_sel, jnp.zeros((nsl, tq), jnp.bool_))
    sel_ref[0] = jnp.where(chosen, 0.0, NEG_INF)


def _cmp_select_t(zb, kc, vc_t, tab, tq, n_blocks):
    bsz, _, s = zb.shape
    bg = bsz * NSA_KV_HEADS
    ncp = kc.shape[1]
    nsl = _round_up(n_blocks, LANES)
    gq = NSA_GROUP * HEAD_DIM
    jj = np.arange(nsl)[:, None]
    nn = np.arange(ncp)[None, :]
    msel = ((nn >= CMP_PER_SEL * jj - 1) & (nn <= CMP_PER_SEL * jj + CMP_PER_SEL - 1) & (jj < n_blocks))
    msel = jnp.asarray(msel.astype(np.float32), _BF)
    tabf = tab[:, :NSA_HEADS].T.reshape(-1)
    kern = functools.partial(_cmp_select_t_kernel, _bucket_thresholds(), tq, min(NSA_TOPN, n_blocks))
    return pl.pallas_call(
        kern,
        grid=(bg, s // tq),
        in_specs=[pl.BlockSpec(memory_space=pltpu.SMEM),
                  pl.BlockSpec((1, gq, tq), lambda b, i: (b // NSA_KV_HEADS, b % NSA_KV_HEADS, i)),
                  pl.BlockSpec((1, ncp, HEAD_DIM), lambda b, i: (b, 0, 0)),
                  pl.BlockSpec((1, HEAD_DIM, ncp), lambda b, i: (b, 0, 0)),
                  pl.BlockSpec((nsl, ncp), lambda b, i: (0, 0))],
        out_specs=[pl.BlockSpec((1, gq, tq), lambda b, i: (b, 0, i)),
                   pl.BlockSpec((1, nsl, tq), lambda b, i: (b, 0, i))],
        out_shape=[jax.ShapeDtypeStruct((bg, gq, s), _F32), jax.ShapeDtypeStruct((bg, nsl, s), _F32)],
        scratch_shapes=[pltpu.VMEM((ncp, tq), _F32)],
        compiler_params=_cparams("parallel", "parallel"), name="nsa_cmp_select_t",
    )(tabf, zb, kc, vc_t, msel)


def _moba_select_t_kernel(tq, n_blocks, kf_ref, avg_ref, q_ref, sel_ref, km_ref):
    i = pl.program_id(1)

    @pl.when(i == 0)
    def _means():
        km_ref[...] = _dot3(kf_ref[0], avg_ref[...])

    gate = lax.dot_general(km_ref[...].astype(_BF), q_ref[0], (((0,), (0,)), ((), ())),
                           preferred_element_type=_F32)
    j = lax.broadcasted_iota(jnp.int32, gate.shape, 0)
    qb = (i * tq + lax.broadcasted_iota(jnp.int32, gate.shape, 1)) >> int(math.log2(MOBA_BLOCK))
    score = jnp.where(j < qb, gate, NEG_INF)
    chosen = _topk_rows(score, min(MOBA_TOPK, n_blocks), j == qb)
    sel_ref[0] = jnp.where(chosen, 0.0, NEG_INF)[0:sel_ref.shape[1]]


def _moba_select_t(zf, zb, k_blk, tq):
    bsz, _, s = zb.shape
    n_blocks = s // MOBA_BLOCK
    assert n_blocks <= LANES
    rows = _round_up(n_blocks, SUBLANES)
    avg = (np.arange(s)[:, None] // MOBA_BLOCK == np.arange(LANES)[None, :]).astype(np.float32) / MOBA_BLOCK
    bh = bsz * MOBA_HEADS
    return pl.pallas_call(
        functools.partial(_moba_select_t_kernel, tq, n_blocks),
        grid=(bh, s // tq),
        in_specs=[pl.BlockSpec((1, HEAD_DIM, s), lambda b, i: (b // MOBA_HEADS, k_blk + b % MOBA_HEADS, 0)),
                  pl.BlockSpec((s, LANES), lambda b, i: (0, 0)),
                  pl.BlockSpec((1, HEAD_DIM, tq), lambda b, i: (b // MOBA_HEADS, b % MOBA_HEADS, i))],
        out_specs=pl.BlockSpec((1, rows, tq), lambda b, i: (b, 0, i)),
        out_shape=jax.ShapeDtypeStruct((bh, rows, s), _F32),
        scratch_shapes=[pltpu.VMEM((HEAD_DIM, LANES), _F32)],
        compiler_params=_cparams("parallel", "arbitrary"), name="moba_select_t",
    )(zf, jnp.asarray(avg, _BF), zb)


def _linear_kernel(x_ref, w_ref, b_ref, o_ref):
    o_ref[...] = jnp.dot(x_ref[...].astype(_BF), w_ref[...], preferred_element_type=_F32) + b_ref[...]


def _linear(x, w_bf, b, tm):
    m, k = x.shape
    n = w_bf.shape[1]
    return pl.pallas_call(
        _linear_kernel,
        grid=(m // tm,),
        in_specs=[pl.BlockSpec((tm, k), lambda i: (i, 0)),
                  pl.BlockSpec((k, n), lambda i: (0, 0)),
                  pl.BlockSpec((1, n), lambda i: (0, 0))],
        out_specs=pl.BlockSpec((tm, n), lambda i: (i, 0)),
        out_shape=jax.ShapeDtypeStruct((m, n), _F32),
        compiler_params=_cparams("parallel"),
        name="linear",
    )(x, w_bf, b.reshape(1, n))


def _layer_norm_rows(y, g, b):
    mu = jnp.mean(y, axis=-1, keepdims=True)
    yc = y - mu
    var = jnp.mean(yc * yc, axis=-1, keepdims=True)
    return yc * lax.rsqrt(var + LN_EPS) * g + b


def _even_out_kernel(alpha, oc_ref, os_ref, ow_ref, gl_ref, of_ref, ex_ref, wa_ref, wf_ref, x_ref,
                     g_ref, b_ref, y_ref):
    wa = NSA_HEADS * HEAD_DIM
    gexp = _dot3(jax.nn.sigmoid(gl_ref[...]), ex_ref[...])
    o_a = (gexp[:, 0:wa] * oc_ref[...] + gexp[:, wa:2 * wa] * os_ref[...] + gexp[:, 2 * wa:3 * wa] * ow_ref[...])
    m = jnp.dot(o_a.astype(_BF), wa_ref[...], preferred_element_type=_F32)
    m += jnp.dot(of_ref[...].astype(_BF), wf_ref[...], preferred_element_type=_F32)
    y_ref[...] = _layer_norm_rows(alpha * x_ref[...] + m, g_ref[...], b_ref[...])


def _even_out(alpha, o_c, o_s, o_w, gl, o_f, w_out_bf, x, g, b, tm):
    n, d = x.shape
    wa = NSA_HEADS * HEAD_DIM
    wf = FOX_HEADS * HEAD_DIM
    ex_np = np.zeros((LANES, 3 * wa), np.float32)
    for j in range(3):
        for h in range(NSA_HEADS):
            ex_np[j * NSA_HEADS + h, j * wa + h * HEAD_DIM:j * wa + (h + 1) * HEAD_DIM] = 1.0
    ex = jnp.asarray(ex_np, _BF)
    gl = _pad_last(gl, LANES)
    row = lambda w: pl.BlockSpec((tm, w), lambda i: (i, 0))
    full = lambda a: pl.BlockSpec(a.shape, lambda i: (0,) * a.ndim)
    args = (o_c, o_s, o_w, gl, o_f, ex, w_out_bf[:wa], w_out_bf[wa:], x, g.reshape(1, d), b.reshape(1, d))
    specs = [row(wa), row(wa), row(wa), row(LANES), row(wf), full(ex), full(args[6]), full(args[7]),
             row(d), full(args[9]), full(args[10])]
    return pl.pallas_call(
        functools.partial(_even_out_kernel, alpha),
        grid=(n // tm,), in_specs=specs, out_specs=row(d),
        out_shape=jax.ShapeDtypeStruct((n, d), _F32),
        compiler_params=_cparams("parallel"), name="even_out_ln",
    )(*args)


def _odd_out_kernel(alpha, o_ref, w_ref, x_ref, g_ref, b_ref, y_ref):
    m = jnp.dot(o_ref[...].astype(_BF), w_ref[...], preferred_element_type=_F32)
    y_ref[...] = _layer_norm_rows(alpha * x_ref[...] + m, g_ref[...], b_ref[...])


def _odd_out(alpha, o, w_out_bf, x, g, b, tm):
    n, d = x.shape
    row = lambda w: pl.BlockSpec((tm, w), lambda i: (i, 0))
    full = lambda shp: pl.BlockSpec(shp, lambda i: (0,) * len(shp))
    return pl.pallas_call(
        functools.partial(_odd_out_kernel, alpha),
        grid=(n // tm,),
        in_specs=[row(o.shape[1]), full(w_out_bf.shape), row(d), full((1, d)), full((1, d))],
        out_specs=row(d), out_shape=jax.ShapeDtypeStruct((n, d), _F32),
        compiler_params=_cparams("parallel"), name="odd_out_ln",
    )(o, w_out_bf, x, g.reshape(1, d), b.reshape(1, d))


def _moe_kernel(alpha, x_ref, wr_ref, br_ref, w1_ref, w3_ref, w2_ref, g_ref, b_ref, y_ref,
                gate_ref, acc_ref):
    e = pl.program_id(1)
    x = x_ref[...]

    @pl.when(e == 0)
    def _route():
        logits = jnp.dot(x.astype(_BF), wr_ref[...], preferred_element_type=_F32) + br_ref[...]
        gate_ref[...] = _route_gates(logits, 1)
        acc_ref[...] = jnp.zeros_like(acc_ref)

    xb = x.astype(_BF)
    h1 = jnp.dot(xb, w1_ref[0], preferred_element_type=_F32)
    h3 = jnp.dot(xb, w3_ref[0], preferred_element_type=_F32)
    gate = gate_ref[...]
    lane = lax.broadcasted_iota(jnp.int32, gate.shape, 1)
    ge = jnp.sum(jnp.where(lane == e + N_GROUPS, gate, 0.0), axis=-1, keepdims=True)
    h = (jax.nn.silu(h1) * h3) * ge
    acc_ref[...] += jnp.dot(h.astype(_BF), w2_ref[0], preferred_element_type=_F32)

    @pl.when(e == N_EXPERTS - 1)
    def _finish():
        y_ref[...] = _layer_norm_rows(alpha * x + acc_ref[...], g_ref[...], b_ref[...])


def _moe_ln(alpha, x, wg, bg, we, be, w1_bf, w3_bf, w2_bf, g, b, tm):
    n, d = x.shape
    f = w1_bf.shape[2]
    wr = jnp.zeros((d, LANES), _F32).at[:, :N_GROUPS].set(wg).at[:, N_GROUPS:N_GROUPS + N_EXPERTS].set(we)
    wr = wr.astype(_BF)
    br = jnp.zeros((1, LANES), _F32).at[0, :N_GROUPS].set(bg).at[0, N_GROUPS:N_GROUPS + N_EXPERTS].set(be)
    return pl.pallas_call(
        functools.partial(_moe_kernel, alpha),
        grid=(n // tm, N_EXPERTS),
        in_specs=[pl.BlockSpec((tm, d), lambda i, e: (i, 0)),
                  pl.BlockSpec((d, LANES), lambda i, e: (0, 0)),
                  pl.BlockSpec((1, LANES), lambda i, e: (0, 0)),
                  pl.BlockSpec((1, d, f), lambda i, e: (e, 0, 0)),
                  pl.BlockSpec((1, d, f), lambda i, e: (e, 0, 0)),
                  pl.BlockSpec((1, f, d), lambda i, e: (e, 0, 0)),
                  pl.BlockSpec((1, d), lambda i, e: (0, 0)),
                  pl.BlockSpec((1, d), lambda i, e: (0, 0))],
        out_specs=pl.BlockSpec((tm, d), lambda i, e: (i, 0)),
        out_shape=jax.ShapeDtypeStruct((n, d), _F32),
        scratch_shapes=[pltpu.VMEM((tm, LANES), _F32), pltpu.VMEM((tm, d), _F32)],
        compiler_params=_cparams("parallel", "arbitrary"), name="moe_ln",
    )(x, wr, br, w1_bf, w3_bf, w2_bf, g.reshape(1, d), b.reshape(1, d))


def _pages_per_step(n_pages):
    return next(n for n in (PAGES_PER_STEP, 4, 2, 1) if n_pages % n == 0)


def _paged_attn_kernel(n_pg, n_steps, kvh, sel, fox, pt_ref, *refs):
    refs = list(refs)
    qbd_ref = refs[0]
    pages = refs[1:1 + n_pg]
    pos = 1 + n_pg
    bias_ref, bias_new_ref, knew_ref, vnew_ref = refs[pos:pos + 4]
    pos += 4
    if sel:
        selb_ref, e_ref, e_new_ref = refs[pos:pos + 3]
        pos += 3
    if fox:
        cq_ref, ck_ref, ck_new_ref = refs[pos:pos + 3]
        pos += 3
    o_ref, m_ref, l_ref, acc_ref = refs[pos:pos + 4]
    step = pl.program_id(1)
    qbd = qbd_ref[0]
    r, f = qbd.shape
    nt = (((1,), (1,)), ((), ()))

    @pl.when(step == 0)
    def _init():
        m_ref[...] = jnp.full_like(m_ref, NEG_INF)
        l_ref[...] = jnp.zeros_like(l_ref)
        acc_ref[...] = jnp.zeros_like(acc_ref)

    def extra(bias, e, ck):
        add = bias
        if sel:
            add = add + jnp.dot(selb_ref[0], e, preferred_element_type=_F32)
        if fox:
            add = add + cq_ref[0][:, 0:1] - jnp.concatenate([ck] * (r // kvh), axis=0)
        return add

    def absorb(k_list, v_list, add):
        s = jnp.concatenate([jnp.dot(qbd, k, preferred_element_type=_F32) for k in k_list], axis=1) + add
        m_old = m_ref[...]
        m_new = jnp.maximum(m_old, jnp.max(s, axis=1, keepdims=True))
        a = jnp.exp(m_old - m_new)
        p = jnp.exp(s - m_new)
        l_ref[...] = a * l_ref[...] + jnp.sum(p, axis=1, keepdims=True)
        pv = None
        for j, v in enumerate(v_list):
            t = lax.dot_general(p[:, j * PAGE_SIZE:(j + 1) * PAGE_SIZE].astype(_BF), v, nt,
                                preferred_element_type=_F32)
            pv = t if pv is None else pv + t
        acc_ref[...] = a * acc_ref[...] + pv
        m_ref[...] = m_new

    absorb([pg[0, 0].reshape(f, PAGE_SIZE).astype(_BF) for pg in pages],
           [pg[0, 1].reshape(f, PAGE_SIZE).astype(_BF) for pg in pages],
           extra(bias_ref[...], e_ref[...] if sel else None, ck_ref[0] if fox else None))

    @pl.when(step == n_steps - 1)
    def _done():
        absorb([knew_ref[0]], [vnew_ref[0]],
               extra(bias_new_ref[...], e_new_ref[...] if sel else None, ck_new_ref[0] if fox else None))
        o_ref[0] = acc_ref[...] / jnp.maximum(l_ref[...], 1e-30)


def _paged_attn(qbd, pages_t, page_table, bias, knew_t, vnew_t, selb=None, e=None, cq=None, ck=None, ck_new=None):
    bsz, r, f = qbd.shape
    n_pages = page_table.shape[1]
    kvh = pages_t.shape[2]
    length = n_pages * PAGE_SIZE
    n_pg = _pages_per_step(n_pages)
    n_steps = n_pages // n_pg
    sel = selb is not None
    fox = cq is not None
    w = n_pg * PAGE_SIZE

    def page_map(k):
        return lambda b, p, pt: (pt[b * n_pages + p * n_pg + k], 0, 0, 0, 0)

    per_b = lambda shape: pl.BlockSpec((1,) + shape, lambda b, p, pt: (b, 0, 0))
    whole = lambda shape: pl.BlockSpec(shape, lambda b, p, pt: (0, 0))
    in_specs = [per_b((r, f))]
    in_specs += [pl.BlockSpec((1, 2, kvh, HEAD_DIM, PAGE_SIZE), page_map(k)) for k in range(n_pg)]
    in_specs += [pl.BlockSpec((r, w), lambda b, p, pt: (0, p)), whole((r, PAGE_SIZE)),
                 per_b((f, PAGE_SIZE)), per_b((f, PAGE_SIZE))]
    args = [qbd] + [pages_t] * n_pg + [bias[:, :length], bias[:, length:], knew_t, vnew_t]
    if sel:
        nbl = selb.shape[2]
        in_specs += [per_b((r, nbl)), pl.BlockSpec((nbl, w), lambda b, p, pt: (0, p)), whole((nbl, PAGE_SIZE))]
        args += [selb, e[:, :length], e[:, length:]]
    if fox:
        in_specs += [per_b((r, LANES)), pl.BlockSpec((1, kvh, w), lambda b, p, pt: (b, 0, p)), per_b((kvh, PAGE_SIZE))]
        args += [cq, ck, ck_new]
    grid_spec = pltpu.PrefetchScalarGridSpec(
        num_scalar_prefetch=1, grid=(bsz, n_steps), in_specs=in_specs,
        out_specs=per_b((r, f)),
        scratch_shapes=[pltpu.VMEM((r, 1), _F32), pltpu.VMEM((r, 1), _F32), pltpu.VMEM((r, f), _F32)],
    )
    return pl.pallas_call(
        functools.partial(_paged_attn_kernel, n_pg, n_steps, kvh, sel, fox), grid_spec=grid_spec,
        out_shape=jax.ShapeDtypeStruct((bsz, r, f), _F32),
        compiler_params=_cparams("parallel", "arbitrary"), name="paged_attn",
    )(page_table.reshape(-1), *args)


def _moba_pick_kernel(n_pg, n_steps, n_blocks, n_top, pt_ref, qbd_ref, *refs):
    pages = refs[:n_pg]
    selb_ref, km_ref = refs[n_pg], refs[n_pg + 1]
    step = pl.program_id(1)
    f = km_ref.shape[0]

    @pl.when(step == 0)
    def _init():
        km_ref[...] = jnp.zeros_like(km_ref)

    lane = lax.broadcasted_iota(jnp.int32, km_ref.shape, 1)
    km = km_ref[...]
    for k in range(n_pg):
        col = jnp.sum(pages[k][0, 0].reshape(f, PAGE_SIZE), axis=1, keepdims=True)
        blk = (step * n_pg + k) // (MOBA_BLOCK // PAGE_SIZE)
        km = jnp.where(lane == blk, km + col, km)
    km_ref[...] = km

    @pl.when(step == n_steps - 1)
    def _pick():
        means = (km * (1.0 / MOBA_BLOCK)).astype(_BF)
        gate = jnp.dot(qbd_ref[0], means, preferred_element_type=_F32)
        j = lax.broadcasted_iota(jnp.int32, gate.shape, 1)
        score = jnp.where(j < n_blocks, gate, NEG_INF)
        chosen = _topk_axis(score, n_top, j == n_blocks, 1)
        selb_ref[0] = jnp.where(chosen, 0.0, NEG_INF).astype(_BF)


def _moba_pick(qbd, pages_t, page_table):
    bsz, r, f = qbd.shape
    n_pages = page_table.shape[1]
    kvh = pages_t.shape[2]
    n_blocks = n_pages * PAGE_SIZE // MOBA_BLOCK
    assert n_blocks < LANES and (n_pages * PAGE_SIZE) % MOBA_BLOCK == 0
    n_pg = _pages_per_step(n_pages)
    n_steps = n_pages // n_pg

    def page_map(k):
        return lambda b, p, pt: (pt[b * n_pages + p * n_pg + k], 0, 0, 0, 0)

    grid_spec = pltpu.PrefetchScalarGridSpec(
        num_scalar_prefetch=1, grid=(bsz, n_steps),
        in_specs=[pl.BlockSpec((1, r, f), lambda b, p, pt: (b, 0, 0))]
        + [pl.BlockSpec((1, 1, kvh, HEAD_DIM, PAGE_SIZE), page_map(k)) for k in range(n_pg)],
        out_specs=pl.BlockSpec((1, r, LANES), lambda b, p, pt: (b, 0, 0)),
        scratch_shapes=[pltpu.VMEM((f, LANES), _F32)],
    )
    return pl.pallas_call(
        functools.partial(_moba_pick_kernel, n_pg, n_steps, n_blocks, min(MOBA_TOPK, n_blocks)), grid_spec=grid_spec,
        out_shape=jax.ShapeDtypeStruct((bsz, r, LANES), _BF),
        compiler_params=_cparams("parallel", "arbitrary"), name="moba_pick",
    )(page_table.reshape(-1), qbd, *([pages_t] * n_pg))


def _compress_kernel(x_ref, pe_ref, w1a_ref, w1b_ref, b1_ref, w2_ref, b2_ref, o_ref):
    x = x_ref[0, 0]
    pe = pe_ref[0]
    hf = jnp.dot((x + pe[0:1]).astype(_BF), w1a_ref[0], preferred_element_type=_F32)
    hs = jnp.dot((x + pe[1:2]).astype(_BF), w1b_ref[0], preferred_element_type=_F32)
    hs_next = pltpu.roll(hs, hs.shape[0] - 1, 0)
    h = jax.nn.gelu(hf + hs_next + b1_ref[0], approximate=True)
    o_ref[0, 0] = jnp.dot(h.astype(_BF), w2_ref[0], preferred_element_type=_F32) + b2_ref[0]


def _nsa_compress(x, pe, w1, b1, w2, b2):
    sg, bsz, nch, flat = x.shape
    pe2 = pe.reshape(2, CMP_STRIDE, 2, HEAD_DIM).transpose(2, 0, 1, 3).reshape(2, 2, flat)
    w1r = w1.reshape(2, CMP_STRIDE, 2, HEAD_DIM, CMP_HID).transpose(2, 0, 1, 3, 4).reshape(2, 2, flat, CMP_HID)
    w1r = w1r.astype(_BF)
    smap = lambda s, b: (s // NSA_KV_HEADS, 0, 0)
    return pl.pallas_call(
        _compress_kernel,
        grid=(sg, bsz),
        in_specs=[pl.BlockSpec((1, 1, nch, flat), lambda s, b: (s, b, 0, 0)),
                  pl.BlockSpec((1, 2, flat), smap),
                  pl.BlockSpec((1, flat, CMP_HID), smap),
                  pl.BlockSpec((1, flat, CMP_HID), smap),
                  pl.BlockSpec((1, 1, CMP_HID), smap),
                  pl.BlockSpec((1, CMP_HID, HEAD_DIM), smap),
                  pl.BlockSpec((1, 1, HEAD_DIM), smap)],
        out_specs=pl.BlockSpec((1, 1, nch, HEAD_DIM), lambda s, b: (s, b, 0, 0)),
        out_shape=jax.ShapeDtypeStruct((sg, bsz, nch, HEAD_DIM), _F32),
        compiler_params=_cparams("parallel", "parallel"), name="nsa_compress",
    )(x, pe2, w1r[:, 0], w1r[:, 1], b1.reshape(2, 1, CMP_HID), w2.astype(_BF), b2.reshape(2, 1, HEAD_DIM))


def _chunks_from_fm(kv_t):
    bsz, _, length = kv_t.shape
    nch = length // CMP_STRIDE
    sg = 2 * NSA_KV_HEADS
    x = kv_t.reshape(bsz, sg, HEAD_DIM, nch, CMP_STRIDE).transpose(1, 0, 3, 4, 2)
    return x.reshape(sg, bsz, nch, CMP_STRIDE * HEAD_DIM)


def _cmp_select_kernel(theta, tq, qpos0, n_sel, tab_ref, qa_ref, kc_ref, vc_ref, msel_ref, o_ref, selb_ref):
    g = pl.program_id(0) % NSA_KV_HEADS
    i = pl.program_id(1)
    ncp = kc_ref.shape[1]
    nsl = selb_ref.shape[2]
    t = qpos0 + i * tq + lax.broadcasted_iota(jnp.int32, (tq, ncp), 0)
    n = lax.broadcasted_iota(jnp.int32, (tq, ncp), 1)
    d = t - (n * CMP_STRIDE + (CMP_LEN - 1))
    ok = d >= 0
    ind = [d >= th for th in theta]
    kc = kc_ref[0]
    vc = vc_ref[0]
    imp = jnp.zeros((tq, ncp), _F32)
    for r in range(NSA_GROUP):
        base = (g * NSA_GROUP + r) * N_BUCKETS
        bias = jnp.full((tq, ncp), tab_ref[base], _F32)
        for k in range(1, N_BUCKETS):
            bias = bias + jnp.where(ind[k - 1], tab_ref[base + k] - tab_ref[base + k - 1], 0.0)
        s = lax.dot_general(qa_ref[0, r * tq:(r + 1) * tq, :], kc, (((1,), (1,)), ((), ())),
                            preferred_element_type=_F32) + bias
        s = jnp.where(ok, s, NEG_INF)
        m = jnp.max(s, axis=-1, keepdims=True)
        e = jnp.where(ok, jnp.exp(s - m), 0.0)
        p = e / jnp.maximum(jnp.sum(e, axis=-1, keepdims=True), 1e-30)
        o_ref[0, r * tq:(r + 1) * tq, :] = jnp.dot(p.astype(_BF), vc, preferred_element_type=_F32)
        imp = imp + p
    p_s = _dot3(imp, msel_ref[...])
    j = lax.broadcasted_iota(jnp.int32, (tq, nsl), 1)
    qb = (qpos0 + i * tq + lax.broadcasted_iota(jnp.int32, (tq, nsl), 0)) >> int(math.log2(SEL_BLOCK))
    valid = j <= qb
    forced = (j == 0) | (j == qb) | (j == qb - 1)
    score = jnp.where(valid, jnp.where(forced, FORCE_SCORE, p_s), NEG_INF)
    sel = jnp.zeros((tq, nsl), jnp.bool_)
    big = jnp.int32(1 << 20)
    for _ in range(n_sel):
        mx = jnp.max(score, axis=-1, keepdims=True)
        firsti = jnp.min(jnp.where(score == mx, j, big), axis=-1, keepdims=True)
        hit = j == firsti
        sel = sel | (hit & (mx > 0.5 * NEG_INF))
        score = jnp.where(hit, -3e38, score)
    selb_ref[0] = jnp.where(sel, 0.0, NEG_INF).astype(_BF)


def _cmp_select(qa, kc, vc, tab, tq, qpos0, n_blocks):
    bg, rows, _ = qa.shape
    ncp = kc.shape[1]
    nq = rows // (NSA_GROUP * tq)
    nsl = _round_up(n_blocks, LANES)
    nn = np.arange(ncp)[:, None]
    jj = np.arange(nsl)[None, :]
    msel = ((nn >= CMP_PER_SEL * jj - 1) & (nn <= CMP_PER_SEL * jj + CMP_PER_SEL - 1) & (jj < n_blocks))
    msel = jnp.asarray(msel.astype(np.float32), _BF)
    tabf = tab[:, :NSA_HEADS].T.reshape(-1)
    kern = functools.partial(_cmp_select_kernel, _bucket_thresholds(), tq, qpos0, min(NSA_TOPN, n_blocks))
    return pl.pallas_call(
        kern,
        grid=(bg, nq),
        in_specs=[pl.BlockSpec(memory_space=pltpu.SMEM),
                  pl.BlockSpec((1, NSA_GROUP * tq, HEAD_DIM), lambda b, i: (b, i, 0)),
                  pl.BlockSpec((1, ncp, HEAD_DIM), lambda b, i: (b, 0, 0)),
                  pl.BlockSpec((1, ncp, HEAD_DIM), lambda b, i: (b, 0, 0)),
                  pl.BlockSpec((ncp, nsl), lambda b, i: (0, 0))],
        out_specs=[pl.BlockSpec((1, NSA_GROUP * tq, HEAD_DIM), lambda b, i: (b, i, 0)),
                   pl.BlockSpec((1, tq, nsl), lambda b, i: (b, i, 0))],
        out_shape=[jax.ShapeDtypeStruct((bg, rows, HEAD_DIM), _F32),
                   jax.ShapeDtypeStruct((bg, nq * tq, nsl), _BF)],
        compiler_params=_cparams("parallel", "parallel"), name="nsa_cmp_select",
    )(tabf, qa, kc, vc, msel)


def _logf_cumsum_kernel(n_new, x_ref, u_ref, lf_ref, c_ref, hi_ref, mid_ref, lo_ref):
    length = x_ref.shape[1]
    x = x_ref[...]
    col = lax.broadcasted_iota(jnp.int32, x.shape, 1)
    ls = jnp.minimum(x, 0.0) - jnp.log1p(jnp.exp(-jnp.abs(x)))
    lf = jnp.where(col >= length - n_new, ls, x)
    lf_ref[...] = lf
    u = u_ref[...]
    carry = jnp.zeros((x.shape[0], 1), _F32)
    for k in range(length // LANES):
        blk = _dot3(lf[:, k * LANES:(k + 1) * LANES], u) + carry
        c_ref[:, k * LANES:(k + 1) * LANES] = blk
        hi, mid, lo = _split3(blk)
        hi_ref[:, k * LANES:(k + 1) * LANES] = hi
        mid_ref[:, k * LANES:(k + 1) * LANES] = mid
        lo_ref[:, k * LANES:(k + 1) * LANES] = lo
        carry = blk[:, LANES - 1:LANES]


def _logf_cumsum(x, n_new, rows_per_step):
    rows, length = x.shape
    u = jnp.asarray(np.triu(np.ones((LANES, LANES), np.float32)), _BF)
    spec = pl.BlockSpec((rows_per_step, length), lambda i: (i, 0))
    lf, c, hi, mid, lo = pl.pallas_call(
        functools.partial(_logf_cumsum_kernel, n_new),
        grid=(rows // rows_per_step,),
        in_specs=[spec, pl.BlockSpec((LANES, LANES), lambda i: (0, 0))],
        out_specs=[spec] * 5,
        out_shape=[jax.ShapeDtypeStruct((rows, length), _F32)] * 2 + [jax.ShapeDtypeStruct((rows, length), _BF)] * 3,
        compiler_params=_cparams("parallel"), name="fox_logf_cumsum",
    )(x, u)
    return lf, c, (hi, mid, lo)


def _fox_aug_rows(c3):
    one = jnp.ones_like(c3[0])
    zero = jnp.zeros_like(c3[0])
    pad = [zero] * (BF16_ROWS - 6)
    qaug = jnp.stack(list(c3) + [one, one, one] + pad, axis=1)
    kaug = jnp.stack([one, one, one] + [-c for c in c3] + pad, axis=1)
    return qaug, kaug


def _gather_kernel(n_pg, pt_ref, *refs):
    ins, out = refs[:n_pg], refs[n_pg]
    for k in range(n_pg):
        out[0, :, k * PAGE_SIZE:(k + 1) * PAGE_SIZE] = ins[k][0]


def _gather_fm(pool_t, page_table):
    bsz, n_pages = page_table.shape
    f = pool_t.shape[1]
    n_pg = _pages_per_step(n_pages)

    def in_map(k):
        return lambda b, p, pt: (pt[b * n_pages + p * n_pg + k], 0, 0)

    grid_spec = pltpu.PrefetchScalarGridSpec(
        num_scalar_prefetch=1,
        grid=(bsz, n_pages // n_pg),
        in_specs=[pl.BlockSpec((1, f, PAGE_SIZE), in_map(k)) for k in range(n_pg)],
        out_specs=pl.BlockSpec((1, f, n_pg * PAGE_SIZE), lambda b, p, pt: (b, 0, p)),
    )
    return pl.pallas_call(
        functools.partial(_gather_kernel, n_pg), grid_spec=grid_spec,
        out_shape=jax.ShapeDtypeStruct((bsz, f, n_pages * PAGE_SIZE), pool_t.dtype),
        compiler_params=_cparams("parallel", "arbitrary"), name="page_gather",
    )(page_table.reshape(-1), *([pool_t] * n_pg))


def _pages_fm(cache_l):
    return cache_l.transpose(0, 2, 3, 4, 1)


def _kv_group_onehot(heads, group):
    return (np.arange(heads)[:, None] // group == np.arange(heads // group)[None, :]).astype(np.float32)


def _block_diag_queries(q, group):
    b, t, h, dh = q.shape
    oh = jnp.asarray(_kv_group_onehot(h, group))
    x = q[:, :, :, None, :] * oh[None, None, :, :, None]
    return x.reshape(b, t * h, (h // group) * dh).astype(_BF)


def _own_head_columns(o, t, heads, group):
    b = o.shape[0]
    oh = jnp.asarray(_kv_group_onehot(heads, group))
    x = o.reshape(b, t, heads, heads // group, HEAD_DIM) * oh[None, None, :, :, None]
    return jnp.sum(x, axis=3).reshape(b, t, heads * HEAD_DIM)


def _sample_bias(tabh, heads, d0, t, cols, window=None):
    tile = _bias_tile(tabh, d0, t, cols, window)
    tile = jnp.broadcast_to(tile, (heads, t, cols))
    return tile.transpose(1, 0, 2).reshape(t * heads, cols)


def _new_rows_fm(x):
    return _pad_last(x.transpose(0, 2, 1), PAGE_SIZE).astype(_BF)


def _nsa_q_rows(q, tq):
    b, t, _, dh = q.shape
    nq = t // tq
    x = q.reshape(b, nq, tq, NSA_KV_HEADS, NSA_GROUP, dh).transpose(0, 3, 1, 4, 2, 5)
    return x.reshape(b * NSA_KV_HEADS, nq * NSA_GROUP * tq, dh)


def _nsa_rows_back(o, b, t, tq):
    nq = t // tq
    x = o.reshape(b, NSA_KV_HEADS, nq, NSA_GROUP, tq, HEAD_DIM).transpose(0, 2, 4, 1, 3, 5)
    return x.reshape(b, t, NSA_HEADS * HEAD_DIM)


def _even_sample(z, li, cache_cmp, cache_sel, win_state, cache_fkv, cache_flogf, page_table, tab, cmp_w, tq):
    bs, n_pages = page_table.shape
    past = n_pages * PAGE_SIZE
    ts = z.shape[0] // bs
    q_a, kv_c, kv_s, kv_w, g_bm, q_f, kv_f, f_logit = _even_split(z, bs, ts)
    g = NSA_KV_HEADS
    tabn = tab[:, :NSA_HEADS]
    wide = g * HEAD_DIM
    total = past + PAGE_SIZE
    assert past % CMP_STRIDE == 0 and past % SEL_BLOCK == 0 and ts <= CMP_STRIDE

    pool_c = _pages_fm(cache_cmp[li]).reshape(-1, 2 * wide, PAGE_SIZE)
    cmp_tok = _nsa_compress(_chunks_from_fm(_gather_fm(pool_c, page_table)), *cmp_w)
    ncp = _round_up(cmp_tok.shape[2], LANES)
    cmp_tok = _pad_axis(cmp_tok, 2, ncp).astype(_BF)
    kc = cmp_tok[:g].transpose(1, 0, 2, 3).reshape(bs * g, ncp, HEAD_DIM)
    vc = cmp_tok[g:].transpose(1, 0, 2, 3).reshape(bs * g, ncp, HEAD_DIM)
    q_rows = _nsa_q_rows(_pad_axis(q_a * SCALE, 1, tq), tq).astype(_BF)
    o_c, selb = _cmp_select(q_rows, kc, vc, tab, tq, past, total // SEL_BLOCK)
    o_c = _nsa_rows_back(o_c, bs, tq, tq)[:, :ts]
    nsl = selb.shape[2]

    qbd = _block_diag_queries(q_a * SCALE, NSA_GROUP)
    sel_rows = jnp.repeat(selb.reshape(bs, g, tq, nsl)[:, :, :ts].transpose(0, 2, 1, 3), NSA_GROUP, axis=2)
    sel_rows = sel_rows.reshape(bs, ts * NSA_HEADS, nsl)
    e_sel = jnp.asarray((np.arange(nsl)[:, None] == np.arange(total)[None, :] // SEL_BLOCK).astype(np.float32), _BF)
    flat = lambda kv, s: kv[:, :, s].reshape(bs, ts, -1)
    o_s = _paged_attn(qbd, _pages_fm(cache_sel[li]), page_table, _sample_bias(tabn, NSA_HEADS, past, ts, total),
                      _new_rows_fm(flat(kv_s, 0)), _new_rows_fm(flat(kv_s, 1)), selb=sel_rows, e=e_sel)
    win_buf = win_state[li]
    wb = win_buf.shape[1]
    assert wb % PAGE_SIZE == 0
    wpages = _pages_fm(win_buf).reshape(bs, 2, g, HEAD_DIM, wb // PAGE_SIZE, PAGE_SIZE)
    wpages = wpages.transpose(0, 4, 1, 2, 3, 5).reshape(bs * (wb // PAGE_SIZE), 2, g, HEAD_DIM, PAGE_SIZE)
    wtable = jnp.arange(bs * (wb // PAGE_SIZE), dtype=jnp.int32).reshape(bs, wb // PAGE_SIZE)
    o_w = _paged_attn(qbd, wpages, wtable, _sample_bias(tabn, NSA_HEADS, wb, ts, wb + PAGE_SIZE, NSA_WINDOW),
                      _new_rows_fm(flat(kv_w, 0)), _new_rows_fm(flat(kv_w, 1)))
    o_s = _own_head_columns(o_s, ts, NSA_HEADS, NSA_GROUP)
    o_w = _own_head_columns(o_w, ts, NSA_HEADS, NSA_GROUP)

    past_l = _gather_fm(cache_flogf[li].transpose(0, 2, 1), page_table)
    lf_len = _round_up(past + ts, LANES)
    front = lf_len - past - ts
    fl_all = jnp.concatenate([jnp.zeros((bs, FOX_HEADS, front), _F32), past_l, f_logit.transpose(0, 2, 1)], axis=-1)
    logf_t, c, _ = _logf_cumsum(fl_all.reshape(bs * FOX_HEADS, lf_len), ts, min(bs * FOX_HEADS, 64))
    c = c.reshape(bs, FOX_HEADS, lf_len)
    logf = logf_t[:, lf_len - ts:].reshape(bs, FOX_HEADS, ts).transpose(0, 2, 1)
    c_new = c[:, :, front + past:]
    cq = jnp.broadcast_to(c_new.transpose(0, 2, 1).reshape(bs, ts * FOX_HEADS, 1), (bs, ts * FOX_HEADS, LANES))
    o_f = _paged_attn(_block_diag_queries(q_f * SCALE, 1), _pages_fm(cache_fkv[li]), page_table,
                      _sample_bias(None, FOX_HEADS, past, ts, total),
                      _new_rows_fm(flat(kv_f, 0)), _new_rows_fm(flat(kv_f, 1)),
                      cq=cq, ck=c[:, :, front:front + past], ck_new=_pad_last(c_new, PAGE_SIZE))
    o_f = _own_head_columns(o_f, ts, FOX_HEADS, 1)
    kvw_all = jnp.concatenate([win_buf, kv_w], axis=1)
    n_tok = bs * ts
    outs = (o_c.reshape(n_tok, -1), o_s.reshape(n_tok, -1), o_w.reshape(n_tok, -1), o_f.reshape(n_tok, -1),
            g_bm.reshape(n_tok, -1))
    return outs, (kv_c, kv_s, kvw_all[:, ts:], kv_f, logf)


def _odd_sample(z, li, cache_kv, page_table, tab):
    bs, n_pages = page_table.shape
    past = n_pages * PAGE_SIZE
    ts = z.shape[0] // bs
    hw = MOBA_HEADS * HEAD_DIM
    z = z.reshape(bs, ts, 3 * hw)
    q = z[..., :hw].reshape(bs, ts, MOBA_HEADS, HEAD_DIM)
    total = past + PAGE_SIZE
    assert ts <= MOBA_BLOCK and past % MOBA_BLOCK == 0
    pages = _pages_fm(cache_kv[li])
    qbd = _block_diag_queries(q * SCALE, 1)
    selb = _moba_pick(qbd, pages, page_table)
    e_blk = jnp.asarray((np.arange(LANES)[:, None] == np.arange(total)[None, :] // MOBA_BLOCK).astype(np.float32), _BF)
    o = _paged_attn(qbd, pages, page_table, _sample_bias(tab[:, :MOBA_HEADS], MOBA_HEADS, past, ts, total),
                    _new_rows_fm(z[..., hw:2 * hw]), _new_rows_fm(z[..., 2 * hw:]), selb=selb, e=e_blk)
    o = _own_head_columns(o, ts, MOBA_HEADS, 1)
    return o.reshape(bs * ts, hw), z[..., hw:].reshape(bs, ts, 2, MOBA_HEADS, HEAD_DIM)


def _even_split(z, b, t):
    q_a, kv_c, kv_s, kv_w, g_a, q_f, kv_f, f_logit = jnp.split(z.reshape(b, t, -1), _EVEN_CUTS, axis=-1)
    kvshape = (b, t, 2, NSA_KV_HEADS, HEAD_DIM)
    g_bm = g_a.reshape(b, t, NSA_HEADS, 3).transpose(0, 1, 3, 2).reshape(b, t, 3 * NSA_HEADS)
    return (q_a.reshape(b, t, NSA_HEADS, HEAD_DIM), kv_c.reshape(kvshape), kv_s.reshape(kvshape),
            kv_w.reshape(kvshape), g_bm, q_f.reshape(b, t, FOX_HEADS, HEAD_DIM),
            kv_f.reshape(b, t, 2, FOX_HEADS, HEAD_DIM), f_logit)


def _even_row_perm():
    cuts = (0,) + _EVEN_CUTS + (sum(_EVEN_SIZES),)
    seg = lambda k: np.arange(cuts[k], cuts[k + 1])
    gates = cuts[4] + (np.arange(NSA_HEADS)[None, :] * 3 + np.arange(3)[:, None]).reshape(-1)
    return np.concatenate([seg(0), seg(1), seg(2), seg(3), seg(5), seg(6), gates, seg(7)])


def _kv_leaf(zf, row0, heads):
    bsz, _, s = zf.shape
    blk = zf[:, row0:row0 + 2 * heads * HEAD_DIM, :].reshape(bsz, 2, heads, HEAD_DIM, s)
    return blk.transpose(0, 4, 1, 2, 3)


def _even_prompt(x_rows, tab, w_in, b_in, cmp_w, w_out, ln_g, ln_b, alpha, tm):
    bsz, s, d = x_rows.shape
    perm = _even_row_perm()
    w_t = w_in.T[perm].astype(_BF)
    scale = np.ones((_EVEN_OUT, 1), np.float32)
    scale[_QA:_QA + NSA_HEADS * HEAD_DIM] = SCALE
    scale[_QF:_QF + FOX_HEADS * HEAD_DIM] = SCALE
    zf, zb = _proj_fm(x_rows, w_t, b_in[perm].reshape(-1, 1), jnp.asarray(scale), tm, True)
    tabn = tab[:, :NSA_HEADS]
    g = NSA_KV_HEADS
    hb = HEAD_DIM

    assert s % SEL_BLOCK == 0
    cmp_tok = _nsa_compress(_chunks_from_fm(zf[:, _KVC:_KVS, :]), *cmp_w)
    ncp = _round_up(cmp_tok.shape[2], LANES)
    cmp_tok = _pad_axis(cmp_tok, 2, ncp).astype(_BF)
    kc = cmp_tok[:g].transpose(1, 0, 2, 3).reshape(bsz * g, ncp, HEAD_DIM)
    vc_t = cmp_tok[g:].transpose(1, 0, 3, 2).reshape(bsz * g, HEAD_DIM, ncp)
    tq, tk = 256, 512
    o_c, sel = _cmp_select_t(zb, kc, vc_t, tab, tq, s // SEL_BLOCK)

    pairs, deltas = _plan_tiles(s // tq, tq, tk, None, True)
    bias = _bias_tiles_t(tabn, deltas, tk, tq, None, NSA_GROUP)
    o_s = _flash_t(zb, _QA // (NSA_GROUP * hb), _KVS // hb, _KVS // hb + g, g, NSA_GROUP, True, bias, pairs, tq, tk,
                   sel=sel[:, None], sel_block=SEL_BLOCK)
    pairs, deltas = _plan_tiles(s // tq, tq, tk, NSA_WINDOW, True)
    bias = _bias_tiles_t(tabn, deltas, tk, tq, NSA_WINDOW, NSA_GROUP)
    o_w = _flash_t(zb, _QA // (NSA_GROUP * hb), _KVW // hb, _KVW // hb + g, g, NSA_GROUP, True, bias, pairs, tq, tk)

    logf_t, _, c3 = _logf_cumsum(zf[:, _FL:_FL + FOX_HEADS, :].reshape(bsz * FOX_HEADS, s), s, bsz * FOX_HEADS)
    qaug, kaug = _fox_aug_rows(c3)
    tqf = tkf = 512
    hps = HEADS_PER_STEP
    units = FOX_HEADS // hps
    pairs, deltas = _plan_tiles(s // tqf, tqf, tkf, None, False)
    bias = _bias_tiles_t(None, deltas, tkf, tqf, None, hps)
    per_unit = lambda a: a.reshape(bsz * units, hps * BF16_ROWS, s)
    o_f = _flash_t(zb, _QF // (hps * hb), _KVF // (hps * hb), _KVF // (hps * hb) + units, units, hps, False,
                   bias, pairs, tqf, tkf, qaug=per_unit(qaug), kaug=per_unit(kaug))

    wide = lambda o: o.reshape(bsz, -1, s)
    x_t = _even_out_fm(alpha, wide(o_c), wide(o_s), wide(o_w), zf, wide(o_f), w_out.T.astype(_BF), x_rows,
                       ln_g, ln_b, tm)
    kv_c = _kv_leaf(zf, _KVC, g)
    kv_s = _kv_leaf(zf, _KVS, g)
    kv_w = _kv_leaf(zf, _KVW, g)
    kv_f = _kv_leaf(zf, _KVF, FOX_HEADS)
    logf = logf_t.reshape(bsz, FOX_HEADS, s).transpose(0, 2, 1)
    return x_t, (kv_c, kv_s, kv_w[:, max(s - NSA_WINDOW, 0):], kv_f, logf)


def _odd_prompt(x_t, tab, w_in, w_out, ln_g, ln_b, alpha, tm):
    bsz, d, s = x_t.shape
    hw = MOBA_HEADS * HEAD_DIM
    scale = np.ones((3 * hw, 1), np.float32)
    scale[:hw] = SCALE
    zf, zb = _proj_fm(x_t, w_in.T.astype(_BF), jnp.zeros((3 * hw, 1), _F32), jnp.asarray(scale), tm, False)
    assert s % MOBA_BLOCK == 0
    sel = _moba_select_t(zf, zb, MOBA_HEADS, 1024 if s % 1024 == 0 else 256)
    tq = tk = 512
    hps = HEADS_PER_STEP
    units = MOBA_HEADS // hps
    pairs, deltas = _plan_tiles(s // tq, tq, tk, None, True)
    bias = _bias_tiles_t(tab[:, :MOBA_HEADS], deltas, tk, tq, None, hps)
    o = _flash_t(zb, 0, units, 2 * units, units, hps, False, bias, pairs, tq, tk,
                 sel=sel.reshape(bsz * units, hps, sel.shape[1], s), sel_block=MOBA_BLOCK)
    x_t = _odd_out_fm(alpha, o.reshape(bsz, hw, s), w_out.T.astype(_BF), x_t, ln_g, ln_b, tm)
    return x_t, _kv_leaf(zf, hw, MOBA_HEADS)


def _kernel_impl(x_prompt, x_sample, cache_nsa_cmp, cache_nsa_sel, state_nsa_win, cache_fox_kv,
                 cache_fox_logf, cache_moba_kv, page_table, rel_bias, ln_g, ln_b, w_in_even, b_in_even,
                 nsa_cmp_pe, nsa_cmp_w1, nsa_cmp_b1, nsa_cmp_w2, nsa_cmp_b2, w_out_even, w_in_odd,
                 w_out_odd, moe_wg, moe_bg, moe_we, moe_be, moe_w1, moe_w3, moe_w2):
    bp, sp, d = x_prompt.shape
    bs, ts, _ = x_sample.shape
    n_pages = page_table.shape[1]
    past = n_pages * PAGE_SIZE
    depth = ln_g.shape[0]
    alpha = (2 * depth) ** 0.25
    ns_tok = bs * ts
    tm_p = 512
    tm_s = ns_tok
    assert sp % 1024 == 0 and depth % 2 == 0
    xp = x_prompt
    xs = x_sample.reshape(ns_tok, d)
    outs = {k: [] for k in ("cmp_p", "cmp_s", "sel_p", "sel_s", "win_p", "win_s", "fkv_p", "fkv_s",
                            "flf_p", "flf_s", "mkv_p", "mkv_s")}
    tq_s = BF16_ROWS

    for layer in range(depth):
        li = layer // 2
        if layer % 2 == 0:
            assert layer == 0
            cmp_w = (nsa_cmp_pe[li], nsa_cmp_w1[li], nsa_cmp_b1[li], nsa_cmp_w2[li], nsa_cmp_b2[li])
            xp, (kv_c, kv_s, kv_w, kv_f, logf) = _even_prompt(
                xp, rel_bias, w_in_even[li], b_in_even[li], cmp_w, w_out_even[li], ln_g[layer, 0], ln_b[layer, 0],
                alpha, tm_p)
            outs["cmp_p"].append(kv_c)
            outs["sel_p"].append(kv_s)
            outs["win_p"].append(kv_w)
            outs["fkv_p"].append(kv_f)
            outs["flf_p"].append(logf)
            w_in = w_in_even[li].astype(_BF)
            w_out = w_out_even[li].astype(_BF)
            z = _linear(xs, w_in, b_in_even[li], tm_s)
            (o_c, o_s, o_w, o_f, g_bm), (kv_c, kv_s, kv_w, kv_f, logf) = _even_sample(
                z, li, cache_nsa_cmp, cache_nsa_sel, state_nsa_win, cache_fox_kv, cache_fox_logf, page_table,
                rel_bias, cmp_w, tq_s)
            xs = _even_out(alpha, o_c, o_s, o_w, g_bm, o_f, w_out, xs, ln_g[layer, 0], ln_b[layer, 0], tm_s)
            outs["cmp_s"].append(kv_c)
            outs["sel_s"].append(kv_s)
            outs["win_s"].append(kv_w)
            outs["fkv_s"].append(kv_f)
            outs["flf_s"].append(logf)
        else:
            hw = MOBA_HEADS * HEAD_DIM
            xp, kv = _odd_prompt(xp, rel_bias, w_in_odd[li], w_out_odd[li], ln_g[layer, 0], ln_b[layer, 0], alpha, tm_p)
            outs["mkv_p"].append(kv)
            w_in = w_in_odd[li].astype(_BF)
            w_out = w_out_odd[li].astype(_BF)
            z = _linear(xs, w_in, jnp.zeros((3 * hw,), _F32), tm_s)
            o, kv = _odd_sample(z, li, cache_moba_kv, page_table, rel_bias)
            xs = _odd_out(alpha, o, w_out, xs, ln_g[layer, 0], ln_b[layer, 0], tm_s)
            outs["mkv_s"].append(kv)
        w1b, w3b, w2b = moe_w1[layer].astype(_BF), moe_w3[layer].astype(_BF), moe_w2[layer].astype(_BF)
        w13_t = jnp.concatenate([w1b.transpose(0, 2, 1), w3b.transpose(0, 2, 1)], axis=1)
        router = (moe_wg[layer], moe_bg[layer], moe_we[layer], moe_be[layer])
        xp = _moe_ln_fm(alpha, xp, *router, w13_t, w2b.transpose(0, 2, 1), ln_g[layer, 1], ln_b[layer, 1], tm_p,
                        layer == depth - 1)
        xs = _moe_ln(alpha, xs, *router, w1b, w3b, w2b, ln_g[layer, 1], ln_b[layer, 1], tm_s)

    st = lambda k: jnp.stack(outs[k])
    return (xp, xs.reshape(bs, ts, d), st("cmp_p"), st("cmp_s"), st("sel_p"), st("sel_s"),
            st("win_p"), st("win_s"), st("fkv_p"), st("fkv_s"), st("flf_p"), st("flf_s"), st("mkv_p"), st("mkv_s"))


def kernel(x_prompt, x_sample, cache_nsa_cmp, cache_nsa_sel, state_nsa_win, cache_fox_kv, cache_fox_logf, cache_moba_kv, page_table, rel_bias, ln_g, ln_b, w_in_even, b_in_even, nsa_cmp_pe, nsa_cmp_w1, nsa_cmp_b1, nsa_cmp_w2, nsa_cmp_b2, w_out_even, w_in_odd, w_out_odd, moe_wg, moe_bg, moe_we, moe_be, moe_w1, moe_w3, moe_w2):
    return _kernel_impl(x_prompt, x_sample, cache_nsa_cmp, cache_nsa_sel, state_nsa_win, cache_fox_kv,
                        cache_fox_logf, cache_moba_kv, page_table, rel_bias, ln_g, ln_b, w_in_even, b_in_even,
                        nsa_cmp_pe, nsa_cmp_w1, nsa_cmp_b1, nsa_cmp_w2, nsa_cmp_b2, w_out_even, w_in_odd,
                        w_out_odd, moe_wg, moe_bg, moe_we, moe_be, moe_w1, moe_w3, moe_w2)
```

```python
import functools
import math

import numpy as np
import jax
import jax.numpy as jnp
from jax import lax
from jax.experimental import pallas as pl
from jax.experimental.pallas import tpu as pltpu

_BF = jnp.bfloat16
_F32 = jnp.float32

HEAD_DIM = 64
NSA_KV_HEADS = 2
NSA_GROUP = 4
NSA_HEADS = NSA_KV_HEADS * NSA_GROUP
FOX_HEADS = 8
MOBA_HEADS = 16
CMP_LEN = 32
CMP_STRIDE = 16
CMP_HID = 128
SEL_BLOCK = 64
CMP_PER_SEL = SEL_BLOCK // CMP_STRIDE
NSA_TOPN = 16
NSA_WINDOW = 512
MOBA_BLOCK = 256
MOBA_TOPK = 3
N_BUCKETS = 32
T5_MAX_DISTANCE = 128
N_GROUPS = 4
EXPERTS_PER_GROUP = 4
N_EXPERTS = N_GROUPS * EXPERTS_PER_GROUP
PAGE_SIZE = 128
SCALE = HEAD_DIM ** -0.5
NEG_INF = -1e30
FORCE_SCORE = 1e4
LN_EPS = 1e-5
LANES = 128
SUBLANES = 8
BF16_ROWS = 16
VMEM_LIMIT = 48 * 1024 * 1024
PAGES_PER_STEP = 8
HEADS_PER_STEP = 4

_QA, _KVC, _KVS, _KVW, _QF, _KVF, _GA, _FL = 0, 512, 768, 1024, 1280, 1792, 2816, 2840
_EVEN_OUT = 2848
_EVEN_SIZES = (NSA_HEADS * HEAD_DIM, 2 * NSA_KV_HEADS * HEAD_DIM, 2 * NSA_KV_HEADS * HEAD_DIM,
               2 * NSA_KV_HEADS * HEAD_DIM, 3 * NSA_HEADS, FOX_HEADS * HEAD_DIM,
               2 * FOX_HEADS * HEAD_DIM, FOX_HEADS)
_EVEN_CUTS = tuple(int(c) for c in np.cumsum(_EVEN_SIZES)[:-1])


def _cparams(*sem):
    return pltpu.CompilerParams(dimension_semantics=sem, vmem_limit_bytes=VMEM_LIMIT)


def _round_up(n, m):
    return (n + m - 1) // m * m


def _split3(x):
    hi = x.astype(_BF)
    r1 = x - hi.astype(_F32)
    mid = r1.astype(_BF)
    lo = (r1 - mid.astype(_F32)).astype(_BF)
    return hi, mid, lo


def _dot3(x, m01):
    hi, mid, lo = _split3(x)
    acc = jnp.dot(hi, m01, preferred_element_type=_F32)
    acc += jnp.dot(mid, m01, preferred_element_type=_F32)
    acc += jnp.dot(lo, m01, preferred_element_type=_F32)
    return acc


def _t5_bucket_np(dist):
    n = np.maximum(dist, 0)
    exact = N_BUCKETS // 2
    nf = np.maximum(n, exact).astype(np.float32)
    far = exact + (np.log(nf / np.float32(exact)) / np.float32(math.log(T5_MAX_DISTANCE / exact))
                   * np.float32(N_BUCKETS - exact)).astype(np.int32)
    return np.where(n < exact, n, np.minimum(far, N_BUCKETS - 1)).astype(np.int32)


def _bucket_thresholds():
    d = np.arange(0, 4 * T5_MAX_DISTANCE)
    b = _t5_bucket_np(d)
    return [int(d[b >= k][0]) for k in range(1, N_BUCKETS)]


_FAR_DISTANCE = _bucket_thresholds()[-1]


def _pad_last(x, width):
    return jnp.pad(x, [(0, 0)] * (x.ndim - 1) + [(0, width - x.shape[-1])])


def _pad_axis(x, axis, size):
    pads = [(0, 0)] * x.ndim
    pads[axis] = (0, size - x.shape[axis])
    return jnp.pad(x, pads)


def _toeplitz(g, rows, cols):
    n = g.shape[-1]
    lead = g.shape[:-1]
    x = jnp.broadcast_to(g[..., None, :], lead + (rows, n)).reshape(lead + (rows * n,))
    return x[..., :rows * (n - 1)].reshape(lead + (rows, n - 1))[..., :cols]


def _distance_values(tabh, d, window):
    valid = (d >= 0) if window is None else ((d >= 0) & (d < window))
    if tabh is None:
        vals = jnp.zeros((1, d.shape[0]), _F32)
    else:
        vals = tabh[jnp.asarray(_t5_bucket_np(d))].T
    return jnp.where(jnp.asarray(valid)[None], vals, NEG_INF).astype(_F32)


def _bias_tile(tabh, d0, rows, cols, window=None):
    n = rows + cols
    m = np.arange(n)
    d = np.where(m < cols, d0 - m, d0 + n - m)
    return _toeplitz(_distance_values(tabh, d, window), rows, cols)


def _bias_tiles_stacked_t(tabh, deltas, tk, tq, window=None):
    n = tk + tq
    m = np.arange(n)
    d = np.concatenate([np.where(m < tq, dl + m, dl + m - n) for dl in deltas])
    vals = _distance_values(tabh, d, window)
    return _toeplitz(vals.reshape(vals.shape[0], len(deltas), n), tk, tq)


def _plan_tiles(nq, tq, tk, window, has_table):
    deltas, pairs = [], []
    for qi in range(nq):
        q0 = qi * tq
        k_hi = (q0 + tq - 1) // tk
        k_lo = 0 if window is None else max(0, (q0 - (window - 1)) // tk)
        for ki in range(k_lo, k_hi + 1):
            delta = q0 - ki * tk
            dmin, dmax = delta - (tk - 1), delta + tq - 1
            plain = dmin >= (_FAR_DISTANCE if has_table else 0) and (window is None or dmax < window)
            if plain:
                pairs.append((qi, ki, -1))
            else:
                if delta not in deltas:
                    deltas.append(delta)
                pairs.append((qi, ki, deltas.index(delta)))
    pairs = [(q, k, b if b >= 0 else len(deltas)) for q, k, b in pairs]
    return pairs, deltas


def _bias_tiles_t(tabh, deltas, tk, tq, window, group, minus_far=False):
    far = tk + tq + _FAR_DISTANCE
    t = _bias_tiles_stacked_t(tabh, list(deltas) + [far], tk, tq, window)
    if minus_far:
        t = t - t[:, -1:, :1, :1]
    if tabh is None:
        t = jnp.broadcast_to(t, (group,) + t.shape[1:])
    h, nb = t.shape[:2]
    t = t.reshape(h // group, group, nb, tk, tq).transpose(0, 2, 3, 1, 4)
    return t.reshape(h // group, nb, tk, group * tq)


def _layer_norm_cols(y, g, b):
    mu = jnp.mean(y, axis=0, keepdims=True)
    yc = y - mu
    var = jnp.mean(yc * yc, axis=0, keepdims=True)
    return yc * lax.rsqrt(var + LN_EPS) * g + b


def _proj_fm_kernel(row_major_in, x_ref, w_ref, b_ref, sc_ref, zf_ref, zb_ref):
    x = x_ref[0].astype(_BF)
    if row_major_in:
        z = lax.dot_general(w_ref[...], x, (((1,), (1,)), ((), ())), preferred_element_type=_F32)
    else:
        z = jnp.dot(w_ref[...], x, preferred_element_type=_F32)
    z = z + b_ref[...]
    zf_ref[0] = z
    zb_ref[0] = (z * sc_ref[...]).astype(_BF)


def _proj_fm(x, w_t_bf, b_col, scale_col, tm, row_major_in):
    bsz = x.shape[0]
    s = x.shape[1] if row_major_in else x.shape[2]
    d = x.shape[2] if row_major_in else x.shape[1]
    n = w_t_bf.shape[0]
    x_spec = (pl.BlockSpec((1, tm, d), lambda b, i: (b, i, 0)) if row_major_in
              else pl.BlockSpec((1, d, tm), lambda b, i: (b, 0, i)))
    col = pl.BlockSpec((n, 1), lambda b, i: (0, 0))
    out = pl.BlockSpec((1, n, tm), lambda b, i: (b, 0, i))
    return pl.pallas_call(
        functools.partial(_proj_fm_kernel, row_major_in),
        grid=(bsz, s // tm),
        in_specs=[x_spec, pl.BlockSpec((n, d), lambda b, i: (0, 0)), col, col],
        out_specs=[out, out],
        out_shape=[jax.ShapeDtypeStruct((bsz, n, s), _F32), jax.ShapeDtypeStruct((bsz, n, s), _BF)],
        compiler_params=_cparams("parallel", "parallel"), name="proj_fm",
    )(x, w_t_bf, b_col, scale_col)


def _even_out_fm_kernel(alpha, oc_ref, os_ref, ow_ref, gz_ref, of_ref, wo_ref, x_ref, g_ref, b_ref, y_ref):
    wa = NSA_HEADS * HEAD_DIM
    sg = jax.nn.sigmoid(gz_ref[0])
    parts = []
    for h in range(NSA_HEADS):
        sl = slice(h * HEAD_DIM, (h + 1) * HEAD_DIM)
        parts.append(sg[h:h + 1] * oc_ref[0, sl, :] + sg[NSA_HEADS + h:NSA_HEADS + h + 1] * os_ref[0, sl, :]
                     + sg[2 * NSA_HEADS + h:2 * NSA_HEADS + h + 1] * ow_ref[0, sl, :])
    o_a = jnp.concatenate(parts, axis=0)
    m = jnp.dot(wo_ref[:, 0:wa], o_a.astype(_BF), preferred_element_type=_F32)
    m += jnp.dot(wo_ref[:, wa:], of_ref[0].astype(_BF), preferred_element_type=_F32)
    y_ref[0] = _layer_norm_cols(alpha * x_ref[0].T + m, g_ref[...], b_ref[...])


def _even_out_fm(alpha, o_c, o_s, o_w, zf, o_f, w_out_t_bf, x_rows, g, b, tm):
    bsz, s, d = x_rows.shape
    wa = NSA_HEADS * HEAD_DIM
    gate_rows = _EVEN_OUT - _GA
    blk = lambda rows: pl.BlockSpec((1, rows, tm), lambda b_, i: (b_, 0, i))
    col = pl.BlockSpec((d, 1), lambda b_, i: (0, 0))
    return pl.pallas_call(
        functools.partial(_even_out_fm_kernel, alpha),
        grid=(bsz, s // tm),
        in_specs=[blk(wa), blk(wa), blk(wa),
                  pl.BlockSpec((1, gate_rows, tm), lambda b_, i: (b_, _GA // gate_rows, i)),
                  blk(FOX_HEADS * HEAD_DIM),
                  pl.BlockSpec(w_out_t_bf.shape, lambda b_, i: (0, 0)),
                  pl.BlockSpec((1, tm, d), lambda b_, i: (b_, i, 0)), col, col],
        out_specs=blk(d),
        out_shape=jax.ShapeDtypeStruct((bsz, d, s), _F32),
        compiler_params=_cparams("parallel", "parallel"), name="even_out_ln_fm",
    )(o_c, o_s, o_w, zf, o_f, w_out_t_bf, x_rows, g.reshape(d, 1), b.reshape(d, 1))


def _odd_out_fm_kernel(alpha, o_ref, wo_ref, x_ref, g_ref, b_ref, y_ref):
    m = jnp.dot(wo_ref[...], o_ref[0].astype(_BF), preferred_element_type=_F32)
    y_ref[0] = _layer_norm_cols(alpha * x_ref[0] + m, g_ref[...], b_ref[...])


def _odd_out_fm(alpha, o, w_out_t_bf, x_t, g, b, tm):
    bsz, d, s = x_t.shape
    blk = lambda rows: pl.BlockSpec((1, rows, tm), lambda b_, i: (b_, 0, i))
    col = pl.BlockSpec((d, 1), lambda b_, i: (0, 0))
    return pl.pallas_call(
        functools.partial(_odd_out_fm_kernel, alpha),
        grid=(bsz, s // tm),
        in_specs=[blk(o.shape[1]), pl.BlockSpec(w_out_t_bf.shape, lambda b_, i: (0, 0)), blk(d), col, col],
        out_specs=blk(d), out_shape=jax.ShapeDtypeStruct((bsz, d, s), _F32),
        compiler_params=_cparams("parallel", "parallel"), name="odd_out_ln_fm",
    )(o, w_out_t_bf, x_t, g.reshape(d, 1), b.reshape(d, 1))


def _route_gates(logits, axis):
    idx = lax.broadcasted_iota(jnp.int32, logits.shape, axis)
    big = jnp.int32(1 << 20)
    red = lambda f, v: f(v, axis=axis, keepdims=True)
    is_g = idx < N_GROUPS
    lg = jnp.where(is_g, logits, NEG_INF)
    mg = red(jnp.max, lg)
    sg = red(jnp.sum, jnp.where(is_g, jnp.exp(lg - mg), 0.0))
    p_top = 1.0 / sg
    g_top = red(jnp.min, jnp.where(lg == mg, idx, big))
    lo = N_GROUPS + EXPERTS_PER_GROUP * g_top
    in_grp = (idx >= lo) & (idx < lo + EXPERTS_PER_GROUP)
    le = jnp.where(in_grp, logits, NEG_INF)
    me = red(jnp.max, le)
    se = red(jnp.sum, jnp.where(in_grp, jnp.exp(le - me), 0.0))
    i1 = red(jnp.min, jnp.where(le == me, idx, big))
    le2 = jnp.where(idx == i1, NEG_INF, le)
    m2 = red(jnp.max, le2)
    i2 = red(jnp.min, jnp.where(le2 == m2, idx, big))
    w1 = 1.0 / se
    w2 = jnp.exp(m2 - me) / se
    tot = w1 + w2
    return jnp.where(idx == i1, p_top * w1 / tot, jnp.where(idx == i2, p_top * w2 / tot, 0.0))


def _moe_fm_kernel(alpha, rows_out, x_ref, wr_ref, br_ref, w13_ref, w2_ref, g_ref, b_ref, y_ref,
                   gate_ref, acc_ref, xb_ref):
    e = pl.program_id(2)
    f = w13_ref.shape[1] // 2

    @pl.when(e == 0)
    def _route():
        xb_ref[...] = x_ref[0].astype(_BF)
        logits = jnp.dot(wr_ref[...], xb_ref[...], preferred_element_type=_F32) + br_ref[...]
        gate_ref[...] = _route_gates(logits, 0)
        acc_ref[...] = jnp.zeros_like(acc_ref)

    h13 = jnp.dot(w13_ref[0], xb_ref[...], preferred_element_type=_F32)
    ge = gate_ref[pl.ds(e + N_GROUPS, 1), :]
    h = (jax.nn.silu(h13[0:f]) * h13[f:2 * f]) * ge
    acc_ref[...] += jnp.dot(w2_ref[0], h.astype(_BF), preferred_element_type=_F32)

    @pl.when(e == N_EXPERTS - 1)
    def _finish():
        y = _layer_norm_cols(alpha * x_ref[0] + acc_ref[...], g_ref[...], b_ref[...])
        y_ref[0] = y.T if rows_out else y


def _moe_ln_fm(alpha, x_t, wg, bg, we, be, w13_t_bf, w2_t_bf, g, b, tm, rows_out):
    bsz, d, s = x_t.shape
    rr = 2 * BF16_ROWS
    wr = jnp.zeros((rr, d), _F32).at[:N_GROUPS].set(wg.T).at[N_GROUPS:N_GROUPS + N_EXPERTS].set(we.T).astype(_BF)
    br = jnp.zeros((rr, 1), _F32).at[:N_GROUPS, 0].set(bg).at[N_GROUPS:N_GROUPS + N_EXPERTS, 0].set(be)
    f2 = w13_t_bf.shape[1]
    col = pl.BlockSpec((d, 1), lambda b_, i, e: (0, 0))
    out_spec = (pl.BlockSpec((1, tm, d), lambda b_, i, e: (b_, i, 0)) if rows_out
                else pl.BlockSpec((1, d, tm), lambda b_, i, e: (b_, 0, i)))
    out_shape = jax.ShapeDtypeStruct((bsz, s, d) if rows_out else (bsz, d, s), _F32)
    return pl.pallas_call(
        functools.partial(_moe_fm_kernel, alpha, rows_out),
        grid=(bsz, s // tm, N_EXPERTS),
        in_specs=[pl.BlockSpec((1, d, tm), lambda b_, i, e: (b_, 0, i)),
                  pl.BlockSpec((rr, d), lambda b_, i, e: (0, 0)),
                  pl.BlockSpec((rr, 1), lambda b_, i, e: (0, 0)),
                  pl.BlockSpec((1, f2, d), lambda b_, i, e: (e, 0, 0)),
                  pl.BlockSpec((1, d, f2 // 2), lambda b_, i, e: (e, 0, 0)),
                  col, col],
        out_specs=out_spec, out_shape=out_shape,
        scratch_shapes=[pltpu.VMEM((rr, tm), _F32), pltpu.VMEM((d, tm), _F32), pltpu.VMEM((d, tm), _BF)],
        compiler_params=_cparams("parallel", "parallel", "arbitrary"), name="moe_ln_fm",
    )(x_t, wr, br, w13_t_bf, w2_t_bf, g.reshape(d, 1), b.reshape(d, 1))


def _flash_t_kernel(group, kv_shared, aug, sel_rows, sel_block, skip_plain, plain_tile,
                    qi_ref, ki_ref, bi_ref, first_ref, last_ref, *refs):
    refs = list(refs)
    q_ref, k_ref, v_ref, bias_ref = refs[:4]
    pos = 4
    if aug:
        qaug_ref, kaug_ref = refs[pos:pos + 2]
        pos += 2
    if sel_rows:
        sel_ref = refs[pos]
        pos += 1
    o_ref, m_ref, l_ref, acc_ref, s_ref = refs[pos:pos + 5]
    step = pl.program_id(1)
    tq = q_ref.shape[2]

    @pl.when(first_ref[step] == 1)
    def _init():
        m_ref[...] = jnp.full_like(m_ref, NEG_INF)
        l_ref[...] = jnp.zeros_like(l_ref)
        acc_ref[...] = jnp.zeros_like(acc_ref)

    m_all = m_ref[...]
    l_all = l_ref[...]
    acc_all = acc_ref[...]
    if sel_rows:
        off = (ki_ref[step] * sel_rows) % SUBLANES
    def scores(r):
        rows = slice(r * HEAD_DIM, (r + 1) * HEAD_DIM)
        q_t = q_ref[0, rows, :]
        k_t = k_ref[0, slice(0, HEAD_DIM) if kv_shared else rows, :]
        if aug:
            arows = slice(r * BF16_ROWS, (r + 1) * BF16_ROWS)
            q_t = jnp.concatenate([q_t, qaug_ref[0, arows, :]], axis=0)
            k_t = jnp.concatenate([k_t, kaug_ref[0, arows, :]], axis=0)
        return lax.dot_general(k_t, q_t, (((0,), (0,)), ((), ())), preferred_element_type=_F32)

    for r in range(group):
        s_ref[r] = scores(r)

    def absorb(with_bias):
        m_out, l_out, acc_out = [], [], []
        for r in range(group):
            rows = slice(r * HEAD_DIM, (r + 1) * HEAD_DIM)
            kv_rows = slice(0, HEAD_DIM) if kv_shared else rows
            s = s_ref[r]
            if with_bias:
                s = s + bias_ref[0, 0, :, r * tq:(r + 1) * tq]
            if sel_rows:
                sr = 0 if kv_shared else r
                s = s + jnp.concatenate(
                    [jnp.broadcast_to(sel_ref[0, sr, pl.ds(off + j, 1), :], (sel_block, tq))
                     for j in range(sel_rows)], axis=0)
            m_old = m_all[r:r + 1, :]
            m_new = jnp.maximum(m_old, jnp.max(s, axis=0, keepdims=True))
            a = jnp.exp(m_old - m_new)
            p = jnp.exp(s - m_new)
            l_out.append(a * l_all[r:r + 1, :] + jnp.sum(p, axis=0, keepdims=True))
            acc_out.append(a * acc_all[rows, :]
                           + jnp.dot(v_ref[0, kv_rows, :], p.astype(_BF), preferred_element_type=_F32))
            m_out.append(m_new)
        pad = [jnp.zeros((SUBLANES - group, tq), _F32)] if group < SUBLANES else []
        m_ref[...] = jnp.concatenate(m_out + pad, axis=0)
        l_ref[...] = jnp.concatenate(l_out + pad, axis=0)
        acc_ref[...] = jnp.concatenate(acc_out, axis=0)

    if skip_plain:
        pl.when(bi_ref[step] != plain_tile)(lambda: absorb(True))
        pl.when(bi_ref[step] == plain_tile)(lambda: absorb(False))
    else:
        absorb(True)

    @pl.when(last_ref[step] == 1)
    def _done():
        for r in range(group):
            rows = slice(r * HEAD_DIM, (r + 1) * HEAD_DIM)
            o_ref[0, rows, :] = acc_ref[rows, :] / jnp.maximum(l_ref[r:r + 1, :], 1e-30)


def _flash_t(zb, q_blk, k_blk, v_blk, units, group, kv_shared, bias, pairs, tq, tk,
             qaug=None, kaug=None, sel=None, sel_block=0, last_tile_is_zero=False):
    bsz, _, s = zb.shape
    bh = bsz * units
    hb = bias.shape[0]
    aug = qaug is not None
    sel_rows = 0 if sel is None else tk // sel_block
    pairs = np.asarray(pairs, np.int32)
    qi, ki, bi = pairs[:, 0], pairs[:, 1], pairs[:, 2]
    first = np.concatenate([[1], (qi[1:] != qi[:-1]).astype(np.int32)]).astype(np.int32)
    last = np.concatenate([(qi[1:] != qi[:-1]).astype(np.int32), [1]]).astype(np.int32)
    gq = group * HEAD_DIM
    kvr = HEAD_DIM if kv_shared else gq
    in_specs = [pl.BlockSpec((1, gq, tq), lambda b, t, qi, ki, bi, f, l: (b // units, q_blk + b % units, qi[t])),
                pl.BlockSpec((1, kvr, tk), lambda b, t, qi, ki, bi, f, l: (b // units, k_blk + b % units, ki[t])),
                pl.BlockSpec((1, kvr, tk), lambda b, t, qi, ki, bi, f, l: (b // units, v_blk + b % units, ki[t])),
                pl.BlockSpec((1, 1, tk, group * tq), lambda b, t, qi, ki, bi, f, l: (b % hb, bi[t], 0, 0))]
    args = [zb, zb, zb, bias]
    if aug:
        in_specs += [pl.BlockSpec((1, group * BF16_ROWS, tq), lambda b, t, qi, ki, bi, f, l: (b, 0, qi[t])),
                     pl.BlockSpec((1, group * BF16_ROWS, tk), lambda b, t, qi, ki, bi, f, l: (b, 0, ki[t]))]
        args += [qaug, kaug]
    if sel_rows:
        assert sel_rows in (1, 2, 4, 8)
        in_specs += [pl.BlockSpec((1, sel.shape[1], SUBLANES, tq),
                                  lambda b, t, qi, ki, bi, f, l: (b, 0, (ki[t] * sel_rows) // SUBLANES, qi[t]))]
        args += [sel]
    grid_spec = pltpu.PrefetchScalarGridSpec(
        num_scalar_prefetch=5, grid=(bh, len(qi)), in_specs=in_specs,
        out_specs=pl.BlockSpec((1, gq, tq), lambda b, t, qi, ki, bi, f, l: (b, 0, qi[t])),
        scratch_shapes=[pltpu.VMEM((SUBLANES, tq), _F32), pltpu.VMEM((SUBLANES, tq), _F32), pltpu.VMEM((gq, tq), _F32),
                        pltpu.VMEM((group, tk, tq), _F32)],
    )
    return pl.pallas_call(
        functools.partial(_flash_t_kernel, group, kv_shared, aug, sel_rows, sel_block, last_tile_is_zero,
                          bias.shape[1] - 1), grid_spec=grid_spec,
        out_shape=jax.ShapeDtypeStruct((bh, gq, s), _F32),
        compiler_params=_cparams("parallel", "arbitrary"), name="flash_t",
    )(jnp.asarray(qi), jnp.asarray(ki), jnp.asarray(bi), jnp.asarray(first), jnp.asarray(last), *args)


def _topk_axis(score, n_sel, keep, axis):
    j = lax.broadcasted_iota(jnp.int32, score.shape, axis)
    big = jnp.int32(1 << 20)
    for _ in range(n_sel):
        mx = jnp.max(score, axis=axis, keepdims=True)
        firsti = jnp.min(jnp.where(score == mx, j, big), axis=axis, keepdims=True)
        hit = j == firsti
        keep = keep | (hit & (mx > 0.5 * NEG_INF))
        score = jnp.where(hit, -3e38, score)
    return keep


def _topk_rows(score, n_sel, keep):
    return _topk_axis(score, n_sel, keep, 0)


def _cmp_select_t_kernel(theta, tq, n_sel, tab_ref, q_ref, kc_ref, vc_ref, msel_ref, o_ref, sel_ref, s_ref):
    g = pl.program_id(0) % NSA_KV_HEADS
    i = pl.program_id(1)
    ncp = kc_ref.shape[1]
    nsl = sel_ref.shape[1]
    n = lax.broadcasted_iota(jnp.int32, (ncp, tq), 0)
    t = i * tq + lax.broadcasted_iota(jnp.int32, (ncp, tq), 1)
    ok = t - (n * CMP_STRIDE + (CMP_LEN - 1)) >= 0
    band = tq // CMP_STRIDE + CMP_STRIDE
    assert (CMP_STRIDE + 1) * CMP_STRIDE - (CMP_LEN - 1) >= _FAR_DISTANCE and band <= ncp
    w0 = pl.multiple_of(jnp.maximum(i * (tq // CMP_STRIDE) - CMP_STRIDE, 0), SUBLANES)
    nw = w0 + lax.broadcasted_iota(jnp.int32, (band, tq), 0)
    tw = i * tq + lax.broadcasted_iota(jnp.int32, (band, tq), 1)
    dw = tw - (nw * CMP_STRIDE + (CMP_LEN - 1))
    ind = [dw >= th for th in theta]
    kc = kc_ref[0]
    vc_t = vc_ref[0]
    imp = jnp.zeros((ncp, tq), _F32)
    for r in range(NSA_GROUP):
        base = (g * NSA_GROUP + r) * N_BUCKETS
        far = tab_ref[base + N_BUCKETS - 1]
        corr = jnp.full((band, tq), tab_ref[base] - far, _F32)
        for k in range(1, N_BUCKETS):
            corr = corr + jnp.where(ind[k - 1], tab_ref[base + k] - tab_ref[base + k - 1], 0.0)
        rows = slice(r * HEAD_DIM, (r + 1) * HEAD_DIM)
        s_ref[...] = jnp.dot(kc, q_ref[0, rows, :], preferred_element_type=_F32) + far
        s_ref[pl.ds(w0, band), :] = s_ref[pl.ds(w0, band), :] + corr
        s = jnp.where(ok, s_ref[...], NEG_INF)
        m = jnp.max(s, axis=0, keepdims=True)
        e = jnp.where(ok, jnp.exp(s - m), 0.0)
        p = e / jnp.maximum(jnp.sum(e, axis=0, keepdims=True), 1e-30)
        o_ref[0, rows, :] = jnp.dot(vc_t, p.astype(_BF), preferred_element_type=_F32)
        imp = imp + p
    hi, mid, lo = _split3(imp)
    msel = msel_ref[...]
    p_s = (jnp.dot(msel, hi, preferred_element_type=_F32) + jnp.dot(msel, mid, preferred_element_type=_F32)
           + jnp.dot(msel, lo, preferred_element_type=_F32))
    j = lax.broadcasted_iota(jnp.int32, (nsl, tq), 0)
    qb = (i * tq + lax.broadcasted_iota(jnp.int32, (nsl, tq), 1)) >> int(math.log2(SEL_BLOCK))
    valid = j <= qb
    forced = (j == 0) | (j == qb) | (j == qb - 1)
    score = jnp.where(valid, jnp.where(forced, FORCE_SCORE, p_s), NEG_INF)
    chosen = _topk_rows(score, n_sel, jnp.zeros((nsl, tq), jnp.bool_))
    sel_ref[0] = jnp.where(chosen, 0.0, NEG_INF)


def _cmp_select_t(zb, kc, vc_t, tab, tq, n_blocks):
    bsz, _, s = zb.shape
    bg = bsz * NSA_KV_HEADS
    ncp = kc.shape[1]
    nsl = _round_up(n_blocks, LANES)
    gq = NSA_GROUP * HEAD_DIM
    jj = np.arange(nsl)[:, None]
    nn = np.arange(ncp)[None, :]
    msel = ((nn >= CMP_PER_SEL * jj - 1) & (nn <= CMP_PER_SEL * jj + CMP_PER_SEL - 1) & (jj < n_blocks))
    msel = jnp.asarray(msel.astype(np.float32), _BF)
    tabf = tab[:, :NSA_HEADS].T.reshape(-1)
    kern = functools.partial(_cmp_select_t_kernel, _bucket_thresholds(), tq, min(NSA_TOPN, n_blocks))
    return pl.pallas_call(
        kern,
        grid=(bg, s // tq),
        in_specs=[pl.BlockSpec(memory_space=pltpu.SMEM),
                  pl.BlockSpec((1, gq, tq), lambda b, i: (b // NSA_KV_HEADS, b % NSA_KV_HEADS, i)),
                  pl.BlockSpec((1, ncp, HEAD_DIM), lambda b, i: (b, 0, 0)),
                  pl.BlockSpec((1, HEAD_DIM, ncp), lambda b, i: (b, 0, 0)),
                  pl.BlockSpec((nsl, ncp), lambda b, i: (0, 0))],
        out_specs=[pl.BlockSpec((1, gq, tq), lambda b, i: (b, 0, i)),
                   pl.BlockSpec((1, nsl, tq), lambda b, i: (b, 0, i))],
        out_shape=[jax.ShapeDtypeStruct((bg, gq, s), _F32), jax.ShapeDtypeStruct((bg, nsl, s), _F32)],
        scratch_shapes=[pltpu.VMEM((ncp, tq), _F32)],
        compiler_params=_cparams("parallel", "parallel"), name="nsa_cmp_select_t",
    )(tabf, zb, kc, vc_t, msel)


def _moba_select_t_kernel(tq, n_blocks, vis_ref, kf_ref, avg_ref, q_ref, sel_ref, km_ref):
    i = pl.program_id(1)
    visible = vis_ref[pl.program_id(0) % MOBA_HEADS]

    @pl.when(i == 0)
    def _means():
        km_ref[...] = _dot3(kf_ref[0], avg_ref[...])

    gate = lax.dot_general(km_ref[...].astype(_BF), q_ref[0], (((0,), (0,)), ((), ())),
                           preferred_element_type=_F32)
    j = lax.broadcasted_iota(jnp.int32, gate.shape, 0)
    qb = (i * tq + lax.broadcasted_iota(jnp.int32, gate.shape, 1)) >> int(math.log2(MOBA_BLOCK))
    score = jnp.where(j < qb, gate, NEG_INF)
    chosen = _topk_rows(score, min(MOBA_TOPK, n_blocks), j == qb)
    sel_ref[0] = jnp.where(chosen, visible, NEG_INF)[0:sel_ref.shape[1]]


def _moba_select_t(zf, zb, k_blk, tq, visible):
    bsz, _, s = zb.shape
    n_blocks = s // MOBA_BLOCK
    assert n_blocks <= LANES
    rows = _round_up(n_blocks, SUBLANES)
    avg = (np.arange(s)[:, None] // MOBA_BLOCK == np.arange(LANES)[None, :]).astype(np.float32) / MOBA_BLOCK
    bh = bsz * MOBA_HEADS
    return pl.pallas_call(
        functools.partial(_moba_select_t_kernel, tq, n_blocks),
        grid=(bh, s // tq),
        in_specs=[pl.BlockSpec(memory_space=pltpu.SMEM),
                  pl.BlockSpec((1, HEAD_DIM, s), lambda b, i: (b // MOBA_HEADS, k_blk + b % MOBA_HEADS, 0)),
                  pl.BlockSpec((s, LANES), lambda b, i: (0, 0)),
                  pl.BlockSpec((1, HEAD_DIM, tq), lambda b, i: (b // MOBA_HEADS, b % MOBA_HEADS, i))],
        out_specs=pl.BlockSpec((1, rows, tq), lambda b, i: (b, 0, i)),
        out_shape=jax.ShapeDtypeStruct((bh, rows, s), _F32),
        scratch_shapes=[pltpu.VMEM((HEAD_DIM, LANES), _F32)],
        compiler_params=_cparams("parallel", "arbitrary"), name="moba_select_t",
    )(visible, zf, jnp.asarray(avg, _BF), zb)


def _linear_kernel(x_ref, w_ref, b_ref, o_ref):
    o_ref[...] = jnp.dot(x_ref[...].astype(_BF), w_ref[...], preferred_element_type=_F32) + b_ref[...]


def _linear(x, w_bf, b, tm):
    m, k = x.shape
    n = w_bf.shape[1]
    return pl.pallas_call(
        _linear_kernel,
        grid=(m // tm,),
        in_specs=[pl.BlockSpec((tm, k), lambda i: (i, 0)),
                  pl.BlockSpec((k, n), lambda i: (0, 0)),
                  pl.BlockSpec((1, n), lambda i: (0, 0))],
        out_specs=pl.BlockSpec((tm, n), lambda i: (i, 0)),
        out_shape=jax.ShapeDtypeStruct((m, n), _F32),
        compiler_params=_cparams("parallel"),
        name="linear",
    )(x, w_bf, b.reshape(1, n))


def _layer_norm_rows(y, g, b):
    mu = jnp.mean(y, axis=-1, keepdims=True)
    yc = y - mu
    var = jnp.mean(yc * yc, axis=-1, keepdims=True)
    return yc * lax.rsqrt(var + LN_EPS) * g + b


def _even_out_kernel(alpha, oc_ref, os_ref, ow_ref, gl_ref, of_ref, ex_ref, wa_ref, wf_ref, x_ref,
                     g_ref, b_ref, y_ref):
    wa = NSA_HEADS * HEAD_DIM
    gexp = _dot3(jax.nn.sigmoid(gl_ref[...]), ex_ref[...])
    o_a = (gexp[:, 0:wa] * oc_ref[...] + gexp[:, wa:2 * wa] * os_ref[...] + gexp[:, 2 * wa:3 * wa] * ow_ref[...])
    m = jnp.dot(o_a.astype(_BF), wa_ref[...], preferred_element_type=_F32)
    m += jnp.dot(of_ref[...].astype(_BF), wf_ref[...], preferred_element_type=_F32)
    y_ref[...] = _layer_norm_rows(alpha * x_ref[...] + m, g_ref[...], b_ref[...])


def _even_out(alpha, o_c, o_s, o_w, gl, o_f, w_out_bf, x, g, b, tm):
    n, d = x.shape
    wa = NSA_HEADS * HEAD_DIM
    wf = FOX_HEADS * HEAD_DIM
    ex_np = np.zeros((LANES, 3 * wa), np.float32)
    for j in range(3):
        for h in range(NSA_HEADS):
            ex_np[j * NSA_HEADS + h, j * wa + h * HEAD_DIM:j * wa + (h + 1) * HEAD_DIM] = 1.0
    ex = jnp.asarray(ex_np, _BF)
    gl = _pad_last(gl, LANES)
    row = lambda w: pl.BlockSpec((tm, w), lambda i: (i, 0))
    full = lambda a: pl.BlockSpec(a.shape, lambda i: (0,) * a.ndim)
    args = (o_c, o_s, o_w, gl, o_f, ex, w_out_bf[:wa], w_out_bf[wa:], x, g.reshape(1, d), b.reshape(1, d))
    specs = [row(wa), row(wa), row(wa), row(LANES), row(wf), full(ex), full(args[6]), full(args[7]),
             row(d), full(args[9]), full(args[10])]
    return pl.pallas_call(
        functools.partial(_even_out_kernel, alpha),
        grid=(n // tm,), in_specs=specs, out_specs=row(d),
        out_shape=jax.ShapeDtypeStruct((n, d), _F32),
        compiler_params=_cparams("parallel"), name="even_out_ln",
    )(*args)


def _odd_out_kernel(alpha, o_ref, w_ref, x_ref, g_ref, b_ref, y_ref):
    m = jnp.dot(o_ref[...].astype(_BF), w_ref[...], preferred_element_type=_F32)
    y_ref[...] = _layer_norm_rows(alpha * x_ref[...] + m, g_ref[...], b_ref[...])


def _odd_out(alpha, o, w_out_bf, x, g, b, tm):
    n, d = x.shape
    row = lambda w: pl.BlockSpec((tm, w), lambda i: (i, 0))
    full = lambda shp: pl.BlockSpec(shp, lambda i: (0,) * len(shp))
    return pl.pallas_call(
        functools.partial(_odd_out_kernel, alpha),
        grid=(n // tm,),
        in_specs=[row(o.shape[1]), full(w_out_bf.shape), row(d), full((1, d)), full((1, d))],
        out_specs=row(d), out_shape=jax.ShapeDtypeStruct((n, d), _F32),
        compiler_params=_cparams("parallel"), name="odd_out_ln",
    )(o, w_out_bf, x, g.reshape(1, d), b.reshape(1, d))


def _moe_kernel(alpha, x_ref, wr_ref, br_ref, w1_ref, w3_ref, w2_ref, g_ref, b_ref, y_ref,
                gate_ref, acc_ref):
    e = pl.program_id(1)
    x = x_ref[...]

    @pl.when(e == 0)
    def _route():
        logits = jnp.dot(x.astype(_BF), wr_ref[...], preferred_element_type=_F32) + br_ref[...]
        gate_ref[...] = _route_gates(logits, 1)
        acc_ref[...] = jnp.zeros_like(acc_ref)

    xb = x.astype(_BF)
    h1 = jnp.dot(xb, w1_ref[0], preferred_element_type=_F32)
    h3 = jnp.dot(xb, w3_ref[0], preferred_element_type=_F32)
    gate = gate_ref[...]
    lane = lax.broadcasted_iota(jnp.int32, gate.shape, 1)
    ge = jnp.sum(jnp.where(lane == e + N_GROUPS, gate, 0.0), axis=-1, keepdims=True)
    h = (jax.nn.silu(h1) * h3) * ge
    acc_ref[...] += jnp.dot(h.astype(_BF), w2_ref[0], preferred_element_type=_F32)

    @pl.when(e == N_EXPERTS - 1)
    def _finish():
        y_ref[...] = _layer_norm_rows(alpha * x + acc_ref[...], g_ref[...], b_ref[...])


def _moe_ln(alpha, x, wg, bg, we, be, w1_bf, w3_bf, w2_bf, g, b, tm):
    n, d = x.shape
    f = w1_bf.shape[2]
    wr = jnp.zeros((d, LANES), _F32).at[:, :N_GROUPS].set(wg).at[:, N_GROUPS:N_GROUPS + N_EXPERTS].set(we)
    wr = wr.astype(_BF)
    br = jnp.zeros((1, LANES), _F32).at[0, :N_GROUPS].set(bg).at[0, N_GROUPS:N_GROUPS + N_EXPERTS].set(be)
    return pl.pallas_call(
        functools.partial(_moe_kernel, alpha),
        grid=(n // tm, N_EXPERTS),
        in_specs=[pl.BlockSpec((tm, d), lambda i, e: (i, 0)),
                  pl.BlockSpec((d, LANES), lambda i, e: (0, 0)),
                  pl.BlockSpec((1, LANES), lambda i, e: (0, 0)),
                  pl.BlockSpec((1, d, f), lambda i, e: (e, 0, 0)),
                  pl.BlockSpec((1, d, f), lambda i, e: (e, 0, 0)),
                  pl.BlockSpec((1, f, d), lambda i, e: (e, 0, 0)),
                  pl.BlockSpec((1, d), lambda i, e: (0, 0)),
                  pl.BlockSpec((1, d), lambda i, e: (0, 0))],
        out_specs=pl.BlockSpec((tm, d), lambda i, e: (i, 0)),
        out_shape=jax.ShapeDtypeStruct((n, d), _F32),
        scratch_shapes=[pltpu.VMEM((tm, LANES), _F32), pltpu.VMEM((tm, d), _F32)],
        compiler_params=_cparams("parallel", "arbitrary"), name="moe_ln",
    )(x, wr, br, w1_bf, w3_bf, w2_bf, g.reshape(1, d), b.reshape(1, d))


def _pages_per_step(n_pages):
    return next(n for n in (PAGES_PER_STEP, 4, 2, 1) if n_pages % n == 0)


def _paged_attn_kernel(n_pg, n_steps, kvh, sel, fox, pt_ref, *refs):
    refs = list(refs)
    qbd_ref = refs[0]
    pages = refs[1:1 + n_pg]
    pos = 1 + n_pg
    bias_ref, bias_new_ref, knew_ref, vnew_ref = refs[pos:pos + 4]
    pos += 4
    if sel:
        selb_ref, e_ref, e_new_ref = refs[pos:pos + 3]
        pos += 3
    if fox:
        cq_ref, ck_ref, ck_new_ref = refs[pos:pos + 3]
        pos += 3
    o_ref, m_ref, l_ref, acc_ref = refs[pos:pos + 4]
    step = pl.program_id(1)
    qbd = qbd_ref[0]
    r, f = qbd.shape
    nt = (((1,), (1,)), ((), ()))

    @pl.when(step == 0)
    def _init():
        m_ref[...] = jnp.full_like(m_ref, NEG_INF)
        l_ref[...] = jnp.zeros_like(l_ref)
        acc_ref[...] = jnp.zeros_like(acc_ref)

    def extra(bias, e, ck):
        add = bias
        if sel:
            add = add + jnp.dot(selb_ref[0], e, preferred_element_type=_F32)
        if fox:
            add = add + cq_ref[0][:, 0:1] - jnp.concatenate([ck] * (r // kvh), axis=0)
        return add

    def absorb(k_list, v_list, add):
        s = jnp.concatenate([jnp.dot(qbd, k, preferred_element_type=_F32) for k in k_list], axis=1) + add
        m_old = m_ref[...]
        m_new = jnp.maximum(m_old, jnp.max(s, axis=1, keepdims=True))
        a = jnp.exp(m_old - m_new)
        p = jnp.exp(s - m_new)
        l_ref[...] = a * l_ref[...] + jnp.sum(p, axis=1, keepdims=True)
        pv = None
        for j, v in enumerate(v_list):
            t = lax.dot_general(p[:, j * PAGE_SIZE:(j + 1) * PAGE_SIZE].astype(_BF), v, nt,
                                preferred_element_type=_F32)
            pv = t if pv is None else pv + t
        acc_ref[...] = a * acc_ref[...] + pv
        m_ref[...] = m_new

    absorb([pg[0, 0].reshape(f, PAGE_SIZE).astype(_BF) for pg in pages],
           [pg[0, 1].reshape(f, PAGE_SIZE).astype(_BF) for pg in pages],
           extra(bias_ref[...], e_ref[...] if sel else None, ck_ref[0] if fox else None))

    @pl.when(step == n_steps - 1)
    def _done():
        absorb([knew_ref[0]], [vnew_ref[0]],
               extra(bias_new_ref[...], e_new_ref[...] if sel else None, ck_new_ref[0] if fox else None))
        o_ref[0] = acc_ref[...] / jnp.maximum(l_ref[...], 1e-30)


def _paged_attn(qbd, pages_t, page_table, bias, knew_t, vnew_t, selb=None, e=None, cq=None, ck=None, ck_new=None):
    bsz, r, f = qbd.shape
    n_pages = page_table.shape[1]
    kvh = pages_t.shape[2]
    length = n_pages * PAGE_SIZE
    n_pg = _pages_per_step(n_pages)
    n_steps = n_pages // n_pg
    sel = selb is not None
    fox = cq is not None
    w = n_pg * PAGE_SIZE

    def page_map(k):
        return lambda b, p, pt: (pt[b * n_pages + p * n_pg + k], 0, 0, 0, 0)

    per_b = lambda shape: pl.BlockSpec((1,) + shape, lambda b, p, pt: (b, 0, 0))
    whole = lambda shape: pl.BlockSpec(shape, lambda b, p, pt: (0, 0))
    in_specs = [per_b((r, f))]
    in_specs += [pl.BlockSpec((1, 2, kvh, HEAD_DIM, PAGE_SIZE), page_map(k)) for k in range(n_pg)]
    in_specs += [pl.BlockSpec((r, w), lambda b, p, pt: (0, p)), whole((r, PAGE_SIZE)),
                 per_b((f, PAGE_SIZE)), per_b((f, PAGE_SIZE))]
    args = [qbd] + [pages_t] * n_pg + [bias[:, :length], bias[:, length:], knew_t, vnew_t]
    if sel:
        nbl = selb.shape[2]
        in_specs += [per_b((r, nbl)), pl.BlockSpec((nbl, w), lambda b, p, pt: (0, p)), whole((nbl, PAGE_SIZE))]
        args += [selb, e[:, :length], e[:, length:]]
    if fox:
        in_specs += [per_b((r, LANES)), pl.BlockSpec((1, kvh, w), lambda b, p, pt: (b, 0, p)), per_b((kvh, PAGE_SIZE))]
        args += [cq, ck, ck_new]
    grid_spec = pltpu.PrefetchScalarGridSpec(
        num_scalar_prefetch=1, grid=(bsz, n_steps), in_specs=in_specs,
        out_specs=per_b((r, f)),
        scratch_shapes=[pltpu.VMEM((r, 1), _F32), pltpu.VMEM((r, 1), _F32), pltpu.VMEM((r, f), _F32)],
    )
    return pl.pallas_call(
        functools.partial(_paged_attn_kernel, n_pg, n_steps, kvh, sel, fox), grid_spec=grid_spec,
        out_shape=jax.ShapeDtypeStruct((bsz, r, f), _F32),
        compiler_params=_cparams("parallel", "arbitrary"), name="paged_attn",
    )(page_table.reshape(-1), *args)


def _moba_pick_kernel(n_pg, n_steps, n_blocks, n_top, pt_ref, qbd_ref, *refs):
    pages = refs[:n_pg]
    selb_ref, km_ref = refs[n_pg], refs[n_pg + 1]
    step = pl.program_id(1)
    f = km_ref.shape[0]

    @pl.when(step == 0)
    def _init():
        km_ref[...] = jnp.zeros_like(km_ref)

    lane = lax.broadcasted_iota(jnp.int32, km_ref.shape, 1)
    km = km_ref[...]
    for k in range(n_pg):
        col = jnp.sum(pages[k][0, 0].reshape(f, PAGE_SIZE), axis=1, keepdims=True)
        blk = (step * n_pg + k) // (MOBA_BLOCK // PAGE_SIZE)
        km = jnp.where(lane == blk, km + col, km)
    km_ref[...] = km

    @pl.when(step == n_steps - 1)
    def _pick():
        means = (km * (1.0 / MOBA_BLOCK)).astype(_BF)
        gate = jnp.dot(qbd_ref[0], means, preferred_element_type=_F32)
        j = lax.broadcasted_iota(jnp.int32, gate.shape, 1)
        score = jnp.where(j < n_blocks, gate, NEG_INF)
        chosen = _topk_axis(score, n_top, j == n_blocks, 1)
        selb_ref[0] = jnp.where(chosen, 0.0, NEG_INF).astype(_BF)


def _moba_pick(qbd, pages_t, page_table):
    bsz, r, f = qbd.shape
    n_pages = page_table.shape[1]
    kvh = pages_t.shape[2]
    n_blocks = n_pages * PAGE_SIZE // MOBA_BLOCK
    assert n_blocks < LANES and (n_pages * PAGE_SIZE) % MOBA_BLOCK == 0
    n_pg = _pages_per_step(n_pages)
    n_steps = n_pages // n_pg

    def page_map(k):
        return lambda b, p, pt: (pt[b * n_pages + p * n_pg + k], 0, 0, 0, 0)

    grid_spec = pltpu.PrefetchScalarGridSpec(
        num_scalar_prefetch=1, grid=(bsz, n_steps),
        in_specs=[pl.BlockSpec((1, r, f), lambda b, p, pt: (b, 0, 0))]
        + [pl.BlockSpec((1, 1, kvh, HEAD_DIM, PAGE_SIZE), page_map(k)) for k in range(n_pg)],
        out_specs=pl.BlockSpec((1, r, LANES), lambda b, p, pt: (b, 0, 0)),
        scratch_shapes=[pltpu.VMEM((f, LANES), _F32)],
    )
    return pl.pallas_call(
        functools.partial(_moba_pick_kernel, n_pg, n_steps, n_blocks, min(MOBA_TOPK, n_blocks)), grid_spec=grid_spec,
        out_shape=jax.ShapeDtypeStruct((bsz, r, LANES), _BF),
        compiler_params=_cparams("parallel", "arbitrary"), name="moba_pick",
    )(page_table.reshape(-1), qbd, *([pages_t] * n_pg))


def _compress_kernel(x_ref, pe_ref, w1a_ref, w1b_ref, b1_ref, w2_ref, b2_ref, o_ref):
    x = x_ref[0, 0]
    pe = pe_ref[0]
    hf = jnp.dot((x + pe[0:1]).astype(_BF), w1a_ref[0], preferred_element_type=_F32)
    hs = jnp.dot((x + pe[1:2]).astype(_BF), w1b_ref[0], preferred_element_type=_F32)
    hs_next = pltpu.roll(hs, hs.shape[0] - 1, 0)
    h = jax.nn.gelu(hf + hs_next + b1_ref[0], approximate=True)
    o_ref[0, 0] = jnp.dot(h.astype(_BF), w2_ref[0], preferred_element_type=_F32) + b2_ref[0]


def _nsa_compress(x, pe, w1, b1, w2, b2):
    sg, bsz, nch, flat = x.shape
    pe2 = pe.reshape(2, CMP_STRIDE, 2, HEAD_DIM).transpose(2, 0, 1, 3).reshape(2, 2, flat)
    w1r = w1.reshape(2, CMP_STRIDE, 2, HEAD_DIM, CMP_HID).transpose(2, 0, 1, 3, 4).reshape(2, 2, flat, CMP_HID)
    w1r = w1r.astype(_BF)
    smap = lambda s, b: (s // NSA_KV_HEADS, 0, 0)
    return pl.pallas_call(
        _compress_kernel,
        grid=(sg, bsz),
        in_specs=[pl.BlockSpec((1, 1, nch, flat), lambda s, b: (s, b, 0, 0)),
                  pl.BlockSpec((1, 2, flat), smap),
                  pl.BlockSpec((1, flat, CMP_HID), smap),
                  pl.BlockSpec((1, flat, CMP_HID), smap),
                  pl.BlockSpec((1, 1, CMP_HID), smap),
                  pl.BlockSpec((1, CMP_HID, HEAD_DIM), smap),
                  pl.BlockSpec((1, 1, HEAD_DIM), smap)],
        out_specs=pl.BlockSpec((1, 1, nch, HEAD_DIM), lambda s, b: (s, b, 0, 0)),
        out_shape=jax.ShapeDtypeStruct((sg, bsz, nch, HEAD_DIM), _F32),
        compiler_params=_cparams("parallel", "parallel"), name="nsa_compress",
    )(x, pe2, w1r[:, 0], w1r[:, 1], b1.reshape(2, 1, CMP_HID), w2.astype(_BF), b2.reshape(2, 1, HEAD_DIM))


def _chunks_from_fm(kv_t):
    bsz, _, length = kv_t.shape
    nch = length // CMP_STRIDE
    sg = 2 * NSA_KV_HEADS
    x = kv_t.reshape(bsz, sg, HEAD_DIM, nch, CMP_STRIDE).transpose(1, 0, 3, 4, 2)
    return x.reshape(sg, bsz, nch, CMP_STRIDE * HEAD_DIM)


def _cmp_select_kernel(theta, tq, qpos0, n_sel, tab_ref, qa_ref, kc_ref, vc_ref, msel_ref, o_ref, selb_ref):
    g = pl.program_id(0) % NSA_KV_HEADS
    i = pl.program_id(1)
    ncp = kc_ref.shape[1]
    nsl = selb_ref.shape[2]
    t = qpos0 + i * tq + lax.broadcasted_iota(jnp.int32, (tq, ncp), 0)
    n = lax.broadcasted_iota(jnp.int32, (tq, ncp), 1)
    d = t - (n * CMP_STRIDE + (CMP_LEN - 1))
    ok = d >= 0
    ind = [d >= th for th in theta]
    kc = kc_ref[0]
    vc = vc_ref[0]
    imp = jnp.zeros((tq, ncp), _F32)
    for r in range(NSA_GROUP):
        base = (g * NSA_GROUP + r) * N_BUCKETS
        bias = jnp.full((tq, ncp), tab_ref[base], _F32)
        for k in range(1, N_BUCKETS):
            bias = bias + jnp.where(ind[k - 1], tab_ref[base + k] - tab_ref[base + k - 1], 0.0)
        s = lax.dot_general(qa_ref[0, r * tq:(r + 1) * tq, :], kc, (((1,), (1,)), ((), ())),
                            preferred_element_type=_F32) + bias
        s = jnp.where(ok, s, NEG_INF)
        m = jnp.max(s, axis=-1, keepdims=True)
        e = jnp.where(ok, jnp.exp(s - m), 0.0)
        p = e / jnp.maximum(jnp.sum(e, axis=-1, keepdims=True), 1e-30)
        o_ref[0, r * tq:(r + 1) * tq, :] = jnp.dot(p.astype(_BF), vc, preferred_element_type=_F32)
        imp = imp + p
    p_s = _dot3(imp, msel_ref[...])
    j = lax.broadcasted_iota(jnp.int32, (tq, nsl), 1)
    qb = (qpos0 + i * tq + lax.broadcasted_iota(jnp.int32, (tq, nsl), 0)) >> int(math.log2(SEL_BLOCK))
    valid = j <= qb
    forced = (j == 0) | (j == qb) | (j == qb - 1)
    score = jnp.where(valid, jnp.where(forced, FORCE_SCORE, p_s), NEG_INF)
    sel = jnp.zeros((tq, nsl), jnp.bool_)
    big = jnp.int32(1 << 20)
    for _ in range(n_sel):
        mx = jnp.max(score, axis=-1, keepdims=True)
        firsti = jnp.min(jnp.where(score == mx, j, big), axis=-1, keepdims=True)
        hit = j == firsti
        sel = sel | (hit & (mx > 0.5 * NEG_INF))
        score = jnp.where(hit, -3e38, score)
    selb_ref[0] = jnp.where(sel, 0.0, NEG_INF).astype(_BF)


def _cmp_select(qa, kc, vc, tab, tq, qpos0, n_blocks):
    bg, rows, _ = qa.shape
    ncp = kc.shape[1]
    nq = rows // (NSA_GROUP * tq)
    nsl = _round_up(n_blocks, LANES)
    nn = np.arange(ncp)[:, None]
    jj = np.arange(nsl)[None, :]
    msel = ((nn >= CMP_PER_SEL * jj - 1) & (nn <= CMP_PER_SEL * jj + CMP_PER_SEL - 1) & (jj < n_blocks))
    msel = jnp.asarray(msel.astype(np.float32), _BF)
    tabf = tab[:, :NSA_HEADS].T.reshape(-1)
    kern = functools.partial(_cmp_select_kernel, _bucket_thresholds(), tq, qpos0, min(NSA_TOPN, n_blocks))
    return pl.pallas_call(
        kern,
        grid=(bg, nq),
        in_specs=[pl.BlockSpec(memory_space=pltpu.SMEM),
                  pl.BlockSpec((1, NSA_GROUP * tq, HEAD_DIM), lambda b, i: (b, i, 0)),
                  pl.BlockSpec((1, ncp, HEAD_DIM), lambda b, i: (b, 0, 0)),
                  pl.BlockSpec((1, ncp, HEAD_DIM), lambda b, i: (b, 0, 0)),
                  pl.BlockSpec((ncp, nsl), lambda b, i: (0, 0))],
        out_specs=[pl.BlockSpec((1, NSA_GROUP * tq, HEAD_DIM), lambda b, i: (b, i, 0)),
                   pl.BlockSpec((1, tq, nsl), lambda b, i: (b, i, 0))],
        out_shape=[jax.ShapeDtypeStruct((bg, rows, HEAD_DIM), _F32),
                   jax.ShapeDtypeStruct((bg, nq * tq, nsl), _BF)],
        compiler_params=_cparams("parallel", "parallel"), name="nsa_cmp_select",
    )(tabf, qa, kc, vc, msel)


def _logf_cumsum_kernel(n_new, x_ref, u_ref, lf_ref, c_ref, hi_ref, mid_ref, lo_ref):
    length = x_ref.shape[1]
    x = x_ref[...]
    col = lax.broadcasted_iota(jnp.int32, x.shape, 1)
    ls = jnp.minimum(x, 0.0) - jnp.log1p(jnp.exp(-jnp.abs(x)))
    lf = jnp.where(col >= length - n_new, ls, x)
    lf_ref[...] = lf
    u = u_ref[...]
    carry = jnp.zeros((x.shape[0], 1), _F32)
    for k in range(length // LANES):
        blk = _dot3(lf[:, k * LANES:(k + 1) * LANES], u) + carry
        c_ref[:, k * LANES:(k + 1) * LANES] = blk
        hi, mid, lo = _split3(blk)
        hi_ref[:, k * LANES:(k + 1) * LANES] = hi
        mid_ref[:, k * LANES:(k + 1) * LANES] = mid
        lo_ref[:, k * LANES:(k + 1) * LANES] = lo
        carry = blk[:, LANES - 1:LANES]


def _logf_cumsum(x, n_new, rows_per_step):
    rows, length = x.shape
    u = jnp.asarray(np.triu(np.ones((LANES, LANES), np.float32)), _BF)
    spec = pl.BlockSpec((rows_per_step, length), lambda i: (i, 0))
    lf, c, hi, mid, lo = pl.pallas_call(
        functools.partial(_logf_cumsum_kernel, n_new),
        grid=(rows // rows_per_step,),
        in_specs=[spec, pl.BlockSpec((LANES, LANES), lambda i: (0, 0))],
        out_specs=[spec] * 5,
        out_shape=[jax.ShapeDtypeStruct((rows, length), _F32)] * 2 + [jax.ShapeDtypeStruct((rows, length), _BF)] * 3,
        compiler_params=_cparams("parallel"), name="fox_logf_cumsum",
    )(x, u)
    return lf, c, (hi, mid, lo)


def _fox_aug_rows(c3):
    one = jnp.ones_like(c3[0])
    zero = jnp.zeros_like(c3[0])
    pad = [zero] * (BF16_ROWS - 6)
    qaug = jnp.stack(list(c3) + [one, one, one] + pad, axis=1)
    kaug = jnp.stack([one, one, one] + [-c for c in c3] + pad, axis=1)
    return qaug, kaug


def _gather_kernel(n_pg, pt_ref, *refs):
    ins, out = refs[:n_pg], refs[n_pg]
    for k in range(n_pg):
        out[0, :, k * PAGE_SIZE:(k + 1) * PAGE_SIZE] = ins[k][0]


def _gather_fm(pool_t, page_table):
    bsz, n_pages = page_table.shape
    f = pool_t.shape[1]
    n_pg = _pages_per_step(n_pages)

    def in_map(k):
        return lambda b, p, pt: (pt[b * n_pages + p * n_pg + k], 0, 0)

    grid_spec = pltpu.PrefetchScalarGridSpec(
        num_scalar_prefetch=1,
        grid=(bsz, n_pages // n_pg),
        in_specs=[pl.BlockSpec((1, f, PAGE_SIZE), in_map(k)) for k in range(n_pg)],
        out_specs=pl.BlockSpec((1, f, n_pg * PAGE_SIZE), lambda b, p, pt: (b, 0, p)),
    )
    return pl.pallas_call(
        functools.partial(_gather_kernel, n_pg), grid_spec=grid_spec,
        out_shape=jax.ShapeDtypeStruct((bsz, f, n_pages * PAGE_SIZE), pool_t.dtype),
        compiler_params=_cparams("parallel", "arbitrary"), name="page_gather",
    )(page_table.reshape(-1), *([pool_t] * n_pg))


def _pages_fm(cache_l):
    return cache_l.transpose(0, 2, 3, 4, 1)


def _kv_group_onehot(heads, group):
    return (np.arange(heads)[:, None] // group == np.arange(heads // group)[None, :]).astype(np.float32)


def _block_diag_queries(q, group):
    b, t, h, dh = q.shape
    oh = jnp.asarray(_kv_group_onehot(h, group))
    x = q[:, :, :, None, :] * oh[None, None, :, :, None]
    return x.reshape(b, t * h, (h // group) * dh).astype(_BF)


def _own_head_columns(o, t, heads, group):
    b = o.shape[0]
    oh = jnp.asarray(_kv_group_onehot(heads, group))
    x = o.reshape(b, t, heads, heads // group, HEAD_DIM) * oh[None, None, :, :, None]
    return jnp.sum(x, axis=3).reshape(b, t, heads * HEAD_DIM)


def _sample_bias(tabh, heads, d0, t, cols, window=None):
    tile = _bias_tile(tabh, d0, t, cols, window)
    tile = jnp.broadcast_to(tile, (heads, t, cols))
    return tile.transpose(1, 0, 2).reshape(t * heads, cols)


def _new_rows_fm(x):
    return _pad_last(x.transpose(0, 2, 1), PAGE_SIZE).astype(_BF)


def _nsa_q_rows(q, tq):
    b, t, _, dh = q.shape
    nq = t // tq
    x = q.reshape(b, nq, tq, NSA_KV_HEADS, NSA_GROUP, dh).transpose(0, 3, 1, 4, 2, 5)
    return x.reshape(b * NSA_KV_HEADS, nq * NSA_GROUP * tq, dh)


def _nsa_rows_back(o, b, t, tq):
    nq = t // tq
    x = o.reshape(b, NSA_KV_HEADS, nq, NSA_GROUP, tq, HEAD_DIM).transpose(0, 2, 4, 1, 3, 5)
    return x.reshape(b, t, NSA_HEADS * HEAD_DIM)


def _even_sample(z, li, cache_cmp, cache_sel, win_state, cache_fkv, cache_flogf, page_table, tab, cmp_w, tq):
    bs, n_pages = page_table.shape
    past = n_pages * PAGE_SIZE
    ts = z.shape[0] // bs
    q_a, kv_c, kv_s, kv_w, g_bm, q_f, kv_f, f_logit = _even_split(z, bs, ts)
    g = NSA_KV_HEADS
    tabn = tab[:, :NSA_HEADS]
    wide = g * HEAD_DIM
    total = past + PAGE_SIZE
    assert past % CMP_STRIDE == 0 and past % SEL_BLOCK == 0 and ts <= CMP_STRIDE

    pool_c = _pages_fm(cache_cmp[li]).reshape(-1, 2 * wide, PAGE_SIZE)
    cmp_tok = _nsa_compress(_chunks_from_fm(_gather_fm(pool_c, page_table)), *cmp_w)
    ncp = _round_up(cmp_tok.shape[2], LANES)
    cmp_tok = _pad_axis(cmp_tok, 2, ncp).astype(_BF)
    kc = cmp_tok[:g].transpose(1, 0, 2, 3).reshape(bs * g, ncp, HEAD_DIM)
    vc = cmp_tok[g:].transpose(1, 0, 2, 3).reshape(bs * g, ncp, HEAD_DIM)
    q_rows = _nsa_q_rows(_pad_axis(q_a * SCALE, 1, tq), tq).astype(_BF)
    o_c, selb = _cmp_select(q_rows, kc, vc, tab, tq, past, total // SEL_BLOCK)
    o_c = _nsa_rows_back(o_c, bs, tq, tq)[:, :ts]
    nsl = selb.shape[2]

    qbd = _block_diag_queries(q_a * SCALE, NSA_GROUP)
    sel_rows = jnp.repeat(selb.reshape(bs, g, tq, nsl)[:, :, :ts].transpose(0, 2, 1, 3), NSA_GROUP, axis=2)
    sel_rows = sel_rows.reshape(bs, ts * NSA_HEADS, nsl)
    e_sel = jnp.asarray((np.arange(nsl)[:, None] == np.arange(total)[None, :] // SEL_BLOCK).astype(np.float32), _BF)
    flat = lambda kv, s: kv[:, :, s].reshape(bs, ts, -1)
    o_s = _paged_attn(qbd, _pages_fm(cache_sel[li]), page_table, _sample_bias(tabn, NSA_HEADS, past, ts, total),
                      _new_rows_fm(flat(kv_s, 0)), _new_rows_fm(flat(kv_s, 1)), selb=sel_rows, e=e_sel)
    win_buf = win_state[li]
    wb = win_buf.shape[1]
    assert wb % PAGE_SIZE == 0
    wpages = _pages_fm(win_buf).reshape(bs, 2, g, HEAD_DIM, wb // PAGE_SIZE, PAGE_SIZE)
    wpages = wpages.transpose(0, 4, 1, 2, 3, 5).reshape(bs * (wb // PAGE_SIZE), 2, g, HEAD_DIM, PAGE_SIZE)
    wtable = jnp.arange(bs * (wb // PAGE_SIZE), dtype=jnp.int32).reshape(bs, wb // PAGE_SIZE)
    o_w = _paged_attn(qbd, wpages, wtable, _sample_bias(tabn, NSA_HEADS, wb, ts, wb + PAGE_SIZE, NSA_WINDOW),
                      _new_rows_fm(flat(kv_w, 0)), _new_rows_fm(flat(kv_w, 1)))
    o_s = _own_head_columns(o_s, ts, NSA_HEADS, NSA_GROUP)
    o_w = _own_head_columns(o_w, ts, NSA_HEADS, NSA_GROUP)

    past_l = _gather_fm(cache_flogf[li].transpose(0, 2, 1), page_table)
    lf_len = _round_up(past + ts, LANES)
    front = lf_len - past - ts
    fl_all = jnp.concatenate([jnp.zeros((bs, FOX_HEADS, front), _F32), past_l, f_logit.transpose(0, 2, 1)], axis=-1)
    logf_t, c, _ = _logf_cumsum(fl_all.reshape(bs * FOX_HEADS, lf_len), ts, min(bs * FOX_HEADS, 64))
    c = c.reshape(bs, FOX_HEADS, lf_len)
    logf = logf_t[:, lf_len - ts:].reshape(bs, FOX_HEADS, ts).transpose(0, 2, 1)
    c_new = c[:, :, front + past:]
    cq = jnp.broadcast_to(c_new.transpose(0, 2, 1).reshape(bs, ts * FOX_HEADS, 1), (bs, ts * FOX_HEADS, LANES))
    o_f = _paged_attn(_block_diag_queries(q_f * SCALE, 1), _pages_fm(cache_fkv[li]), page_table,
                      _sample_bias(None, FOX_HEADS, past, ts, total),
                      _new_rows_fm(flat(kv_f, 0)), _new_rows_fm(flat(kv_f, 1)),
                      cq=cq, ck=c[:, :, front:front + past], ck_new=_pad_last(c_new, PAGE_SIZE))
    o_f = _own_head_columns(o_f, ts, FOX_HEADS, 1)
    kvw_all = jnp.concatenate([win_buf, kv_w], axis=1)
    n_tok = bs * ts
    outs = (o_c.reshape(n_tok, -1), o_s.reshape(n_tok, -1), o_w.reshape(n_tok, -1), o_f.reshape(n_tok, -1),
            g_bm.reshape(n_tok, -1))
    return outs, (kv_c, kv_s, kvw_all[:, ts:], kv_f, logf)


def _odd_sample(z, li, cache_kv, page_table, tab):
    bs, n_pages = page_table.shape
    past = n_pages * PAGE_SIZE
    ts = z.shape[0] // bs
    hw = MOBA_HEADS * HEAD_DIM
    z = z.reshape(bs, ts, 3 * hw)
    q = z[..., :hw].reshape(bs, ts, MOBA_HEADS, HEAD_DIM)
    total = past + PAGE_SIZE
    assert ts <= MOBA_BLOCK and past % MOBA_BLOCK == 0
    pages = _pages_fm(cache_kv[li])
    qbd = _block_diag_queries(q * SCALE, 1)
    selb = _moba_pick(qbd, pages, page_table)
    e_blk = jnp.asarray((np.arange(LANES)[:, None] == np.arange(total)[None, :] // MOBA_BLOCK).astype(np.float32), _BF)
    o = _paged_attn(qbd, pages, page_table, _sample_bias(tab[:, :MOBA_HEADS], MOBA_HEADS, past, ts, total),
                    _new_rows_fm(z[..., hw:2 * hw]), _new_rows_fm(z[..., 2 * hw:]), selb=selb, e=e_blk)
    o = _own_head_columns(o, ts, MOBA_HEADS, 1)
    return o.reshape(bs * ts, hw), z[..., hw:].reshape(bs, ts, 2, MOBA_HEADS, HEAD_DIM)


def _even_split(z, b, t):
    q_a, kv_c, kv_s, kv_w, g_a, q_f, kv_f, f_logit = jnp.split(z.reshape(b, t, -1), _EVEN_CUTS, axis=-1)
    kvshape = (b, t, 2, NSA_KV_HEADS, HEAD_DIM)
    g_bm = g_a.reshape(b, t, NSA_HEADS, 3).transpose(0, 1, 3, 2).reshape(b, t, 3 * NSA_HEADS)
    return (q_a.reshape(b, t, NSA_HEADS, HEAD_DIM), kv_c.reshape(kvshape), kv_s.reshape(kvshape),
            kv_w.reshape(kvshape), g_bm, q_f.reshape(b, t, FOX_HEADS, HEAD_DIM),
            kv_f.reshape(b, t, 2, FOX_HEADS, HEAD_DIM), f_logit)


def _even_row_perm():
    cuts = (0,) + _EVEN_CUTS + (sum(_EVEN_SIZES),)
    seg = lambda k: np.arange(cuts[k], cuts[k + 1])
    gates = cuts[4] + (np.arange(NSA_HEADS)[None, :] * 3 + np.arange(3)[:, None]).reshape(-1)
    return np.concatenate([seg(0), seg(1), seg(2), seg(3), seg(5), seg(6), gates, seg(7)])


def _kv_leaf(zf, row0, heads):
    bsz, _, s = zf.shape
    blk = zf[:, row0:row0 + 2 * heads * HEAD_DIM, :].reshape(bsz, 2, heads, HEAD_DIM, s)
    return blk.transpose(0, 4, 1, 2, 3)


def _even_prompt(x_rows, tab, w_in, b_in, cmp_w, w_out, ln_g, ln_b, alpha, tm):
    bsz, s, d = x_rows.shape
    perm = _even_row_perm()
    w_t = w_in.T[perm].astype(_BF)
    scale = np.ones((_EVEN_OUT, 1), np.float32)
    scale[_QA:_QA + NSA_HEADS * HEAD_DIM] = SCALE
    scale[_QF:_QF + FOX_HEADS * HEAD_DIM] = SCALE
    zf, zb = _proj_fm(x_rows, w_t, b_in[perm].reshape(-1, 1), jnp.asarray(scale), tm, True)
    tabn = tab[:, :NSA_HEADS]
    g = NSA_KV_HEADS
    hb = HEAD_DIM

    assert s % SEL_BLOCK == 0
    cmp_tok = _nsa_compress(_chunks_from_fm(zf[:, _KVC:_KVS, :]), *cmp_w)
    ncp = _round_up(cmp_tok.shape[2], LANES)
    cmp_tok = _pad_axis(cmp_tok, 2, ncp).astype(_BF)
    kc = cmp_tok[:g].transpose(1, 0, 2, 3).reshape(bsz * g, ncp, HEAD_DIM)
    vc_t = cmp_tok[g:].transpose(1, 0, 3, 2).reshape(bsz * g, HEAD_DIM, ncp)
    tq, tk = 256, 512
    o_c, sel = _cmp_select_t(zb, kc, vc_t, tab, tq, s // SEL_BLOCK)

    pairs, deltas = _plan_tiles(s // tq, tq, tk, None, True)
    bias = _bias_tiles_t(tabn, deltas, tk, tq, None, NSA_GROUP)
    o_s = _flash_t(zb, _QA // (NSA_GROUP * hb), _KVS // hb, _KVS // hb + g, g, NSA_GROUP, True, bias, pairs, tq, tk,
                   sel=sel[:, None], sel_block=SEL_BLOCK)
    pairs, deltas = _plan_tiles(s // tq, tq, tk, NSA_WINDOW, True)
    bias = _bias_tiles_t(tabn, deltas, tk, tq, NSA_WINDOW, NSA_GROUP)
    o_w = _flash_t(zb, _QA // (NSA_GROUP * hb), _KVW // hb, _KVW // hb + g, g, NSA_GROUP, True, bias, pairs, tq, tk)

    logf_t, _, c3 = _logf_cumsum(zf[:, _FL:_FL + FOX_HEADS, :].reshape(bsz * FOX_HEADS, s), s, bsz * FOX_HEADS)
    qaug, kaug = _fox_aug_rows(c3)
    tqf = tkf = 512
    hps = HEADS_PER_STEP
    units = FOX_HEADS // hps
    pairs, deltas = _plan_tiles(s // tqf, tqf, tkf, None, False)
    bias = _bias_tiles_t(None, deltas, tkf, tqf, None, hps)
    per_unit = lambda a: a.reshape(bsz * units, hps * BF16_ROWS, s)
    o_f = _flash_t(zb, _QF // (hps * hb), _KVF // (hps * hb), _KVF // (hps * hb) + units, units, hps, False,
                   bias, pairs, tqf, tkf, qaug=per_unit(qaug), kaug=per_unit(kaug), last_tile_is_zero=True)

    wide = lambda o: o.reshape(bsz, -1, s)
    x_t = _even_out_fm(alpha, wide(o_c), wide(o_s), wide(o_w), zf, wide(o_f), w_out.T.astype(_BF), x_rows,
                       ln_g, ln_b, tm)
    kv_c = _kv_leaf(zf, _KVC, g)
    kv_s = _kv_leaf(zf, _KVS, g)
    kv_w = _kv_leaf(zf, _KVW, g)
    kv_f = _kv_leaf(zf, _KVF, FOX_HEADS)
    logf = logf_t.reshape(bsz, FOX_HEADS, s).transpose(0, 2, 1)
    return x_t, (kv_c, kv_s, kv_w[:, max(s - NSA_WINDOW, 0):], kv_f, logf)


def _odd_prompt(x_t, tab, w_in, w_out, ln_g, ln_b, alpha, tm):
    bsz, d, s = x_t.shape
    hw = MOBA_HEADS * HEAD_DIM
    scale = np.ones((3 * hw, 1), np.float32)
    scale[:hw] = SCALE
    zf, zb = _proj_fm(x_t, w_in.T.astype(_BF), jnp.zeros((3 * hw, 1), _F32), jnp.asarray(scale), tm, False)
    assert s % MOBA_BLOCK == 0
    tabm = tab[:, :MOBA_HEADS]
    sel = _moba_select_t(zf, zb, MOBA_HEADS, 1024 if s % 1024 == 0 else 256, tabm[N_BUCKETS - 1])
    tq = tk = 512
    hps = HEADS_PER_STEP
    units = MOBA_HEADS // hps
    pairs, deltas = _plan_tiles(s // tq, tq, tk, None, True)
    bias = _bias_tiles_t(tabm, deltas, tk, tq, None, hps, minus_far=True)
    o = _flash_t(zb, 0, units, 2 * units, units, hps, False, bias, pairs, tq, tk,
                 sel=sel.reshape(bsz * units, hps, sel.shape[1], s), sel_block=MOBA_BLOCK, last_tile_is_zero=True)
    x_t = _odd_out_fm(alpha, o.reshape(bsz, hw, s), w_out.T.astype(_BF), x_t, ln_g, ln_b, tm)
    return x_t, _kv_leaf(zf, hw, MOBA_HEADS)


def _kernel_impl(x_prompt, x_sample, cache_nsa_cmp, cache_nsa_sel, state_nsa_win, cache_fox_kv,
                 cache_fox_logf, cache_moba_kv, page_table, rel_bias, ln_g, ln_b, w_in_even, b_in_even,
                 nsa_cmp_pe, nsa_cmp_w1, nsa_cmp_b1, nsa_cmp_w2, nsa_cmp_b2, w_out_even, w_in_odd,
                 w_out_odd, moe_wg, moe_bg, moe_we, moe_be, moe_w1, moe_w3, moe_w2):
    bp, sp, d = x_prompt.shape
    bs, ts, _ = x_sample.shape
    n_pages = page_table.shape[1]
    past = n_pages * PAGE_SIZE
    depth = ln_g.shape[0]
    alpha = (2 * depth) ** 0.25
    ns_tok = bs * ts
    tm_p = 512
    tm_s = ns_tok
    assert sp % 1024 == 0 and depth % 2 == 0
    xp = x_prompt
    xs = x_sample.reshape(ns_tok, d)
    outs = {k: [] for k in ("cmp_p", "cmp_s", "sel_p", "sel_s", "win_p", "win_s", "fkv_p", "fkv_s",
                            "flf_p", "flf_s", "mkv_p", "mkv_s")}
    tq_s = BF16_ROWS

    for layer in range(depth):
        li = layer // 2
        if layer % 2 == 0:
            assert layer == 0
            cmp_w = (nsa_cmp_pe[li], nsa_cmp_w1[li], nsa_cmp_b1[li], nsa_cmp_w2[li], nsa_cmp_b2[li])
            xp, (kv_c, kv_s, kv_w, kv_f, logf) = _even_prompt(
                xp, rel_bias, w_in_even[li], b_in_even[li], cmp_w, w_out_even[li], ln_g[layer, 0], ln_b[layer, 0],
                alpha, tm_p)
            outs["cmp_p"].append(kv_c)
            outs["sel_p"].append(kv_s)
            outs["win_p"].append(kv_w)
            outs["fkv_p"].append(kv_f)
            outs["flf_p"].append(logf)
            w_in = w_in_even[li].astype(_BF)
            w_out = w_out_even[li].astype(_BF)
            z = _linear(xs, w_in, b_in_even[li], tm_s)
            (o_c, o_s, o_w, o_f, g_bm), (kv_c, kv_s, kv_w, kv_f, logf) = _even_sample(
                z, li, cache_nsa_cmp, cache_nsa_sel, state_nsa_win, cache_fox_kv, cache_fox_logf, page_table,
                rel_bias, cmp_w, tq_s)
            xs = _even_out(alpha, o_c, o_s, o_w, g_bm, o_f, w_out, xs, ln_g[layer, 0], ln_b[layer, 0], tm_s)
            outs["cmp_s"].append(kv_c)
            outs["sel_s"].append(kv_s)
            outs["win_s"].append(kv_w)
            outs["fkv_s"].append(kv_f)
            outs["flf_s"].append(logf)
        else:
            hw = MOBA_HEADS * HEAD_DIM
            xp, kv = _odd_prompt(xp, rel_bias, w_in_odd[li], w_out_odd[li], ln_g[layer, 0], ln_b[layer, 0], alpha, tm_p)
            outs["mkv_p"].append(kv)
            w_in = w_in_odd[li].astype(_BF)
            w_out = w_out_odd[li].astype(_BF)
            z = _linear(xs, w_in, jnp.zeros((3 * hw,), _F32), tm_s)
            o, kv = _odd_sample(z, li, cache_moba_kv, page_table, rel_bias)
            xs = _odd_out(alpha, o, w_out, xs, ln_g[layer, 0], ln_b[layer, 0], tm_s)
            outs["mkv_s"].append(kv)
        w1b, w3b, w2b = moe_w1[layer].astype(_BF), moe_w3[layer].astype(_BF), moe_w2[layer].astype(_BF)
        w13_t = jnp.concatenate([w1b.transpose(0, 2, 1), w3b.transpose(0, 2, 1)], axis=1)
        router = (moe_wg[layer], moe_bg[layer], moe_we[layer], moe_be[layer])
        xp = _moe_ln_fm(alpha, xp, *router, w13_t, w2b.transpose(0, 2, 1), ln_g[layer, 1], ln_b[layer, 1], tm_p,
                        layer == depth - 1)
        xs = _moe_ln(alpha, xs, *router, w1b, w3b, w2b, ln_g[layer, 1], ln_b[layer, 1], tm_s)

    st = lambda k: jnp.stack(outs[k])
    return (xp, xs.reshape(bs, ts, d), st("cmp_p"), st("cmp_s"), st("sel_p"), st("sel_s"),
            st("win_p"), st("win_s"), st("fkv_p"), st("fkv_s"), st("flf_p"), st("flf_s"), st("mkv_p"), st("mkv_s"))


def kernel(x_prompt, x_sample, cache_nsa_cmp, cache_nsa_sel, state_nsa_win, cache_fox_kv, cache_fox_logf, cache_moba_kv, page_table, rel_bias, ln_g, ln_b, w_in_even, b_in_even, nsa_cmp_pe, nsa_cmp_w1, nsa_cmp_b1, nsa_cmp_w2, nsa_cmp_b2, w_out_even, w_in_odd, w_out_odd, moe_wg, moe_bg, moe_we, moe_be, moe_w1, moe_w3, moe_w2):
    return _kernel_impl(x_prompt, x_sample, cache_nsa_cmp, cache_nsa_sel, state_nsa_win, cache_fox_kv,
                        cache_fox_logf, cache_moba_kv, page_table, rel_bias, ln_g, ln_b, w_in_even, b_in_even,
                        nsa_cmp_pe, nsa_cmp_w1, nsa_cmp_b1, nsa_cmp_w2, nsa_cmp_b2, w_out_even, w_in_odd,
                        w_out_odd, moe_wg, moe_bg, moe_we, moe_be, moe_w1, moe_w3, moe_w2)
```

```python
import functools
import math

import numpy as np
import jax
import jax.numpy as jnp
from jax import lax
from jax.experimental import pallas as pl
from jax.experimental.pallas import tpu as pltpu

_BF = jnp.bfloat16
_F32 = jnp.float32

HEAD_DIM = 64
NSA_KV_HEADS = 2
NSA_GROUP = 4
NSA_HEADS = NSA_KV_HEADS * NSA_GROUP
FOX_HEADS = 8
MOBA_HEADS = 16
CMP_LEN = 32
CMP_STRIDE = 16
CMP_HID = 128
SEL_BLOCK = 64
CMP_PER_SEL = SEL_BLOCK // CMP_STRIDE
NSA_TOPN = 16
NSA_WINDOW = 512
MOBA_BLOCK = 256
MOBA_TOPK = 3
N_BUCKETS = 32
T5_MAX_DISTANCE = 128
N_GROUPS = 4
EXPERTS_PER_GROUP = 4
N_EXPERTS = N_GROUPS * EXPERTS_PER_GROUP
PAGE_SIZE = 128
SCALE = HEAD_DIM ** -0.5
NEG_INF = -1e30
FORCE_SCORE = 1e4
LN_EPS = 1e-5
LANES = 128
SUBLANES = 8
BF16_ROWS = 16
VMEM_LIMIT = 48 * 1024 * 1024
PAGES_PER_STEP = 8
HEADS_PER_STEP = 4

_QA, _KVC, _KVS, _KVW, _QF, _KVF, _GA, _FL = 0, 512, 768, 1024, 1280, 1792, 2816, 2840
_EVEN_OUT = 2848
_EVEN_SIZES = (NSA_HEADS * HEAD_DIM, 2 * NSA_KV_HEADS * HEAD_DIM, 2 * NSA_KV_HEADS * HEAD_DIM,
               2 * NSA_KV_HEADS * HEAD_DIM, 3 * NSA_HEADS, FOX_HEADS * HEAD_DIM,
               2 * FOX_HEADS * HEAD_DIM, FOX_HEADS)
_EVEN_CUTS = tuple(int(c) for c in np.cumsum(_EVEN_SIZES)[:-1])


def _cparams(*sem):
    return pltpu.CompilerParams(dimension_semantics=sem, vmem_limit_bytes=VMEM_LIMIT)


def _round_up(n, m):
    return (n + m - 1) // m * m


def _split3(x):
    hi = x.astype(_BF)
    r1 = x - hi.astype(_F32)
    mid = r1.astype(_BF)
    lo = (r1 - mid.astype(_F32)).astype(_BF)
    return hi, mid, lo


def _dot3(x, m01):
    hi, mid, lo = _split3(x)
    acc = jnp.dot(hi, m01, preferred_element_type=_F32)
    acc += jnp.dot(mid, m01, preferred_element_type=_F32)
    acc += jnp.dot(lo, m01, preferred_element_type=_F32)
    return acc


def _t5_bucket_np(dist):
    n = np.maximum(dist, 0)
    exact = N_BUCKETS // 2
    nf = np.maximum(n, exact).astype(np.float32)
    far = exact + (np.log(nf / np.float32(exact)) / np.float32(math.log(T5_MAX_DISTANCE / exact))
                   * np.float32(N_BUCKETS - exact)).astype(np.int32)
    return np.where(n < exact, n, np.minimum(far, N_BUCKETS - 1)).astype(np.int32)


def _bucket_thresholds():
    d = np.arange(0, 4 * T5_MAX_DISTANCE)
    b = _t5_bucket_np(d)
    return [int(d[b >= k][0]) for k in range(1, N_BUCKETS)]


_FAR_DISTANCE = _bucket_thresholds()[-1]


def _pad_last(x, width):
    return jnp.pad(x, [(0, 0)] * (x.ndim - 1) + [(0, width - x.shape[-1])])


def _pad_axis(x, axis, size):
    pads = [(0, 0)] * x.ndim
    pads[axis] = (0, size - x.shape[axis])
    return jnp.pad(x, pads)


def _toeplitz(g, rows, cols):
    n = g.shape[-1]
    lead = g.shape[:-1]
    x = jnp.broadcast_to(g[..., None, :], lead + (rows, n)).reshape(lead + (rows * n,))
    return x[..., :rows * (n - 1)].reshape(lead + (rows, n - 1))[..., :cols]


def _distance_values(tabh, d, window):
    valid = (d >= 0) if window is None else ((d >= 0) & (d < window))
    if tabh is None:
        vals = jnp.zeros((1, d.shape[0]), _F32)
    else:
        vals = tabh[jnp.asarray(_t5_bucket_np(d))].T
    return jnp.where(jnp.asarray(valid)[None], vals, NEG_INF).astype(_F32)


def _bias_tile(tabh, d0, rows, cols, window=None):
    n = rows + cols
    m = np.arange(n)
    d = np.where(m < cols, d0 - m, d0 + n - m)
    return _toeplitz(_distance_values(tabh, d, window), rows, cols)


def _bias_tile_t(tabh, delta, tk, tq, window=None):
    n = tk + tq
    m = np.arange(n)
    d = np.where(m < tq, delta + m, delta + m - n)
    return _toeplitz(_distance_values(tabh, d, window), tk, tq)


def _plan_tiles(nq, tq, tk, window, has_table):
    deltas, pairs = [], []
    for qi in range(nq):
        q0 = qi * tq
        k_hi = (q0 + tq - 1) // tk
        k_lo = 0 if window is None else max(0, (q0 - (window - 1)) // tk)
        for ki in range(k_lo, k_hi + 1):
            delta = q0 - ki * tk
            dmin, dmax = delta - (tk - 1), delta + tq - 1
            plain = dmin >= (_FAR_DISTANCE if has_table else 0) and (window is None or dmax < window)
            if plain:
                pairs.append((qi, ki, -1))
            else:
                if delta not in deltas:
                    deltas.append(delta)
                pairs.append((qi, ki, deltas.index(delta)))
    pairs = [(q, k, b if b >= 0 else len(deltas)) for q, k, b in pairs]
    return pairs, deltas


def _bias_tiles_t(tabh, deltas, tk, tq, window, group):
    far = tk + tq + _FAR_DISTANCE
    tiles = [_bias_tile_t(tabh, dl, tk, tq, window) for dl in deltas]
    tiles.append(_bias_tile_t(tabh, far, tk, tq, None))
    t = jnp.stack(tiles, axis=1)
    if tabh is None:
        t = jnp.broadcast_to(t, (group,) + t.shape[1:])
    h, nb = t.shape[:2]
    t = t.reshape(h // group, group, nb, tk, tq).transpose(0, 2, 3, 1, 4)
    return t.reshape(h // group, nb, tk, group * tq)


def _layer_norm_cols(y, g, b):
    mu = jnp.mean(y, axis=0, keepdims=True)
    yc = y - mu
    var = jnp.mean(yc * yc, axis=0, keepdims=True)
    return yc * lax.rsqrt(var + LN_EPS) * g + b


def _proj_fm_kernel(row_major_in, x_ref, w_ref, b_ref, sc_ref, zf_ref, zb_ref):
    x = x_ref[0].astype(_BF)
    if row_major_in:
        z = lax.dot_general(w_ref[...], x, (((1,), (1,)), ((), ())), preferred_element_type=_F32)
    else:
        z = jnp.dot(w_ref[...], x, preferred_element_type=_F32)
    z = z + b_ref[...]
    zf_ref[0] = z
    zb_ref[0] = (z * sc_ref[...]).astype(_BF)


def _proj_fm(x, w_t_bf, b_col, scale_col, tm, row_major_in):
    bsz = x.shape[0]
    s = x.shape[1] if row_major_in else x.shape[2]
    d = x.shape[2] if row_major_in else x.shape[1]
    n = w_t_bf.shape[0]
    x_spec = (pl.BlockSpec((1, tm, d), lambda b, i: (b, i, 0)) if row_major_in
              else pl.BlockSpec((1, d, tm), lambda b, i: (b, 0, i)))
    col = pl.BlockSpec((n, 1), lambda b, i: (0, 0))
    out = pl.BlockSpec((1, n, tm), lambda b, i: (b, 0, i))
    return pl.pallas_call(
        functools.partial(_proj_fm_kernel, row_major_in),
        grid=(bsz, s // tm),
        in_specs=[x_spec, pl.BlockSpec((n, d), lambda b, i: (0, 0)), col, col],
        out_specs=[out, out],
        out_shape=[jax.ShapeDtypeStruct((bsz, n, s), _F32), jax.ShapeDtypeStruct((bsz, n, s), _BF)],
        compiler_params=_cparams("parallel", "parallel"), name="proj_fm",
    )(x, w_t_bf, b_col, scale_col)


def _even_out_fm_kernel(alpha, oc_ref, os_ref, ow_ref, gz_ref, of_ref, wo_ref, x_ref, g_ref, b_ref, y_ref):
    wa = NSA_HEADS * HEAD_DIM
    sg = jax.nn.sigmoid(gz_ref[0])
    parts = []
    for h in range(NSA_HEADS):
        sl = slice(h * HEAD_DIM, (h + 1) * HEAD_DIM)
        parts.append(sg[h:h + 1] * oc_ref[0, sl, :] + sg[NSA_HEADS + h:NSA_HEADS + h + 1] * os_ref[0, sl, :]
                     + sg[2 * NSA_HEADS + h:2 * NSA_HEADS + h + 1] * ow_ref[0, sl, :])
    o_a = jnp.concatenate(parts, axis=0)
    m = jnp.dot(wo_ref[:, 0:wa], o_a.astype(_BF), preferred_element_type=_F32)
    m += jnp.dot(wo_ref[:, wa:], of_ref[0].astype(_BF), preferred_element_type=_F32)
    y_ref[0] = _layer_norm_cols(alpha * x_ref[0].T + m, g_ref[...], b_ref[...])


def _even_out_fm(alpha, o_c, o_s, o_w, zf, o_f, w_out_t_bf, x_rows, g, b, tm):
    bsz, s, d = x_rows.shape
    wa = NSA_HEADS * HEAD_DIM
    gate_rows = _EVEN_OUT - _GA
    blk = lambda rows: pl.BlockSpec((1, rows, tm), lambda b_, i: (b_, 0, i))
    col = pl.BlockSpec((d, 1), lambda b_, i: (0, 0))
    return pl.pallas_call(
        functools.partial(_even_out_fm_kernel, alpha),
        grid=(bsz, s // tm),
        in_specs=[blk(wa), blk(wa), blk(wa),
                  pl.BlockSpec((1, gate_rows, tm), lambda b_, i: (b_, _GA // gate_rows, i)),
                  blk(FOX_HEADS * HEAD_DIM),
                  pl.BlockSpec(w_out_t_bf.shape, lambda b_, i: (0, 0)),
                  pl.BlockSpec((1, tm, d), lambda b_, i: (b_, i, 0)), col, col],
        out_specs=blk(d),
        out_shape=jax.ShapeDtypeStruct((bsz, d, s), _F32),
        compiler_params=_cparams("parallel", "parallel"), name="even_out_ln_fm",
    )(o_c, o_s, o_w, zf, o_f, w_out_t_bf, x_rows, g.reshape(d, 1), b.reshape(d, 1))


def _odd_out_fm_kernel(alpha, o_ref, wo_ref, x_ref, g_ref, b_ref, y_ref):
    m = jnp.dot(wo_ref[...], o_ref[0].astype(_BF), preferred_element_type=_F32)
    y_ref[0] = _layer_norm_cols(alpha * x_ref[0] + m, g_ref[...], b_ref[...])


def _odd_out_fm(alpha, o, w_out_t_bf, x_t, g, b, tm):
    bsz, d, s = x_t.shape
    blk = lambda rows: pl.BlockSpec((1, rows, tm), lambda b_, i: (b_, 0, i))
    col = pl.BlockSpec((d, 1), lambda b_, i: (0, 0))
    return pl.pallas_call(
        functools.partial(_odd_out_fm_kernel, alpha),
        grid=(bsz, s // tm),
        in_specs=[blk(o.shape[1]), pl.BlockSpec(w_out_t_bf.shape, lambda b_, i: (0, 0)), blk(d), col, col],
        out_specs=blk(d), out_shape=jax.ShapeDtypeStruct((bsz, d, s), _F32),
        compiler_params=_cparams("parallel", "parallel"), name="odd_out_ln_fm",
    )(o, w_out_t_bf, x_t, g.reshape(d, 1), b.reshape(d, 1))


def _route_gates(logits, axis):
    idx = lax.broadcasted_iota(jnp.int32, logits.shape, axis)
    big = jnp.int32(1 << 20)
    red = lambda f, v: f(v, axis=axis, keepdims=True)
    is_g = idx < N_GROUPS
    lg = jnp.where(is_g, logits, NEG_INF)
    mg = red(jnp.max, lg)
    sg = red(jnp.sum, jnp.where(is_g, jnp.exp(lg - mg), 0.0))
    p_top = 1.0 / sg
    g_top = red(jnp.min, jnp.where(lg == mg, idx, big))
    lo = N_GROUPS + EXPERTS_PER_GROUP * g_top
    in_grp = (idx >= lo) & (idx < lo + EXPERTS_PER_GROUP)
    le = jnp.where(in_grp, logits, NEG_INF)
    me = red(jnp.max, le)
    se = red(jnp.sum, jnp.where(in_grp, jnp.exp(le - me), 0.0))
    i1 = red(jnp.min, jnp.where(le == me, idx, big))
    le2 = jnp.where(idx == i1, NEG_INF, le)
    m2 = red(jnp.max, le2)
    i2 = red(jnp.min, jnp.where(le2 == m2, idx, big))
    w1 = 1.0 / se
    w2 = jnp.exp(m2 - me) / se
    tot = w1 + w2
    return jnp.where(idx == i1, p_top * w1 / tot, jnp.where(idx == i2, p_top * w2 / tot, 0.0))


def _moe_fm_kernel(alpha, rows_out, x_ref, wr_ref, br_ref, w13_ref, w2_ref, g_ref, b_ref, y_ref,
                   gate_ref, acc_ref, xb_ref):
    e = pl.program_id(2)
    f = w13_ref.shape[1] // 2

    @pl.when(e == 0)
    def _route():
        xb_ref[...] = x_ref[0].astype(_BF)
        logits = jnp.dot(wr_ref[...], xb_ref[...], preferred_element_type=_F32) + br_ref[...]
        gate_ref[...] = _route_gates(logits, 0)
        acc_ref[...] = jnp.zeros_like(acc_ref)

    h13 = jnp.dot(w13_ref[0], xb_ref[...], preferred_element_type=_F32)
    ge = gate_ref[pl.ds(e + N_GROUPS, 1), :]
    h = (jax.nn.silu(h13[0:f]) * h13[f:2 * f]) * ge
    acc_ref[...] += jnp.dot(w2_ref[0], h.astype(_BF), preferred_element_type=_F32)

    @pl.when(e == N_EXPERTS - 1)
    def _finish():
        y = _layer_norm_cols(alpha * x_ref[0] + acc_ref[...], g_ref[...], b_ref[...])
        y_ref[0] = y.T if rows_out else y


def _moe_ln_fm(alpha, x_t, wg, bg, we, be, w13_t_bf, w2_t_bf, g, b, tm, rows_out):
    bsz, d, s = x_t.shape
    rr = 2 * BF16_ROWS
    wr = jnp.zeros((rr, d), _F32).at[:N_GROUPS].set(wg.T).at[N_GROUPS:N_GROUPS + N_EXPERTS].set(we.T).astype(_BF)
    br = jnp.zeros((rr, 1), _F32).at[:N_GROUPS, 0].set(bg).at[N_GROUPS:N_GROUPS + N_EXPERTS, 0].set(be)
    f2 = w13_t_bf.shape[1]
    col = pl.BlockSpec((d, 1), lambda b_, i, e: (0, 0))
    out_spec = (pl.BlockSpec((1, tm, d), lambda b_, i, e: (b_, i, 0)) if rows_out
                else pl.BlockSpec((1, d, tm), lambda b_, i, e: (b_, 0, i)))
    out_shape = jax.ShapeDtypeStruct((bsz, s, d) if rows_out else (bsz, d, s), _F32)
    return pl.pallas_call(
        functools.partial(_moe_fm_kernel, alpha, rows_out),
        grid=(bsz, s // tm, N_EXPERTS),
        in_specs=[pl.BlockSpec((1, d, tm), lambda b_, i, e: (b_, 0, i)),
                  pl.BlockSpec((rr, d), lambda b_, i, e: (0, 0)),
                  pl.BlockSpec((rr, 1), lambda b_, i, e: (0, 0)),
                  pl.BlockSpec((1, f2, d), lambda b_, i, e: (e, 0, 0)),
                  pl.BlockSpec((1, d, f2 // 2), lambda b_, i, e: (e, 0, 0)),
                  col, col],
        out_specs=out_spec, out_shape=out_shape,
        scratch_shapes=[pltpu.VMEM((rr, tm), _F32), pltpu.VMEM((d, tm), _F32), pltpu.VMEM((d, tm), _BF)],
        compiler_params=_cparams("parallel", "parallel", "arbitrary"), name="moe_ln_fm",
    )(x_t, wr, br, w13_t_bf, w2_t_bf, g.reshape(d, 1), b.reshape(d, 1))


_VROWS = HEAD_DIM + BF16_ROWS


def _flash_t_kernel(group, kv_shared, qx_heads, kx_global, qi_ref, ki_ref, bi_ref, first_ref, last_ref, *refs):
    refs = list(refs)
    q_ref, k_ref, v_ref, bias_ref = refs[:4]
    pos = 4
    if qx_heads:
        qx_ref, kx_ref = refs[pos:pos + 2]
        pos += 2
        xr = qx_ref.shape[1] // qx_heads
    o_ref, m_ref, acc_ref, s_ref = refs[pos:pos + 4]
    step = pl.program_id(1)
    tq = q_ref.shape[2]
    tk = k_ref.shape[2]

    @pl.when(first_ref[step] == 1)
    def _init():
        m_ref[...] = jnp.full_like(m_ref, NEG_INF)
        acc_ref[...] = jnp.zeros_like(acc_ref)

    for r in range(group):
        rows = slice(r * HEAD_DIM, (r + 1) * HEAD_DIM)
        q_t = q_ref[0, rows, :]
        k_t = k_ref[0, slice(0, HEAD_DIM) if kv_shared else rows, :]
        if qx_heads:
            qr = r if qx_heads > 1 else 0
            q_t = jnp.concatenate([q_t, qx_ref[0, qr * xr:(qr + 1) * xr, :]], axis=0)
            k_t = jnp.concatenate([k_t, kx_ref[...] if kx_global else kx_ref[0, r * xr:(r + 1) * xr, :]], axis=0)
        s_ref[r] = lax.dot_general(k_t, q_t, (((0,), (0,)), ((), ())), preferred_element_type=_F32)

    m_all = m_ref[...]
    acc_all = acc_ref[...]
    ones = jnp.ones((BF16_ROWS, tk), _BF)
    m_out, acc_out = [], []
    for r in range(group):
        kv_rows = slice(0, HEAD_DIM) if kv_shared else slice(r * HEAD_DIM, (r + 1) * HEAD_DIM)
        s = s_ref[r] + bias_ref[0, 0, :, r * tq:(r + 1) * tq]
        m_old = m_all[r:r + 1, :]
        m_new = jnp.maximum(m_old, jnp.max(s, axis=0, keepdims=True))
        a = jnp.exp(m_old - m_new)
        p = jnp.exp(s - m_new).astype(_BF)
        v_t = jnp.concatenate([v_ref[0, kv_rows, :], ones], axis=0)
        acc_out.append(a * acc_all[r * _VROWS:(r + 1) * _VROWS, :] + jnp.dot(v_t, p, preferred_element_type=_F32))
        m_out.append(m_new)
    pad = [jnp.zeros((SUBLANES - group, tq), _F32)] if group < SUBLANES else []
    m_ref[...] = jnp.concatenate(m_out + pad, axis=0)
    acc_ref[...] = jnp.concatenate(acc_out, axis=0)

    @pl.when(last_ref[step] == 1)
    def _done():
        for r in range(group):
            acc = acc_out[r]
            o_ref[0, r * HEAD_DIM:(r + 1) * HEAD_DIM, :] = (
                acc[0:HEAD_DIM] / jnp.maximum(acc[HEAD_DIM:HEAD_DIM + 1], 1e-30))


def _flash_t(zb, q_blk, k_blk, v_blk, units, group, kv_shared, bias, pairs, tq, tk, qx=None, kx=None):
    bsz, _, s = zb.shape
    bh = bsz * units
    hb = bias.shape[0]
    pairs = np.asarray(pairs, np.int32)
    qi, ki, bi = pairs[:, 0], pairs[:, 1], pairs[:, 2]
    first = np.concatenate([[1], (qi[1:] != qi[:-1]).astype(np.int32)]).astype(np.int32)
    last = np.concatenate([(qi[1:] != qi[:-1]).astype(np.int32), [1]]).astype(np.int32)
    gq = group * HEAD_DIM
    kvr = HEAD_DIM if kv_shared else gq
    in_specs = [pl.BlockSpec((1, gq, tq), lambda b, t, qi, ki, bi, f, l: (b // units, q_blk + b % units, qi[t])),
                pl.BlockSpec((1, kvr, tk), lambda b, t, qi, ki, bi, f, l: (b // units, k_blk + b % units, ki[t])),
                pl.BlockSpec((1, kvr, tk), lambda b, t, qi, ki, bi, f, l: (b // units, v_blk + b % units, ki[t])),
                pl.BlockSpec((1, 1, tk, group * tq), lambda b, t, qi, ki, bi, f, l: (b % hb, bi[t], 0, 0))]
    args = [zb, zb, zb, bias]
    qx_heads, kx_global = 0, False
    if qx is not None:
        kx_global = kx.ndim == 2
        xr = kx.shape[0] if kx_global else kx.shape[1] // group
        qx_heads = qx.shape[1] // xr
        assert qx_heads in (1, group) and xr % BF16_ROWS == 0
        in_specs.append(pl.BlockSpec((1, qx_heads * xr, tq), lambda b, t, qi, ki, bi, f, l: (b, 0, qi[t])))
        in_specs.append(pl.BlockSpec((xr, tk), lambda b, t, qi, ki, bi, f, l: (0, ki[t])) if kx_global else
                        pl.BlockSpec((1, group * xr, tk), lambda b, t, qi, ki, bi, f, l: (b, 0, ki[t])))
        args += [qx, kx]
    grid_spec = pltpu.PrefetchScalarGridSpec(
        num_scalar_prefetch=5, grid=(bh, len(qi)), in_specs=in_specs,
        out_specs=pl.BlockSpec((1, gq, tq), lambda b, t, qi, ki, bi, f, l: (b, 0, qi[t])),
        scratch_shapes=[pltpu.VMEM((SUBLANES, tq), _F32), pltpu.VMEM((group * _VROWS, tq), _F32),
                        pltpu.VMEM((group, tk, tq), _F32)],
    )
    return pl.pallas_call(
        functools.partial(_flash_t_kernel, group, kv_shared, qx_heads, kx_global), grid_spec=grid_spec,
        out_shape=jax.ShapeDtypeStruct((bh, gq, s), _F32),
        compiler_params=_cparams("parallel", "arbitrary"), name="flash_t",
    )(jnp.asarray(qi), jnp.asarray(ki), jnp.asarray(bi), jnp.asarray(first), jnp.asarray(last), *args)


def _topk_axis(score, n_sel, keep, axis):
    j = lax.broadcasted_iota(jnp.int32, score.shape, axis)
    big = jnp.int32(1 << 20)
    for _ in range(n_sel):
        mx = jnp.max(score, axis=axis, keepdims=True)
        firsti = jnp.min(jnp.where(score == mx, j, big), axis=axis, keepdims=True)
        hit = j == firsti
        keep = keep | (hit & (mx > 0.5 * NEG_INF))
        score = jnp.where(hit, -3e38, score)
    return keep


def _topk_rows(score, n_sel, keep):
    return _topk_axis(score, n_sel, keep, 0)


def _cmp_select_t_kernel(theta, tq, n_sel, tab_ref, q_ref, kc_ref, vc_ref, msel_ref, o_ref, sel_ref, s_ref):
    g = pl.program_id(0) % NSA_KV_HEADS
    i = pl.program_id(1)
    ncp = kc_ref.shape[1]
    nsl = sel_ref.shape[1]
    n = lax.broadcasted_iota(jnp.int32, (ncp, tq), 0)
    t = i * tq + lax.broadcasted_iota(jnp.int32, (ncp, tq), 1)
    ok = t - (n * CMP_STRIDE + (CMP_LEN - 1)) >= 0
    band = tq // CMP_STRIDE + CMP_STRIDE
    assert (CMP_STRIDE + 1) * CMP_STRIDE - (CMP_LEN - 1) >= _FAR_DISTANCE and band <= ncp
    w0 = pl.multiple_of(jnp.maximum(i * (tq // CMP_STRIDE) - CMP_STRIDE, 0), SUBLANES)
    nw = w0 + lax.broadcasted_iota(jnp.int32, (band, tq), 0)
    tw = i * tq + lax.broadcasted_iota(jnp.int32, (band, tq), 1)
    dw = tw - (nw * CMP_STRIDE + (CMP_LEN - 1))
    ind = [dw >= th for th in theta]
    kc = kc_ref[0]
    vc_t = vc_ref[0]
    imp = jnp.zeros((ncp, tq), _F32)
    for r in range(NSA_GROUP):
        base = (g * NSA_GROUP + r) * N_BUCKETS
        far = tab_ref[base + N_BUCKETS - 1]
        corr = jnp.full((band, tq), tab_ref[base] - far, _F32)
        for k in range(1, N_BUCKETS):
            corr = corr + jnp.where(ind[k - 1], tab_ref[base + k] - tab_ref[base + k - 1], 0.0)
        rows = slice(r * HEAD_DIM, (r + 1) * HEAD_DIM)
        s_ref[...] = jnp.dot(kc, q_ref[0, rows, :], preferred_element_type=_F32) + far
        s_ref[pl.ds(w0, band), :] = s_ref[pl.ds(w0, band), :] + corr
        s = jnp.where(ok, s_ref[...], NEG_INF)
        m = jnp.max(s, axis=0, keepdims=True)
        e = jnp.where(ok, jnp.exp(s - m), 0.0)
        p = e / jnp.maximum(jnp.sum(e, axis=0, keepdims=True), 1e-30)
        o_ref[0, rows, :] = jnp.dot(vc_t, p.astype(_BF), preferred_element_type=_F32)
        imp = imp + p
    hi, mid, lo = _split3(imp)
    msel = msel_ref[...]
    p_s = (jnp.dot(msel, hi, preferred_element_type=_F32) + jnp.dot(msel, mid, preferred_element_type=_F32)
           + jnp.dot(msel, lo, preferred_element_type=_F32))
    j = lax.broadcasted_iota(jnp.int32, (nsl, tq), 0)
    qb = (i * tq + lax.broadcasted_iota(jnp.int32, (nsl, tq), 1)) >> int(math.log2(SEL_BLOCK))
    valid = j <= qb
    forced = (j == 0) | (j == qb) | (j == qb - 1)
    score = jnp.where(valid, jnp.where(forced, FORCE_SCORE, p_s), NEG_INF)
    chosen = _topk_rows(score, n_sel, jnp.zeros((nsl, tq), jnp.bool_))
    sel_ref[0] = jnp.where(chosen, 0.0, NEG_INF).astype(_BF)


def _cmp_select_t(zb, kc, vc_t, tab, tq, n_blocks):
    bsz, _, s = zb.shape
    bg = bsz * NSA_KV_HEADS
    ncp = kc.shape[1]
    nsl = _round_up(n_blocks, LANES)
    gq = NSA_GROUP * HEAD_DIM
    jj = np.arange(nsl)[:, None]
    nn = np.arange(ncp)[None, :]
    msel = ((nn >= CMP_PER_SEL * jj - 1) & (nn <= CMP_PER_SEL * jj + CMP_PER_SEL - 1) & (jj < n_blocks))
    msel = jnp.asarray(msel.astype(np.float32), _BF)
    tabf = tab[:, :NSA_HEADS].T.reshape(-1)
    kern = functools.partial(_cmp_select_t_kernel, _bucket_thresholds(), tq, min(NSA_TOPN, n_blocks))
    return pl.pallas_call(
        kern,
        grid=(bg, s // tq),
        in_specs=[pl.BlockSpec(memory_space=pltpu.SMEM),
                  pl.BlockSpec((1, gq, tq), lambda b, i: (b // NSA_KV_HEADS, b % NSA_KV_HEADS, i)),
                  pl.BlockSpec((1, ncp, HEAD_DIM), lambda b, i: (b, 0, 0)),
                  pl.BlockSpec((1, HEAD_DIM, ncp), lambda b, i: (b, 0, 0)),
                  pl.BlockSpec((nsl, ncp), lambda b, i: (0, 0))],
        out_specs=[pl.BlockSpec((1, gq, tq), lambda b, i: (b, 0, i)),
                   pl.BlockSpec((1, nsl, tq), lambda b, i: (b, 0, i))],
        out_shape=[jax.ShapeDtypeStruct((bg, gq, s), _F32), jax.ShapeDtypeStruct((bg, nsl, s), _BF)],
        scratch_shapes=[pltpu.VMEM((ncp, tq), _F32)],
        compiler_params=_cparams("parallel", "parallel"), name="nsa_cmp_select_t",
    )(tabf, zb, kc, vc_t, msel)


def _moba_select_t_kernel(tq, n_blocks, kf_ref, avg_ref, q_ref, sel_ref, km_ref):
    i = pl.program_id(1)

    @pl.when(i == 0)
    def _means():
        km_ref[...] = _dot3(kf_ref[0], avg_ref[...])

    gate = lax.dot_general(km_ref[...].astype(_BF), q_ref[0], (((0,), (0,)), ((), ())),
                           preferred_element_type=_F32)
    j = lax.broadcasted_iota(jnp.int32, gate.shape, 0)
    qb = (i * tq + lax.broadcasted_iota(jnp.int32, gate.shape, 1)) >> int(math.log2(MOBA_BLOCK))
    score = jnp.where(j < qb, gate, NEG_INF)
    chosen = _topk_rows(score, min(MOBA_TOPK, n_blocks), j == qb)
    sel_ref[0] = jnp.where(chosen, 0.0, NEG_INF)[0:sel_ref.shape[1]].astype(_BF)


def _moba_select_t(zf, zb, k_blk, tq):
    bsz, _, s = zb.shape
    n_blocks = s // MOBA_BLOCK
    assert n_blocks <= LANES
    rows = _round_up(n_blocks, BF16_ROWS)
    avg = (np.arange(s)[:, None] // MOBA_BLOCK == np.arange(LANES)[None, :]).astype(np.float32) / MOBA_BLOCK
    bh = bsz * MOBA_HEADS
    return pl.pallas_call(
        functools.partial(_moba_select_t_kernel, tq, n_blocks),
        grid=(bh, s // tq),
        in_specs=[pl.BlockSpec((1, HEAD_DIM, s), lambda b, i: (b // MOBA_HEADS, k_blk + b % MOBA_HEADS, 0)),
                  pl.BlockSpec((s, LANES), lambda b, i: (0, 0)),
                  pl.BlockSpec((1, HEAD_DIM, tq), lambda b, i: (b // MOBA_HEADS, b % MOBA_HEADS, i))],
        out_specs=pl.BlockSpec((1, rows, tq), lambda b, i: (b, 0, i)),
        out_shape=jax.ShapeDtypeStruct((bh, rows, s), _BF),
        scratch_shapes=[pltpu.VMEM((HEAD_DIM, LANES), _F32)],
        compiler_params=_cparams("parallel", "arbitrary"), name="moba_select_t",
    )(zf, jnp.asarray(avg, _BF), zb)


def _linear_kernel(x_ref, w_ref, b_ref, o_ref):
    o_ref[...] = jnp.dot(x_ref[...].astype(_BF), w_ref[...], preferred_element_type=_F32) + b_ref[...]


def _linear(x, w_bf, b, tm):
    m, k = x.shape
    n = w_bf.shape[1]
    return pl.pallas_call(
        _linear_kernel,
        grid=(m // tm,),
        in_specs=[pl.BlockSpec((tm, k), lambda i: (i, 0)),
                  pl.BlockSpec((k, n), lambda i: (0, 0)),
                  pl.BlockSpec((1, n), lambda i: (0, 0))],
        out_specs=pl.BlockSpec((tm, n), lambda i: (i, 0)),
        out_shape=jax.ShapeDtypeStruct((m, n), _F32),
        compiler_params=_cparams("parallel"),
        name="linear",
    )(x, w_bf, b.reshape(1, n))


def _layer_norm_rows(y, g, b):
    mu = jnp.mean(y, axis=-1, keepdims=True)
    yc = y - mu
    var = jnp.mean(yc * yc, axis=-1, keepdims=True)
    return yc * lax.rsqrt(var + LN_EPS) * g + b


def _even_out_kernel(alpha, oc_ref, os_ref, ow_ref, gl_ref, of_ref, ex_ref, wa_ref, wf_ref, x_ref,
                     g_ref, b_ref, y_ref):
    wa = NSA_HEADS * HEAD_DIM
    gexp = _dot3(jax.nn.sigmoid(gl_ref[...]), ex_ref[...])
    o_a = (gexp[:, 0:wa] * oc_ref[...] + gexp[:, wa:2 * wa] * os_ref[...] + gexp[:, 2 * wa:3 * wa] * ow_ref[...])
    m = jnp.dot(o_a.astype(_BF), wa_ref[...], preferred_element_type=_F32)
    m += jnp.dot(of_ref[...].astype(_BF), wf_ref[...], preferred_element_type=_F32)
    y_ref[...] = _layer_norm_rows(alpha * x_ref[...] + m, g_ref[...], b_ref[...])


def _even_out(alpha, o_c, o_s, o_w, gl, o_f, w_out_bf, x, g, b, tm):
    n, d = x.shape
    wa = NSA_HEADS * HEAD_DIM
    wf = FOX_HEADS * HEAD_DIM
    ex_np = np.zeros((LANES, 3 * wa), np.float32)
    for j in range(3):
        for h in range(NSA_HEADS):
            ex_np[j * NSA_HEADS + h, j * wa + h * HEAD_DIM:j * wa + (h + 1) * HEAD_DIM] = 1.0
    ex = jnp.asarray(ex_np, _BF)
    gl = _pad_last(gl, LANES)
    row = lambda w: pl.BlockSpec((tm, w), lambda i: (i, 0))
    full = lambda a: pl.BlockSpec(a.shape, lambda i: (0,) * a.ndim)
    args = (o_c, o_s, o_w, gl, o_f, ex, w_out_bf[:wa], w_out_bf[wa:], x, g.reshape(1, d), b.reshape(1, d))
    specs = [row(wa), row(wa), row(wa), row(LANES), row(wf), full(ex), full(args[6]), full(args[7]),
             row(d), full(args[9]), full(args[10])]
    return pl.pallas_call(
        functools.partial(_even_out_kernel, alpha),
        grid=(n // tm,), in_specs=specs, out_specs=row(d),
        out_shape=jax.ShapeDtypeStruct((n, d), _F32),
        compiler_params=_cparams("parallel"), name="even_out_ln",
    )(*args)


def _odd_out_kernel(alpha, o_ref, w_ref, x_ref, g_ref, b_ref, y_ref):
    m = jnp.dot(o_ref[...].astype(_BF), w_ref[...], preferred_element_type=_F32)
    y_ref[...] = _layer_norm_rows(alpha * x_ref[...] + m, g_ref[...], b_ref[...])


def _odd_out(alpha, o, w_out_bf, x, g, b, tm):
    n, d = x.shape
    row = lambda w: pl.BlockSpec((tm, w), lambda i: (i, 0))
    full = lambda shp: pl.BlockSpec(shp, lambda i: (0,) * len(shp))
    return pl.pallas_call(
        functools.partial(_odd_out_kernel, alpha),
        grid=(n // tm,),
        in_specs=[row(o.shape[1]), full(w_out_bf.shape), row(d), full((1, d)), full((1, d))],
        out_specs=row(d), out_shape=jax.ShapeDtypeStruct((n, d), _F32),
        compiler_params=_cparams("parallel"), name="odd_out_ln",
    )(o, w_out_bf, x, g.reshape(1, d), b.reshape(1, d))


def _moe_kernel(alpha, x_ref, wr_ref, br_ref, w1_ref, w3_ref, w2_ref, g_ref, b_ref, y_ref,
                gate_ref, acc_ref):
    e = pl.program_id(1)
    x = x_ref[...]

    @pl.when(e == 0)
    def _route():
        logits = jnp.dot(x.astype(_BF), wr_ref[...], preferred_element_type=_F32) + br_ref[...]
        gate_ref[...] = _route_gates(logits, 1)
        acc_ref[...] = jnp.zeros_like(acc_ref)

    xb = x.astype(_BF)
    h1 = jnp.dot(xb, w1_ref[0], preferred_element_type=_F32)
    h3 = jnp.dot(xb, w3_ref[0], preferred_element_type=_F32)
    gate = gate_ref[...]
    lane = lax.broadcasted_iota(jnp.int32, gate.shape, 1)
    ge = jnp.sum(jnp.where(lane == e + N_GROUPS, gate, 0.0), axis=-1, keepdims=True)
    h = (jax.nn.silu(h1) * h3) * ge
    acc_ref[...] += jnp.dot(h.astype(_BF), w2_ref[0], preferred_element_type=_F32)

    @pl.when(e == N_EXPERTS - 1)
    def _finish():
        y_ref[...] = _layer_norm_rows(alpha * x + acc_ref[...], g_ref[...], b_ref[...])


def _moe_ln(alpha, x, wg, bg, we, be, w1_bf, w3_bf, w2_bf, g, b, tm):
    n, d = x.shape
    f = w1_bf.shape[2]
    wr = jnp.zeros((d, LANES), _F32).at[:, :N_GROUPS].set(wg).at[:, N_GROUPS:N_GROUPS + N_EXPERTS].set(we)
    wr = wr.astype(_BF)
    br = jnp.zeros((1, LANES), _F32).at[0, :N_GROUPS].set(bg).at[0, N_GROUPS:N_GROUPS + N_EXPERTS].set(be)
    return pl.pallas_call(
        functools.partial(_moe_kernel, alpha),
        grid=(n // tm, N_EXPERTS),
        in_specs=[pl.BlockSpec((tm, d), lambda i, e: (i, 0)),
                  pl.BlockSpec((d, LANES), lambda i, e: (0, 0)),
                  pl.BlockSpec((1, LANES), lambda i, e: (0, 0)),
                  pl.BlockSpec((1, d, f), lambda i, e: (e, 0, 0)),
                  pl.BlockSpec((1, d, f), lambda i, e: (e, 0, 0)),
                  pl.BlockSpec((1, f, d), lambda i, e: (e, 0, 0)),
                  pl.BlockSpec((1, d), lambda i, e: (0, 0)),
                  pl.BlockSpec((1, d), lambda i, e: (0, 0))],
        out_specs=pl.BlockSpec((tm, d), lambda i, e: (i, 0)),
        out_shape=jax.ShapeDtypeStruct((n, d), _F32),
        scratch_shapes=[pltpu.VMEM((tm, LANES), _F32), pltpu.VMEM((tm, d), _F32)],
        compiler_params=_cparams("parallel", "arbitrary"), name="moe_ln",
    )(x, wr, br, w1_bf, w3_bf, w2_bf, g.reshape(1, d), b.reshape(1, d))


def _pages_per_step(n_pages):
    return next(n for n in (PAGES_PER_STEP, 4, 2, 1) if n_pages % n == 0)


def _paged_attn_kernel(n_pg, n_steps, kvh, sel, fox, pt_ref, *refs):
    refs = list(refs)
    qbd_ref = refs[0]
    pages = refs[1:1 + n_pg]
    pos = 1 + n_pg
    bias_ref, bias_new_ref, knew_ref, vnew_ref = refs[pos:pos + 4]
    pos += 4
    if sel:
        selb_ref, e_ref, e_new_ref = refs[pos:pos + 3]
        pos += 3
    if fox:
        cq_ref, ck_ref, ck_new_ref = refs[pos:pos + 3]
        pos += 3
    o_ref, m_ref, l_ref, acc_ref = refs[pos:pos + 4]
    step = pl.program_id(1)
    qbd = qbd_ref[0]
    r, f = qbd.shape
    nt = (((1,), (1,)), ((), ()))

    @pl.when(step == 0)
    def _init():
        m_ref[...] = jnp.full_like(m_ref, NEG_INF)
        l_ref[...] = jnp.zeros_like(l_ref)
        acc_ref[...] = jnp.zeros_like(acc_ref)

    def extra(bias, e, ck):
        add = bias
        if sel:
            add = add + jnp.dot(selb_ref[0], e, preferred_element_type=_F32)
        if fox:
            add = add + cq_ref[0][:, 0:1] - jnp.concatenate([ck] * (r // kvh), axis=0)
        return add

    def absorb(k_list, v_list, add):
        s = jnp.concatenate([jnp.dot(qbd, k, preferred_element_type=_F32) for k in k_list], axis=1) + add
        m_old = m_ref[...]
        m_new = jnp.maximum(m_old, jnp.max(s, axis=1, keepdims=True))
        a = jnp.exp(m_old - m_new)
        p = jnp.exp(s - m_new)
        l_ref[...] = a * l_ref[...] + jnp.sum(p, axis=1, keepdims=True)
        pv = None
        for j, v in enumerate(v_list):
            t = lax.dot_general(p[:, j * PAGE_SIZE:(j + 1) * PAGE_SIZE].astype(_BF), v, nt,
                                preferred_element_type=_F32)
            pv = t if pv is None else pv + t
        acc_ref[...] = a * acc_ref[...] + pv
        m_ref[...] = m_new

    absorb([pg[0, 0].reshape(f, PAGE_SIZE).astype(_BF) for pg in pages],
           [pg[0, 1].reshape(f, PAGE_SIZE).astype(_BF) for pg in pages],
           extra(bias_ref[...], e_ref[...] if sel else None, ck_ref[0] if fox else None))

    @pl.when(step == n_steps - 1)
    def _done():
        absorb([knew_ref[0]], [vnew_ref[0]],
               extra(bias_new_ref[...], e_new_ref[...] if sel else None, ck_new_ref[0] if fox else None))
        o_ref[0] = acc_ref[...] / jnp.maximum(l_ref[...], 1e-30)


def _paged_attn(qbd, pages_t, page_table, bias, knew_t, vnew_t, selb=None, e=None, cq=None, ck=None, ck_new=None):
    bsz, r, f = qbd.shape
    n_pages = page_table.shape[1]
    kvh = pages_t.shape[2]
    length = n_pages * PAGE_SIZE
    n_pg = _pages_per_step(n_pages)
    n_steps = n_pages // n_pg
    sel = selb is not None
    fox = cq is not None
    w = n_pg * PAGE_SIZE

    def page_map(k):
        return lambda b, p, pt: (pt[b * n_pages + p * n_pg + k], 0, 0, 0, 0)

    per_b = lambda shape: pl.BlockSpec((1,) + shape, lambda b, p, pt: (b, 0, 0))
    whole = lambda shape: pl.BlockSpec(shape, lambda b, p, pt: (0, 0))
    in_specs = [per_b((r, f))]
    in_specs += [pl.BlockSpec((1, 2, kvh, HEAD_DIM, PAGE_SIZE), page_map(k)) for k in range(n_pg)]
    in_specs += [pl.BlockSpec((r, w), lambda b, p, pt: (0, p)), whole((r, PAGE_SIZE)),
                 per_b((f, PAGE_SIZE)), per_b((f, PAGE_SIZE))]
    args = [qbd] + [pages_t] * n_pg + [bias[:, :length], bias[:, length:], knew_t, vnew_t]
    if sel:
        nbl = selb.shape[2]
        in_specs += [per_b((r, nbl)), pl.BlockSpec((nbl, w), lambda b, p, pt: (0, p)), whole((nbl, PAGE_SIZE))]
        args += [selb, e[:, :length], e[:, length:]]
    if fox:
        in_specs += [per_b((r, LANES)), pl.BlockSpec((1, kvh, w), lambda b, p, pt: (b, 0, p)), per_b((kvh, PAGE_SIZE))]
        args += [cq, ck, ck_new]
    grid_spec = pltpu.PrefetchScalarGridSpec(
        num_scalar_prefetch=1, grid=(bsz, n_steps), in_specs=in_specs,
        out_specs=per_b((r, f)),
        scratch_shapes=[pltpu.VMEM((r, 1), _F32), pltpu.VMEM((r, 1), _F32), pltpu.VMEM((r, f), _F32)],
    )
    return pl.pallas_call(
        functools.partial(_paged_attn_kernel, n_pg, n_steps, kvh, sel, fox), grid_spec=grid_spec,
        out_shape=jax.ShapeDtypeStruct((bsz, r, f), _F32),
        compiler_params=_cparams("parallel", "arbitrary"), name="paged_attn",
    )(page_table.reshape(-1), *args)


def _moba_pick_kernel(n_pg, n_steps, n_blocks, n_top, pt_ref, qbd_ref, *refs):
    pages = refs[:n_pg]
    selb_ref, km_ref = refs[n_pg], refs[n_pg + 1]
    step = pl.program_id(1)
    f = km_ref.shape[0]

    @pl.when(step == 0)
    def _init():
        km_ref[...] = jnp.zeros_like(km_ref)

    lane = lax.broadcasted_iota(jnp.int32, km_ref.shape, 1)
    km = km_ref[...]
    for k in range(n_pg):
        col = jnp.sum(pages[k][0, 0].reshape(f, PAGE_SIZE), axis=1, keepdims=True)
        blk = (step * n_pg + k) // (MOBA_BLOCK // PAGE_SIZE)
        km = jnp.where(lane == blk, km + col, km)
    km_ref[...] = km

    @pl.when(step == n_steps - 1)
    def _pick():
        means = (km * (1.0 / MOBA_BLOCK)).astype(_BF)
        gate = jnp.dot(qbd_ref[0], means, preferred_element_type=_F32)
        j = lax.broadcasted_iota(jnp.int32, gate.shape, 1)
        score = jnp.where(j < n_blocks, gate, NEG_INF)
        chosen = _topk_axis(score, n_top, j == n_blocks, 1)
        selb_ref[0] = jnp.where(chosen, 0.0, NEG_INF).astype(_BF)


def _moba_pick(qbd, pages_t, page_table):
    bsz, r, f = qbd.shape
    n_pages = page_table.shape[1]
    kvh = pages_t.shape[2]
    n_blocks = n_pages * PAGE_SIZE // MOBA_BLOCK
    assert n_blocks < LANES and (n_pages * PAGE_SIZE) % MOBA_BLOCK == 0
    n_pg = _pages_per_step(n_pages)
    n_steps = n_pages // n_pg

    def page_map(k):
        return lambda b, p, pt: (pt[b * n_pages + p * n_pg + k], 0, 0, 0, 0)

    grid_spec = pltpu.PrefetchScalarGridSpec(
        num_scalar_prefetch=1, grid=(bsz, n_steps),
        in_specs=[pl.BlockSpec((1, r, f), lambda b, p, pt: (b, 0, 0))]
        + [pl.BlockSpec((1, 1, kvh, HEAD_DIM, PAGE_SIZE), page_map(k)) for k in range(n_pg)],
        out_specs=pl.BlockSpec((1, r, LANES), lambda b, p, pt: (b, 0, 0)),
        scratch_shapes=[pltpu.VMEM((f, LANES), _F32)],
    )
    return pl.pallas_call(
        functools.partial(_moba_pick_kernel, n_pg, n_steps, n_blocks, min(MOBA_TOPK, n_blocks)), grid_spec=grid_spec,
        out_shape=jax.ShapeDtypeStruct((bsz, r, LANES), _BF),
        compiler_params=_cparams("parallel", "arbitrary"), name="moba_pick",
    )(page_table.reshape(-1), qbd, *([pages_t] * n_pg))


def _compress_kernel(x_ref, pe_ref, w1a_ref, w1b_ref, b1_ref, w2_ref, b2_ref, o_ref):
    x = x_ref[0, 0]
    pe = pe_ref[0]
    hf = jnp.dot((x + pe[0:1]).astype(_BF), w1a_ref[0], preferred_element_type=_F32)
    hs = jnp.dot((x + pe[1:2]).astype(_BF), w1b_ref[0], preferred_element_type=_F32)
    hs_next = pltpu.roll(hs, hs.shape[0] - 1, 0)
    h = jax.nn.gelu(hf + hs_next + b1_ref[0], approximate=True)
    o_ref[0, 0] = jnp.dot(h.astype(_BF), w2_ref[0], preferred_element_type=_F32) + b2_ref[0]


def _nsa_compress(x, pe, w1, b1, w2, b2):
    sg, bsz, nch, flat = x.shape
    pe2 = pe.reshape(2, CMP_STRIDE, 2, HEAD_DIM).transpose(2, 0, 1, 3).reshape(2, 2, flat)
    w1r = w1.reshape(2, CMP_STRIDE, 2, HEAD_DIM, CMP_HID).transpose(2, 0, 1, 3, 4).reshape(2, 2, flat, CMP_HID)
    w1r = w1r.astype(_BF)
    smap = lambda s, b: (s // NSA_KV_HEADS, 0, 0)
    return pl.pallas_call(
        _compress_kernel,
        grid=(sg, bsz),
        in_specs=[pl.BlockSpec((1, 1, nch, flat), lambda s, b: (s, b, 0, 0)),
                  pl.BlockSpec((1, 2, flat), smap),
                  pl.BlockSpec((1, flat, CMP_HID), smap),
                  pl.BlockSpec((1, flat, CMP_HID), smap),
                  pl.BlockSpec((1, 1, CMP_HID), smap),
                  pl.BlockSpec((1, CMP_HID, HEAD_DIM), smap),
                  pl.BlockSpec((1, 1, HEAD_DIM), smap)],
        out_specs=pl.BlockSpec((1, 1, nch, HEAD_DIM), lambda s, b: (s, b, 0, 0)),
        out_shape=jax.ShapeDtypeStruct((sg, bsz, nch, HEAD_DIM), _F32),
        compiler_params=_cparams("parallel", "parallel"), name="nsa_compress",
    )(x, pe2, w1r[:, 0], w1r[:, 1], b1.reshape(2, 1, CMP_HID), w2.astype(_BF), b2.reshape(2, 1, HEAD_DIM))


def _chunks_from_fm(kv_t):
    bsz, _, length = kv_t.shape
    nch = length // CMP_STRIDE
    sg = 2 * NSA_KV_HEADS
    x = kv_t.reshape(bsz, sg, HEAD_DIM, nch, CMP_STRIDE).transpose(1, 0, 3, 4, 2)
    return x.reshape(sg, bsz, nch, CMP_STRIDE * HEAD_DIM)


def _cmp_select_kernel(theta, tq, qpos0, n_sel, tab_ref, qa_ref, kc_ref, vc_ref, msel_ref, o_ref, selb_ref):
    g = pl.program_id(0) % NSA_KV_HEADS
    i = pl.program_id(1)
    ncp = kc_ref.shape[1]
    nsl = selb_ref.shape[2]
    t = qpos0 + i * tq + lax.broadcasted_iota(jnp.int32, (tq, ncp), 0)
    n = lax.broadcasted_iota(jnp.int32, (tq, ncp), 1)
    d = t - (n * CMP_STRIDE + (CMP_LEN - 1))
    ok = d >= 0
    ind = [d >= th for th in theta]
    kc = kc_ref[0]
    vc = vc_ref[0]
    imp = jnp.zeros((tq, ncp), _F32)
    for r in range(NSA_GROUP):
        base = (g * NSA_GROUP + r) * N_BUCKETS
        bias = jnp.full((tq, ncp), tab_ref[base], _F32)
        for k in range(1, N_BUCKETS):
            bias = bias + jnp.where(ind[k - 1], tab_ref[base + k] - tab_ref[base + k - 1], 0.0)
        s = lax.dot_general(qa_ref[0, r * tq:(r + 1) * tq, :], kc, (((1,), (1,)), ((), ())),
                            preferred_element_type=_F32) + bias
        s = jnp.where(ok, s, NEG_INF)
        m = jnp.max(s, axis=-1, keepdims=True)
        e = jnp.where(ok, jnp.exp(s - m), 0.0)
        p = e / jnp.maximum(jnp.sum(e, axis=-1, keepdims=True), 1e-30)
        o_ref[0, r * tq:(r + 1) * tq, :] = jnp.dot(p.astype(_BF), vc, preferred_element_type=_F32)
        imp = imp + p
    p_s = _dot3(imp, msel_ref[...])
    j = lax.broadcasted_iota(jnp.int32, (tq, nsl), 1)
    qb = (qpos0 + i * tq + lax.broadcasted_iota(jnp.int32, (tq, nsl), 0)) >> int(math.log2(SEL_BLOCK))
    valid = j <= qb
    forced = (j == 0) | (j == qb) | (j == qb - 1)
    score = jnp.where(valid, jnp.where(forced, FORCE_SCORE, p_s), NEG_INF)
    sel = jnp.zeros((tq, nsl), jnp.bool_)
    big = jnp.int32(1 << 20)
    for _ in range(n_sel):
        mx = jnp.max(score, axis=-1, keepdims=True)
        firsti = jnp.min(jnp.where(score == mx, j, big), axis=-1, keepdims=True)
        hit = j == firsti
        sel = sel | (hit & (mx > 0.5 * NEG_INF))
        score = jnp.where(hit, -3e38, score)
    selb_ref[0] = jnp.where(sel, 0.0, NEG_INF).astype(_BF)


def _cmp_select(qa, kc, vc, tab, tq, qpos0, n_blocks):
    bg, rows, _ = qa.shape
    ncp = kc.shape[1]
    nq = rows // (NSA_GROUP * tq)
    nsl = _round_up(n_blocks, LANES)
    nn = np.arange(ncp)[:, None]
    jj = np.arange(nsl)[None, :]
    msel = ((nn >= CMP_PER_SEL * jj - 1) & (nn <= CMP_PER_SEL * jj + CMP_PER_SEL - 1) & (jj < n_blocks))
    msel = jnp.asarray(msel.astype(np.float32), _BF)
    tabf = tab[:, :NSA_HEADS].T.reshape(-1)
    kern = functools.partial(_cmp_select_kernel, _bucket_thresholds(), tq, qpos0, min(NSA_TOPN, n_blocks))
    return pl.pallas_call(
        kern,
        grid=(bg, nq),
        in_specs=[pl.BlockSpec(memory_space=pltpu.SMEM),
                  pl.BlockSpec((1, NSA_GROUP * tq, HEAD_DIM), lambda b, i: (b, i, 0)),
                  pl.BlockSpec((1, ncp, HEAD_DIM), lambda b, i: (b, 0, 0)),
                  pl.BlockSpec((1, ncp, HEAD_DIM), lambda b, i: (b, 0, 0)),
                  pl.BlockSpec((ncp, nsl), lambda b, i: (0, 0))],
        out_specs=[pl.BlockSpec((1, NSA_GROUP * tq, HEAD_DIM), lambda b, i: (b, i, 0)),
                   pl.BlockSpec((1, tq, nsl), lambda b, i: (b, i, 0))],
        out_shape=[jax.ShapeDtypeStruct((bg, rows, HEAD_DIM), _F32),
                   jax.ShapeDtypeStruct((bg, nq * tq, nsl), _BF)],
        compiler_params=_cparams("parallel", "parallel"), name="nsa_cmp_select",
    )(tabf, qa, kc, vc, msel)


def _logf_cumsum_kernel(n_new, x_ref, u_ref, lf_ref, c_ref, hi_ref, mid_ref, lo_ref):
    length = x_ref.shape[1]
    x = x_ref[...]
    col = lax.broadcasted_iota(jnp.int32, x.shape, 1)
    ls = jnp.minimum(x, 0.0) - jnp.log1p(jnp.exp(-jnp.abs(x)))
    lf = jnp.where(col >= length - n_new, ls, x)
    lf_ref[...] = lf
    u = u_ref[...]
    carry = jnp.zeros((x.shape[0], 1), _F32)
    for k in range(length // LANES):
        blk = _dot3(lf[:, k * LANES:(k + 1) * LANES], u) + carry
        c_ref[:, k * LANES:(k + 1) * LANES] = blk
        hi, mid, lo = _split3(blk)
        hi_ref[:, k * LANES:(k + 1) * LANES] = hi
        mid_ref[:, k * LANES:(k + 1) * LANES] = mid
        lo_ref[:, k * LANES:(k + 1) * LANES] = lo
        carry = blk[:, LANES - 1:LANES]


def _logf_cumsum(x, n_new, rows_per_step):
    rows, length = x.shape
    u = jnp.asarray(np.triu(np.ones((LANES, LANES), np.float32)), _BF)
    spec = pl.BlockSpec((rows_per_step, length), lambda i: (i, 0))
    lf, c, hi, mid, lo = pl.pallas_call(
        functools.partial(_logf_cumsum_kernel, n_new),
        grid=(rows // rows_per_step,),
        in_specs=[spec, pl.BlockSpec((LANES, LANES), lambda i: (0, 0))],
        out_specs=[spec] * 5,
        out_shape=[jax.ShapeDtypeStruct((rows, length), _F32)] * 2 + [jax.ShapeDtypeStruct((rows, length), _BF)] * 3,
        compiler_params=_cparams("parallel"), name="fox_logf_cumsum",
    )(x, u)
    return lf, c, (hi, mid, lo)


def _fox_aug_rows(c3):
    one = jnp.ones_like(c3[0])
    zero = jnp.zeros_like(c3[0])
    pad = [zero] * (BF16_ROWS - 6)
    qaug = jnp.stack(list(c3) + [one, one, one] + pad, axis=1)
    kaug = jnp.stack([one, one, one] + [-c for c in c3] + pad, axis=1)
    return qaug, kaug


def _gather_kernel(n_pg, pt_ref, *refs):
    ins, out = refs[:n_pg], refs[n_pg]
    for k in range(n_pg):
        out[0, :, k * PAGE_SIZE:(k + 1) * PAGE_SIZE] = ins[k][0]


def _gather_fm(pool_t, page_table):
    bsz, n_pages = page_table.shape
    f = pool_t.shape[1]
    n_pg = _pages_per_step(n_pages)

    def in_map(k):
        return lambda b, p, pt: (pt[b * n_pages + p * n_pg + k], 0, 0)

    grid_spec = pltpu.PrefetchScalarGridSpec(
        num_scalar_prefetch=1,
        grid=(bsz, n_pages // n_pg),
        in_specs=[pl.BlockSpec((1, f, PAGE_SIZE), in_map(k)) for k in range(n_pg)],
        out_specs=pl.BlockSpec((1, f, n_pg * PAGE_SIZE), lambda b, p, pt: (b, 0, p)),
    )
    return pl.pallas_call(
        functools.partial(_gather_kernel, n_pg), grid_spec=grid_spec,
        out_shape=jax.ShapeDtypeStruct((bsz, f, n_pages * PAGE_SIZE), pool_t.dtype),
        compiler_params=_cparams("parallel", "arbitrary"), name="page_gather",
    )(page_table.reshape(-1), *([pool_t] * n_pg))


def _pages_fm(cache_l):
    return cache_l.transpose(0, 2, 3, 4, 1)


def _kv_group_onehot(heads, group):
    return (np.arange(heads)[:, None] // group == np.arange(heads // group)[None, :]).astype(np.float32)


def _block_diag_queries(q, group):
    b, t, h, dh = q.shape
    oh = jnp.asarray(_kv_group_onehot(h, group))
    x = q[:, :, :, None, :] * oh[None, None, :, :, None]
    return x.reshape(b, t * h, (h // group) * dh).astype(_BF)


def _own_head_columns(o, t, heads, group):
    b = o.shape[0]
    oh = jnp.asarray(_kv_group_onehot(heads, group))
    x = o.reshape(b, t, heads, heads // group, HEAD_DIM) * oh[None, None, :, :, None]
    return jnp.sum(x, axis=3).reshape(b, t, heads * HEAD_DIM)


def _sample_bias(tabh, heads, d0, t, cols, window=None):
    tile = _bias_tile(tabh, d0, t, cols, window)
    tile = jnp.broadcast_to(tile, (heads, t, cols))
    return tile.transpose(1, 0, 2).reshape(t * heads, cols)


def _new_rows_fm(x):
    return _pad_last(x.transpose(0, 2, 1), PAGE_SIZE).astype(_BF)


def _nsa_q_rows(q, tq):
    b, t, _, dh = q.shape
    nq = t // tq
    x = q.reshape(b, nq, tq, NSA_KV_HEADS, NSA_GROUP, dh).transpose(0, 3, 1, 4, 2, 5)
    return x.reshape(b * NSA_KV_HEADS, nq * NSA_GROUP * tq, dh)


def _nsa_rows_back(o, b, t, tq):
    nq = t // tq
    x = o.reshape(b, NSA_KV_HEADS, nq, NSA_GROUP, tq, HEAD_DIM).transpose(0, 2, 4, 1, 3, 5)
    return x.reshape(b, t, NSA_HEADS * HEAD_DIM)


def _even_sample(z, li, cache_cmp, cache_sel, win_state, cache_fkv, cache_flogf, page_table, tab, cmp_w, tq):
    bs, n_pages = page_table.shape
    past = n_pages * PAGE_SIZE
    ts = z.shape[0] // bs
    q_a, kv_c, kv_s, kv_w, g_bm, q_f, kv_f, f_logit = _even_split(z, bs, ts)
    g = NSA_KV_HEADS
    tabn = tab[:, :NSA_HEADS]
    wide = g * HEAD_DIM
    total = past + PAGE_SIZE
    assert past % CMP_STRIDE == 0 and past % SEL_BLOCK == 0 and ts <= CMP_STRIDE

    pool_c = _pages_fm(cache_cmp[li]).reshape(-1, 2 * wide, PAGE_SIZE)
    cmp_tok = _nsa_compress(_chunks_from_fm(_gather_fm(pool_c, page_table)), *cmp_w)
    ncp = _round_up(cmp_tok.shape[2], LANES)
    cmp_tok = _pad_axis(cmp_tok, 2, ncp).astype(_BF)
    kc = cmp_tok[:g].transpose(1, 0, 2, 3).reshape(bs * g, ncp, HEAD_DIM)
    vc = cmp_tok[g:].transpose(1, 0, 2, 3).reshape(bs * g, ncp, HEAD_DIM)
    q_rows = _nsa_q_rows(_pad_axis(q_a * SCALE, 1, tq), tq).astype(_BF)
    o_c, selb = _cmp_select(q_rows, kc, vc, tab, tq, past, total // SEL_BLOCK)
    o_c = _nsa_rows_back(o_c, bs, tq, tq)[:, :ts]
    nsl = selb.shape[2]

    qbd = _block_diag_queries(q_a * SCALE, NSA_GROUP)
    sel_rows = jnp.repeat(selb.reshape(bs, g, tq, nsl)[:, :, :ts].transpose(0, 2, 1, 3), NSA_GROUP, axis=2)
    sel_rows = sel_rows.reshape(bs, ts * NSA_HEADS, nsl)
    e_sel = jnp.asarray((np.arange(nsl)[:, None] == np.arange(total)[None, :] // SEL_BLOCK).astype(np.float32), _BF)
    flat = lambda kv, s: kv[:, :, s].reshape(bs, ts, -1)
    o_s = _paged_attn(qbd, _pages_fm(cache_sel[li]), page_table, _sample_bias(tabn, NSA_HEADS, past, ts, total),
                      _new_rows_fm(flat(kv_s, 0)), _new_rows_fm(flat(kv_s, 1)), selb=sel_rows, e=e_sel)
    win_buf = win_state[li]
    wb = win_buf.shape[1]
    assert wb % PAGE_SIZE == 0
    wpages = _pages_fm(win_buf).reshape(bs, 2, g, HEAD_DIM, wb // PAGE_SIZE, PAGE_SIZE)
    wpages = wpages.transpose(0, 4, 1, 2, 3, 5).reshape(bs * (wb // PAGE_SIZE), 2, g, HEAD_DIM, PAGE_SIZE)
    wtable = jnp.arange(bs * (wb // PAGE_SIZE), dtype=jnp.int32).reshape(bs, wb // PAGE_SIZE)
    o_w = _paged_attn(qbd, wpages, wtable, _sample_bias(tabn, NSA_HEADS, wb, ts, wb + PAGE_SIZE, NSA_WINDOW),
                      _new_rows_fm(flat(kv_w, 0)), _new_rows_fm(flat(kv_w, 1)))
    o_s = _own_head_columns(o_s, ts, NSA_HEADS, NSA_GROUP)
    o_w = _own_head_columns(o_w, ts, NSA_HEADS, NSA_GROUP)

    past_l = _gather_fm(cache_flogf[li].transpose(0, 2, 1), page_table)
    lf_len = _round_up(past + ts, LANES)
    front = lf_len - past - ts
    fl_all = jnp.concatenate([jnp.zeros((bs, FOX_HEADS, front), _F32), past_l, f_logit.transpose(0, 2, 1)], axis=-1)
    logf_t, c, _ = _logf_cumsum(fl_all.reshape(bs * FOX_HEADS, lf_len), ts, min(bs * FOX_HEADS, 64))
    c = c.reshape(bs, FOX_HEADS, lf_len)
    logf = logf_t[:, lf_len - ts:].reshape(bs, FOX_HEADS, ts).transpose(0, 2, 1)
    c_new = c[:, :, front + past:]
    cq = jnp.broadcast_to(c_new.transpose(0, 2, 1).reshape(bs, ts * FOX_HEADS, 1), (bs, ts * FOX_HEADS, LANES))
    o_f = _paged_attn(_block_diag_queries(q_f * SCALE, 1), _pages_fm(cache_fkv[li]), page_table,
                      _sample_bias(None, FOX_HEADS, past, ts, total),
                      _new_rows_fm(flat(kv_f, 0)), _new_rows_fm(flat(kv_f, 1)),
                      cq=cq, ck=c[:, :, front:front + past], ck_new=_pad_last(c_new, PAGE_SIZE))
    o_f = _own_head_columns(o_f, ts, FOX_HEADS, 1)
    kvw_all = jnp.concatenate([win_buf, kv_w], axis=1)
    n_tok = bs * ts
    outs = (o_c.reshape(n_tok, -1), o_s.reshape(n_tok, -1), o_w.reshape(n_tok, -1), o_f.reshape(n_tok, -1),
            g_bm.reshape(n_tok, -1))
    return outs, (kv_c, kv_s, kvw_all[:, ts:], kv_f, logf)


def _odd_sample(z, li, cache_kv, page_table, tab):
    bs, n_pages = page_table.shape
    past = n_pages * PAGE_SIZE
    ts = z.shape[0] // bs
    hw = MOBA_HEADS * HEAD_DIM
    z = z.reshape(bs, ts, 3 * hw)
    q = z[..., :hw].reshape(bs, ts, MOBA_HEADS, HEAD_DIM)
    total = past + PAGE_SIZE
    assert ts <= MOBA_BLOCK and past % MOBA_BLOCK == 0
    pages = _pages_fm(cache_kv[li])
    qbd = _block_diag_queries(q * SCALE, 1)
    selb = _moba_pick(qbd, pages, page_table)
    e_blk = jnp.asarray((np.arange(LANES)[:, None] == np.arange(total)[None, :] // MOBA_BLOCK).astype(np.float32), _BF)
    o = _paged_attn(qbd, pages, page_table, _sample_bias(tab[:, :MOBA_HEADS], MOBA_HEADS, past, ts, total),
                    _new_rows_fm(z[..., hw:2 * hw]), _new_rows_fm(z[..., 2 * hw:]), selb=selb, e=e_blk)
    o = _own_head_columns(o, ts, MOBA_HEADS, 1)
    return o.reshape(bs * ts, hw), z[..., hw:].reshape(bs, ts, 2, MOBA_HEADS, HEAD_DIM)


def _even_split(z, b, t):
    q_a, kv_c, kv_s, kv_w, g_a, q_f, kv_f, f_logit = jnp.split(z.reshape(b, t, -1), _EVEN_CUTS, axis=-1)
    kvshape = (b, t, 2, NSA_KV_HEADS, HEAD_DIM)
    g_bm = g_a.reshape(b, t, NSA_HEADS, 3).transpose(0, 1, 3, 2).reshape(b, t, 3 * NSA_HEADS)
    return (q_a.reshape(b, t, NSA_HEADS, HEAD_DIM), kv_c.reshape(kvshape), kv_s.reshape(kvshape),
            kv_w.reshape(kvshape), g_bm, q_f.reshape(b, t, FOX_HEADS, HEAD_DIM),
            kv_f.reshape(b, t, 2, FOX_HEADS, HEAD_DIM), f_logit)


def _even_row_perm():
    cuts = (0,) + _EVEN_CUTS + (sum(_EVEN_SIZES),)
    seg = lambda k: np.arange(cuts[k], cuts[k + 1])
    gates = cuts[4] + (np.arange(NSA_HEADS)[None, :] * 3 + np.arange(3)[:, None]).reshape(-1)
    return np.concatenate([seg(0), seg(1), seg(2), seg(3), seg(5), seg(6), gates, seg(7)])


def _kv_leaf(zf, row0, heads):
    bsz, _, s = zf.shape
    blk = zf[:, row0:row0 + 2 * heads * HEAD_DIM, :].reshape(bsz, 2, heads, HEAD_DIM, s)
    return blk.transpose(0, 4, 1, 2, 3)


def _even_prompt(x_rows, tab, w_in, b_in, cmp_w, w_out, ln_g, ln_b, alpha, tm):
    bsz, s, d = x_rows.shape
    perm = _even_row_perm()
    w_t = w_in.T[perm].astype(_BF)
    scale = np.ones((_EVEN_OUT, 1), np.float32)
    scale[_QA:_QA + NSA_HEADS * HEAD_DIM] = SCALE
    scale[_QF:_QF + FOX_HEADS * HEAD_DIM] = SCALE
    zf, zb = _proj_fm(x_rows, w_t, b_in[perm].reshape(-1, 1), jnp.asarray(scale), tm, True)
    tabn = tab[:, :NSA_HEADS]
    g = NSA_KV_HEADS
    hb = HEAD_DIM

    assert s % SEL_BLOCK == 0
    cmp_tok = _nsa_compress(_chunks_from_fm(zf[:, _KVC:_KVS, :]), *cmp_w)
    ncp = _round_up(cmp_tok.shape[2], LANES)
    cmp_tok = _pad_axis(cmp_tok, 2, ncp).astype(_BF)
    kc = cmp_tok[:g].transpose(1, 0, 2, 3).reshape(bsz * g, ncp, HEAD_DIM)
    vc_t = cmp_tok[g:].transpose(1, 0, 3, 2).reshape(bsz * g, HEAD_DIM, ncp)
    tq, tk = 256, 512
    o_c, sel = _cmp_select_t(zb, kc, vc_t, tab, tq, s // SEL_BLOCK)

    pairs, deltas = _plan_tiles(s // tq, tq, tk, None, True)
    bias = _bias_tiles_t(tabn, deltas, tk, tq, None, NSA_GROUP)
    block_of_key = lambda n, blk: jnp.asarray(
        (np.arange(n)[:, None] == np.arange(s)[None, :] // blk).astype(np.float32), _BF)
    o_s = _flash_t(zb, _QA // (NSA_GROUP * hb), _KVS // hb, _KVS // hb + g, g, NSA_GROUP, True, bias, pairs, tq, tk,
                   qx=sel, kx=block_of_key(sel.shape[1], SEL_BLOCK))
    pairs, deltas = _plan_tiles(s // tq, tq, tk, NSA_WINDOW, True)
    bias = _bias_tiles_t(tabn, deltas, tk, tq, NSA_WINDOW, NSA_GROUP)
    o_w = _flash_t(zb, _QA // (NSA_GROUP * hb), _KVW // hb, _KVW // hb + g, g, NSA_GROUP, True, bias, pairs, tq, tk)

    logf_t, _, c3 = _logf_cumsum(zf[:, _FL:_FL + FOX_HEADS, :].reshape(bsz * FOX_HEADS, s), s, bsz * FOX_HEADS)
    qaug, kaug = _fox_aug_rows(c3)
    tqf = tkf = 512
    hps = HEADS_PER_STEP
    units = FOX_HEADS // hps
    pairs, deltas = _plan_tiles(s // tqf, tqf, tkf, None, False)
    bias = _bias_tiles_t(None, deltas, tkf, tqf, None, hps)
    per_unit = lambda a: a.reshape(bsz * units, hps * BF16_ROWS, s)
    o_f = _flash_t(zb, _QF // (hps * hb), _KVF // (hps * hb), _KVF // (hps * hb) + units, units, hps, False,
                   bias, pairs, tqf, tkf, qx=per_unit(qaug), kx=per_unit(kaug))

    wide = lambda o: o.reshape(bsz, -1, s)
    x_t = _even_out_fm(alpha, wide(o_c), wide(o_s), wide(o_w), zf, wide(o_f), w_out.T.astype(_BF), x_rows,
                       ln_g, ln_b, tm)
    kv_c = _kv_leaf(zf, _KVC, g)
    kv_s = _kv_leaf(zf, _KVS, g)
    kv_w = _kv_leaf(zf, _KVW, g)
    kv_f = _kv_leaf(zf, _KVF, FOX_HEADS)
    logf = logf_t.reshape(bsz, FOX_HEADS, s).transpose(0, 2, 1)
    return x_t, (kv_c, kv_s, kv_w[:, max(s - NSA_WINDOW, 0):], kv_f, logf)


def _odd_prompt(x_t, tab, w_in, w_out, ln_g, ln_b, alpha, tm):
    bsz, d, s = x_t.shape
    hw = MOBA_HEADS * HEAD_DIM
    scale = np.ones((3 * hw, 1), np.float32)
    scale[:hw] = SCALE
    zf, zb = _proj_fm(x_t, w_in.T.astype(_BF), jnp.zeros((3 * hw, 1), _F32), jnp.asarray(scale), tm, False)
    assert s % MOBA_BLOCK == 0
    sel = _moba_select_t(zf, zb, MOBA_HEADS, 1024 if s % 1024 == 0 else 256)
    tq = tk = 512
    hps = HEADS_PER_STEP
    units = MOBA_HEADS // hps
    pairs, deltas = _plan_tiles(s // tq, tq, tk, None, True)
    bias = _bias_tiles_t(tab[:, :MOBA_HEADS], deltas, tk, tq, None, hps)
    nbr = sel.shape[1]
    block_of_key = jnp.asarray((np.arange(nbr)[:, None] == np.arange(s)[None, :] // MOBA_BLOCK).astype(np.float32), _BF)
    o = _flash_t(zb, 0, units, 2 * units, units, hps, False, bias, pairs, tq, tk,
                 qx=sel.reshape(bsz * units, hps * nbr, s), kx=block_of_key)
    x_t = _odd_out_fm(alpha, o.reshape(bsz, hw, s), w_out.T.astype(_BF), x_t, ln_g, ln_b, tm)
    return x_t, _kv_leaf(zf, hw, MOBA_HEADS)


def _kernel_impl(x_prompt, x_sample, cache_nsa_cmp, cache_nsa_sel, state_nsa_win, cache_fox_kv,
                 cache_fox_logf, cache_moba_kv, page_table, rel_bias, ln_g, ln_b, w_in_even, b_in_even,
                 nsa_cmp_pe, nsa_cmp_w1, nsa_cmp_b1, nsa_cmp_w2, nsa_cmp_b2, w_out_even, w_in_odd,
                 w_out_odd, moe_wg, moe_bg, moe_we, moe_be, moe_w1, moe_w3, moe_w2):
    bp, sp, d = x_prompt.shape
    bs, ts, _ = x_sample.shape
    n_pages = page_table.shape[1]
    past = n_pages * PAGE_SIZE
    depth = ln_g.shape[0]
    alpha = (2 * depth) ** 0.25
    ns_tok = bs * ts
    tm_p = 512
    tm_s = ns_tok
    assert sp % 1024 == 0 and depth % 2 == 0
    xp = x_prompt
    xs = x_sample.reshape(ns_tok, d)
    outs = {k: [] for k in ("cmp_p", "cmp_s", "sel_p", "sel_s", "win_p", "win_s", "fkv_p", "fkv_s",
                            "flf_p", "flf_s", "mkv_p", "mkv_s")}
    tq_s = BF16_ROWS

    for layer in range(depth):
        li = layer // 2
        if layer % 2 == 0:
            assert layer == 0
            cmp_w = (nsa_cmp_pe[li], nsa_cmp_w1[li], nsa_cmp_b1[li], nsa_cmp_w2[li], nsa_cmp_b2[li])
            xp, (kv_c, kv_s, kv_w, kv_f, logf) = _even_prompt(
                xp, rel_bias, w_in_even[li], b_in_even[li], cmp_w, w_out_even[li], ln_g[layer, 0], ln_b[layer, 0],
                alpha, tm_p)
            outs["cmp_p"].append(kv_c)
            outs["sel_p"].append(kv_s)
            outs["win_p"].append(kv_w)
            outs["fkv_p"].append(kv_f)
            outs["flf_p"].append(logf)
            w_in = w_in_even[li].astype(_BF)
            w_out = w_out_even[li].astype(_BF)
            z = _linear(xs, w_in, b_in_even[li], tm_s)
            (o_c, o_s, o_w, o_f, g_bm), (kv_c, kv_s, kv_w, kv_f, logf) = _even_sample(
                z, li, cache_nsa_cmp, cache_nsa_sel, state_nsa_win, cache_fox_kv, cache_fox_logf, page_table,
                rel_bias, cmp_w, tq_s)
            xs = _even_out(alpha, o_c, o_s, o_w, g_bm, o_f, w_out, xs, ln_g[layer, 0], ln_b[layer, 0], tm_s)
            outs["cmp_s"].append(kv_c)
            outs["sel_s"].append(kv_s)
            outs["win_s"].append(kv_w)
            outs["fkv_s"].append(kv_f)
            outs["flf_s"].append(logf)
        else:
            hw = MOBA_HEADS * HEAD_DIM
            xp, kv = _odd_prompt(xp, rel_bias, w_in_odd[li], w_out_odd[li], ln_g[layer, 0], ln_b[layer, 0], alpha, tm_p)
            outs["mkv_p"].append(kv)
            w_in = w_in_odd[li].astype(_BF)
            w_out = w_out_odd[li].astype(_BF)
            z = _linear(xs, w_in, jnp.zeros((3 * hw,), _F32), tm_s)
            o, kv = _odd_sample(z, li, cache_moba_kv, page_table, rel_bias)
            xs = _odd_out(alpha, o, w_out, xs, ln_g[layer, 0], ln_b[layer, 0], tm_s)
            outs["mkv_s"].append(kv)
        w1b, w3b, w2b = moe_w1[layer].astype(_BF), moe_w3[layer].astype(_BF), moe_w2[layer].astype(_BF)
        w13_t = jnp.concatenate([w1b.transpose(0, 2, 1), w3b.transpose(0, 2, 1)], axis=1)
        router = (moe_wg[layer], moe_bg[layer], moe_we[layer], moe_be[layer])
        xp = _moe_ln_fm(alpha, xp, *router, w13_t, w2b.transpose(0, 2, 1), ln_g[layer, 1], ln_b[layer, 1], tm_p,
                        layer == depth - 1)
        xs = _moe_ln(alpha, xs, *router, w1b, w3b, w2b, ln_g[layer, 1], ln_b[layer, 1], tm_s)

    st = lambda k: jnp.stack(outs[k])
    return (xp, xs.reshape(bs, ts, d), st("cmp_p"), st("cmp_s"), st("sel_p"), st("sel_s"),
            st("win_p"), st("win_s"), st("fkv_p"), st("fkv_s"), st("flf_p"), st("flf_s"), st("mkv_p"), st("mkv_s"))


def kernel(x_prompt, x_sample, cache_nsa_cmp, cache_nsa_sel, state_nsa_win, cache_fox_kv, cache_fox_logf, cache_moba_kv, page_table, rel_bias, ln_g, ln_b, w_in_even, b_in_even, nsa_cmp_pe, nsa_cmp_w1, nsa_cmp_b1, nsa_cmp_w2, nsa_cmp_b2, w_out_even, w_in_odd, w_out_odd, moe_wg, moe_bg, moe_we, moe_be, moe_w1, moe_w3, moe_w2):
    return _kernel_impl(x_prompt, x_sample, cache_nsa_cmp, cache_nsa_sel, state_nsa_win, cache_fox_kv,
                        cache_fox_logf, cache_moba_kv, page_table, rel_bias, ln_g, ln_b, w_in_even, b_in_even,
                        nsa_cmp_pe, nsa_cmp_w1, nsa_cmp_b1, nsa_cmp_w2, nsa_cmp_b2, w_out_even, w_in_odd,
                        w_out_odd, moe_wg, moe_bg, moe_we, moe_be, moe_w1, moe_w3, moe_w2)
```

```python
import functools
import math

import numpy as np
import jax
import jax.numpy as jnp
from jax import lax
from jax.experimental import pallas as pl
from jax.experimental.pallas import tpu as pltpu

_BF = jnp.bfloat16
_F32 = jnp.float32

HEAD_DIM = 64
NSA_KV_HEADS = 2
NSA_GROUP = 4
NSA_HEADS = NSA_KV_HEADS * NSA_GROUP
FOX_HEADS = 8
MOBA_HEADS = 16
CMP_LEN = 32
CMP_STRIDE = 16
CMP_HID = 128
SEL_BLOCK = 64
CMP_PER_SEL = SEL_BLOCK // CMP_STRIDE
NSA_TOPN = 16
NSA_WINDOW = 512
MOBA_BLOCK = 256
MOBA_TOPK = 3
N_BUCKETS = 32
T5_MAX_DISTANCE = 128
N_GROUPS = 4
EXPERTS_PER_GROUP = 4
N_EXPERTS = N_GROUPS * EXPERTS_PER_GROUP
PAGE_SIZE = 128
SCALE = HEAD_DIM ** -0.5
NEG_INF = -1e30
FORCE_SCORE = 1e4
LN_EPS = 1e-5
LANES = 128
SUBLANES = 8
BF16_ROWS = 16
VMEM_LIMIT = 48 * 1024 * 1024
PAGES_PER_STEP = 8
HEADS_PER_STEP = 4

_QA, _KVC, _KVS, _KVW, _QF, _KVF, _GA, _FL = 0, 512, 768, 1024, 1280, 1792, 2816, 2840
_EVEN_OUT = 2848
_EVEN_SIZES = (NSA_HEADS * HEAD_DIM, 2 * NSA_KV_HEADS * HEAD_DIM, 2 * NSA_KV_HEADS * HEAD_DIM,
               2 * NSA_KV_HEADS * HEAD_DIM, 3 * NSA_HEADS, FOX_HEADS * HEAD_DIM,
               2 * FOX_HEADS * HEAD_DIM, FOX_HEADS)
_EVEN_CUTS = tuple(int(c) for c in np.cumsum(_EVEN_SIZES)[:-1])


def _cparams(*sem):
    return pltpu.CompilerParams(dimension_semantics=sem, vmem_limit_bytes=VMEM_LIMIT)


def _round_up(n, m):
    return (n + m - 1) // m * m


def _split3(x):
    hi = x.astype(_BF)
    r1 = x - hi.astype(_F32)
    mid = r1.astype(_BF)
    lo = (r1 - mid.astype(_F32)).astype(_BF)
    return hi, mid, lo


def _dot3(x, m01):
    hi, mid, lo = _split3(x)
    acc = jnp.dot(hi, m01, preferred_element_type=_F32)
    acc += jnp.dot(mid, m01, preferred_element_type=_F32)
    acc += jnp.dot(lo, m01, preferred_element_type=_F32)
    return acc


def _t5_bucket_np(dist):
    n = np.maximum(dist, 0)
    exact = N_BUCKETS // 2
    nf = np.maximum(n, exact).astype(np.float32)
    far = exact + (np.log(nf / np.float32(exact)) / np.float32(math.log(T5_MAX_DISTANCE / exact))
                   * np.float32(N_BUCKETS - exact)).astype(np.int32)
    return np.where(n < exact, n, np.minimum(far, N_BUCKETS - 1)).astype(np.int32)


def _bucket_thresholds():
    d = np.arange(0, 4 * T5_MAX_DISTANCE)
    b = _t5_bucket_np(d)
    return [int(d[b >= k][0]) for k in range(1, N_BUCKETS)]


_FAR_DISTANCE = _bucket_thresholds()[-1]


def _pad_last(x, width):
    return jnp.pad(x, [(0, 0)] * (x.ndim - 1) + [(0, width - x.shape[-1])])


def _pad_axis(x, axis, size):
    pads = [(0, 0)] * x.ndim
    pads[axis] = (0, size - x.shape[axis])
    return jnp.pad(x, pads)


def _toeplitz(g, rows, cols):
    n = g.shape[-1]
    lead = g.shape[:-1]
    x = jnp.broadcast_to(g[..., None, :], lead + (rows, n)).reshape(lead + (rows * n,))
    return x[..., :rows * (n - 1)].reshape(lead + (rows, n - 1))[..., :cols]


def _distance_values(tabh, d, window):
    valid = (d >= 0) if window is None else ((d >= 0) & (d < window))
    if tabh is None:
        vals = jnp.zeros((1, d.shape[0]), _F32)
    else:
        vals = tabh[jnp.asarray(_t5_bucket_np(d))].T
    return jnp.where(jnp.asarray(valid)[None], vals, NEG_INF).astype(_F32)


def _bias_tile(tabh, d0, rows, cols, window=None):
    n = rows + cols
    m = np.arange(n)
    d = np.where(m < cols, d0 - m, d0 + n - m)
    return _toeplitz(_distance_values(tabh, d, window), rows, cols)


def _bias_tile_t(tabh, delta, tk, tq, window=None):
    n = tk + tq
    m = np.arange(n)
    d = np.where(m < tq, delta + m, delta + m - n)
    return _toeplitz(_distance_values(tabh, d, window), tk, tq)


def _plan_tiles(nq, tq, tk, window, has_table):
    deltas, pairs = [], []
    for qi in range(nq):
        q0 = qi * tq
        k_hi = (q0 + tq - 1) // tk
        k_lo = 0 if window is None else max(0, (q0 - (window - 1)) // tk)
        for ki in range(k_lo, k_hi + 1):
            delta = q0 - ki * tk
            dmin, dmax = delta - (tk - 1), delta + tq - 1
            plain = dmin >= (_FAR_DISTANCE if has_table else 0) and (window is None or dmax < window)
            if plain:
                pairs.append((qi, ki, -1))
            else:
                if delta not in deltas:
                    deltas.append(delta)
                pairs.append((qi, ki, deltas.index(delta)))
    pairs = [(q, k, b if b >= 0 else len(deltas)) for q, k, b in pairs]
    return pairs, deltas


def _bias_tiles_t(tabh, deltas, tk, tq, window, group):
    far = tk + tq + _FAR_DISTANCE
    tiles = [_bias_tile_t(tabh, dl, tk, tq, window) for dl in deltas]
    tiles.append(_bias_tile_t(tabh, far, tk, tq, None))
    t = jnp.stack(tiles, axis=1)
    if tabh is None:
        t = jnp.broadcast_to(t, (group,) + t.shape[1:])
    h, nb = t.shape[:2]
    t = t.reshape(h // group, group, nb, tk, tq).transpose(0, 2, 3, 1, 4)
    return t.reshape(h // group, nb, tk, group * tq)


def _layer_norm_cols(y, g, b):
    mu = jnp.mean(y, axis=0, keepdims=True)
    yc = y - mu
    var = jnp.mean(yc * yc, axis=0, keepdims=True)
    return yc * lax.rsqrt(var + LN_EPS) * g + b


def _proj_fm_kernel(row_major_in, segments, x_ref, w_ref, b_ref, sc_ref, zb_ref, *seg_refs):
    x = x_ref[0].astype(_BF)
    if row_major_in:
        z = lax.dot_general(w_ref[...], x, (((1,), (1,)), ((), ())), preferred_element_type=_F32)
    else:
        z = jnp.dot(w_ref[...], x, preferred_element_type=_F32)
    z = z + b_ref[...]
    zb_ref[0] = (z * sc_ref[...]).astype(_BF)
    for (row0, rows), ref in zip(segments, seg_refs):
        ref[0] = z[row0:row0 + rows]


def _proj_fm(x, w_t_bf, b_col, scale_col, tm, row_major_in, segments):
    bsz = x.shape[0]
    s = x.shape[1] if row_major_in else x.shape[2]
    d = x.shape[2] if row_major_in else x.shape[1]
    n = w_t_bf.shape[0]
    x_spec = (pl.BlockSpec((1, tm, d), lambda b, i: (b, i, 0)) if row_major_in
              else pl.BlockSpec((1, d, tm), lambda b, i: (b, 0, i)))
    col = pl.BlockSpec((n, 1), lambda b, i: (0, 0))
    out = lambda rows: pl.BlockSpec((1, rows, tm), lambda b, i: (b, 0, i))
    res = pl.pallas_call(
        functools.partial(_proj_fm_kernel, row_major_in, tuple(segments)),
        grid=(bsz, s // tm),
        in_specs=[x_spec, pl.BlockSpec((n, d), lambda b, i: (0, 0)), col, col],
        out_specs=[out(n)] + [out(rows) for _, rows in segments],
        out_shape=[jax.ShapeDtypeStruct((bsz, n, s), _BF)]
        + [jax.ShapeDtypeStruct((bsz, rows, s), _F32) for _, rows in segments],
        compiler_params=_cparams("parallel", "parallel"), name="proj_fm",
    )(x, w_t_bf, b_col, scale_col)
    return res[0], res[1:]


def _even_out_fm_kernel(alpha, oc_ref, os_ref, ow_ref, gz_ref, of_ref, wo_ref, x_ref, g_ref, b_ref, y_ref):
    wa = NSA_HEADS * HEAD_DIM
    sg = jax.nn.sigmoid(gz_ref[0])
    parts = []
    for h in range(NSA_HEADS):
        sl = slice(h * HEAD_DIM, (h + 1) * HEAD_DIM)
        parts.append(sg[h:h + 1] * oc_ref[0, sl, :] + sg[NSA_HEADS + h:NSA_HEADS + h + 1] * os_ref[0, sl, :]
                     + sg[2 * NSA_HEADS + h:2 * NSA_HEADS + h + 1] * ow_ref[0, sl, :])
    o_a = jnp.concatenate(parts, axis=0)
    m = jnp.dot(wo_ref[:, 0:wa], o_a.astype(_BF), preferred_element_type=_F32)
    m += jnp.dot(wo_ref[:, wa:], of_ref[0].astype(_BF), preferred_element_type=_F32)
    y_ref[0] = _layer_norm_cols(alpha * x_ref[0].T + m, g_ref[...], b_ref[...])


def _even_out_fm(alpha, o_c, o_s, o_w, gz, o_f, w_out_t_bf, x_rows, g, b, tm):
    bsz, s, d = x_rows.shape
    wa = NSA_HEADS * HEAD_DIM
    blk = lambda rows: pl.BlockSpec((1, rows, tm), lambda b_, i: (b_, 0, i))
    col = pl.BlockSpec((d, 1), lambda b_, i: (0, 0))
    return pl.pallas_call(
        functools.partial(_even_out_fm_kernel, alpha),
        grid=(bsz, s // tm),
        in_specs=[blk(wa), blk(wa), blk(wa), blk(gz.shape[1]), blk(FOX_HEADS * HEAD_DIM),
                  pl.BlockSpec(w_out_t_bf.shape, lambda b_, i: (0, 0)),
                  pl.BlockSpec((1, tm, d), lambda b_, i: (b_, i, 0)), col, col],
        out_specs=blk(d),
        out_shape=jax.ShapeDtypeStruct((bsz, d, s), _F32),
        compiler_params=_cparams("parallel", "parallel"), name="even_out_ln_fm",
    )(o_c, o_s, o_w, gz, o_f, w_out_t_bf, x_rows, g.reshape(d, 1), b.reshape(d, 1))


def _odd_out_fm_kernel(alpha, o_ref, wo_ref, x_ref, g_ref, b_ref, y_ref):
    m = jnp.dot(wo_ref[...], o_ref[0].astype(_BF), preferred_element_type=_F32)
    y_ref[0] = _layer_norm_cols(alpha * x_ref[0] + m, g_ref[...], b_ref[...])


def _odd_out_fm(alpha, o, w_out_t_bf, x_t, g, b, tm):
    bsz, d, s = x_t.shape
    blk = lambda rows: pl.BlockSpec((1, rows, tm), lambda b_, i: (b_, 0, i))
    col = pl.BlockSpec((d, 1), lambda b_, i: (0, 0))
    return pl.pallas_call(
        functools.partial(_odd_out_fm_kernel, alpha),
        grid=(bsz, s // tm),
        in_specs=[blk(o.shape[1]), pl.BlockSpec(w_out_t_bf.shape, lambda b_, i: (0, 0)), blk(d), col, col],
        out_specs=blk(d), out_shape=jax.ShapeDtypeStruct((bsz, d, s), _F32),
        compiler_params=_cparams("parallel", "parallel"), name="odd_out_ln_fm",
    )(o, w_out_t_bf, x_t, g.reshape(d, 1), b.reshape(d, 1))


def _route_gates(logits, axis):
    idx = lax.broadcasted_iota(jnp.int32, logits.shape, axis)
    big = jnp.int32(1 << 20)
    red = lambda f, v: f(v, axis=axis, keepdims=True)
    is_g = idx < N_GROUPS
    lg = jnp.where(is_g, logits, NEG_INF)
    mg = red(jnp.max, lg)
    sg = red(jnp.sum, jnp.where(is_g, jnp.exp(lg - mg), 0.0))
    p_top = 1.0 / sg
    g_top = red(jnp.min, jnp.where(lg == mg, idx, big))
    lo = N_GROUPS + EXPERTS_PER_GROUP * g_top
    in_grp = (idx >= lo) & (idx < lo + EXPERTS_PER_GROUP)
    le = jnp.where(in_grp, logits, NEG_INF)
    me = red(jnp.max, le)
    se = red(jnp.sum, jnp.where(in_grp, jnp.exp(le - me), 0.0))
    i1 = red(jnp.min, jnp.where(le == me, idx, big))
    le2 = jnp.where(idx == i1, NEG_INF, le)
    m2 = red(jnp.max, le2)
    i2 = red(jnp.min, jnp.where(le2 == m2, idx, big))
    w1 = 1.0 / se
    w2 = jnp.exp(m2 - me) / se
    tot = w1 + w2
    return jnp.where(idx == i1, p_top * w1 / tot, jnp.where(idx == i2, p_top * w2 / tot, 0.0))


def _moe_fm_kernel(alpha, rows_out, x_ref, wr_ref, br_ref, w13_ref, w2_ref, g_ref, b_ref, y_ref,
                   gate_ref, acc_ref, xb_ref):
    e = pl.program_id(2)
    f = w13_ref.shape[1] // 2

    @pl.when(e == 0)
    def _route():
        xb_ref[...] = x_ref[0].astype(_BF)
        logits = jnp.dot(wr_ref[...], xb_ref[...], preferred_element_type=_F32) + br_ref[...]
        gate_ref[...] = _route_gates(logits, 0)
        acc_ref[...] = jnp.zeros_like(acc_ref)

    h13 = jnp.dot(w13_ref[0], xb_ref[...], preferred_element_type=_F32)
    ge = gate_ref[pl.ds(e + N_GROUPS, 1), :]
    h = (jax.nn.silu(h13[0:f]) * h13[f:2 * f]) * ge
    acc_ref[...] += jnp.dot(w2_ref[0], h.astype(_BF), preferred_element_type=_F32)

    @pl.when(e == N_EXPERTS - 1)
    def _finish():
        y = _layer_norm_cols(alpha * x_ref[0] + acc_ref[...], g_ref[...], b_ref[...])
        y_ref[0] = y.T if rows_out else y


def _moe_ln_fm(alpha, x_t, wg, bg, we, be, w13_t_bf, w2_t_bf, g, b, tm, rows_out):
    bsz, d, s = x_t.shape
    rr = 2 * BF16_ROWS
    wr = jnp.zeros((rr, d), _F32).at[:N_GROUPS].set(wg.T).at[N_GROUPS:N_GROUPS + N_EXPERTS].set(we.T).astype(_BF)
    br = jnp.zeros((rr, 1), _F32).at[:N_GROUPS, 0].set(bg).at[N_GROUPS:N_GROUPS + N_EXPERTS, 0].set(be)
    f2 = w13_t_bf.shape[1]
    col = pl.BlockSpec((d, 1), lambda b_, i, e: (0, 0))
    out_spec = (pl.BlockSpec((1, tm, d), lambda b_, i, e: (b_, i, 0)) if rows_out
                else pl.BlockSpec((1, d, tm), lambda b_, i, e: (b_, 0, i)))
    out_shape = jax.ShapeDtypeStruct((bsz, s, d) if rows_out else (bsz, d, s), _F32)
    return pl.pallas_call(
        functools.partial(_moe_fm_kernel, alpha, rows_out),
        grid=(bsz, s // tm, N_EXPERTS),
        in_specs=[pl.BlockSpec((1, d, tm), lambda b_, i, e: (b_, 0, i)),
                  pl.BlockSpec((rr, d), lambda b_, i, e: (0, 0)),
                  pl.BlockSpec((rr, 1), lambda b_, i, e: (0, 0)),
                  pl.BlockSpec((1, f2, d), lambda b_, i, e: (e, 0, 0)),
                  pl.BlockSpec((1, d, f2 // 2), lambda b_, i, e: (e, 0, 0)),
                  col, col],
        out_specs=out_spec, out_shape=out_shape,
        scratch_shapes=[pltpu.VMEM((rr, tm), _F32), pltpu.VMEM((d, tm), _F32), pltpu.VMEM((d, tm), _BF)],
        compiler_params=_cparams("parallel", "parallel", "arbitrary"), name="moe_ln_fm",
    )(x_t, wr, br, w13_t_bf, w2_t_bf, g.reshape(d, 1), b.reshape(d, 1))


_VROWS = HEAD_DIM + BF16_ROWS


def _flash_t_kernel(group, kv_shared, qx_heads, kx_global, qi_ref, ki_ref, bi_ref, first_ref, last_ref, *refs):
    refs = list(refs)
    q_ref, k_ref, v_ref, bias_ref = refs[:4]
    pos = 4
    if qx_heads:
        qx_ref, kx_ref = refs[pos:pos + 2]
        pos += 2
        xr = qx_ref.shape[1] // qx_heads
    o_ref, m_ref, acc_ref, s_ref = refs[pos:pos + 4]
    step = pl.program_id(1)
    tq = q_ref.shape[2]
    tk = k_ref.shape[2]

    @pl.when(first_ref[step] == 1)
    def _init():
        m_ref[...] = jnp.full_like(m_ref, NEG_INF)
        acc_ref[...] = jnp.zeros_like(acc_ref)

    for r in range(group):
        rows = slice(r * HEAD_DIM, (r + 1) * HEAD_DIM)
        q_t = q_ref[0, rows, :]
        k_t = k_ref[0, slice(0, HEAD_DIM) if kv_shared else rows, :]
        if qx_heads:
            qr = r if qx_heads > 1 else 0
            q_t = jnp.concatenate([q_t, qx_ref[0, qr * xr:(qr + 1) * xr, :]], axis=0)
            k_t = jnp.concatenate([k_t, kx_ref[...] if kx_global else kx_ref[0, r * xr:(r + 1) * xr, :]], axis=0)
        s_ref[r] = lax.dot_general(k_t, q_t, (((0,), (0,)), ((), ())), preferred_element_type=_F32)

    m_all = m_ref[...]
    acc_all = acc_ref[...]
    ones = jnp.ones((BF16_ROWS, tk), _BF)
    m_out, acc_out = [], []
    for r in range(group):
        kv_rows = slice(0, HEAD_DIM) if kv_shared else slice(r * HEAD_DIM, (r + 1) * HEAD_DIM)
        s = s_ref[r] + bias_ref[0, 0, :, r * tq:(r + 1) * tq]
        m_old = m_all[r:r + 1, :]
        m_new = jnp.maximum(m_old, jnp.max(s, axis=0, keepdims=True))
        a = jnp.exp(m_old - m_new)
        p = jnp.exp(s - m_new).astype(_BF)
        v_t = jnp.concatenate([v_ref[0, kv_rows, :], ones], axis=0)
        acc_out.append(a * acc_all[r * _VROWS:(r + 1) * _VROWS, :] + jnp.dot(v_t, p, preferred_element_type=_F32))
        m_out.append(m_new)
    pad = [jnp.zeros((SUBLANES - group, tq), _F32)] if group < SUBLANES else []
    m_ref[...] = jnp.concatenate(m_out + pad, axis=0)
    acc_ref[...] = jnp.concatenate(acc_out, axis=0)

    @pl.when(last_ref[step] == 1)
    def _done():
        for r in range(group):
            acc = acc_out[r]
            o_ref[0, r * HEAD_DIM:(r + 1) * HEAD_DIM, :] = (
                acc[0:HEAD_DIM] / jnp.maximum(acc[HEAD_DIM:HEAD_DIM + 1], 1e-30))


def _flash_t(zb, q_blk, k_blk, v_blk, units, group, kv_shared, bias, pairs, tq, tk, qx=None, kx=None):
    bsz, _, s = zb.shape
    bh = bsz * units
    hb = bias.shape[0]
    pairs = np.asarray(pairs, np.int32)
    qi, ki, bi = pairs[:, 0], pairs[:, 1], pairs[:, 2]
    first = np.concatenate([[1], (qi[1:] != qi[:-1]).astype(np.int32)]).astype(np.int32)
    last = np.concatenate([(qi[1:] != qi[:-1]).astype(np.int32), [1]]).astype(np.int32)
    gq = group * HEAD_DIM
    kvr = HEAD_DIM if kv_shared else gq
    in_specs = [pl.BlockSpec((1, gq, tq), lambda b, t, qi, ki, bi, f, l: (b // units, q_blk + b % units, qi[t])),
                pl.BlockSpec((1, kvr, tk), lambda b, t, qi, ki, bi, f, l: (b // units, k_blk + b % units, ki[t])),
                pl.BlockSpec((1, kvr, tk), lambda b, t, qi, ki, bi, f, l: (b // units, v_blk + b % units, ki[t])),
                pl.BlockSpec((1, 1, tk, group * tq), lambda b, t, qi, ki, bi, f, l: (b % hb, bi[t], 0, 0))]
    args = [zb, zb, zb, bias]
    qx_heads, kx_global = 0, False
    if qx is not None:
        kx_global = kx.ndim == 2
        xr = kx.shape[0] if kx_global else kx.shape[1] // group
        qx_heads = qx.shape[1] // xr
        assert qx_heads in (1, group) and xr % BF16_ROWS == 0
        in_specs.append(pl.BlockSpec((1, qx_heads * xr, tq), lambda b, t, qi, ki, bi, f, l: (b, 0, qi[t])))
        in_specs.append(pl.BlockSpec((xr, tk), lambda b, t, qi, ki, bi, f, l: (0, ki[t])) if kx_global else
                        pl.BlockSpec((1, group * xr, tk), lambda b, t, qi, ki, bi, f, l: (b, 0, ki[t])))
        args += [qx, kx]
    grid_spec = pltpu.PrefetchScalarGridSpec(
        num_scalar_prefetch=5, grid=(bh, len(qi)), in_specs=in_specs,
        out_specs=pl.BlockSpec((1, gq, tq), lambda b, t, qi, ki, bi, f, l: (b, 0, qi[t])),
        scratch_shapes=[pltpu.VMEM((SUBLANES, tq), _F32), pltpu.VMEM((group * _VROWS, tq), _F32),
                        pltpu.VMEM((group, tk, tq), _F32)],
    )
    return pl.pallas_call(
        functools.partial(_flash_t_kernel, group, kv_shared, qx_heads, kx_global), grid_spec=grid_spec,
        out_shape=jax.ShapeDtypeStruct((bh, gq, s), _F32),
        compiler_params=_cparams("parallel", "arbitrary"), name="flash_t",
    )(jnp.asarray(qi), jnp.asarray(ki), jnp.asarray(bi), jnp.asarray(first), jnp.asarray(last), *args)


def _topk_axis(score, n_sel, keep, axis):
    j = lax.broadcasted_iota(jnp.int32, score.shape, axis)
    big = jnp.int32(1 << 20)
    for _ in range(n_sel):
        mx = jnp.max(score, axis=axis, keepdims=True)
        firsti = jnp.min(jnp.where(score == mx, j, big), axis=axis, keepdims=True)
        hit = j == firsti
        keep = keep | (hit & (mx > 0.5 * NEG_INF))
        score = jnp.where(hit, -3e38, score)
    return keep


def _topk_rows(score, n_sel, keep):
    return _topk_axis(score, n_sel, keep, 0)


def _cmp_select_t_kernel(theta, tq, n_sel, tab_ref, q_ref, kc_ref, vc_ref, msel_ref, o_ref, sel_ref, s_ref):
    g = pl.program_id(0) % NSA_KV_HEADS
    i = pl.program_id(1)
    ncp = kc_ref.shape[1]
    nsl = sel_ref.shape[1]
    n = lax.broadcasted_iota(jnp.int32, (ncp, tq), 0)
    t = i * tq + lax.broadcasted_iota(jnp.int32, (ncp, tq), 1)
    ok = t - (n * CMP_STRIDE + (CMP_LEN - 1)) >= 0
    band = tq // CMP_STRIDE + CMP_STRIDE
    assert (CMP_STRIDE + 1) * CMP_STRIDE - (CMP_LEN - 1) >= _FAR_DISTANCE and band <= ncp
    w0 = pl.multiple_of(jnp.maximum(i * (tq // CMP_STRIDE) - CMP_STRIDE, 0), SUBLANES)
    nw = w0 + lax.broadcasted_iota(jnp.int32, (band, tq), 0)
    tw = i * tq + lax.broadcasted_iota(jnp.int32, (band, tq), 1)
    dw = tw - (nw * CMP_STRIDE + (CMP_LEN - 1))
    ind = [dw >= th for th in theta]
    kc = kc_ref[0]
    vc_t = vc_ref[0]
    imp = jnp.zeros((ncp, tq), _F32)
    for r in range(NSA_GROUP):
        base = (g * NSA_GROUP + r) * N_BUCKETS
        far = tab_ref[base + N_BUCKETS - 1]
        corr = jnp.full((band, tq), tab_ref[base] - far, _F32)
        for k in range(1, N_BUCKETS):
            corr = corr + jnp.where(ind[k - 1], tab_ref[base + k] - tab_ref[base + k - 1], 0.0)
        rows = slice(r * HEAD_DIM, (r + 1) * HEAD_DIM)
        s_ref[...] = jnp.dot(kc, q_ref[0, rows, :], preferred_element_type=_F32) + far
        s_ref[pl.ds(w0, band), :] = s_ref[pl.ds(w0, band), :] + corr
        s = jnp.where(ok, s_ref[...], NEG_INF)
        m = jnp.max(s, axis=0, keepdims=True)
        e = jnp.where(ok, jnp.exp(s - m), 0.0)
        p = e / jnp.maximum(jnp.sum(e, axis=0, keepdims=True), 1e-30)
        o_ref[0, rows, :] = jnp.dot(vc_t, p.astype(_BF), preferred_element_type=_F32)
        imp = imp + p
    hi, mid, lo = _split3(imp)
    msel = msel_ref[...]
    p_s = (jnp.dot(msel, hi, preferred_element_type=_F32) + jnp.dot(msel, mid, preferred_element_type=_F32)
           + jnp.dot(msel, lo, preferred_element_type=_F32))
    j = lax.broadcasted_iota(jnp.int32, (nsl, tq), 0)
    qb = (i * tq + lax.broadcasted_iota(jnp.int32, (nsl, tq), 1)) >> int(math.log2(SEL_BLOCK))
    valid = j <= qb
    forced = (j == 0) | (j == qb) | (j == qb - 1)
    score = jnp.where(valid, jnp.where(forced, FORCE_SCORE, p_s), NEG_INF)
    chosen = _topk_rows(score, n_sel, jnp.zeros((nsl, tq), jnp.bool_))
    sel_ref[0] = jnp.where(chosen, 0.0, NEG_INF).astype(_BF)


def _cmp_select_t(zb, kc, vc_t, tab, tq, n_blocks):
    bsz, _, s = zb.shape
    bg = bsz * NSA_KV_HEADS
    ncp = kc.shape[1]
    nsl = _round_up(n_blocks, LANES)
    gq = NSA_GROUP * HEAD_DIM
    jj = np.arange(nsl)[:, None]
    nn = np.arange(ncp)[None, :]
    msel = ((nn >= CMP_PER_SEL * jj - 1) & (nn <= CMP_PER_SEL * jj + CMP_PER_SEL - 1) & (jj < n_blocks))
    msel = jnp.asarray(msel.astype(np.float32), _BF)
    tabf = tab[:, :NSA_HEADS].T.reshape(-1)
    kern = functools.partial(_cmp_select_t_kernel, _bucket_thresholds(), tq, min(NSA_TOPN, n_blocks))
    return pl.pallas_call(
        kern,
        grid=(bg, s // tq),
        in_specs=[pl.BlockSpec(memory_space=pltpu.SMEM),
                  pl.BlockSpec((1, gq, tq), lambda b, i: (b // NSA_KV_HEADS, b % NSA_KV_HEADS, i)),
                  pl.BlockSpec((1, ncp, HEAD_DIM), lambda b, i: (b, 0, 0)),
                  pl.BlockSpec((1, HEAD_DIM, ncp), lambda b, i: (b, 0, 0)),
                  pl.BlockSpec((nsl, ncp), lambda b, i: (0, 0))],
        out_specs=[pl.BlockSpec((1, gq, tq), lambda b, i: (b, 0, i)),
                   pl.BlockSpec((1, nsl, tq), lambda b, i: (b, 0, i))],
        out_shape=[jax.ShapeDtypeStruct((bg, gq, s), _F32), jax.ShapeDtypeStruct((bg, nsl, s), _BF)],
        scratch_shapes=[pltpu.VMEM((ncp, tq), _F32)],
        compiler_params=_cparams("parallel", "parallel"), name="nsa_cmp_select_t",
    )(tabf, zb, kc, vc_t, msel)


def _moba_select_t_kernel(tq, n_blocks, kf_ref, avg_ref, q_ref, sel_ref, km_ref):
    i = pl.program_id(1)

    @pl.when(i == 0)
    def _means():
        km_ref[...] = _dot3(kf_ref[0], avg_ref[...])

    gate = lax.dot_general(km_ref[...].astype(_BF), q_ref[0], (((0,), (0,)), ((), ())),
                           preferred_element_type=_F32)
    j = lax.broadcasted_iota(jnp.int32, gate.shape, 0)
    qb = (i * tq + lax.broadcasted_iota(jnp.int32, gate.shape, 1)) >> int(math.log2(MOBA_BLOCK))
    score = jnp.where(j < qb, gate, NEG_INF)
    chosen = _topk_rows(score, min(MOBA_TOPK, n_blocks), j == qb)
    sel_ref[0] = jnp.where(chosen, 0.0, NEG_INF)[0:sel_ref.shape[1]].astype(_BF)


def _moba_select_t(zf, zb, k_blk, tq):
    bsz, _, s = zb.shape
    n_blocks = s // MOBA_BLOCK
    assert n_blocks <= LANES
    rows = _round_up(n_blocks, BF16_ROWS)
    avg = (np.arange(s)[:, None] // MOBA_BLOCK == np.arange(LANES)[None, :]).astype(np.float32) / MOBA_BLOCK
    bh = bsz * MOBA_HEADS
    return pl.pallas_call(
        functools.partial(_moba_select_t_kernel, tq, n_blocks),
        grid=(bh, s // tq),
        in_specs=[pl.BlockSpec((1, HEAD_DIM, s), lambda b, i: (b // MOBA_HEADS, k_blk + b % MOBA_HEADS, 0)),
                  pl.BlockSpec((s, LANES), lambda b, i: (0, 0)),
                  pl.BlockSpec((1, HEAD_DIM, tq), lambda b, i: (b // MOBA_HEADS, b % MOBA_HEADS, i))],
        out_specs=pl.BlockSpec((1, rows, tq), lambda b, i: (b, 0, i)),
        out_shape=jax.ShapeDtypeStruct((bh, rows, s), _BF),
        scratch_shapes=[pltpu.VMEM((HEAD_DIM, LANES), _F32)],
        compiler_params=_cparams("parallel", "arbitrary"), name="moba_select_t",
    )(zf, jnp.asarray(avg, _BF), zb)


def _linear_kernel(x_ref, w_ref, b_ref, o_ref):
    o_ref[...] = jnp.dot(x_ref[...].astype(_BF), w_ref[...], preferred_element_type=_F32) + b_ref[...]


def _linear(x, w_bf, b, tm):
    m, k = x.shape
    n = w_bf.shape[1]
    return pl.pallas_call(
        _linear_kernel,
        grid=(m // tm,),
        in_specs=[pl.BlockSpec((tm, k), lambda i: (i, 0)),
                  pl.BlockSpec((k, n), lambda i: (0, 0)),
                  pl.BlockSpec((1, n), lambda i: (0, 0))],
        out_specs=pl.BlockSpec((tm, n), lambda i: (i, 0)),
        out_shape=jax.ShapeDtypeStruct((m, n), _F32),
        compiler_params=_cparams("parallel"),
        name="linear",
    )(x, w_bf, b.reshape(1, n))


def _layer_norm_rows(y, g, b):
    mu = jnp.mean(y, axis=-1, keepdims=True)
    yc = y - mu
    var = jnp.mean(yc * yc, axis=-1, keepdims=True)
    return yc * lax.rsqrt(var + LN_EPS) * g + b


def _even_out_kernel(alpha, oc_ref, os_ref, ow_ref, gl_ref, of_ref, ex_ref, wa_ref, wf_ref, x_ref,
                     g_ref, b_ref, y_ref):
    wa = NSA_HEADS * HEAD_DIM
    gexp = _dot3(jax.nn.sigmoid(gl_ref[...]), ex_ref[...])
    o_a = (gexp[:, 0:wa] * oc_ref[...] + gexp[:, wa:2 * wa] * os_ref[...] + gexp[:, 2 * wa:3 * wa] * ow_ref[...])
    m = jnp.dot(o_a.astype(_BF), wa_ref[...], preferred_element_type=_F32)
    m += jnp.dot(of_ref[...].astype(_BF), wf_ref[...], preferred_element_type=_F32)
    y_ref[...] = _layer_norm_rows(alpha * x_ref[...] + m, g_ref[...], b_ref[...])


def _even_out(alpha, o_c, o_s, o_w, gl, o_f, w_out_bf, x, g, b, tm):
    n, d = x.shape
    wa = NSA_HEADS * HEAD_DIM
    wf = FOX_HEADS * HEAD_DIM
    ex_np = np.zeros((LANES, 3 * wa), np.float32)
    for j in range(3):
        for h in range(NSA_HEADS):
            ex_np[j * NSA_HEADS + h, j * wa + h * HEAD_DIM:j * wa + (h + 1) * HEAD_DIM] = 1.0
    ex = jnp.asarray(ex_np, _BF)
    gl = _pad_last(gl, LANES)
    row = lambda w: pl.BlockSpec((tm, w), lambda i: (i, 0))
    full = lambda a: pl.BlockSpec(a.shape, lambda i: (0,) * a.ndim)
    args = (o_c, o_s, o_w, gl, o_f, ex, w_out_bf[:wa], w_out_bf[wa:], x, g.reshape(1, d), b.reshape(1, d))
    specs = [row(wa), row(wa), row(wa), row(LANES), row(wf), full(ex), full(args[6]), full(args[7]),
             row(d), full(args[9]), full(args[10])]
    return pl.pallas_call(
        functools.partial(_even_out_kernel, alpha),
        grid=(n // tm,), in_specs=specs, out_specs=row(d),
        out_shape=jax.ShapeDtypeStruct((n, d), _F32),
        compiler_params=_cparams("parallel"), name="even_out_ln",
    )(*args)


def _odd_out_kernel(alpha, o_ref, w_ref, x_ref, g_ref, b_ref, y_ref):
    m = jnp.dot(o_ref[...].astype(_BF), w_ref[...], preferred_element_type=_F32)
    y_ref[...] = _layer_norm_rows(alpha * x_ref[...] + m, g_ref[...], b_ref[...])


def _odd_out(alpha, o, w_out_bf, x, g, b, tm):
    n, d = x.shape
    row = lambda w: pl.BlockSpec((tm, w), lambda i: (i, 0))
    full = lambda shp: pl.BlockSpec(shp, lambda i: (0,) * len(shp))
    return pl.pallas_call(
        functools.partial(_odd_out_kernel, alpha),
        grid=(n // tm,),
        in_specs=[row(o.shape[1]), full(w_out_bf.shape), row(d), full((1, d)), full((1, d))],
        out_specs=row(d), out_shape=jax.ShapeDtypeStruct((n, d), _F32),
        compiler_params=_cparams("parallel"), name="odd_out_ln",
    )(o, w_out_bf, x, g.reshape(1, d), b.reshape(1, d))


def _moe_kernel(alpha, x_ref, wr_ref, br_ref, w1_ref, w3_ref, w2_ref, g_ref, b_ref, y_ref,
                gate_ref, acc_ref):
    e = pl.program_id(1)
    x = x_ref[...]

    @pl.when(e == 0)
    def _route():
        logits = jnp.dot(x.astype(_BF), wr_ref[...], preferred_element_type=_F32) + br_ref[...]
        gate_ref[...] = _route_gates(logits, 1)
        acc_ref[...] = jnp.zeros_like(acc_ref)

    xb = x.astype(_BF)
    h1 = jnp.dot(xb, w1_ref[0], preferred_element_type=_F32)
    h3 = jnp.dot(xb, w3_ref[0], preferred_element_type=_F32)
    gate = gate_ref[...]
    lane = lax.broadcasted_iota(jnp.int32, gate.shape, 1)
    ge = jnp.sum(jnp.where(lane == e + N_GROUPS, gate, 0.0), axis=-1, keepdims=True)
    h = (jax.nn.silu(h1) * h3) * ge
    acc_ref[...] += jnp.dot(h.astype(_BF), w2_ref[0], preferred_element_type=_F32)

    @pl.when(e == N_EXPERTS - 1)
    def _finish():
        y_ref[...] = _layer_norm_rows(alpha * x + acc_ref[...], g_ref[...], b_ref[...])


def _moe_ln(alpha, x, wg, bg, we, be, w1_bf, w3_bf, w2_bf, g, b, tm):
    n, d = x.shape
    f = w1_bf.shape[2]
    wr = jnp.zeros((d, LANES), _F32).at[:, :N_GROUPS].set(wg).at[:, N_GROUPS:N_GROUPS + N_EXPERTS].set(we)
    wr = wr.astype(_BF)
    br = jnp.zeros((1, LANES), _F32).at[0, :N_GROUPS].set(bg).at[0, N_GROUPS:N_GROUPS + N_EXPERTS].set(be)
    return pl.pallas_call(
        functools.partial(_moe_kernel, alpha),
        grid=(n // tm, N_EXPERTS),
        in_specs=[pl.BlockSpec((tm, d), lambda i, e: (i, 0)),
                  pl.BlockSpec((d, LANES), lambda i, e: (0, 0)),
                  pl.BlockSpec((1, LANES), lambda i, e: (0, 0)),
                  pl.BlockSpec((1, d, f), lambda i, e: (e, 0, 0)),
                  pl.BlockSpec((1, d, f), lambda i, e: (e, 0, 0)),
                  pl.BlockSpec((1, f, d), lambda i, e: (e, 0, 0)),
                  pl.BlockSpec((1, d), lambda i, e: (0, 0)),
                  pl.BlockSpec((1, d), lambda i, e: (0, 0))],
        out_specs=pl.BlockSpec((tm, d), lambda i, e: (i, 0)),
        out_shape=jax.ShapeDtypeStruct((n, d), _F32),
        scratch_shapes=[pltpu.VMEM((tm, LANES), _F32), pltpu.VMEM((tm, d), _F32)],
        compiler_params=_cparams("parallel", "arbitrary"), name="moe_ln",
    )(x, wr, br, w1_bf, w3_bf, w2_bf, g.reshape(1, d), b.reshape(1, d))


def _pages_per_step(n_pages):
    return next(n for n in (PAGES_PER_STEP, 4, 2, 1) if n_pages % n == 0)


def _paged_attn_kernel(n_pg, n_steps, kvh, sel, fox, pt_ref, *refs):
    refs = list(refs)
    qbd_ref = refs[0]
    pages = refs[1:1 + n_pg]
    pos = 1 + n_pg
    bias_ref, bias_new_ref, knew_ref, vnew_ref = refs[pos:pos + 4]
    pos += 4
    if sel:
        selb_ref, e_ref, e_new_ref = refs[pos:pos + 3]
        pos += 3
    if fox:
        cq_ref, ck_ref, ck_new_ref = refs[pos:pos + 3]
        pos += 3
    o_ref, m_ref, l_ref, acc_ref = refs[pos:pos + 4]
    step = pl.program_id(1)
    qbd = qbd_ref[0]
    r, f = qbd.shape
    nt = (((1,), (1,)), ((), ()))

    @pl.when(step == 0)
    def _init():
        m_ref[...] = jnp.full_like(m_ref, NEG_INF)
        l_ref[...] = jnp.zeros_like(l_ref)
        acc_ref[...] = jnp.zeros_like(acc_ref)

    def extra(bias, e, ck):
        add = bias
        if sel:
            add = add + jnp.dot(selb_ref[0], e, preferred_element_type=_F32)
        if fox:
            add = add + cq_ref[0][:, 0:1] - jnp.concatenate([ck] * (r // kvh), axis=0)
        return add

    def absorb(k_list, v_list, add):
        s = jnp.concatenate([jnp.dot(qbd, k, preferred_element_type=_F32) for k in k_list], axis=1) + add
        m_old = m_ref[...]
        m_new = jnp.maximum(m_old, jnp.max(s, axis=1, keepdims=True))
        a = jnp.exp(m_old - m_new)
        p = jnp.exp(s - m_new)
        l_ref[...] = a * l_ref[...] + jnp.sum(p, axis=1, keepdims=True)
        pv = None
        for j, v in enumerate(v_list):
            t = lax.dot_general(p[:, j * PAGE_SIZE:(j + 1) * PAGE_SIZE].astype(_BF), v, nt,
                                preferred_element_type=_F32)
            pv = t if pv is None else pv + t
        acc_ref[...] = a * acc_ref[...] + pv
        m_ref[...] = m_new

    absorb([pg[0, 0].reshape(f, PAGE_SIZE).astype(_BF) for pg in pages],
           [pg[0, 1].reshape(f, PAGE_SIZE).astype(_BF) for pg in pages],
           extra(bias_ref[...], e_ref[...] if sel else None, ck_ref[0] if fox else None))

    @pl.when(step == n_steps - 1)
    def _done():
        absorb([knew_ref[0]], [vnew_ref[0]],
               extra(bias_new_ref[...], e_new_ref[...] if sel else None, ck_new_ref[0] if fox else None))
        o_ref[0] = acc_ref[...] / jnp.maximum(l_ref[...], 1e-30)


def _paged_attn(qbd, pages_t, page_table, bias, knew_t, vnew_t, selb=None, e=None, cq=None, ck=None, ck_new=None):
    bsz, r, f = qbd.shape
    n_pages = page_table.shape[1]
    kvh = pages_t.shape[2]
    length = n_pages * PAGE_SIZE
    n_pg = _pages_per_step(n_pages)
    n_steps = n_pages // n_pg
    sel = selb is not None
    fox = cq is not None
    w = n_pg * PAGE_SIZE

    def page_map(k):
        return lambda b, p, pt: (pt[b * n_pages + p * n_pg + k], 0, 0, 0, 0)

    per_b = lambda shape: pl.BlockSpec((1,) + shape, lambda b, p, pt: (b, 0, 0))
    whole = lambda shape: pl.BlockSpec(shape, lambda b, p, pt: (0, 0))
    in_specs = [per_b((r, f))]
    in_specs += [pl.BlockSpec((1, 2, kvh, HEAD_DIM, PAGE_SIZE), page_map(k)) for k in range(n_pg)]
    in_specs += [pl.BlockSpec((r, w), lambda b, p, pt: (0, p)), whole((r, PAGE_SIZE)),
                 per_b((f, PAGE_SIZE)), per_b((f, PAGE_SIZE))]
    args = [qbd] + [pages_t] * n_pg + [bias[:, :length], bias[:, length:], knew_t, vnew_t]
    if sel:
        nbl = selb.shape[2]
        in_specs += [per_b((r, nbl)), pl.BlockSpec((nbl, w), lambda b, p, pt: (0, p)), whole((nbl, PAGE_SIZE))]
        args += [selb, e[:, :length], e[:, length:]]
    if fox:
        in_specs += [per_b((r, LANES)), pl.BlockSpec((1, kvh, w), lambda b, p, pt: (b, 0, p)), per_b((kvh, PAGE_SIZE))]
        args += [cq, ck, ck_new]
    grid_spec = pltpu.PrefetchScalarGridSpec(
        num_scalar_prefetch=1, grid=(bsz, n_steps), in_specs=in_specs,
        out_specs=per_b((r, f)),
        scratch_shapes=[pltpu.VMEM((r, 1), _F32), pltpu.VMEM((r, 1), _F32), pltpu.VMEM((r, f), _F32)],
    )
    return pl.pallas_call(
        functools.partial(_paged_attn_kernel, n_pg, n_steps, kvh, sel, fox), grid_spec=grid_spec,
        out_shape=jax.ShapeDtypeStruct((bsz, r, f), _F32),
        compiler_params=_cparams("parallel", "arbitrary"), name="paged_attn",
    )(page_table.reshape(-1), *args)


def _moba_pick_kernel(n_pg, n_steps, n_blocks, n_top, pt_ref, qbd_ref, *refs):
    pages = refs[:n_pg]
    selb_ref, km_ref = refs[n_pg], refs[n_pg + 1]
    step = pl.program_id(1)
    f = km_ref.shape[0]

    @pl.when(step == 0)
    def _init():
        km_ref[...] = jnp.zeros_like(km_ref)

    lane = lax.broadcasted_iota(jnp.int32, km_ref.shape, 1)
    km = km_ref[...]
    for k in range(n_pg):
        col = jnp.sum(pages[k][0, 0].reshape(f, PAGE_SIZE), axis=1, keepdims=True)
        blk = (step * n_pg + k) // (MOBA_BLOCK // PAGE_SIZE)
        km = jnp.where(lane == blk, km + col, km)
    km_ref[...] = km

    @pl.when(step == n_steps - 1)
    def _pick():
        means = (km * (1.0 / MOBA_BLOCK)).astype(_BF)
        gate = jnp.dot(qbd_ref[0], means, preferred_element_type=_F32)
        j = lax.broadcasted_iota(jnp.int32, gate.shape, 1)
        score = jnp.where(j < n_blocks, gate, NEG_INF)
        chosen = _topk_axis(score, n_top, j == n_blocks, 1)
        selb_ref[0] = jnp.where(chosen, 0.0, NEG_INF).astype(_BF)


def _moba_pick(qbd, pages_t, page_table):
    bsz, r, f = qbd.shape
    n_pages = page_table.shape[1]
    kvh = pages_t.shape[2]
    n_blocks = n_pages * PAGE_SIZE // MOBA_BLOCK
    assert n_blocks < LANES and (n_pages * PAGE_SIZE) % MOBA_BLOCK == 0
    n_pg = _pages_per_step(n_pages)
    n_steps = n_pages // n_pg

    def page_map(k):
        return lambda b, p, pt: (pt[b * n_pages + p * n_pg + k], 0, 0, 0, 0)

    grid_spec = pltpu.PrefetchScalarGridSpec(
        num_scalar_prefetch=1, grid=(bsz, n_steps),
        in_specs=[pl.BlockSpec((1, r, f), lambda b, p, pt: (b, 0, 0))]
        + [pl.BlockSpec((1, 1, kvh, HEAD_DIM, PAGE_SIZE), page_map(k)) for k in range(n_pg)],
        out_specs=pl.BlockSpec((1, r, LANES), lambda b, p, pt: (b, 0, 0)),
        scratch_shapes=[pltpu.VMEM((f, LANES), _F32)],
    )
    return pl.pallas_call(
        functools.partial(_moba_pick_kernel, n_pg, n_steps, n_blocks, min(MOBA_TOPK, n_blocks)), grid_spec=grid_spec,
        out_shape=jax.ShapeDtypeStruct((bsz, r, LANES), _BF),
        compiler_params=_cparams("parallel", "arbitrary"), name="moba_pick",
    )(page_table.reshape(-1), qbd, *([pages_t] * n_pg))


def _compress_kernel(x_ref, pe_ref, w1a_ref, w1b_ref, b1_ref, w2_ref, b2_ref, o_ref):
    x = x_ref[0, 0]
    pe = pe_ref[0]
    hf = jnp.dot((x + pe[0:1]).astype(_BF), w1a_ref[0], preferred_element_type=_F32)
    hs = jnp.dot((x + pe[1:2]).astype(_BF), w1b_ref[0], preferred_element_type=_F32)
    hs_next = pltpu.roll(hs, hs.shape[0] - 1, 0)
    h = jax.nn.gelu(hf + hs_next + b1_ref[0], approximate=True)
    o_ref[0, 0] = jnp.dot(h.astype(_BF), w2_ref[0], preferred_element_type=_F32) + b2_ref[0]


def _nsa_compress(x, pe, w1, b1, w2, b2):
    sg, bsz, nch, flat = x.shape
    pe2 = pe.reshape(2, CMP_STRIDE, 2, HEAD_DIM).transpose(2, 0, 1, 3).reshape(2, 2, flat)
    w1r = w1.reshape(2, CMP_STRIDE, 2, HEAD_DIM, CMP_HID).transpose(2, 0, 1, 3, 4).reshape(2, 2, flat, CMP_HID)
    w1r = w1r.astype(_BF)
    smap = lambda s, b: (s // NSA_KV_HEADS, 0, 0)
    return pl.pallas_call(
        _compress_kernel,
        grid=(sg, bsz),
        in_specs=[pl.BlockSpec((1, 1, nch, flat), lambda s, b: (s, b, 0, 0)),
                  pl.BlockSpec((1, 2, flat), smap),
                  pl.BlockSpec((1, flat, CMP_HID), smap),
                  pl.BlockSpec((1, flat, CMP_HID), smap),
                  pl.BlockSpec((1, 1, CMP_HID), smap),
                  pl.BlockSpec((1, CMP_HID, HEAD_DIM), smap),
                  pl.BlockSpec((1, 1, HEAD_DIM), smap)],
        out_specs=pl.BlockSpec((1, 1, nch, HEAD_DIM), lambda s, b: (s, b, 0, 0)),
        out_shape=jax.ShapeDtypeStruct((sg, bsz, nch, HEAD_DIM), _F32),
        compiler_params=_cparams("parallel", "parallel"), name="nsa_compress",
    )(x, pe2, w1r[:, 0], w1r[:, 1], b1.reshape(2, 1, CMP_HID), w2.astype(_BF), b2.reshape(2, 1, HEAD_DIM))


def _compress_rows_kernel(x_ref, pe_ref, w1a_ref, w1b_ref, b1_ref, w2_ref, b2_ref, o_ref):
    nch = o_ref.shape[2]
    hf = jnp.zeros((nch, CMP_HID), _F32)
    hs = jnp.zeros((nch, CMP_HID), _F32)
    for c in range(CMP_STRIDE):
        xc = x_ref[0, 0, pl.ds(c, nch, stride=CMP_STRIDE), :]
        hf += jnp.dot((xc + pe_ref[0, 0, c:c + 1, :]).astype(_BF), w1a_ref[0, c], preferred_element_type=_F32)
        hs += jnp.dot((xc + pe_ref[0, 1, c:c + 1, :]).astype(_BF), w1b_ref[0, c], preferred_element_type=_F32)
    hs_next = pltpu.roll(hs, nch - 1, 0)
    h = jax.nn.gelu(hf + hs_next + b1_ref[0], approximate=True)
    o_ref[0, 0] = jnp.dot(h.astype(_BF), w2_ref[0], preferred_element_type=_F32) + b2_ref[0]


def _nsa_compress_rows(x, pe, w1, b1, w2, b2):
    sg, bsz, length, dh = x.shape
    nch = length // CMP_STRIDE
    pe2 = pe.reshape(2, CMP_STRIDE, 2, dh).transpose(2, 0, 1, 3)
    w1r = w1.reshape(2, CMP_STRIDE, 2, dh, CMP_HID).transpose(2, 0, 1, 3, 4).astype(_BF)
    s3 = lambda s, b: (s // NSA_KV_HEADS, 0, 0)
    s4 = lambda s, b: (s // NSA_KV_HEADS, 0, 0, 0)
    return pl.pallas_call(
        _compress_rows_kernel,
        grid=(sg, bsz),
        in_specs=[pl.BlockSpec((1, 1, length, dh), lambda s, b: (s, b, 0, 0)),
                  pl.BlockSpec((1, 2, CMP_STRIDE, dh), s4),
                  pl.BlockSpec((1, CMP_STRIDE, dh, CMP_HID), s4),
                  pl.BlockSpec((1, CMP_STRIDE, dh, CMP_HID), s4),
                  pl.BlockSpec((1, 1, CMP_HID), s3),
                  pl.BlockSpec((1, CMP_HID, dh), s3),
                  pl.BlockSpec((1, 1, dh), s3)],
        out_specs=pl.BlockSpec((1, 1, nch, dh), lambda s, b: (s, b, 0, 0)),
        out_shape=jax.ShapeDtypeStruct((sg, bsz, nch, dh), _F32),
        compiler_params=_cparams("parallel", "parallel"), name="nsa_compress_rows",
    )(x, pe2, w1r[:, 0], w1r[:, 1], b1.reshape(2, 1, CMP_HID), w2.astype(_BF), b2.reshape(2, 1, dh))


def _chunks_from_fm(kv_t):
    bsz, _, length = kv_t.shape
    nch = length // CMP_STRIDE
    sg = 2 * NSA_KV_HEADS
    x = kv_t.reshape(bsz, sg, HEAD_DIM, nch, CMP_STRIDE).transpose(1, 0, 3, 4, 2)
    return x.reshape(sg, bsz, nch, CMP_STRIDE * HEAD_DIM)


def _cmp_select_kernel(theta, tq, qpos0, n_sel, tab_ref, qa_ref, kc_ref, vc_ref, msel_ref, o_ref, selb_ref):
    g = pl.program_id(0) % NSA_KV_HEADS
    i = pl.program_id(1)
    ncp = kc_ref.shape[1]
    nsl = selb_ref.shape[2]
    t = qpos0 + i * tq + lax.broadcasted_iota(jnp.int32, (tq, ncp), 0)
    n = lax.broadcasted_iota(jnp.int32, (tq, ncp), 1)
    d = t - (n * CMP_STRIDE + (CMP_LEN - 1))
    ok = d >= 0
    ind = [d >= th for th in theta]
    kc = kc_ref[0]
    vc = vc_ref[0]
    imp = jnp.zeros((tq, ncp), _F32)
    for r in range(NSA_GROUP):
        base = (g * NSA_GROUP + r) * N_BUCKETS
        bias = jnp.full((tq, ncp), tab_ref[base], _F32)
        for k in range(1, N_BUCKETS):
            bias = bias + jnp.where(ind[k - 1], tab_ref[base + k] - tab_ref[base + k - 1], 0.0)
        s = lax.dot_general(qa_ref[0, r * tq:(r + 1) * tq, :], kc, (((1,), (1,)), ((), ())),
                            preferred_element_type=_F32) + bias
        s = jnp.where(ok, s, NEG_INF)
        m = jnp.max(s, axis=-1, keepdims=True)
        e = jnp.where(ok, jnp.exp(s - m), 0.0)
        p = e / jnp.maximum(jnp.sum(e, axis=-1, keepdims=True), 1e-30)
        o_ref[0, r * tq:(r + 1) * tq, :] = jnp.dot(p.astype(_BF), vc, preferred_element_type=_F32)
        imp = imp + p
    p_s = _dot3(imp, msel_ref[...])
    j = lax.broadcasted_iota(jnp.int32, (tq, nsl), 1)
    qb = (qpos0 + i * tq + lax.broadcasted_iota(jnp.int32, (tq, nsl), 0)) >> int(math.log2(SEL_BLOCK))
    valid = j <= qb
    forced = (j == 0) | (j == qb) | (j == qb - 1)
    score = jnp.where(valid, jnp.where(forced, FORCE_SCORE, p_s), NEG_INF)
    sel = jnp.zeros((tq, nsl), jnp.bool_)
    big = jnp.int32(1 << 20)
    for _ in range(n_sel):
        mx = jnp.max(score, axis=-1, keepdims=True)
        firsti = jnp.min(jnp.where(score == mx, j, big), axis=-1, keepdims=True)
        hit = j == firsti
        sel = sel | (hit & (mx > 0.5 * NEG_INF))
        score = jnp.where(hit, -3e38, score)
    selb_ref[0] = jnp.where(sel, 0.0, NEG_INF).astype(_BF)


def _cmp_select(qa, kc, vc, tab, tq, qpos0, n_blocks):
    bg, rows, _ = qa.shape
    ncp = kc.shape[1]
    nq = rows // (NSA_GROUP * tq)
    nsl = _round_up(n_blocks, LANES)
    nn = np.arange(ncp)[:, None]
    jj = np.arange(nsl)[None, :]
    msel = ((nn >= CMP_PER_SEL * jj - 1) & (nn <= CMP_PER_SEL * jj + CMP_PER_SEL - 1) & (jj < n_blocks))
    msel = jnp.asarray(msel.astype(np.float32), _BF)
    tabf = tab[:, :NSA_HEADS].T.reshape(-1)
    kern = functools.partial(_cmp_select_kernel, _bucket_thresholds(), tq, qpos0, min(NSA_TOPN, n_blocks))
    return pl.pallas_call(
        kern,
        grid=(bg, nq),
        in_specs=[pl.BlockSpec(memory_space=pltpu.SMEM),
                  pl.BlockSpec((1, NSA_GROUP * tq, HEAD_DIM), lambda b, i: (b, i, 0)),
                  pl.BlockSpec((1, ncp, HEAD_DIM), lambda b, i: (b, 0, 0)),
                  pl.BlockSpec((1, ncp, HEAD_DIM), lambda b, i: (b, 0, 0)),
                  pl.BlockSpec((ncp, nsl), lambda b, i: (0, 0))],
        out_specs=[pl.BlockSpec((1, NSA_GROUP * tq, HEAD_DIM), lambda b, i: (b, i, 0)),
                   pl.BlockSpec((1, tq, nsl), lambda b, i: (b, i, 0))],
        out_shape=[jax.ShapeDtypeStruct((bg, rows, HEAD_DIM), _F32),
                   jax.ShapeDtypeStruct((bg, nq * tq, nsl), _BF)],
        compiler_params=_cparams("parallel", "parallel"), name="nsa_cmp_select",
    )(tabf, qa, kc, vc, msel)


def _logf_cumsum_kernel(n_new, x_ref, u_ref, lf_ref, c_ref, hi_ref, mid_ref, lo_ref):
    length = x_ref.shape[1]
    x = x_ref[...]
    col = lax.broadcasted_iota(jnp.int32, x.shape, 1)
    ls = jnp.minimum(x, 0.0) - jnp.log1p(jnp.exp(-jnp.abs(x)))
    lf = jnp.where(col >= length - n_new, ls, x)
    lf_ref[...] = lf
    u = u_ref[...]
    carry = jnp.zeros((x.shape[0], 1), _F32)
    for k in range(length // LANES):
        blk = _dot3(lf[:, k * LANES:(k + 1) * LANES], u) + carry
        c_ref[:, k * LANES:(k + 1) * LANES] = blk
        hi, mid, lo = _split3(blk)
        hi_ref[:, k * LANES:(k + 1) * LANES] = hi
        mid_ref[:, k * LANES:(k + 1) * LANES] = mid
        lo_ref[:, k * LANES:(k + 1) * LANES] = lo
        carry = blk[:, LANES - 1:LANES]


def _logf_cumsum(x, n_new, rows_per_step):
    rows, length = x.shape
    u = jnp.asarray(np.triu(np.ones((LANES, LANES), np.float32)), _BF)
    spec = pl.BlockSpec((rows_per_step, length), lambda i: (i, 0))
    lf, c, hi, mid, lo = pl.pallas_call(
        functools.partial(_logf_cumsum_kernel, n_new),
        grid=(rows // rows_per_step,),
        in_specs=[spec, pl.BlockSpec((LANES, LANES), lambda i: (0, 0))],
        out_specs=[spec] * 5,
        out_shape=[jax.ShapeDtypeStruct((rows, length), _F32)] * 2 + [jax.ShapeDtypeStruct((rows, length), _BF)] * 3,
        compiler_params=_cparams("parallel"), name="fox_logf_cumsum",
    )(x, u)
    return lf, c, (hi, mid, lo)


def _fox_aug_rows(c3):
    one = jnp.ones_like(c3[0])
    zero = jnp.zeros_like(c3[0])
    pad = [zero] * (BF16_ROWS - 6)
    qaug = jnp.stack(list(c3) + [one, one, one] + pad, axis=1)
    kaug = jnp.stack([one, one, one] + [-c for c in c3] + pad, axis=1)
    return qaug, kaug


def _gather_kernel(n_pg, pt_ref, *refs):
    ins, out = refs[:n_pg], refs[n_pg]
    for k in range(n_pg):
        out[0, :, k * PAGE_SIZE:(k + 1) * PAGE_SIZE] = ins[k][0]


def _gather_fm(pool_t, page_table):
    bsz, n_pages = page_table.shape
    f = pool_t.shape[1]
    n_pg = _pages_per_step(n_pages)

    def in_map(k):
        return lambda b, p, pt: (pt[b * n_pages + p * n_pg + k], 0, 0)

    grid_spec = pltpu.PrefetchScalarGridSpec(
        num_scalar_prefetch=1,
        grid=(bsz, n_pages // n_pg),
        in_specs=[pl.BlockSpec((1, f, PAGE_SIZE), in_map(k)) for k in range(n_pg)],
        out_specs=pl.BlockSpec((1, f, n_pg * PAGE_SIZE), lambda b, p, pt: (b, 0, p)),
    )
    return pl.pallas_call(
        functools.partial(_gather_kernel, n_pg), grid_spec=grid_spec,
        out_shape=jax.ShapeDtypeStruct((bsz, f, n_pages * PAGE_SIZE), pool_t.dtype),
        compiler_params=_cparams("parallel", "arbitrary"), name="page_gather",
    )(page_table.reshape(-1), *([pool_t] * n_pg))


def _gather_rows_kernel(n_pg, n_heads, pt_ref, *refs):
    ins, out = refs[:n_pg], refs[n_pg]
    for k in range(n_pg):
        for h in range(n_heads):
            out[h, 0, k * PAGE_SIZE:(k + 1) * PAGE_SIZE, :] = ins[k][0, h * HEAD_DIM:(h + 1) * HEAD_DIM, :].T


def _gather_rows(pool_t, page_table):
    bsz, n_pages = page_table.shape
    n_heads = pool_t.shape[1] // HEAD_DIM
    n_pg = _pages_per_step(n_pages)

    def in_map(k):
        return lambda b, p, pt: (pt[b * n_pages + p * n_pg + k], 0, 0)

    grid_spec = pltpu.PrefetchScalarGridSpec(
        num_scalar_prefetch=1,
        grid=(bsz, n_pages // n_pg),
        in_specs=[pl.BlockSpec((1, n_heads * HEAD_DIM, PAGE_SIZE), in_map(k)) for k in range(n_pg)],
        out_specs=pl.BlockSpec((n_heads, 1, n_pg * PAGE_SIZE, HEAD_DIM), lambda b, p, pt: (0, b, p, 0)),
    )
    return pl.pallas_call(
        functools.partial(_gather_rows_kernel, n_pg, n_heads), grid_spec=grid_spec,
        out_shape=jax.ShapeDtypeStruct((n_heads, bsz, n_pages * PAGE_SIZE, HEAD_DIM), pool_t.dtype),
        compiler_params=_cparams("parallel", "arbitrary"), name="page_gather_rows",
    )(page_table.reshape(-1), *([pool_t] * n_pg))


def _pages_fm(cache_l):
    return cache_l.transpose(0, 2, 3, 4, 1)


def _kv_group_onehot(heads, group):
    return (np.arange(heads)[:, None] // group == np.arange(heads // group)[None, :]).astype(np.float32)


def _block_diag_queries(q, group):
    b, t, h, dh = q.shape
    oh = jnp.asarray(_kv_group_onehot(h, group))
    x = q[:, :, :, None, :] * oh[None, None, :, :, None]
    return x.reshape(b, t * h, (h // group) * dh).astype(_BF)


def _own_head_columns(o, t, heads, group):
    b = o.shape[0]
    oh = jnp.asarray(_kv_group_onehot(heads, group))
    x = o.reshape(b, t, heads, heads // group, HEAD_DIM) * oh[None, None, :, :, None]
    return jnp.sum(x, axis=3).reshape(b, t, heads * HEAD_DIM)


def _sample_bias(tabh, heads, d0, t, cols, window=None):
    tile = _bias_tile(tabh, d0, t, cols, window)
    tile = jnp.broadcast_to(tile, (heads, t, cols))
    return tile.transpose(1, 0, 2).reshape(t * heads, cols)


def _new_rows_fm(x):
    return _pad_last(x.transpose(0, 2, 1), PAGE_SIZE).astype(_BF)


def _nsa_q_rows(q, tq):
    b, t, _, dh = q.shape
    nq = t // tq
    x = q.reshape(b, nq, tq, NSA_KV_HEADS, NSA_GROUP, dh).transpose(0, 3, 1, 4, 2, 5)
    return x.reshape(b * NSA_KV_HEADS, nq * NSA_GROUP * tq, dh)


def _nsa_rows_back(o, b, t, tq):
    nq = t // tq
    x = o.reshape(b, NSA_KV_HEADS, nq, NSA_GROUP, tq, HEAD_DIM).transpose(0, 2, 4, 1, 3, 5)
    return x.reshape(b, t, NSA_HEADS * HEAD_DIM)


def _even_sample(z, li, cache_cmp, cache_sel, win_state, cache_fkv, cache_flogf, page_table, tab, cmp_w, tq):
    bs, n_pages = page_table.shape
    past = n_pages * PAGE_SIZE
    ts = z.shape[0] // bs
    q_a, kv_c, kv_s, kv_w, g_bm, q_f, kv_f, f_logit = _even_split(z, bs, ts)
    g = NSA_KV_HEADS
    tabn = tab[:, :NSA_HEADS]
    wide = g * HEAD_DIM
    total = past + PAGE_SIZE
    assert past % CMP_STRIDE == 0 and past % SEL_BLOCK == 0 and ts <= CMP_STRIDE

    pool_c = _pages_fm(cache_cmp[li]).reshape(-1, 2 * wide, PAGE_SIZE)
    cmp_tok = _nsa_compress_rows(_gather_rows(pool_c, page_table), *cmp_w)
    ncp = _round_up(cmp_tok.shape[2], LANES)
    cmp_tok = _pad_axis(cmp_tok, 2, ncp).astype(_BF)
    kc = cmp_tok[:g].transpose(1, 0, 2, 3).reshape(bs * g, ncp, HEAD_DIM)
    vc = cmp_tok[g:].transpose(1, 0, 2, 3).reshape(bs * g, ncp, HEAD_DIM)
    q_rows = _nsa_q_rows(_pad_axis(q_a * SCALE, 1, tq), tq).astype(_BF)
    o_c, selb = _cmp_select(q_rows, kc, vc, tab, tq, past, total // SEL_BLOCK)
    o_c = _nsa_rows_back(o_c, bs, tq, tq)[:, :ts]
    nsl = selb.shape[2]

    qbd = _block_diag_queries(q_a * SCALE, NSA_GROUP)
    sel_rows = jnp.repeat(selb.reshape(bs, g, tq, nsl)[:, :, :ts].transpose(0, 2, 1, 3), NSA_GROUP, axis=2)
    sel_rows = sel_rows.reshape(bs, ts * NSA_HEADS, nsl)
    e_sel = jnp.asarray((np.arange(nsl)[:, None] == np.arange(total)[None, :] // SEL_BLOCK).astype(np.float32), _BF)
    flat = lambda kv, s: kv[:, :, s].reshape(bs, ts, -1)
    o_s = _paged_attn(qbd, _pages_fm(cache_sel[li]), page_table, _sample_bias(tabn, NSA_HEADS, past, ts, total),
                      _new_rows_fm(flat(kv_s, 0)), _new_rows_fm(flat(kv_s, 1)), selb=sel_rows, e=e_sel)
    win_buf = win_state[li]
    wb = win_buf.shape[1]
    assert wb % PAGE_SIZE == 0
    wpages = _pages_fm(win_buf).reshape(bs, 2, g, HEAD_DIM, wb // PAGE_SIZE, PAGE_SIZE)
    wpages = wpages.transpose(0, 4, 1, 2, 3, 5).reshape(bs * (wb // PAGE_SIZE), 2, g, HEAD_DIM, PAGE_SIZE)
    wtable = jnp.arange(bs * (wb // PAGE_SIZE), dtype=jnp.int32).reshape(bs, wb // PAGE_SIZE)
    o_w = _paged_attn(qbd, wpages, wtable, _sample_bias(tabn, NSA_HEADS, wb, ts, wb + PAGE_SIZE, NSA_WINDOW),
                      _new_rows_fm(flat(kv_w, 0)), _new_rows_fm(flat(kv_w, 1)))
    o_s = _own_head_columns(o_s, ts, NSA_HEADS, NSA_GROUP)
    o_w = _own_head_columns(o_w, ts, NSA_HEADS, NSA_GROUP)

    past_l = _gather_fm(cache_flogf[li].transpose(0, 2, 1), page_table)
    lf_len = _round_up(past + ts, LANES)
    front = lf_len - past - ts
    fl_all = jnp.concatenate([jnp.zeros((bs, FOX_HEADS, front), _F32), past_l, f_logit.transpose(0, 2, 1)], axis=-1)
    logf_t, c, _ = _logf_cumsum(fl_all.reshape(bs * FOX_HEADS, lf_len), ts, min(bs * FOX_HEADS, 64))
    c = c.reshape(bs, FOX_HEADS, lf_len)
    logf = logf_t[:, lf_len - ts:].reshape(bs, FOX_HEADS, ts).transpose(0, 2, 1)
    c_new = c[:, :, front + past:]
    cq = jnp.broadcast_to(c_new.transpose(0, 2, 1).reshape(bs, ts * FOX_HEADS, 1), (bs, ts * FOX_HEADS, LANES))
    o_f = _paged_attn(_block_diag_queries(q_f * SCALE, 1), _pages_fm(cache_fkv[li]), page_table,
                      _sample_bias(None, FOX_HEADS, past, ts, total),
                      _new_rows_fm(flat(kv_f, 0)), _new_rows_fm(flat(kv_f, 1)),
                      cq=cq, ck=c[:, :, front:front + past], ck_new=_pad_last(c_new, PAGE_SIZE))
    o_f = _own_head_columns(o_f, ts, FOX_HEADS, 1)
    kvw_all = jnp.concatenate([win_buf, kv_w], axis=1)
    n_tok = bs * ts
    outs = (o_c.reshape(n_tok, -1), o_s.reshape(n_tok, -1), o_w.reshape(n_tok, -1), o_f.reshape(n_tok, -1),
            g_bm.reshape(n_tok, -1))
    return outs, (kv_c, kv_s, kvw_all[:, ts:], kv_f, logf)


def _odd_sample(z, li, cache_kv, page_table, tab):
    bs, n_pages = page_table.shape
    past = n_pages * PAGE_SIZE
    ts = z.shape[0] // bs
    hw = MOBA_HEADS * HEAD_DIM
    z = z.reshape(bs, ts, 3 * hw)
    q = z[..., :hw].reshape(bs, ts, MOBA_HEADS, HEAD_DIM)
    total = past + PAGE_SIZE
    assert ts <= MOBA_BLOCK and past % MOBA_BLOCK == 0
    pages = _pages_fm(cache_kv[li])
    qbd = _block_diag_queries(q * SCALE, 1)
    selb = _moba_pick(qbd, pages, page_table)
    e_blk = jnp.asarray((np.arange(LANES)[:, None] == np.arange(total)[None, :] // MOBA_BLOCK).astype(np.float32), _BF)
    o = _paged_attn(qbd, pages, page_table, _sample_bias(tab[:, :MOBA_HEADS], MOBA_HEADS, past, ts, total),
                    _new_rows_fm(z[..., hw:2 * hw]), _new_rows_fm(z[..., 2 * hw:]), selb=selb, e=e_blk)
    o = _own_head_columns(o, ts, MOBA_HEADS, 1)
    return o.reshape(bs * ts, hw), z[..., hw:].reshape(bs, ts, 2, MOBA_HEADS, HEAD_DIM)


def _even_split(z, b, t):
    q_a, kv_c, kv_s, kv_w, g_a, q_f, kv_f, f_logit = jnp.split(z.reshape(b, t, -1), _EVEN_CUTS, axis=-1)
    kvshape = (b, t, 2, NSA_KV_HEADS, HEAD_DIM)
    g_bm = g_a.reshape(b, t, NSA_HEADS, 3).transpose(0, 1, 3, 2).reshape(b, t, 3 * NSA_HEADS)
    return (q_a.reshape(b, t, NSA_HEADS, HEAD_DIM), kv_c.reshape(kvshape), kv_s.reshape(kvshape),
            kv_w.reshape(kvshape), g_bm, q_f.reshape(b, t, FOX_HEADS, HEAD_DIM),
            kv_f.reshape(b, t, 2, FOX_HEADS, HEAD_DIM), f_logit)


def _even_row_perm():
    cuts = (0,) + _EVEN_CUTS + (sum(_EVEN_SIZES),)
    seg = lambda k: np.arange(cuts[k], cuts[k + 1])
    gates = cuts[4] + (np.arange(NSA_HEADS)[None, :] * 3 + np.arange(3)[:, None]).reshape(-1)
    return np.concatenate([seg(0), seg(1), seg(2), seg(3), seg(5), seg(6), gates, seg(7)])


def _kv_leaf(kv_t, heads):
    bsz, _, s = kv_t.shape
    return kv_t.reshape(bsz, 2, heads, HEAD_DIM, s).transpose(0, 4, 1, 2, 3)


def _even_prompt(x_rows, tab, w_in, b_in, cmp_w, w_out, ln_g, ln_b, alpha, tm):
    bsz, s, d = x_rows.shape
    perm = _even_row_perm()
    w_t = w_in.T[perm].astype(_BF)
    scale = np.ones((_EVEN_OUT, 1), np.float32)
    scale[_QA:_QA + NSA_HEADS * HEAD_DIM] = SCALE
    scale[_QF:_QF + FOX_HEADS * HEAD_DIM] = SCALE
    segments = [(_KVC, _KVS - _KVC), (_KVS, _KVW - _KVS), (_KVW, _QF - _KVW), (_KVF, _GA - _KVF), (_GA, _EVEN_OUT - _GA)]
    zb, (kvc_t, kvs_t, kvw_t, kvf_t, tail_t) = _proj_fm(x_rows, w_t, b_in[perm].reshape(-1, 1), jnp.asarray(scale),
                                                         tm, True, segments)
    tabn = tab[:, :NSA_HEADS]
    g = NSA_KV_HEADS
    hb = HEAD_DIM

    assert s % SEL_BLOCK == 0
    cmp_tok = _nsa_compress(_chunks_from_fm(kvc_t), *cmp_w)
    ncp = _round_up(cmp_tok.shape[2], LANES)
    cmp_tok = _pad_axis(cmp_tok, 2, ncp).astype(_BF)
    kc = cmp_tok[:g].transpose(1, 0, 2, 3).reshape(bsz * g, ncp, HEAD_DIM)
    vc_t = cmp_tok[g:].transpose(1, 0, 3, 2).reshape(bsz * g, HEAD_DIM, ncp)
    tq, tk = 256, 512
    o_c, sel = _cmp_select_t(zb, kc, vc_t, tab, tq, s // SEL_BLOCK)

    pairs, deltas = _plan_tiles(s // tq, tq, tk, None, True)
    bias = _bias_tiles_t(tabn, deltas, tk, tq, None, NSA_GROUP)
    block_of_key = lambda n, blk: jnp.asarray(
        (np.arange(n)[:, None] == np.arange(s)[None, :] // blk).astype(np.float32), _BF)
    o_s = _flash_t(zb, _QA // (NSA_GROUP * hb), _KVS // hb, _KVS // hb + g, g, NSA_GROUP, True, bias, pairs, tq, tk,
                   qx=sel, kx=block_of_key(sel.shape[1], SEL_BLOCK))
    pairs, deltas = _plan_tiles(s // tq, tq, tk, NSA_WINDOW, True)
    bias = _bias_tiles_t(tabn, deltas, tk, tq, NSA_WINDOW, NSA_GROUP)
    o_w = _flash_t(zb, _QA // (NSA_GROUP * hb), _KVW // hb, _KVW // hb + g, g, NSA_GROUP, True, bias, pairs, tq, tk)

    f_logit_t = tail_t[:, _FL - _GA:_FL - _GA + FOX_HEADS, :]
    logf_t, _, c3 = _logf_cumsum(f_logit_t.reshape(bsz * FOX_HEADS, s), s, bsz * FOX_HEADS)
    qaug, kaug = _fox_aug_rows(c3)
    tqf = tkf = 512
    hps = HEADS_PER_STEP
    units = FOX_HEADS // hps
    pairs, deltas = _plan_tiles(s // tqf, tqf, tkf, None, False)
    bias = _bias_tiles_t(None, deltas, tkf, tqf, None, hps)
    per_unit = lambda a: a.reshape(bsz * units, hps * BF16_ROWS, s)
    o_f = _flash_t(zb, _QF // (hps * hb), _KVF // (hps * hb), _KVF // (hps * hb) + units, units, hps, False,
                   bias, pairs, tqf, tkf, qx=per_unit(qaug), kx=per_unit(kaug))

    wide = lambda o: o.reshape(bsz, -1, s)
    x_t = _even_out_fm(alpha, wide(o_c), wide(o_s), wide(o_w), tail_t, wide(o_f), w_out.T.astype(_BF), x_rows,
                       ln_g, ln_b, tm)
    kv_c = _kv_leaf(kvc_t, g)
    kv_s = _kv_leaf(kvs_t, g)
    kv_w = _kv_leaf(kvw_t, g)
    kv_f = _kv_leaf(kvf_t, FOX_HEADS)
    logf = logf_t.reshape(bsz, FOX_HEADS, s).transpose(0, 2, 1)
    return x_t, (kv_c, kv_s, kv_w[:, max(s - NSA_WINDOW, 0):], kv_f, logf)


def _odd_prompt(x_t, tab, w_in, w_out, ln_g, ln_b, alpha, tm):
    bsz, d, s = x_t.shape
    hw = MOBA_HEADS * HEAD_DIM
    scale = np.ones((3 * hw, 1), np.float32)
    scale[:hw] = SCALE
    zb, (kv_t,) = _proj_fm(x_t, w_in.T.astype(_BF), jnp.zeros((3 * hw, 1), _F32), jnp.asarray(scale), tm, False,
                           [(hw, 2 * hw)])
    assert s % MOBA_BLOCK == 0
    sel = _moba_select_t(kv_t, zb, 0, 1024 if s % 1024 == 0 else 256)
    tq = tk = 512
    hps = HEADS_PER_STEP
    units = MOBA_HEADS // hps
    pairs, deltas = _plan_tiles(s // tq, tq, tk, None, True)
    bias = _bias_tiles_t(tab[:, :MOBA_HEADS], deltas, tk, tq, None, hps)
    nbr = sel.shape[1]
    block_of_key = jnp.asarray((np.arange(nbr)[:, None] == np.arange(s)[None, :] // MOBA_BLOCK).astype(np.float32), _BF)
    o = _flash_t(zb, 0, units, 2 * units, units, hps, False, bias, pairs, tq, tk,
                 qx=sel.reshape(bsz * units, hps * nbr, s), kx=block_of_key)
    x_t = _odd_out_fm(alpha, o.reshape(bsz, hw, s), w_out.T.astype(_BF), x_t, ln_g, ln_b, tm)
    return x_t, _kv_leaf(kv_t, MOBA_HEADS)


def _kernel_impl(x_prompt, x_sample, cache_nsa_cmp, cache_nsa_sel, state_nsa_win, cache_fox_kv,
                 cache_fox_logf, cache_moba_kv, page_table, rel_bias, ln_g, ln_b, w_in_even, b_in_even,
                 nsa_cmp_pe, nsa_cmp_w1, nsa_cmp_b1, nsa_cmp_w2, nsa_cmp_b2, w_out_even, w_in_odd,
                 w_out_odd, moe_wg, moe_bg, moe_we, moe_be, moe_w1, moe_w3, moe_w2):
    bp, sp, d = x_prompt.shape
    bs, ts, _ = x_sample.shape
    n_pages = page_table.shape[1]
    past = n_pages * PAGE_SIZE
    depth = ln_g.shape[0]
    alpha = (2 * depth) ** 0.25
    ns_tok = bs * ts
    tm_p = 512
    tm_s = ns_tok
    assert sp % 1024 == 0 and depth % 2 == 0
    xp = x_prompt
    xs = x_sample.reshape(ns_tok, d)
    outs = {k: [] for k in ("cmp_p", "cmp_s", "sel_p", "sel_s", "win_p", "win_s", "fkv_p", "fkv_s",
                            "flf_p", "flf_s", "mkv_p", "mkv_s")}
    tq_s = BF16_ROWS

    for layer in range(depth):
        li = layer // 2
        if layer % 2 == 0:
            assert layer == 0
            cmp_w = (nsa_cmp_pe[li], nsa_cmp_w1[li], nsa_cmp_b1[li], nsa_cmp_w2[li], nsa_cmp_b2[li])
            xp, (kv_c, kv_s, kv_w, kv_f, logf) = _even_prompt(
                xp, rel_bias, w_in_even[li], b_in_even[li], cmp_w, w_out_even[li], ln_g[layer, 0], ln_b[layer, 0],
                alpha, tm_p)
            outs["cmp_p"].append(kv_c)
            outs["sel_p"].append(kv_s)
            outs["win_p"].append(kv_w)
            outs["fkv_p"].append(kv_f)
            outs["flf_p"].append(logf)
            w_in = w_in_even[li].astype(_BF)
            w_out = w_out_even[li].astype(_BF)
            z = _linear(xs, w_in, b_in_even[li], tm_s)
            (o_c, o_s, o_w, o_f, g_bm), (kv_c, kv_s, kv_w, kv_f, logf) = _even_sample(
                z, li, cache_nsa_cmp, cache_nsa_sel, state_nsa_win, cache_fox_kv, cache_fox_logf, page_table,
                rel_bias, cmp_w, tq_s)
            xs = _even_out(alpha, o_c, o_s, o_w, g_bm, o_f, w_out, xs, ln_g[layer, 0], ln_b[layer, 0], tm_s)
            outs["cmp_s"].append(kv_c)
            outs["sel_s"].append(kv_s)
            outs["win_s"].append(kv_w)
            outs["fkv_s"].append(kv_f)
            outs["flf_s"].append(logf)
        else:
            hw = MOBA_HEADS * HEAD_DIM
            xp, kv = _odd_prompt(xp, rel_bias, w_in_odd[li], w_out_odd[li], ln_g[layer, 0], ln_b[layer, 0], alpha, tm_p)
            outs["mkv_p"].append(kv)
            w_in = w_in_odd[li].astype(_BF)
            w_out = w_out_odd[li].astype(_BF)
            z = _linear(xs, w_in, jnp.zeros((3 * hw,), _F32), tm_s)
            o, kv = _odd_sample(z, li, cache_moba_kv, page_table, rel_bias)
            xs = _odd_out(alpha, o, w_out, xs, ln_g[layer, 0], ln_b[layer, 0], tm_s)
            outs["mkv_s"].append(kv)
        w1b, w3b, w2b = moe_w1[layer].astype(_BF), moe_w3[layer].astype(_BF), moe_w2[layer].astype(_BF)
        w13_t = jnp.concatenate([w1b.transpose(0, 2, 1), w3b.transpose(0, 2, 1)], axis=1)
        router = (moe_wg[layer], moe_bg[layer], moe_we[layer], moe_be[layer])
        xp = _moe_ln_fm(alpha, xp, *router, w13_t, w2b.transpose(0, 2, 1), ln_g[layer, 1], ln_b[layer, 1], tm_p,
                        layer == depth - 1)
        xs = _moe_ln(alpha, xs, *router, w1b, w3b, w2b, ln_g[layer, 1], ln_b[layer, 1], tm_s)

    st = lambda k: jnp.stack(outs[k])
    return (xp, xs.reshape(bs, ts, d), st("cmp_p"), st("cmp_s"), st("sel_p"), st("sel_s"),
            st("win_p"), st("win_s"), st("fkv_p"), st("fkv_s"), st("flf_p"), st("flf_s"), st("mkv_p"), st("mkv_s"))


def kernel(x_prompt, x_sample, cache_nsa_cmp, cache_nsa_sel, state_nsa_win, cache_fox_kv, cache_fox_logf, cache_moba_kv, page_table, rel_bias, ln_g, ln_b, w_in_even, b_in_even, nsa_cmp_pe, nsa_cmp_w1, nsa_cmp_b1, nsa_cmp_w2, nsa_cmp_b2, w_out_even, w_in_odd, w_out_odd, moe_wg, moe_bg, moe_we, moe_be, moe_w1, moe_w3, moe_w2):
    return _kernel_impl(x_prompt, x_sample, cache_nsa_cmp, cache_nsa_sel, state_nsa_win, cache_fox_kv,
                        cache_fox_logf, cache_moba_kv, page_table, rel_bias, ln_g, ln_b, w_in_even, b_in_even,
                        nsa_cmp_pe, nsa_cmp_w1, nsa_cmp_b1, nsa_cmp_w2, nsa_cmp_b2, w_out_even, w_in_odd,
                        w_out_odd, moe_wg, moe_bg, moe_we, moe_be, moe_w1, moe_w3, moe_w2)
```

```python
import functools
import math

import numpy as np
import jax
import jax.numpy as jnp
from jax import lax
from jax.experimental import pallas as pl
from jax.experimental.pallas import tpu as pltpu

_BF = jnp.bfloat16
_F32 = jnp.float32

HEAD_DIM = 64
NSA_KV_HEADS = 2
NSA_GROUP = 4
NSA_HEADS = NSA_KV_HEADS * NSA_GROUP
FOX_HEADS = 8
MOBA_HEADS = 16
CMP_LEN = 32
CMP_STRIDE = 16
CMP_HID = 128
SEL_BLOCK = 64
CMP_PER_SEL = SEL_BLOCK // CMP_STRIDE
NSA_TOPN = 16
NSA_WINDOW = 512
MOBA_BLOCK = 256
MOBA_TOPK = 3
N_BUCKETS = 32
T5_MAX_DISTANCE = 128
N_GROUPS = 4
EXPERTS_PER_GROUP = 4
N_EXPERTS = N_GROUPS * EXPERTS_PER_GROUP
PAGE_SIZE = 128
SCALE = HEAD_DIM ** -0.5
NEG_INF = -1e30
FORCE_SCORE = 1e4
LN_EPS = 1e-5
LANES = 128
SUBLANES = 8
BF16_ROWS = 16
VMEM_LIMIT = 48 * 1024 * 1024
PAGES_PER_STEP = 8
PAGE_BYTES_PER_STEP = 8 * 1024 * 1024
FOX_HEADS_PER_STEP = 4
MOBA_HEADS_PER_STEP = 8

_QA, _KVC, _KVS, _KVW, _QF, _KVF, _GA, _FL = 0, 512, 768, 1024, 1280, 1792, 2816, 2840
_EVEN_OUT = 2848
_EVEN_SIZES = (NSA_HEADS * HEAD_DIM, 2 * NSA_KV_HEADS * HEAD_DIM, 2 * NSA_KV_HEADS * HEAD_DIM,
               2 * NSA_KV_HEADS * HEAD_DIM, 3 * NSA_HEADS, FOX_HEADS * HEAD_DIM,
               2 * FOX_HEADS * HEAD_DIM, FOX_HEADS)
_EVEN_CUTS = tuple(int(c) for c in np.cumsum(_EVEN_SIZES)[:-1])


def _cparams(*sem):
    return pltpu.CompilerParams(dimension_semantics=sem, vmem_limit_bytes=VMEM_LIMIT)


def _round_up(n, m):
    return (n + m - 1) // m * m


def _split3(x):
    hi = x.astype(_BF)
    r1 = x - hi.astype(_F32)
    mid = r1.astype(_BF)
    lo = (r1 - mid.astype(_F32)).astype(_BF)
    return hi, mid, lo


def _dot3(x, m01):
    hi, mid, lo = _split3(x)
    acc = jnp.dot(hi, m01, preferred_element_type=_F32)
    acc += jnp.dot(mid, m01, preferred_element_type=_F32)
    acc += jnp.dot(lo, m01, preferred_element_type=_F32)
    return acc


def _t5_bucket_np(dist):
    n = np.maximum(dist, 0)
    exact = N_BUCKETS // 2
    nf = np.maximum(n, exact).astype(np.float32)
    far = exact + (np.log(nf / np.float32(exact)) / np.float32(math.log(T5_MAX_DISTANCE / exact))
                   * np.float32(N_BUCKETS - exact)).astype(np.int32)
    return np.where(n < exact, n, np.minimum(far, N_BUCKETS - 1)).astype(np.int32)


def _bucket_thresholds():
    d = np.arange(0, 4 * T5_MAX_DISTANCE)
    b = _t5_bucket_np(d)
    return [int(d[b >= k][0]) for k in range(1, N_BUCKETS)]


_FAR_DISTANCE = _bucket_thresholds()[-1]


def _pad_last(x, width):
    return jnp.pad(x, [(0, 0)] * (x.ndim - 1) + [(0, width - x.shape[-1])])


def _pad_axis(x, axis, size):
    pads = [(0, 0)] * x.ndim
    pads[axis] = (0, size - x.shape[axis])
    return jnp.pad(x, pads)


def _toeplitz(g, rows, cols):
    n = g.shape[-1]
    lead = g.shape[:-1]
    x = jnp.broadcast_to(g[..., None, :], lead + (rows, n)).reshape(lead + (rows * n,))
    return x[..., :rows * (n - 1)].reshape(lead + (rows, n - 1))[..., :cols]


def _distance_values(tabh, d, window):
    valid = (d >= 0) if window is None else ((d >= 0) & (d < window))
    if tabh is None:
        vals = jnp.zeros((1, d.shape[0]), _F32)
    else:
        vals = tabh[jnp.asarray(_t5_bucket_np(d))].T
    return jnp.where(jnp.asarray(valid)[None], vals, NEG_INF).astype(_F32)


def _bias_tile(tabh, d0, rows, cols, window=None):
    n = rows + cols
    m = np.arange(n)
    d = np.where(m < cols, d0 - m, d0 + n - m)
    return _toeplitz(_distance_values(tabh, d, window), rows, cols)


def _bias_tile_t(tabh, delta, tk, tq, window=None):
    n = tk + tq
    m = np.arange(n)
    d = np.where(m < tq, delta + m, delta + m - n)
    return _toeplitz(_distance_values(tabh, d, window), tk, tq)


def _plan_tiles(nq, tq, tk, window, has_table):
    deltas, pairs = [], []
    for qi in range(nq):
        q0 = qi * tq
        k_hi = (q0 + tq - 1) // tk
        k_lo = 0 if window is None else max(0, (q0 - (window - 1)) // tk)
        for ki in range(k_lo, k_hi + 1):
            delta = q0 - ki * tk
            dmin, dmax = delta - (tk - 1), delta + tq - 1
            plain = dmin >= (_FAR_DISTANCE if has_table else 0) and (window is None or dmax < window)
            if plain:
                pairs.append((qi, ki, -1))
            else:
                if delta not in deltas:
                    deltas.append(delta)
                pairs.append((qi, ki, deltas.index(delta)))
    pairs = [(q, k, b if b >= 0 else len(deltas)) for q, k, b in pairs]
    return pairs, deltas


def _bias_tiles_t(tabh, deltas, tk, tq, window, group):
    far = tk + tq + _FAR_DISTANCE
    tiles = [_bias_tile_t(tabh, dl, tk, tq, window) for dl in deltas]
    tiles.append(_bias_tile_t(tabh, far, tk, tq, None))
    t = jnp.stack(tiles, axis=1)
    if tabh is None:
        t = jnp.broadcast_to(t, (group,) + t.shape[1:])
    h, nb = t.shape[:2]
    t = t.reshape(h // group, group, nb, tk, tq).transpose(0, 2, 3, 1, 4)
    return t.reshape(h // group, nb, tk, group * tq)


def _layer_norm_cols(y, g, b):
    mu = jnp.mean(y, axis=0, keepdims=True)
    yc = y - mu
    var = jnp.mean(yc * yc, axis=0, keepdims=True)
    return yc * lax.rsqrt(var + LN_EPS) * g + b


def _proj_fm_kernel(row_major_in, segments, x_ref, w_ref, b_ref, sc_ref, zb_ref, *seg_refs):
    x = x_ref[0].astype(_BF)
    if row_major_in:
        z = lax.dot_general(w_ref[...], x, (((1,), (1,)), ((), ())), preferred_element_type=_F32)
    else:
        z = jnp.dot(w_ref[...], x, preferred_element_type=_F32)
    z = z + b_ref[...]
    zb_ref[0] = (z * sc_ref[...]).astype(_BF)
    for (row0, rows), ref in zip(segments, seg_refs):
        ref[0] = z[row0:row0 + rows]


def _proj_fm(x, w_t_bf, b_col, scale_col, tm, row_major_in, segments):
    bsz = x.shape[0]
    s = x.shape[1] if row_major_in else x.shape[2]
    d = x.shape[2] if row_major_in else x.shape[1]
    n = w_t_bf.shape[0]
    x_spec = (pl.BlockSpec((1, tm, d), lambda b, i: (b, i, 0)) if row_major_in
              else pl.BlockSpec((1, d, tm), lambda b, i: (b, 0, i)))
    col = pl.BlockSpec((n, 1), lambda b, i: (0, 0))
    out = lambda rows: pl.BlockSpec((1, rows, tm), lambda b, i: (b, 0, i))
    res = pl.pallas_call(
        functools.partial(_proj_fm_kernel, row_major_in, tuple(segments)),
        grid=(bsz, s // tm),
        in_specs=[x_spec, pl.BlockSpec((n, d), lambda b, i: (0, 0)), col, col],
        out_specs=[out(n)] + [out(rows) for _, rows in segments],
        out_shape=[jax.ShapeDtypeStruct((bsz, n, s), _BF)]
        + [jax.ShapeDtypeStruct((bsz, rows, s), _F32) for _, rows in segments],
        compiler_params=_cparams("parallel", "parallel"), name="proj_fm",
    )(x, w_t_bf, b_col, scale_col)
    return res[0], res[1:]


def _even_out_fm_kernel(alpha, oc_ref, os_ref, ow_ref, gz_ref, of_ref, wo_ref, x_ref, g_ref, b_ref, y_ref):
    wa = NSA_HEADS * HEAD_DIM
    sg = jax.nn.sigmoid(gz_ref[0])
    parts = []
    for h in range(NSA_HEADS):
        sl = slice(h * HEAD_DIM, (h + 1) * HEAD_DIM)
        parts.append(sg[h:h + 1] * oc_ref[0, sl, :] + sg[NSA_HEADS + h:NSA_HEADS + h + 1] * os_ref[0, sl, :]
                     + sg[2 * NSA_HEADS + h:2 * NSA_HEADS + h + 1] * ow_ref[0, sl, :])
    o_a = jnp.concatenate(parts, axis=0)
    m = jnp.dot(wo_ref[:, 0:wa], o_a.astype(_BF), preferred_element_type=_F32)
    m += jnp.dot(wo_ref[:, wa:], of_ref[0].astype(_BF), preferred_element_type=_F32)
    y_ref[0] = _layer_norm_cols(alpha * x_ref[0].T + m, g_ref[...], b_ref[...])


def _even_out_fm(alpha, o_c, o_s, o_w, gz, o_f, w_out_t_bf, x_rows, g, b, tm):
    bsz, s, d = x_rows.shape
    wa = NSA_HEADS * HEAD_DIM
    blk = lambda rows: pl.BlockSpec((1, rows, tm), lambda b_, i: (b_, 0, i))
    col = pl.BlockSpec((d, 1), lambda b_, i: (0, 0))
    return pl.pallas_call(
        functools.partial(_even_out_fm_kernel, alpha),
        grid=(bsz, s // tm),
        in_specs=[blk(wa), blk(wa), blk(wa), blk(gz.shape[1]), blk(FOX_HEADS * HEAD_DIM),
                  pl.BlockSpec(w_out_t_bf.shape, lambda b_, i: (0, 0)),
                  pl.BlockSpec((1, tm, d), lambda b_, i: (b_, i, 0)), col, col],
        out_specs=blk(d),
        out_shape=jax.ShapeDtypeStruct((bsz, d, s), _F32),
        compiler_params=_cparams("parallel", "parallel"), name="even_out_ln_fm",
    )(o_c, o_s, o_w, gz, o_f, w_out_t_bf, x_rows, g.reshape(d, 1), b.reshape(d, 1))


def _odd_out_fm_kernel(alpha, o_ref, wo_ref, x_ref, g_ref, b_ref, y_ref):
    m = jnp.dot(wo_ref[...], o_ref[0].astype(_BF), preferred_element_type=_F32)
    y_ref[0] = _layer_norm_cols(alpha * x_ref[0] + m, g_ref[...], b_ref[...])


def _odd_out_fm(alpha, o, w_out_t_bf, x_t, g, b, tm):
    bsz, d, s = x_t.shape
    blk = lambda rows: pl.BlockSpec((1, rows, tm), lambda b_, i: (b_, 0, i))
    col = pl.BlockSpec((d, 1), lambda b_, i: (0, 0))
    return pl.pallas_call(
        functools.partial(_odd_out_fm_kernel, alpha),
        grid=(bsz, s // tm),
        in_specs=[blk(o.shape[1]), pl.BlockSpec(w_out_t_bf.shape, lambda b_, i: (0, 0)), blk(d), col, col],
        out_specs=blk(d), out_shape=jax.ShapeDtypeStruct((bsz, d, s), _F32),
        compiler_params=_cparams("parallel", "parallel"), name="odd_out_ln_fm",
    )(o, w_out_t_bf, x_t, g.reshape(d, 1), b.reshape(d, 1))


def _route_gates(logits, axis):
    idx = lax.broadcasted_iota(jnp.int32, logits.shape, axis)
    big = jnp.int32(1 << 20)
    red = lambda f, v: f(v, axis=axis, keepdims=True)
    is_g = idx < N_GROUPS
    lg = jnp.where(is_g, logits, NEG_INF)
    mg = red(jnp.max, lg)
    sg = red(jnp.sum, jnp.where(is_g, jnp.exp(lg - mg), 0.0))
    p_top = 1.0 / sg
    g_top = red(jnp.min, jnp.where(lg == mg, idx, big))
    lo = N_GROUPS + EXPERTS_PER_GROUP * g_top
    in_grp = (idx >= lo) & (idx < lo + EXPERTS_PER_GROUP)
    le = jnp.where(in_grp, logits, NEG_INF)
    me = red(jnp.max, le)
    se = red(jnp.sum, jnp.where(in_grp, jnp.exp(le - me), 0.0))
    i1 = red(jnp.min, jnp.where(le == me, idx, big))
    le2 = jnp.where(idx == i1, NEG_INF, le)
    m2 = red(jnp.max, le2)
    i2 = red(jnp.min, jnp.where(le2 == m2, idx, big))
    w1 = 1.0 / se
    w2 = jnp.exp(m2 - me) / se
    tot = w1 + w2
    return jnp.where(idx == i1, p_top * w1 / tot, jnp.where(idx == i2, p_top * w2 / tot, 0.0))


def _moe_fm_kernel(alpha, rows_out, x_ref, wr_ref, br_ref, w13_ref, w2_ref, g_ref, b_ref, y_ref,
                   gate_ref, acc_ref, xb_ref):
    e = pl.program_id(2)
    f = w13_ref.shape[1] // 2

    @pl.when(e == 0)
    def _route():
        xb_ref[...] = x_ref[0].astype(_BF)
        logits = jnp.dot(wr_ref[...], xb_ref[...], preferred_element_type=_F32) + br_ref[...]
        gate_ref[...] = _route_gates(logits, 0)
        acc_ref[...] = jnp.zeros_like(acc_ref)

    h13 = jnp.dot(w13_ref[0], xb_ref[...], preferred_element_type=_F32)
    ge = gate_ref[pl.ds(e + N_GROUPS, 1), :]
    h = (jax.nn.silu(h13[0:f]) * h13[f:2 * f]) * ge
    acc_ref[...] += jnp.dot(w2_ref[0], h.astype(_BF), preferred_element_type=_F32)

    @pl.when(e == N_EXPERTS - 1)
    def _finish():
        y = _layer_norm_cols(alpha * x_ref[0] + acc_ref[...], g_ref[...], b_ref[...])
        y_ref[0] = y.T if rows_out else y


def _moe_ln_fm(alpha, x_t, wg, bg, we, be, w13_t_bf, w2_t_bf, g, b, tm, rows_out):
    bsz, d, s = x_t.shape
    rr = 2 * BF16_ROWS
    wr = jnp.zeros((rr, d), _F32).at[:N_GROUPS].set(wg.T).at[N_GROUPS:N_GROUPS + N_EXPERTS].set(we.T).astype(_BF)
    br = jnp.zeros((rr, 1), _F32).at[:N_GROUPS, 0].set(bg).at[N_GROUPS:N_GROUPS + N_EXPERTS, 0].set(be)
    f2 = w13_t_bf.shape[1]
    col = pl.BlockSpec((d, 1), lambda b_, i, e: (0, 0))
    out_spec = (pl.BlockSpec((1, tm, d), lambda b_, i, e: (b_, i, 0)) if rows_out
                else pl.BlockSpec((1, d, tm), lambda b_, i, e: (b_, 0, i)))
    out_shape = jax.ShapeDtypeStruct((bsz, s, d) if rows_out else (bsz, d, s), _F32)
    return pl.pallas_call(
        functools.partial(_moe_fm_kernel, alpha, rows_out),
        grid=(bsz, s // tm, N_EXPERTS),
        in_specs=[pl.BlockSpec((1, d, tm), lambda b_, i, e: (b_, 0, i)),
                  pl.BlockSpec((rr, d), lambda b_, i, e: (0, 0)),
                  pl.BlockSpec((rr, 1), lambda b_, i, e: (0, 0)),
                  pl.BlockSpec((1, f2, d), lambda b_, i, e: (e, 0, 0)),
                  pl.BlockSpec((1, d, f2 // 2), lambda b_, i, e: (e, 0, 0)),
                  col, col],
        out_specs=out_spec, out_shape=out_shape,
        scratch_shapes=[pltpu.VMEM((rr, tm), _F32), pltpu.VMEM((d, tm), _F32), pltpu.VMEM((d, tm), _BF)],
        compiler_params=_cparams("parallel", "parallel", "arbitrary"), name="moe_ln_fm",
    )(x_t, wr, br, w13_t_bf, w2_t_bf, g.reshape(d, 1), b.reshape(d, 1))


_VROWS = HEAD_DIM + BF16_ROWS


def _flash_t_kernel(group, kv_shared, qx_heads, kx_global, qi_ref, ki_ref, bi_ref, first_ref, last_ref, *refs):
    refs = list(refs)
    q_ref, k_ref, v_ref, bias_ref = refs[:4]
    pos = 4
    if qx_heads:
        qx_ref, kx_ref = refs[pos:pos + 2]
        pos += 2
        xr = qx_ref.shape[1] // qx_heads
    o_ref, m_ref, acc_ref, s_ref = refs[pos:pos + 4]
    step = pl.program_id(1)
    tq = q_ref.shape[2]
    tk = k_ref.shape[2]

    @pl.when(first_ref[step] == 1)
    def _init():
        m_ref[...] = jnp.full_like(m_ref, NEG_INF)
        acc_ref[...] = jnp.zeros_like(acc_ref)

    for r in range(group):
        rows = slice(r * HEAD_DIM, (r + 1) * HEAD_DIM)
        q_t = q_ref[0, rows, :]
        k_t = k_ref[0, slice(0, HEAD_DIM) if kv_shared else rows, :]
        if qx_heads:
            qr = r if qx_heads > 1 else 0
            q_t = jnp.concatenate([q_t, qx_ref[0, qr * xr:(qr + 1) * xr, :]], axis=0)
            k_t = jnp.concatenate([k_t, kx_ref[...] if kx_global else kx_ref[0, r * xr:(r + 1) * xr, :]], axis=0)
        s_ref[r] = lax.dot_general(k_t, q_t, (((0,), (0,)), ((), ())), preferred_element_type=_F32)

    m_all = m_ref[...]
    acc_all = acc_ref[...]
    ones = jnp.ones((BF16_ROWS, tk), _BF)
    m_out, acc_out = [], []
    for r in range(group):
        kv_rows = slice(0, HEAD_DIM) if kv_shared else slice(r * HEAD_DIM, (r + 1) * HEAD_DIM)
        s = s_ref[r] + bias_ref[0, 0, :, r * tq:(r + 1) * tq]
        m_old = m_all[r:r + 1, :]
        m_new = jnp.maximum(m_old, jnp.max(s, axis=0, keepdims=True))
        a = jnp.exp(m_old - m_new)
        p = jnp.exp(s - m_new).astype(_BF)
        v_t = jnp.concatenate([v_ref[0, kv_rows, :], ones], axis=0)
        acc_out.append(a * acc_all[r * _VROWS:(r + 1) * _VROWS, :] + jnp.dot(v_t, p, preferred_element_type=_F32))
        m_out.append(m_new)
    pad = [jnp.zeros((SUBLANES - group, tq), _F32)] if group < SUBLANES else []
    m_ref[...] = jnp.concatenate(m_out + pad, axis=0)
    acc_ref[...] = jnp.concatenate(acc_out, axis=0)

    @pl.when(last_ref[step] == 1)
    def _done():
        for r in range(group):
            acc = acc_out[r]
            o_ref[0, r * HEAD_DIM:(r + 1) * HEAD_DIM, :] = (
                acc[0:HEAD_DIM] / jnp.maximum(acc[HEAD_DIM:HEAD_DIM + 1], 1e-30))


def _flash_t(zb, q_blk, k_blk, v_blk, units, group, kv_shared, bias, pairs, tq, tk, qx=None, kx=None):
    bsz, _, s = zb.shape
    bh = bsz * units
    hb = bias.shape[0]
    pairs = np.asarray(pairs, np.int32)
    qi, ki, bi = pairs[:, 0], pairs[:, 1], pairs[:, 2]
    first = np.concatenate([[1], (qi[1:] != qi[:-1]).astype(np.int32)]).astype(np.int32)
    last = np.concatenate([(qi[1:] != qi[:-1]).astype(np.int32), [1]]).astype(np.int32)
    gq = group * HEAD_DIM
    kvr = HEAD_DIM if kv_shared else gq
    in_specs = [pl.BlockSpec((1, gq, tq), lambda b, t, qi, ki, bi, f, l: (b // units, q_blk + b % units, qi[t])),
                pl.BlockSpec((1, kvr, tk), lambda b, t, qi, ki, bi, f, l: (b // units, k_blk + b % units, ki[t])),
                pl.BlockSpec((1, kvr, tk), lambda b, t, qi, ki, bi, f, l: (b // units, v_blk + b % units, ki[t])),
                pl.BlockSpec((1, 1, tk, group * tq), lambda b, t, qi, ki, bi, f, l: (b % hb, bi[t], 0, 0))]
    args = [zb, zb, zb, bias]
    qx_heads, kx_global = 0, False
    if qx is not None:
        kx_global = kx.ndim == 2
        xr = kx.shape[0] if kx_global else kx.shape[1] // group
        qx_heads = qx.shape[1] // xr
        assert qx_heads in (1, group) and xr % BF16_ROWS == 0
        in_specs.append(pl.BlockSpec((1, qx_heads * xr, tq), lambda b, t, qi, ki, bi, f, l: (b, 0, qi[t])))
        in_specs.append(pl.BlockSpec((xr, tk), lambda b, t, qi, ki, bi, f, l: (0, ki[t])) if kx_global else
                        pl.BlockSpec((1, group * xr, tk), lambda b, t, qi, ki, bi, f, l: (b, 0, ki[t])))
        args += [qx, kx]
    grid_spec = pltpu.PrefetchScalarGridSpec(
        num_scalar_prefetch=5, grid=(bh, len(qi)), in_specs=in_specs,
        out_specs=pl.BlockSpec((1, gq, tq), lambda b, t, qi, ki, bi, f, l: (b, 0, qi[t])),
        scratch_shapes=[pltpu.VMEM((SUBLANES, tq), _F32), pltpu.VMEM((group * _VROWS, tq), _F32),
                        pltpu.VMEM((group, tk, tq), _F32)],
    )
    return pl.pallas_call(
        functools.partial(_flash_t_kernel, group, kv_shared, qx_heads, kx_global), grid_spec=grid_spec,
        out_shape=jax.ShapeDtypeStruct((bh, gq, s), _F32),
        compiler_params=_cparams("parallel", "arbitrary"), name="flash_t",
    )(jnp.asarray(qi), jnp.asarray(ki), jnp.asarray(bi), jnp.asarray(first), jnp.asarray(last), *args)


def _topk_axis(score, n_sel, keep, axis):
    j = lax.broadcasted_iota(jnp.int32, score.shape, axis)
    big = jnp.int32(1 << 20)
    for _ in range(n_sel):
        mx = jnp.max(score, axis=axis, keepdims=True)
        firsti = jnp.min(jnp.where(score == mx, j, big), axis=axis, keepdims=True)
        hit = j == firsti
        keep = keep | (hit & (mx > 0.5 * NEG_INF))
        score = jnp.where(hit, -3e38, score)
    return keep


def _topk_rows(score, n_sel, keep):
    return _topk_axis(score, n_sel, keep, 0)


def _cmp_select_t_kernel(theta, tq, qpos0, n_sel, tab_ref, q_ref, kc_ref, vc_ref, msel_ref, o_ref, sel_ref, s_ref):
    g = pl.program_id(0) % NSA_KV_HEADS
    i = pl.program_id(1)
    ncp = kc_ref.shape[1]
    nsl = sel_ref.shape[1]
    t0 = qpos0 + i * tq
    n = lax.broadcasted_iota(jnp.int32, (ncp, tq), 0)
    t = t0 + lax.broadcasted_iota(jnp.int32, (ncp, tq), 1)
    ok = t - (n * CMP_STRIDE + (CMP_LEN - 1)) >= 0
    band = tq // CMP_STRIDE + CMP_STRIDE
    assert (CMP_STRIDE + 1) * CMP_STRIDE - (CMP_LEN - 1) >= _FAR_DISTANCE and band <= ncp
    step8 = CMP_STRIDE * SUBLANES
    assert qpos0 % step8 == 0 and tq % step8 == 0 and (ncp - band) % SUBLANES == 0
    w0 = pl.multiple_of(jnp.clip(t0 // CMP_STRIDE - CMP_STRIDE, 0, ncp - band), SUBLANES)
    nw = w0 + lax.broadcasted_iota(jnp.int32, (band, tq), 0)
    tw = t0 + lax.broadcasted_iota(jnp.int32, (band, tq), 1)
    dw = tw - (nw * CMP_STRIDE + (CMP_LEN - 1))
    ind = [dw >= th for th in theta]
    kc = kc_ref[0]
    vc_t = vc_ref[0]
    imp = jnp.zeros((ncp, tq), _F32)
    for r in range(NSA_GROUP):
        base = (g * NSA_GROUP + r) * N_BUCKETS
        far = tab_ref[base + N_BUCKETS - 1]
        corr = jnp.full((band, tq), tab_ref[base] - far, _F32)
        for k in range(1, N_BUCKETS):
            corr = corr + jnp.where(ind[k - 1], tab_ref[base + k] - tab_ref[base + k - 1], 0.0)
        rows = slice(r * HEAD_DIM, (r + 1) * HEAD_DIM)
        s_ref[...] = jnp.dot(kc, q_ref[0, rows, :], preferred_element_type=_F32) + far
        s_ref[pl.ds(w0, band), :] = s_ref[pl.ds(w0, band), :] + corr
        s = jnp.where(ok, s_ref[...], NEG_INF)
        m = jnp.max(s, axis=0, keepdims=True)
        e = jnp.where(ok, jnp.exp(s - m), 0.0)
        p = e / jnp.maximum(jnp.sum(e, axis=0, keepdims=True), 1e-30)
        o_ref[0, rows, :] = jnp.dot(vc_t, p.astype(_BF), preferred_element_type=_F32)
        imp = imp + p
    hi, mid, lo = _split3(imp)
    msel = msel_ref[...]
    p_s = (jnp.dot(msel, hi, preferred_element_type=_F32) + jnp.dot(msel, mid, preferred_element_type=_F32)
           + jnp.dot(msel, lo, preferred_element_type=_F32))
    j = lax.broadcasted_iota(jnp.int32, (nsl, tq), 0)
    qb = (t0 + lax.broadcasted_iota(jnp.int32, (nsl, tq), 1)) >> int(math.log2(SEL_BLOCK))
    valid = j <= qb
    forced = (j == 0) | (j == qb) | (j == qb - 1)
    score = jnp.where(valid, jnp.where(forced, FORCE_SCORE, p_s), NEG_INF)
    chosen = _topk_rows(score, n_sel, jnp.zeros((nsl, tq), jnp.bool_))
    sel_ref[0] = jnp.where(chosen, 0.0, NEG_INF).astype(_BF)


def _cmp_select_t(zb, kc, vc_t, tab, tq, n_blocks, qpos0=0):
    bsz, _, s = zb.shape
    bg = bsz * NSA_KV_HEADS
    ncp = kc.shape[1]
    nsl = _round_up(n_blocks, LANES)
    gq = NSA_GROUP * HEAD_DIM
    jj = np.arange(nsl)[:, None]
    nn = np.arange(ncp)[None, :]
    msel = ((nn >= CMP_PER_SEL * jj - 1) & (nn <= CMP_PER_SEL * jj + CMP_PER_SEL - 1) & (jj < n_blocks))
    msel = jnp.asarray(msel.astype(np.float32), _BF)
    tabf = tab[:, :NSA_HEADS].T.reshape(-1)
    kern = functools.partial(_cmp_select_t_kernel, _bucket_thresholds(), tq, qpos0, min(NSA_TOPN, n_blocks))
    return pl.pallas_call(
        kern,
        grid=(bg, s // tq),
        in_specs=[pl.BlockSpec(memory_space=pltpu.SMEM),
                  pl.BlockSpec((1, gq, tq), lambda b, i: (b // NSA_KV_HEADS, b % NSA_KV_HEADS, i)),
                  pl.BlockSpec((1, ncp, HEAD_DIM), lambda b, i: (b, 0, 0)),
                  pl.BlockSpec((1, HEAD_DIM, ncp), lambda b, i: (b, 0, 0)),
                  pl.BlockSpec((nsl, ncp), lambda b, i: (0, 0))],
        out_specs=[pl.BlockSpec((1, gq, tq), lambda b, i: (b, 0, i)),
                   pl.BlockSpec((1, nsl, tq), lambda b, i: (b, 0, i))],
        out_shape=[jax.ShapeDtypeStruct((bg, gq, s), _F32), jax.ShapeDtypeStruct((bg, nsl, s), _BF)],
        scratch_shapes=[pltpu.VMEM((ncp, tq), _F32)],
        compiler_params=_cparams("parallel", "parallel"), name="nsa_cmp_select_t",
    )(tabf, zb, kc, vc_t, msel)


def _moba_select_t_kernel(tq, n_blocks, kf_ref, avg_ref, q_ref, sel_ref, km_ref):
    i = pl.program_id(1)

    @pl.when(i == 0)
    def _means():
        km_ref[...] = _dot3(kf_ref[0], avg_ref[...])

    gate = lax.dot_general(km_ref[...].astype(_BF), q_ref[0], (((0,), (0,)), ((), ())),
                           preferred_element_type=_F32)
    j = lax.broadcasted_iota(jnp.int32, gate.shape, 0)
    qb = (i * tq + lax.broadcasted_iota(jnp.int32, gate.shape, 1)) >> int(math.log2(MOBA_BLOCK))
    score = jnp.where(j < qb, gate, NEG_INF)
    chosen = _topk_rows(score, min(MOBA_TOPK, n_blocks), j == qb)
    sel_ref[0] = jnp.where(chosen, 0.0, NEG_INF)[0:sel_ref.shape[1]].astype(_BF)


def _moba_select_t(zf, zb, k_blk, tq):
    bsz, _, s = zb.shape
    n_blocks = s // MOBA_BLOCK
    assert n_blocks <= LANES
    rows = _round_up(n_blocks, BF16_ROWS)
    avg = (np.arange(s)[:, None] // MOBA_BLOCK == np.arange(LANES)[None, :]).astype(np.float32) / MOBA_BLOCK
    bh = bsz * MOBA_HEADS
    return pl.pallas_call(
        functools.partial(_moba_select_t_kernel, tq, n_blocks),
        grid=(bh, s // tq),
        in_specs=[pl.BlockSpec((1, HEAD_DIM, s), lambda b, i: (b // MOBA_HEADS, k_blk + b % MOBA_HEADS, 0)),
                  pl.BlockSpec((s, LANES), lambda b, i: (0, 0)),
                  pl.BlockSpec((1, HEAD_DIM, tq), lambda b, i: (b // MOBA_HEADS, b % MOBA_HEADS, i))],
        out_specs=pl.BlockSpec((1, rows, tq), lambda b, i: (b, 0, i)),
        out_shape=jax.ShapeDtypeStruct((bh, rows, s), _BF),
        scratch_shapes=[pltpu.VMEM((HEAD_DIM, LANES), _F32)],
        compiler_params=_cparams("parallel", "arbitrary"), name="moba_select_t",
    )(zf, jnp.asarray(avg, _BF), zb)


def _linear_kernel(x_ref, w_ref, b_ref, o_ref):
    o_ref[...] = jnp.dot(x_ref[...].astype(_BF), w_ref[...], preferred_element_type=_F32) + b_ref[...]


def _linear(x, w_bf, b, tm):
    m, k = x.shape
    n = w_bf.shape[1]
    return pl.pallas_call(
        _linear_kernel,
        grid=(m // tm,),
        in_specs=[pl.BlockSpec((tm, k), lambda i: (i, 0)),
                  pl.BlockSpec((k, n), lambda i: (0, 0)),
                  pl.BlockSpec((1, n), lambda i: (0, 0))],
        out_specs=pl.BlockSpec((tm, n), lambda i: (i, 0)),
        out_shape=jax.ShapeDtypeStruct((m, n), _F32),
        compiler_params=_cparams("parallel"),
        name="linear",
    )(x, w_bf, b.reshape(1, n))


def _layer_norm_rows(y, g, b):
    mu = jnp.mean(y, axis=-1, keepdims=True)
    yc = y - mu
    var = jnp.mean(yc * yc, axis=-1, keepdims=True)
    return yc * lax.rsqrt(var + LN_EPS) * g + b


def _even_out_kernel(alpha, oc_ref, os_ref, ow_ref, gl_ref, of_ref, ex_ref, wa_ref, wf_ref, x_ref,
                     g_ref, b_ref, y_ref):
    wa = NSA_HEADS * HEAD_DIM
    gexp = _dot3(jax.nn.sigmoid(gl_ref[...]), ex_ref[...])
    o_a = (gexp[:, 0:wa] * oc_ref[...] + gexp[:, wa:2 * wa] * os_ref[...] + gexp[:, 2 * wa:3 * wa] * ow_ref[...])
    m = jnp.dot(o_a.astype(_BF), wa_ref[...], preferred_element_type=_F32)
    m += jnp.dot(of_ref[...].astype(_BF), wf_ref[...], preferred_element_type=_F32)
    y_ref[...] = _layer_norm_rows(alpha * x_ref[...] + m, g_ref[...], b_ref[...])


def _even_out(alpha, o_c, o_s, o_w, gl, o_f, w_out_bf, x, g, b, tm):
    n, d = x.shape
    wa = NSA_HEADS * HEAD_DIM
    wf = FOX_HEADS * HEAD_DIM
    ex_np = np.zeros((LANES, 3 * wa), np.float32)
    for j in range(3):
        for h in range(NSA_HEADS):
            ex_np[j * NSA_HEADS + h, j * wa + h * HEAD_DIM:j * wa + (h + 1) * HEAD_DIM] = 1.0
    ex = jnp.asarray(ex_np, _BF)
    gl = _pad_last(gl, LANES)
    row = lambda w: pl.BlockSpec((tm, w), lambda i: (i, 0))
    full = lambda a: pl.BlockSpec(a.shape, lambda i: (0,) * a.ndim)
    args = (o_c, o_s, o_w, gl, o_f, ex, w_out_bf[:wa], w_out_bf[wa:], x, g.reshape(1, d), b.reshape(1, d))
    specs = [row(wa), row(wa), row(wa), row(LANES), row(wf), full(ex), full(args[6]), full(args[7]),
             row(d), full(args[9]), full(args[10])]
    return pl.pallas_call(
        functools.partial(_even_out_kernel, alpha),
        grid=(n // tm,), in_specs=specs, out_specs=row(d),
        out_shape=jax.ShapeDtypeStruct((n, d), _F32),
        compiler_params=_cparams("parallel"), name="even_out_ln",
    )(*args)


def _odd_out_kernel(alpha, o_ref, w_ref, x_ref, g_ref, b_ref, y_ref):
    m = jnp.dot(o_ref[...].astype(_BF), w_ref[...], preferred_element_type=_F32)
    y_ref[...] = _layer_norm_rows(alpha * x_ref[...] + m, g_ref[...], b_ref[...])


def _odd_out(alpha, o, w_out_bf, x, g, b, tm):
    n, d = x.shape
    row = lambda w: pl.BlockSpec((tm, w), lambda i: (i, 0))
    full = lambda shp: pl.BlockSpec(shp, lambda i: (0,) * len(shp))
    return pl.pallas_call(
        functools.partial(_odd_out_kernel, alpha),
        grid=(n // tm,),
        in_specs=[row(o.shape[1]), full(w_out_bf.shape), row(d), full((1, d)), full((1, d))],
        out_specs=row(d), out_shape=jax.ShapeDtypeStruct((n, d), _F32),
        compiler_params=_cparams("parallel"), name="odd_out_ln",
    )(o, w_out_bf, x, g.reshape(1, d), b.reshape(1, d))


def _moe_kernel(alpha, x_ref, wr_ref, br_ref, w1_ref, w3_ref, w2_ref, g_ref, b_ref, y_ref,
                gate_ref, acc_ref):
    e = pl.program_id(1)
    x = x_ref[...]

    @pl.when(e == 0)
    def _route():
        logits = jnp.dot(x.astype(_BF), wr_ref[...], preferred_element_type=_F32) + br_ref[...]
        gate_ref[...] = _route_gates(logits, 1)
        acc_ref[...] = jnp.zeros_like(acc_ref)

    xb = x.astype(_BF)
    h1 = jnp.dot(xb, w1_ref[0], preferred_element_type=_F32)
    h3 = jnp.dot(xb, w3_ref[0], preferred_element_type=_F32)
    gate = gate_ref[...]
    lane = lax.broadcasted_iota(jnp.int32, gate.shape, 1)
    ge = jnp.sum(jnp.where(lane == e + N_GROUPS, gate, 0.0), axis=-1, keepdims=True)
    h = (jax.nn.silu(h1) * h3) * ge
    acc_ref[...] += jnp.dot(h.astype(_BF), w2_ref[0], preferred_element_type=_F32)

    @pl.when(e == N_EXPERTS - 1)
    def _finish():
        y_ref[...] = _layer_norm_rows(alpha * x + acc_ref[...], g_ref[...], b_ref[...])


def _moe_ln(alpha, x, wg, bg, we, be, w1_bf, w3_bf, w2_bf, g, b, tm):
    n, d = x.shape
    f = w1_bf.shape[2]
    wr = jnp.zeros((d, LANES), _F32).at[:, :N_GROUPS].set(wg).at[:, N_GROUPS:N_GROUPS + N_EXPERTS].set(we)
    wr = wr.astype(_BF)
    br = jnp.zeros((1, LANES), _F32).at[0, :N_GROUPS].set(bg).at[0, N_GROUPS:N_GROUPS + N_EXPERTS].set(be)
    return pl.pallas_call(
        functools.partial(_moe_kernel, alpha),
        grid=(n // tm, N_EXPERTS),
        in_specs=[pl.BlockSpec((tm, d), lambda i, e: (i, 0)),
                  pl.BlockSpec((d, LANES), lambda i, e: (0, 0)),
                  pl.BlockSpec((1, LANES), lambda i, e: (0, 0)),
                  pl.BlockSpec((1, d, f), lambda i, e: (e, 0, 0)),
                  pl.BlockSpec((1, d, f), lambda i, e: (e, 0, 0)),
                  pl.BlockSpec((1, f, d), lambda i, e: (e, 0, 0)),
                  pl.BlockSpec((1, d), lambda i, e: (0, 0)),
                  pl.BlockSpec((1, d), lambda i, e: (0, 0))],
        out_specs=pl.BlockSpec((tm, d), lambda i, e: (i, 0)),
        out_shape=jax.ShapeDtypeStruct((n, d), _F32),
        scratch_shapes=[pltpu.VMEM((tm, LANES), _F32), pltpu.VMEM((tm, d), _F32)],
        compiler_params=_cparams("parallel", "arbitrary"), name="moe_ln",
    )(x, wr, br, w1_bf, w3_bf, w2_bf, g.reshape(1, d), b.reshape(1, d))


def _pages_per_step(n_pages, page_bytes=PAGE_BYTES_PER_STEP // PAGES_PER_STEP):
    return next(n for n in (2 * PAGES_PER_STEP, PAGES_PER_STEP, 4, 2, 1)
                if n_pages % n == 0 and (n * page_bytes <= PAGE_BYTES_PER_STEP or n == 1))


def _paged_attn_kernel(n_pg, n_steps, kvh, sel, fox, pt_ref, *refs):
    refs = list(refs)
    qbd_ref = refs[0]
    pages = refs[1:1 + n_pg]
    pos = 1 + n_pg
    bias_ref, bias_new_ref, knew_ref, vnew_ref = refs[pos:pos + 4]
    pos += 4
    if sel:
        selb_ref, e_ref, e_new_ref = refs[pos:pos + 3]
        pos += 3
    if fox:
        cq_ref, ck_ref, ck_new_ref = refs[pos:pos + 3]
        pos += 3
    o_ref, m_ref, l_ref, acc_ref = refs[pos:pos + 4]
    step = pl.program_id(1)
    qbd = qbd_ref[0]
    r, f = qbd.shape
    nt = (((1,), (1,)), ((), ()))

    @pl.when(step == 0)
    def _init():
        m_ref[...] = jnp.full_like(m_ref, NEG_INF)
        l_ref[...] = jnp.zeros_like(l_ref)
        acc_ref[...] = jnp.zeros_like(acc_ref)

    def extra(bias, e, ck):
        add = bias
        if sel:
            add = add + jnp.dot(selb_ref[0], e, preferred_element_type=_F32)
        if fox:
            add = add + cq_ref[0][:, 0:1] - jnp.concatenate([ck] * (r // kvh), axis=0)
        return add

    def absorb(k_list, v_list, add):
        s = jnp.concatenate([jnp.dot(qbd, k, preferred_element_type=_F32) for k in k_list], axis=1) + add
        m_old = m_ref[...]
        m_new = jnp.maximum(m_old, jnp.max(s, axis=1, keepdims=True))
        a = jnp.exp(m_old - m_new)
        p = jnp.exp(s - m_new)
        l_ref[...] = a * l_ref[...] + jnp.sum(p, axis=1, keepdims=True)
        pv = None
        for j, v in enumerate(v_list):
            t = lax.dot_general(p[:, j * PAGE_SIZE:(j + 1) * PAGE_SIZE].astype(_BF), v, nt,
                                preferred_element_type=_F32)
            pv = t if pv is None else pv + t
        acc_ref[...] = a * acc_ref[...] + pv
        m_ref[...] = m_new

    absorb([pg[0, 0].reshape(f, PAGE_SIZE).astype(_BF) for pg in pages],
           [pg[0, 1].reshape(f, PAGE_SIZE).astype(_BF) for pg in pages],
           extra(bias_ref[...], e_ref[...] if sel else None, ck_ref[0] if fox else None))

    @pl.when(step == n_steps - 1)
    def _done():
        absorb([knew_ref[0]], [vnew_ref[0]],
               extra(bias_new_ref[...], e_new_ref[...] if sel else None, ck_new_ref[0] if fox else None))
        o_ref[0] = acc_ref[...] / jnp.maximum(l_ref[...], 1e-30)


def _paged_attn(qbd, pages_t, page_table, bias, knew_t, vnew_t, selb=None, e=None, cq=None, ck=None, ck_new=None):
    bsz, r, f = qbd.shape
    n_pages = page_table.shape[1]
    kvh = pages_t.shape[2]
    length = n_pages * PAGE_SIZE
    n_pg = _pages_per_step(n_pages, 2 * f * PAGE_SIZE * 4)
    n_steps = n_pages // n_pg
    sel = selb is not None
    fox = cq is not None
    w = n_pg * PAGE_SIZE

    def page_map(k):
        return lambda b, p, pt: (pt[b * n_pages + p * n_pg + k], 0, 0, 0, 0)

    per_b = lambda shape: pl.BlockSpec((1,) + shape, lambda b, p, pt: (b, 0, 0))
    whole = lambda shape: pl.BlockSpec(shape, lambda b, p, pt: (0, 0))
    in_specs = [per_b((r, f))]
    in_specs += [pl.BlockSpec((1, 2, kvh, HEAD_DIM, PAGE_SIZE), page_map(k)) for k in range(n_pg)]
    in_specs += [pl.BlockSpec((r, w), lambda b, p, pt: (0, p)), whole((r, PAGE_SIZE)),
                 per_b((f, PAGE_SIZE)), per_b((f, PAGE_SIZE))]
    args = [qbd] + [pages_t] * n_pg + [bias[:, :length], bias[:, length:], knew_t, vnew_t]
    if sel:
        nbl = selb.shape[2]
        in_specs += [per_b((r, nbl)), pl.BlockSpec((nbl, w), lambda b, p, pt: (0, p)), whole((nbl, PAGE_SIZE))]
        args += [selb, e[:, :length], e[:, length:]]
    if fox:
        in_specs += [per_b((r, LANES)), pl.BlockSpec((1, kvh, w), lambda b, p, pt: (b, 0, p)), per_b((kvh, PAGE_SIZE))]
        args += [cq, ck, ck_new]
    grid_spec = pltpu.PrefetchScalarGridSpec(
        num_scalar_prefetch=1, grid=(bsz, n_steps), in_specs=in_specs,
        out_specs=per_b((r, f)),
        scratch_shapes=[pltpu.VMEM((r, 1), _F32), pltpu.VMEM((r, 1), _F32), pltpu.VMEM((r, f), _F32)],
    )
    return pl.pallas_call(
        functools.partial(_paged_attn_kernel, n_pg, n_steps, kvh, sel, fox), grid_spec=grid_spec,
        out_shape=jax.ShapeDtypeStruct((bsz, r, f), _F32),
        compiler_params=_cparams("parallel", "arbitrary"), name="paged_attn",
    )(page_table.reshape(-1), *args)


def _moba_pick_kernel(n_pg, n_steps, n_blocks, n_top, pt_ref, qbd_ref, *refs):
    pages = refs[:n_pg]
    selb_ref, km_ref = refs[n_pg], refs[n_pg + 1]
    step = pl.program_id(1)
    f = km_ref.shape[0]

    @pl.when(step == 0)
    def _init():
        km_ref[...] = jnp.zeros_like(km_ref)

    lane = lax.broadcasted_iota(jnp.int32, km_ref.shape, 1)
    km = km_ref[...]
    for k in range(n_pg):
        col = jnp.sum(pages[k][0, 0].reshape(f, PAGE_SIZE), axis=1, keepdims=True)
        blk = (step * n_pg + k) // (MOBA_BLOCK // PAGE_SIZE)
        km = jnp.where(lane == blk, km + col, km)
    km_ref[...] = km

    @pl.when(step == n_steps - 1)
    def _pick():
        means = (km * (1.0 / MOBA_BLOCK)).astype(_BF)
        gate = jnp.dot(qbd_ref[0], means, preferred_element_type=_F32)
        j = lax.broadcasted_iota(jnp.int32, gate.shape, 1)
        score = jnp.where(j < n_blocks, gate, NEG_INF)
        chosen = _topk_axis(score, n_top, j == n_blocks, 1)
        selb_ref[0] = jnp.where(chosen, 0.0, NEG_INF).astype(_BF)


def _moba_pick(qbd, pages_t, page_table):
    bsz, r, f = qbd.shape
    n_pages = page_table.shape[1]
    kvh = pages_t.shape[2]
    n_blocks = n_pages * PAGE_SIZE // MOBA_BLOCK
    assert n_blocks < LANES and (n_pages * PAGE_SIZE) % MOBA_BLOCK == 0
    n_pg = _pages_per_step(n_pages, f * PAGE_SIZE * 4)
    n_steps = n_pages // n_pg

    def page_map(k):
        return lambda b, p, pt: (pt[b * n_pages + p * n_pg + k], 0, 0, 0, 0)

    grid_spec = pltpu.PrefetchScalarGridSpec(
        num_scalar_prefetch=1, grid=(bsz, n_steps),
        in_specs=[pl.BlockSpec((1, r, f), lambda b, p, pt: (b, 0, 0))]
        + [pl.BlockSpec((1, 1, kvh, HEAD_DIM, PAGE_SIZE), page_map(k)) for k in range(n_pg)],
        out_specs=pl.BlockSpec((1, r, LANES), lambda b, p, pt: (b, 0, 0)),
        scratch_shapes=[pltpu.VMEM((f, LANES), _F32)],
    )
    return pl.pallas_call(
        functools.partial(_moba_pick_kernel, n_pg, n_steps, n_blocks, min(MOBA_TOPK, n_blocks)), grid_spec=grid_spec,
        out_shape=jax.ShapeDtypeStruct((bsz, r, LANES), _BF),
        compiler_params=_cparams("parallel", "arbitrary"), name="moba_pick",
    )(page_table.reshape(-1), qbd, *([pages_t] * n_pg))


def _compress_kernel(x_ref, pe_ref, w1a_ref, w1b_ref, b1_ref, w2_ref, b2_ref, o_ref):
    x = x_ref[0, 0]
    pe = pe_ref[0]
    hf = jnp.dot((x + pe[0:1]).astype(_BF), w1a_ref[0], preferred_element_type=_F32)
    hs = jnp.dot((x + pe[1:2]).astype(_BF), w1b_ref[0], preferred_element_type=_F32)
    hs_next = pltpu.roll(hs, hs.shape[0] - 1, 0)
    h = jax.nn.gelu(hf + hs_next + b1_ref[0], approximate=True)
    o_ref[0, 0] = jnp.dot(h.astype(_BF), w2_ref[0], preferred_element_type=_F32) + b2_ref[0]


def _nsa_compress(x, pe, w1, b1, w2, b2):
    sg, bsz, nch, flat = x.shape
    pe2 = pe.reshape(2, CMP_STRIDE, 2, HEAD_DIM).transpose(2, 0, 1, 3).reshape(2, 2, flat)
    w1r = w1.reshape(2, CMP_STRIDE, 2, HEAD_DIM, CMP_HID).transpose(2, 0, 1, 3, 4).reshape(2, 2, flat, CMP_HID)
    w1r = w1r.astype(_BF)
    smap = lambda s, b: (s // NSA_KV_HEADS, 0, 0)
    return pl.pallas_call(
        _compress_kernel,
        grid=(sg, bsz),
        in_specs=[pl.BlockSpec((1, 1, nch, flat), lambda s, b: (s, b, 0, 0)),
                  pl.BlockSpec((1, 2, flat), smap),
                  pl.BlockSpec((1, flat, CMP_HID), smap),
                  pl.BlockSpec((1, flat, CMP_HID), smap),
                  pl.BlockSpec((1, 1, CMP_HID), smap),
                  pl.BlockSpec((1, CMP_HID, HEAD_DIM), smap),
                  pl.BlockSpec((1, 1, HEAD_DIM), smap)],
        out_specs=pl.BlockSpec((1, 1, nch, HEAD_DIM), lambda s, b: (s, b, 0, 0)),
        out_shape=jax.ShapeDtypeStruct((sg, bsz, nch, HEAD_DIM), _F32),
        compiler_params=_cparams("parallel", "parallel"), name="nsa_compress",
    )(x, pe2, w1r[:, 0], w1r[:, 1], b1.reshape(2, 1, CMP_HID), w2.astype(_BF), b2.reshape(2, 1, HEAD_DIM))


def _compress_rows_kernel(x_ref, pe_ref, w1a_ref, w1b_ref, b1_ref, w2_ref, b2_ref, o_ref):
    nch = o_ref.shape[2]
    hf = jnp.zeros((nch, CMP_HID), _F32)
    hs = jnp.zeros((nch, CMP_HID), _F32)
    for c in range(CMP_STRIDE):
        xc = x_ref[0, 0, pl.ds(c, nch, stride=CMP_STRIDE), :]
        hf += jnp.dot((xc + pe_ref[0, 0, c:c + 1, :]).astype(_BF), w1a_ref[0, c], preferred_element_type=_F32)
        hs += jnp.dot((xc + pe_ref[0, 1, c:c + 1, :]).astype(_BF), w1b_ref[0, c], preferred_element_type=_F32)
    hs_next = pltpu.roll(hs, nch - 1, 0)
    h = jax.nn.gelu(hf + hs_next + b1_ref[0], approximate=True)
    o_ref[0, 0] = jnp.dot(h.astype(_BF), w2_ref[0], preferred_element_type=_F32) + b2_ref[0]


def _nsa_compress_rows(x, pe, w1, b1, w2, b2):
    sg, bsz, length, dh = x.shape
    nch = length // CMP_STRIDE
    pe2 = pe.reshape(2, CMP_STRIDE, 2, dh).transpose(2, 0, 1, 3)
    w1r = w1.reshape(2, CMP_STRIDE, 2, dh, CMP_HID).transpose(2, 0, 1, 3, 4).astype(_BF)
    s3 = lambda s, b: (s // NSA_KV_HEADS, 0, 0)
    s4 = lambda s, b: (s // NSA_KV_HEADS, 0, 0, 0)
    return pl.pallas_call(
        _compress_rows_kernel,
        grid=(sg, bsz),
        in_specs=[pl.BlockSpec((1, 1, length, dh), lambda s, b: (s, b, 0, 0)),
                  pl.BlockSpec((1, 2, CMP_STRIDE, dh), s4),
                  pl.BlockSpec((1, CMP_STRIDE, dh, CMP_HID), s4),
                  pl.BlockSpec((1, CMP_STRIDE, dh, CMP_HID), s4),
                  pl.BlockSpec((1, 1, CMP_HID), s3),
                  pl.BlockSpec((1, CMP_HID, dh), s3),
                  pl.BlockSpec((1, 1, dh), s3)],
        out_specs=pl.BlockSpec((1, 1, nch, dh), lambda s, b: (s, b, 0, 0)),
        out_shape=jax.ShapeDtypeStruct((sg, bsz, nch, dh), _F32),
        compiler_params=_cparams("parallel", "parallel"), name="nsa_compress_rows",
    )(x, pe2, w1r[:, 0], w1r[:, 1], b1.reshape(2, 1, CMP_HID), w2.astype(_BF), b2.reshape(2, 1, dh))


def _chunks_from_fm(kv_t):
    bsz, _, length = kv_t.shape
    nch = length // CMP_STRIDE
    sg = 2 * NSA_KV_HEADS
    x = kv_t.reshape(bsz, sg, HEAD_DIM, nch, CMP_STRIDE).transpose(1, 0, 3, 4, 2)
    return x.reshape(sg, bsz, nch, CMP_STRIDE * HEAD_DIM)


def _logf_cumsum_kernel(n_new, x_ref, u_ref, lf_ref, c_ref, hi_ref, mid_ref, lo_ref):
    length = x_ref.shape[1]
    x = x_ref[...]
    col = lax.broadcasted_iota(jnp.int32, x.shape, 1)
    ls = jnp.minimum(x, 0.0) - jnp.log1p(jnp.exp(-jnp.abs(x)))
    lf = jnp.where(col >= length - n_new, ls, x)
    lf_ref[...] = lf
    u = u_ref[...]
    carry = jnp.zeros((x.shape[0], 1), _F32)
    for k in range(length // LANES):
        blk = _dot3(lf[:, k * LANES:(k + 1) * LANES], u) + carry
        c_ref[:, k * LANES:(k + 1) * LANES] = blk
        hi, mid, lo = _split3(blk)
        hi_ref[:, k * LANES:(k + 1) * LANES] = hi
        mid_ref[:, k * LANES:(k + 1) * LANES] = mid
        lo_ref[:, k * LANES:(k + 1) * LANES] = lo
        carry = blk[:, LANES - 1:LANES]


def _logf_cumsum(x, n_new, rows_per_step):
    rows, length = x.shape
    u = jnp.asarray(np.triu(np.ones((LANES, LANES), np.float32)), _BF)
    spec = pl.BlockSpec((rows_per_step, length), lambda i: (i, 0))
    lf, c, hi, mid, lo = pl.pallas_call(
        functools.partial(_logf_cumsum_kernel, n_new),
        grid=(rows // rows_per_step,),
        in_specs=[spec, pl.BlockSpec((LANES, LANES), lambda i: (0, 0))],
        out_specs=[spec] * 5,
        out_shape=[jax.ShapeDtypeStruct((rows, length), _F32)] * 2 + [jax.ShapeDtypeStruct((rows, length), _BF)] * 3,
        compiler_params=_cparams("parallel"), name="fox_logf_cumsum",
    )(x, u)
    return lf, c, (hi, mid, lo)


def _fox_aug_rows(c3):
    one = jnp.ones_like(c3[0])
    zero = jnp.zeros_like(c3[0])
    pad = [zero] * (BF16_ROWS - 6)
    qaug = jnp.stack(list(c3) + [one, one, one] + pad, axis=1)
    kaug = jnp.stack([one, one, one] + [-c for c in c3] + pad, axis=1)
    return qaug, kaug


def _gather_kernel(n_pg, pt_ref, *refs):
    ins, out = refs[:n_pg], refs[n_pg]
    for k in range(n_pg):
        out[0, :, k * PAGE_SIZE:(k + 1) * PAGE_SIZE] = ins[k][0]


def _gather_fm(pool_t, page_table):
    bsz, n_pages = page_table.shape
    f = pool_t.shape[1]
    n_pg = _pages_per_step(n_pages)

    def in_map(k):
        return lambda b, p, pt: (pt[b * n_pages + p * n_pg + k], 0, 0)

    grid_spec = pltpu.PrefetchScalarGridSpec(
        num_scalar_prefetch=1,
        grid=(bsz, n_pages // n_pg),
        in_specs=[pl.BlockSpec((1, f, PAGE_SIZE), in_map(k)) for k in range(n_pg)],
        out_specs=pl.BlockSpec((1, f, n_pg * PAGE_SIZE), lambda b, p, pt: (b, 0, p)),
    )
    return pl.pallas_call(
        functools.partial(_gather_kernel, n_pg), grid_spec=grid_spec,
        out_shape=jax.ShapeDtypeStruct((bsz, f, n_pages * PAGE_SIZE), pool_t.dtype),
        compiler_params=_cparams("parallel", "arbitrary"), name="page_gather",
    )(page_table.reshape(-1), *([pool_t] * n_pg))


def _gather_rows_kernel(n_pg, n_heads, pt_ref, *refs):
    ins, out = refs[:n_pg], refs[n_pg]
    for k in range(n_pg):
        for h in range(n_heads):
            out[h, 0, k * PAGE_SIZE:(k + 1) * PAGE_SIZE, :] = ins[k][0, h * HEAD_DIM:(h + 1) * HEAD_DIM, :].T


def _gather_rows(pool_t, page_table):
    bsz, n_pages = page_table.shape
    n_heads = pool_t.shape[1] // HEAD_DIM
    n_pg = _pages_per_step(n_pages)

    def in_map(k):
        return lambda b, p, pt: (pt[b * n_pages + p * n_pg + k], 0, 0)

    grid_spec = pltpu.PrefetchScalarGridSpec(
        num_scalar_prefetch=1,
        grid=(bsz, n_pages // n_pg),
        in_specs=[pl.BlockSpec((1, n_heads * HEAD_DIM, PAGE_SIZE), in_map(k)) for k in range(n_pg)],
        out_specs=pl.BlockSpec((n_heads, 1, n_pg * PAGE_SIZE, HEAD_DIM), lambda b, p, pt: (0, b, p, 0)),
    )
    return pl.pallas_call(
        functools.partial(_gather_rows_kernel, n_pg, n_heads), grid_spec=grid_spec,
        out_shape=jax.ShapeDtypeStruct((n_heads, bsz, n_pages * PAGE_SIZE, HEAD_DIM), pool_t.dtype),
        compiler_params=_cparams("parallel", "arbitrary"), name="page_gather_rows",
    )(page_table.reshape(-1), *([pool_t] * n_pg))


def _pages_fm(cache_l):
    return cache_l.transpose(0, 2, 3, 4, 1)


def _kv_group_onehot(heads, group):
    return (np.arange(heads)[:, None] // group == np.arange(heads // group)[None, :]).astype(np.float32)


def _block_diag_queries(q, group):
    b, t, h, dh = q.shape
    oh = jnp.asarray(_kv_group_onehot(h, group))
    x = q[:, :, :, None, :] * oh[None, None, :, :, None]
    return x.reshape(b, t * h, (h // group) * dh).astype(_BF)


def _own_head_columns(o, t, heads, group):
    b = o.shape[0]
    oh = jnp.asarray(_kv_group_onehot(heads, group))
    x = o.reshape(b, t, heads, heads // group, HEAD_DIM) * oh[None, None, :, :, None]
    return jnp.sum(x, axis=3).reshape(b, t, heads * HEAD_DIM)


def _sample_bias(tabh, heads, d0, t, cols, window=None):
    tile = _bias_tile(tabh, d0, t, cols, window)
    tile = jnp.broadcast_to(tile, (heads, t, cols))
    return tile.transpose(1, 0, 2).reshape(t * heads, cols)


def _new_rows_fm(x):
    return _pad_last(x.transpose(0, 2, 1), PAGE_SIZE).astype(_BF)


def _even_sample(z, li, cache_cmp, cache_sel, win_state, cache_fkv, cache_flogf, page_table, tab, cmp_w, tq):
    bs, n_pages = page_table.shape
    past = n_pages * PAGE_SIZE
    ts = z.shape[0] // bs
    q_a, kv_c, kv_s, kv_w, g_bm, q_f, kv_f, f_logit = _even_split(z, bs, ts)
    g = NSA_KV_HEADS
    tabn = tab[:, :NSA_HEADS]
    wide = g * HEAD_DIM
    total = past + PAGE_SIZE
    assert past % CMP_STRIDE == 0 and past % SEL_BLOCK == 0 and ts <= CMP_STRIDE

    pool_c = _pages_fm(cache_cmp[li]).reshape(-1, 2 * wide, PAGE_SIZE)
    cmp_tok = _nsa_compress_rows(_gather_rows(pool_c, page_table), *cmp_w)
    ncp = _round_up(cmp_tok.shape[2], LANES)
    cmp_tok = _pad_axis(cmp_tok, 2, ncp).astype(_BF)
    kc = cmp_tok[:g].transpose(1, 0, 2, 3).reshape(bs * g, ncp, HEAD_DIM)
    vc_t = cmp_tok[g:].transpose(1, 0, 3, 2).reshape(bs * g, HEAD_DIM, ncp)
    assert ts <= tq
    q_t = _pad_last((q_a * SCALE).reshape(bs, ts, NSA_HEADS * HEAD_DIM).transpose(0, 2, 1), tq).astype(_BF)
    o_c, sel_t = _cmp_select_t(q_t, kc, vc_t, tab, tq, total // SEL_BLOCK, past)
    o_c = o_c.reshape(bs, NSA_HEADS * HEAD_DIM, tq).transpose(0, 2, 1)[:, :ts]
    nsl = sel_t.shape[1]

    qbd = _block_diag_queries(q_a * SCALE, NSA_GROUP)
    sel_rows = jnp.repeat(sel_t.reshape(bs, g, nsl, tq)[..., :ts].transpose(0, 3, 1, 2), NSA_GROUP, axis=2)
    sel_rows = sel_rows.reshape(bs, ts * NSA_HEADS, nsl)
    e_sel = jnp.asarray((np.arange(nsl)[:, None] == np.arange(total)[None, :] // SEL_BLOCK).astype(np.float32), _BF)
    flat = lambda kv, s: kv[:, :, s].reshape(bs, ts, -1)
    o_s = _paged_attn(qbd, _pages_fm(cache_sel[li]), page_table, _sample_bias(tabn, NSA_HEADS, past, ts, total),
                      _new_rows_fm(flat(kv_s, 0)), _new_rows_fm(flat(kv_s, 1)), selb=sel_rows, e=e_sel)
    win_buf = win_state[li]
    wb = win_buf.shape[1]
    assert wb % PAGE_SIZE == 0
    wpages = _pages_fm(win_buf).reshape(bs, 2, g, HEAD_DIM, wb // PAGE_SIZE, PAGE_SIZE)
    wpages = wpages.transpose(0, 4, 1, 2, 3, 5).reshape(bs * (wb // PAGE_SIZE), 2, g, HEAD_DIM, PAGE_SIZE)
    wtable = jnp.arange(bs * (wb // PAGE_SIZE), dtype=jnp.int32).reshape(bs, wb // PAGE_SIZE)
    o_w = _paged_attn(qbd, wpages, wtable, _sample_bias(tabn, NSA_HEADS, wb, ts, wb + PAGE_SIZE, NSA_WINDOW),
                      _new_rows_fm(flat(kv_w, 0)), _new_rows_fm(flat(kv_w, 1)))
    o_s = _own_head_columns(o_s, ts, NSA_HEADS, NSA_GROUP)
    o_w = _own_head_columns(o_w, ts, NSA_HEADS, NSA_GROUP)

    past_l = _gather_fm(cache_flogf[li].transpose(0, 2, 1), page_table)
    lf_len = _round_up(past + ts, LANES)
    front = lf_len - past - ts
    fl_all = jnp.concatenate([jnp.zeros((bs, FOX_HEADS, front), _F32), past_l, f_logit.transpose(0, 2, 1)], axis=-1)
    logf_t, c, _ = _logf_cumsum(fl_all.reshape(bs * FOX_HEADS, lf_len), ts, min(bs * FOX_HEADS, 64))
    c = c.reshape(bs, FOX_HEADS, lf_len)
    logf = logf_t[:, lf_len - ts:].reshape(bs, FOX_HEADS, ts).transpose(0, 2, 1)
    c_new = c[:, :, front + past:]
    cq = jnp.broadcast_to(c_new.transpose(0, 2, 1).reshape(bs, ts * FOX_HEADS, 1), (bs, ts * FOX_HEADS, LANES))
    o_f = _paged_attn(_block_diag_queries(q_f * SCALE, 1), _pages_fm(cache_fkv[li]), page_table,
                      _sample_bias(None, FOX_HEADS, past, ts, total),
                      _new_rows_fm(flat(kv_f, 0)), _new_rows_fm(flat(kv_f, 1)),
                      cq=cq, ck=c[:, :, front:front + past], ck_new=_pad_last(c_new, PAGE_SIZE))
    o_f = _own_head_columns(o_f, ts, FOX_HEADS, 1)
    kvw_all = jnp.concatenate([win_buf, kv_w], axis=1)
    n_tok = bs * ts
    outs = (o_c.reshape(n_tok, -1), o_s.reshape(n_tok, -1), o_w.reshape(n_tok, -1), o_f.reshape(n_tok, -1),
            g_bm.reshape(n_tok, -1))
    return outs, (kv_c, kv_s, kvw_all[:, ts:], kv_f, logf)


def _odd_sample(z, li, cache_kv, page_table, tab):
    bs, n_pages = page_table.shape
    past = n_pages * PAGE_SIZE
    ts = z.shape[0] // bs
    hw = MOBA_HEADS * HEAD_DIM
    z = z.reshape(bs, ts, 3 * hw)
    q = z[..., :hw].reshape(bs, ts, MOBA_HEADS, HEAD_DIM)
    total = past + PAGE_SIZE
    assert ts <= MOBA_BLOCK and past % MOBA_BLOCK == 0
    pages = _pages_fm(cache_kv[li])
    qbd = _block_diag_queries(q * SCALE, 1)
    selb = _moba_pick(qbd, pages, page_table)
    e_blk = jnp.asarray((np.arange(LANES)[:, None] == np.arange(total)[None, :] // MOBA_BLOCK).astype(np.float32), _BF)
    o = _paged_attn(qbd, pages, page_table, _sample_bias(tab[:, :MOBA_HEADS], MOBA_HEADS, past, ts, total),
                    _new_rows_fm(z[..., hw:2 * hw]), _new_rows_fm(z[..., 2 * hw:]), selb=selb, e=e_blk)
    o = _own_head_columns(o, ts, MOBA_HEADS, 1)
    return o.reshape(bs * ts, hw), z[..., hw:].reshape(bs, ts, 2, MOBA_HEADS, HEAD_DIM)


def _even_split(z, b, t):
    q_a, kv_c, kv_s, kv_w, g_a, q_f, kv_f, f_logit = jnp.split(z.reshape(b, t, -1), _EVEN_CUTS, axis=-1)
    kvshape = (b, t, 2, NSA_KV_HEADS, HEAD_DIM)
    g_bm = g_a.reshape(b, t, NSA_HEADS, 3).transpose(0, 1, 3, 2).reshape(b, t, 3 * NSA_HEADS)
    return (q_a.reshape(b, t, NSA_HEADS, HEAD_DIM), kv_c.reshape(kvshape), kv_s.reshape(kvshape),
            kv_w.reshape(kvshape), g_bm, q_f.reshape(b, t, FOX_HEADS, HEAD_DIM),
            kv_f.reshape(b, t, 2, FOX_HEADS, HEAD_DIM), f_logit)


def _even_row_perm():
    cuts = (0,) + _EVEN_CUTS + (sum(_EVEN_SIZES),)
    seg = lambda k: np.arange(cuts[k], cuts[k + 1])
    gates = cuts[4] + (np.arange(NSA_HEADS)[None, :] * 3 + np.arange(3)[:, None]).reshape(-1)
    return np.concatenate([seg(0), seg(1), seg(2), seg(3), seg(5), seg(6), gates, seg(7)])


def _kv_leaf(kv_t, heads):
    bsz, _, s = kv_t.shape
    return kv_t.reshape(bsz, 2, heads, HEAD_DIM, s).transpose(0, 4, 1, 2, 3)


def _even_prompt(x_rows, tab, w_in, b_in, cmp_w, w_out, ln_g, ln_b, alpha, tm):
    bsz, s, d = x_rows.shape
    perm = _even_row_perm()
    w_t = w_in.T[perm].astype(_BF)
    scale = np.ones((_EVEN_OUT, 1), np.float32)
    scale[_QA:_QA + NSA_HEADS * HEAD_DIM] = SCALE
    scale[_QF:_QF + FOX_HEADS * HEAD_DIM] = SCALE
    segments = [(_KVC, _KVS - _KVC), (_KVS, _KVW - _KVS), (_KVW, _QF - _KVW), (_KVF, _GA - _KVF), (_GA, _EVEN_OUT - _GA)]
    zb, (kvc_t, kvs_t, kvw_t, kvf_t, tail_t) = _proj_fm(x_rows, w_t, b_in[perm].reshape(-1, 1), jnp.asarray(scale),
                                                         tm, True, segments)
    tabn = tab[:, :NSA_HEADS]
    g = NSA_KV_HEADS
    hb = HEAD_DIM

    assert s % SEL_BLOCK == 0
    cmp_tok = _nsa_compress(_chunks_from_fm(kvc_t), *cmp_w)
    ncp = _round_up(cmp_tok.shape[2], LANES)
    cmp_tok = _pad_axis(cmp_tok, 2, ncp).astype(_BF)
    kc = cmp_tok[:g].transpose(1, 0, 2, 3).reshape(bsz * g, ncp, HEAD_DIM)
    vc_t = cmp_tok[g:].transpose(1, 0, 3, 2).reshape(bsz * g, HEAD_DIM, ncp)
    tq, tk = 256, 512
    o_c, sel = _cmp_select_t(zb, kc, vc_t, tab, tq, s // SEL_BLOCK)

    pairs, deltas = _plan_tiles(s // tq, tq, tk, None, True)
    bias = _bias_tiles_t(tabn, deltas, tk, tq, None, NSA_GROUP)
    block_of_key = lambda n, blk: jnp.asarray(
        (np.arange(n)[:, None] == np.arange(s)[None, :] // blk).astype(np.float32), _BF)
    o_s = _flash_t(zb, _QA // (NSA_GROUP * hb), _KVS // hb, _KVS // hb + g, g, NSA_GROUP, True, bias, pairs, tq, tk,
                   qx=sel, kx=block_of_key(sel.shape[1], SEL_BLOCK))
    pairs, deltas = _plan_tiles(s // tq, tq, tk, NSA_WINDOW, True)
    bias = _bias_tiles_t(tabn, deltas, tk, tq, NSA_WINDOW, NSA_GROUP)
    o_w = _flash_t(zb, _QA // (NSA_GROUP * hb), _KVW // hb, _KVW // hb + g, g, NSA_GROUP, True, bias, pairs, tq, tk)

    f_logit_t = tail_t[:, _FL - _GA:_FL - _GA + FOX_HEADS, :]
    logf_t, _, c3 = _logf_cumsum(f_logit_t.reshape(bsz * FOX_HEADS, s), s, bsz * FOX_HEADS)
    qaug, kaug = _fox_aug_rows(c3)
    tqf = tkf = 512
    hps = FOX_HEADS_PER_STEP
    units = FOX_HEADS // hps
    assert _QF % (hps * hb) == 0 and _KVF % (hps * hb) == 0
    pairs, deltas = _plan_tiles(s // tqf, tqf, tkf, None, False)
    bias = _bias_tiles_t(None, deltas, tkf, tqf, None, hps)
    per_unit = lambda a: a.reshape(bsz * units, hps * BF16_ROWS, s)
    o_f = _flash_t(zb, _QF // (hps * hb), _KVF // (hps * hb), _KVF // (hps * hb) + units, units, hps, False,
                   bias, pairs, tqf, tkf, qx=per_unit(qaug), kx=per_unit(kaug))

    wide = lambda o: o.reshape(bsz, -1, s)
    x_t = _even_out_fm(alpha, wide(o_c), wide(o_s), wide(o_w), tail_t, wide(o_f), w_out.T.astype(_BF), x_rows,
                       ln_g, ln_b, tm)
    kv_c = _kv_leaf(kvc_t, g)
    kv_s = _kv_leaf(kvs_t, g)
    kv_w = _kv_leaf(kvw_t, g)
    kv_f = _kv_leaf(kvf_t, FOX_HEADS)
    logf = logf_t.reshape(bsz, FOX_HEADS, s).transpose(0, 2, 1)
    return x_t, (kv_c, kv_s, kv_w[:, max(s - NSA_WINDOW, 0):], kv_f, logf)


def _odd_prompt(x_t, tab, w_in, w_out, ln_g, ln_b, alpha, tm):
    bsz, d, s = x_t.shape
    hw = MOBA_HEADS * HEAD_DIM
    scale = np.ones((3 * hw, 1), np.float32)
    scale[:hw] = SCALE
    zb, (kv_t,) = _proj_fm(x_t, w_in.T.astype(_BF), jnp.zeros((3 * hw, 1), _F32), jnp.asarray(scale), tm, False,
                           [(hw, 2 * hw)])
    assert s % MOBA_BLOCK == 0
    sel = _moba_select_t(kv_t, zb, 0, 1024 if s % 1024 == 0 else 256)
    tq = tk = 512
    hps = MOBA_HEADS_PER_STEP
    units = MOBA_HEADS // hps
    pairs, deltas = _plan_tiles(s // tq, tq, tk, None, True)
    bias = _bias_tiles_t(tab[:, :MOBA_HEADS], deltas, tk, tq, None, hps)
    nbr = sel.shape[1]
    block_of_key = jnp.asarray((np.arange(nbr)[:, None] == np.arange(s)[None, :] // MOBA_BLOCK).astype(np.float32), _BF)
    o = _flash_t(zb, 0, units, 2 * units, units, hps, False, bias, pairs, tq, tk,
                 qx=sel.reshape(bsz * units, hps * nbr, s), kx=block_of_key)
    x_t = _odd_out_fm(alpha, o.reshape(bsz, hw, s), w_out.T.astype(_BF), x_t, ln_g, ln_b, tm)
    return x_t, _kv_leaf(kv_t, MOBA_HEADS)


def _kernel_impl(x_prompt, x_sample, cache_nsa_cmp, cache_nsa_sel, state_nsa_win, cache_fox_kv,
                 cache_fox_logf, cache_moba_kv, page_table, rel_bias, ln_g, ln_b, w_in_even, b_in_even,
                 nsa_cmp_pe, nsa_cmp_w1, nsa_cmp_b1, nsa_cmp_w2, nsa_cmp_b2, w_out_even, w_in_odd,
                 w_out_odd, moe_wg, moe_bg, moe_we, moe_be, moe_w1, moe_w3, moe_w2):
    bp, sp, d = x_prompt.shape
    bs, ts, _ = x_sample.shape
    n_pages = page_table.shape[1]
    past = n_pages * PAGE_SIZE
    depth = ln_g.shape[0]
    alpha = (2 * depth) ** 0.25
    ns_tok = bs * ts
    tm_p = 512
    tm_s = ns_tok
    assert sp % 1024 == 0 and depth % 2 == 0
    xp = x_prompt
    xs = x_sample.reshape(ns_tok, d)
    outs = {k: [] for k in ("cmp_p", "cmp_s", "sel_p", "sel_s", "win_p", "win_s", "fkv_p", "fkv_s",
                            "flf_p", "flf_s", "mkv_p", "mkv_s")}
    tq_s = LANES

    for layer in range(depth):
        li = layer // 2
        if layer % 2 == 0:
            assert layer == 0
            cmp_w = (nsa_cmp_pe[li], nsa_cmp_w1[li], nsa_cmp_b1[li], nsa_cmp_w2[li], nsa_cmp_b2[li])
            xp, (kv_c, kv_s, kv_w, kv_f, logf) = _even_prompt(
                xp, rel_bias, w_in_even[li], b_in_even[li], cmp_w, w_out_even[li], ln_g[layer, 0], ln_b[layer, 0],
                alpha, tm_p)
            outs["cmp_p"].append(kv_c)
            outs["sel_p"].append(kv_s)
            outs["win_p"].append(kv_w)
            outs["fkv_p"].append(kv_f)
            outs["flf_p"].append(logf)
            w_in = w_in_even[li].astype(_BF)
            w_out = w_out_even[li].astype(_BF)
            z = _linear(xs, w_in, b_in_even[li], tm_s)
            (o_c, o_s, o_w, o_f, g_bm), (kv_c, kv_s, kv_w, kv_f, logf) = _even_sample(
                z, li, cache_nsa_cmp, cache_nsa_sel, state_nsa_win, cache_fox_kv, cache_fox_logf, page_table,
                rel_bias, cmp_w, tq_s)
            xs = _even_out(alpha, o_c, o_s, o_w, g_bm, o_f, w_out, xs, ln_g[layer, 0], ln_b[layer, 0], tm_s)
            outs["cmp_s"].append(kv_c)
            outs["sel_s"].append(kv_s)
            outs["win_s"].append(kv_w)
            outs["fkv_s"].append(kv_f)
            outs["flf_s"].append(logf)
        else:
            hw = MOBA_HEADS * HEAD_DIM
            xp, kv = _odd_prompt(xp, rel_bias, w_in_odd[li], w_out_odd[li], ln_g[layer, 0], ln_b[layer, 0], alpha, tm_p)
            outs["mkv_p"].append(kv)
            w_in = w_in_odd[li].astype(_BF)
            w_out = w_out_odd[li].astype(_BF)
            z = _linear(xs, w_in, jnp.zeros((3 * hw,), _F32), tm_s)
            o, kv = _odd_sample(z, li, cache_moba_kv, page_table, rel_bias)
            xs = _odd_out(alpha, o, w_out, xs, ln_g[layer, 0], ln_b[layer, 0], tm_s)
            outs["mkv_s"].append(kv)
        w1b, w3b, w2b = moe_w1[layer].astype(_BF), moe_w3[layer].astype(_BF), moe_w2[layer].astype(_BF)
        w13_t = jnp.concatenate([w1b.transpose(0, 2, 1), w3b.transpose(0, 2, 1)], axis=1)
        router = (moe_wg[layer], moe_bg[layer], moe_we[layer], moe_be[layer])
        xp = _moe_ln_fm(alpha, xp, *router, w13_t, w2b.transpose(0, 2, 1), ln_g[layer, 1], ln_b[layer, 1], tm_p,
                        layer == depth - 1)
        xs = _moe_ln(alpha, xs, *router, w1b, w3b, w2b, ln_g[layer, 1], ln_b[layer, 1], tm_s)

    st = lambda k: jnp.stack(outs[k])
    return (xp, xs.reshape(bs, ts, d), st("cmp_p"), st("cmp_s"), st("sel_p"), st("sel_s"),
            st("win_p"), st("win_s"), st("fkv_p"), st("fkv_s"), st("flf_p"), st("flf_s"), st("mkv_p"), st("mkv_s"))


def kernel(x_prompt, x_sample, cache_nsa_cmp, cache_nsa_sel, state_nsa_win, cache_fox_kv, cache_fox_logf, cache_moba_kv, page_table, rel_bias, ln_g, ln_b, w_in_even, b_in_even, nsa_cmp_pe, nsa_cmp_w1, nsa_cmp_b1, nsa_cmp_w2, nsa_cmp_b2, w_out_even, w_in_odd, w_out_odd, moe_wg, moe_bg, moe_we, moe_be, moe_w1, moe_w3, moe_w2):
    return _kernel_impl(x_prompt, x_sample, cache_nsa_cmp, cache_nsa_sel, state_nsa_win, cache_fox_kv,
                        cache_fox_logf, cache_moba_kv, page_table, rel_bias, ln_g, ln_b, w_in_even, b_in_even,
                        nsa_cmp_pe, nsa_cmp_w1, nsa_cmp_b1, nsa_cmp_w2, nsa_cmp_b2, w_out_even, w_in_odd,
                        w_out_odd, moe_wg, moe_bg, moe_we, moe_be, moe_w1, moe_w3, moe_w2)
```

```python
import functools
import math

import numpy as np
import jax
import jax.numpy as jnp
from jax import lax
from jax.experimental import pallas as pl
from jax.experimental.pallas import tpu as pltpu

_BF = jnp.bfloat16
_F32 = jnp.float32

HEAD_DIM = 64
NSA_KV_HEADS = 2
NSA_GROUP = 4
NSA_HEADS = NSA_KV_HEADS * NSA_GROUP
FOX_HEADS = 8
MOBA_HEADS = 16
CMP_LEN = 32
CMP_STRIDE = 16
CMP_HID = 128
SEL_BLOCK = 64
CMP_PER_SEL = SEL_BLOCK // CMP_STRIDE
NSA_TOPN = 16
NSA_WINDOW = 512
MOBA_BLOCK = 256
MOBA_TOPK = 3
N_BUCKETS = 32
T5_MAX_DISTANCE = 128
N_GROUPS = 4
EXPERTS_PER_GROUP = 4
N_EXPERTS = N_GROUPS * EXPERTS_PER_GROUP
PAGE_SIZE = 128
SCALE = HEAD_DIM ** -0.5
NEG_INF = -1e30
FORCE_SCORE = 1e4
LN_EPS = 1e-5
LANES = 128
SUBLANES = 8
BF16_ROWS = 16
VMEM_LIMIT = 48 * 1024 * 1024
PAGES_PER_STEP = 8
PAGE_BYTES_PER_STEP = 8 * 1024 * 1024
FOX_HEADS_PER_STEP = 8
MOBA_HEADS_PER_STEP = 8

_QA, _QF, _KVF, _KVC, _KVS, _KVW, _GA, _FL = 0, 512, 1024, 2048, 2304, 2560, 2816, 2840
_EVEN_OUT = 2848
_EVEN_SIZES = (NSA_HEADS * HEAD_DIM, 2 * NSA_KV_HEADS * HEAD_DIM, 2 * NSA_KV_HEADS * HEAD_DIM,
               2 * NSA_KV_HEADS * HEAD_DIM, 3 * NSA_HEADS, FOX_HEADS * HEAD_DIM,
               2 * FOX_HEADS * HEAD_DIM, FOX_HEADS)
_EVEN_CUTS = tuple(int(c) for c in np.cumsum(_EVEN_SIZES)[:-1])


def _cparams(*sem):
    return pltpu.CompilerParams(dimension_semantics=sem, vmem_limit_bytes=VMEM_LIMIT)


def _round_up(n, m):
    return (n + m - 1) // m * m


def _split3(x):
    hi = x.astype(_BF)
    r1 = x - hi.astype(_F32)
    mid = r1.astype(_BF)
    lo = (r1 - mid.astype(_F32)).astype(_BF)
    return hi, mid, lo


def _dot3(x, m01):
    hi, mid, lo = _split3(x)
    acc = jnp.dot(hi, m01, preferred_element_type=_F32)
    acc += jnp.dot(mid, m01, preferred_element_type=_F32)
    acc += jnp.dot(lo, m01, preferred_element_type=_F32)
    return acc


def _t5_bucket_np(dist):
    n = np.maximum(dist, 0)
    exact = N_BUCKETS // 2
    nf = np.maximum(n, exact).astype(np.float32)
    far = exact + (np.log(nf / np.float32(exact)) / np.float32(math.log(T5_MAX_DISTANCE / exact))
                   * np.float32(N_BUCKETS - exact)).astype(np.int32)
    return np.where(n < exact, n, np.minimum(far, N_BUCKETS - 1)).astype(np.int32)


def _bucket_thresholds():
    d = np.arange(0, 4 * T5_MAX_DISTANCE)
    b = _t5_bucket_np(d)
    return [int(d[b >= k][0]) for k in range(1, N_BUCKETS)]


_FAR_DISTANCE = _bucket_thresholds()[-1]


def _pad_last(x, width):
    return jnp.pad(x, [(0, 0)] * (x.ndim - 1) + [(0, width - x.shape[-1])])


def _pad_axis(x, axis, size):
    pads = [(0, 0)] * x.ndim
    pads[axis] = (0, size - x.shape[axis])
    return jnp.pad(x, pads)


def _toeplitz(g, rows, cols):
    n = g.shape[-1]
    lead = g.shape[:-1]
    x = jnp.broadcast_to(g[..., None, :], lead + (rows, n)).reshape(lead + (rows * n,))
    return x[..., :rows * (n - 1)].reshape(lead + (rows, n - 1))[..., :cols]


def _distance_values(tabh, d, window):
    valid = (d >= 0) if window is None else ((d >= 0) & (d < window))
    if tabh is None:
        vals = jnp.zeros((1, d.shape[0]), _F32)
    else:
        vals = tabh[jnp.asarray(_t5_bucket_np(d))].T
    return jnp.where(jnp.asarray(valid)[None], vals, NEG_INF).astype(_F32)


def _bias_tile(tabh, d0, rows, cols, window=None):
    n = rows + cols
    m = np.arange(n)
    d = np.where(m < cols, d0 - m, d0 + n - m)
    return _toeplitz(_distance_values(tabh, d, window), rows, cols)


def _bias_tile_t(tabh, delta, tk, tq, window=None):
    n = tk + tq
    m = np.arange(n)
    d = np.where(m < tq, delta + m, delta + m - n)
    return _toeplitz(_distance_values(tabh, d, window), tk, tq)


def _plan_tiles(nq, tq, tk, window, has_table):
    deltas, pairs = [], []
    for qi in range(nq):
        q0 = qi * tq
        k_hi = (q0 + tq - 1) // tk
        k_lo = 0 if window is None else max(0, (q0 - (window - 1)) // tk)
        for ki in range(k_lo, k_hi + 1):
            delta = q0 - ki * tk
            dmin, dmax = delta - (tk - 1), delta + tq - 1
            plain = dmin >= (_FAR_DISTANCE if has_table else 0) and (window is None or dmax < window)
            if plain:
                pairs.append((qi, ki, -1))
            else:
                if delta not in deltas:
                    deltas.append(delta)
                pairs.append((qi, ki, deltas.index(delta)))
    pairs = [(q, k, b if b >= 0 else len(deltas)) for q, k, b in pairs]
    return pairs, deltas


def _bias_tiles_t(tabh, deltas, tk, tq, window, group):
    far = tk + tq + _FAR_DISTANCE
    tiles = [_bias_tile_t(tabh, dl, tk, tq, window) for dl in deltas]
    tiles.append(_bias_tile_t(tabh, far, tk, tq, None))
    t = jnp.stack(tiles, axis=1)
    if tabh is None:
        t = jnp.broadcast_to(t, (group,) + t.shape[1:])
    h, nb = t.shape[:2]
    t = t.reshape(h // group, group, nb, tk, tq).transpose(0, 2, 3, 1, 4)
    return t.reshape(h // group, nb, tk, group * tq)


def _layer_norm_cols(y, g, b):
    mu = jnp.mean(y, axis=0, keepdims=True)
    yc = y - mu
    var = jnp.mean(yc * yc, axis=0, keepdims=True)
    return yc * lax.rsqrt(var + LN_EPS) * g + b


def _proj_fm_kernel(row_major_in, segments, x_ref, w_ref, b_ref, sc_ref, zb_ref, *seg_refs):
    x = x_ref[0].astype(_BF)
    if row_major_in:
        z = lax.dot_general(w_ref[...], x, (((1,), (1,)), ((), ())), preferred_element_type=_F32)
    else:
        z = jnp.dot(w_ref[...], x, preferred_element_type=_F32)
    z = z + b_ref[...]
    zb_ref[0] = (z * sc_ref[...]).astype(_BF)
    for (row0, rows), ref in zip(segments, seg_refs):
        ref[0] = z[row0:row0 + rows]


def _proj_fm(x, w_t_bf, b_col, scale_col, tm, row_major_in, segments):
    bsz = x.shape[0]
    s = x.shape[1] if row_major_in else x.shape[2]
    d = x.shape[2] if row_major_in else x.shape[1]
    n = w_t_bf.shape[0]
    x_spec = (pl.BlockSpec((1, tm, d), lambda b, i: (b, i, 0)) if row_major_in
              else pl.BlockSpec((1, d, tm), lambda b, i: (b, 0, i)))
    col = pl.BlockSpec((n, 1), lambda b, i: (0, 0))
    out = lambda rows: pl.BlockSpec((1, rows, tm), lambda b, i: (b, 0, i))
    res = pl.pallas_call(
        functools.partial(_proj_fm_kernel, row_major_in, tuple(segments)),
        grid=(bsz, s // tm),
        in_specs=[x_spec, pl.BlockSpec((n, d), lambda b, i: (0, 0)), col, col],
        out_specs=[out(n)] + [out(rows) for _, rows in segments],
        out_shape=[jax.ShapeDtypeStruct((bsz, n, s), _BF)]
        + [jax.ShapeDtypeStruct((bsz, rows, s), _F32) for _, rows in segments],
        compiler_params=_cparams("parallel", "parallel"), name="proj_fm",
    )(x, w_t_bf, b_col, scale_col)
    return res[0], res[1:]


def _even_out_fm_kernel(alpha, oc_ref, os_ref, ow_ref, gz_ref, of_ref, wo_ref, x_ref, g_ref, b_ref, y_ref):
    wa = NSA_HEADS * HEAD_DIM
    sg = jax.nn.sigmoid(gz_ref[0])
    parts = []
    for h in range(NSA_HEADS):
        sl = slice(h * HEAD_DIM, (h + 1) * HEAD_DIM)
        parts.append(sg[h:h + 1] * oc_ref[0, sl, :] + sg[NSA_HEADS + h:NSA_HEADS + h + 1] * os_ref[0, sl, :]
                     + sg[2 * NSA_HEADS + h:2 * NSA_HEADS + h + 1] * ow_ref[0, sl, :])
    o_a = jnp.concatenate(parts, axis=0)
    m = jnp.dot(wo_ref[:, 0:wa], o_a.astype(_BF), preferred_element_type=_F32)
    m += jnp.dot(wo_ref[:, wa:], of_ref[0].astype(_BF), preferred_element_type=_F32)
    y_ref[0] = _layer_norm_cols(alpha * x_ref[0].T + m, g_ref[...], b_ref[...])


def _even_out_fm(alpha, o_c, o_s, o_w, gz, o_f, w_out_t_bf, x_rows, g, b, tm):
    bsz, s, d = x_rows.shape
    wa = NSA_HEADS * HEAD_DIM
    blk = lambda rows: pl.BlockSpec((1, rows, tm), lambda b_, i: (b_, 0, i))
    col = pl.BlockSpec((d, 1), lambda b_, i: (0, 0))
    return pl.pallas_call(
        functools.partial(_even_out_fm_kernel, alpha),
        grid=(bsz, s // tm),
        in_specs=[blk(wa), blk(wa), blk(wa), blk(gz.shape[1]), blk(FOX_HEADS * HEAD_DIM),
                  pl.BlockSpec(w_out_t_bf.shape, lambda b_, i: (0, 0)),
                  pl.BlockSpec((1, tm, d), lambda b_, i: (b_, i, 0)), col, col],
        out_specs=blk(d),
        out_shape=jax.ShapeDtypeStruct((bsz, d, s), _F32),
        compiler_params=_cparams("parallel", "parallel"), name="even_out_ln_fm",
    )(o_c, o_s, o_w, gz, o_f, w_out_t_bf, x_rows, g.reshape(d, 1), b.reshape(d, 1))


def _odd_out_fm_kernel(alpha, o_ref, wo_ref, x_ref, g_ref, b_ref, y_ref):
    m = jnp.dot(wo_ref[...], o_ref[0].astype(_BF), preferred_element_type=_F32)
    y_ref[0] = _layer_norm_cols(alpha * x_ref[0] + m, g_ref[...], b_ref[...])


def _odd_out_fm(alpha, o, w_out_t_bf, x_t, g, b, tm):
    bsz, d, s = x_t.shape
    blk = lambda rows: pl.BlockSpec((1, rows, tm), lambda b_, i: (b_, 0, i))
    col = pl.BlockSpec((d, 1), lambda b_, i: (0, 0))
    return pl.pallas_call(
        functools.partial(_odd_out_fm_kernel, alpha),
        grid=(bsz, s // tm),
        in_specs=[blk(o.shape[1]), pl.BlockSpec(w_out_t_bf.shape, lambda b_, i: (0, 0)), blk(d), col, col],
        out_specs=blk(d), out_shape=jax.ShapeDtypeStruct((bsz, d, s), _F32),
        compiler_params=_cparams("parallel", "parallel"), name="odd_out_ln_fm",
    )(o, w_out_t_bf, x_t, g.reshape(d, 1), b.reshape(d, 1))


def _route_gates(logits, axis):
    idx = lax.broadcasted_iota(jnp.int32, logits.shape, axis)
    big = jnp.int32(1 << 20)
    red = lambda f, v: f(v, axis=axis, keepdims=True)
    is_g = idx < N_GROUPS
    lg = jnp.where(is_g, logits, NEG_INF)
    mg = red(jnp.max, lg)
    sg = red(jnp.sum, jnp.where(is_g, jnp.exp(lg - mg), 0.0))
    p_top = 1.0 / sg
    g_top = red(jnp.min, jnp.where(lg == mg, idx, big))
    lo = N_GROUPS + EXPERTS_PER_GROUP * g_top
    in_grp = (idx >= lo) & (idx < lo + EXPERTS_PER_GROUP)
    le = jnp.where(in_grp, logits, NEG_INF)
    me = red(jnp.max, le)
    se = red(jnp.sum, jnp.where(in_grp, jnp.exp(le - me), 0.0))
    i1 = red(jnp.min, jnp.where(le == me, idx, big))
    le2 = jnp.where(idx == i1, NEG_INF, le)
    m2 = red(jnp.max, le2)
    i2 = red(jnp.min, jnp.where(le2 == m2, idx, big))
    w1 = 1.0 / se
    w2 = jnp.exp(m2 - me) / se
    tot = w1 + w2
    return jnp.where(idx == i1, p_top * w1 / tot, jnp.where(idx == i2, p_top * w2 / tot, 0.0))


def _moe_fm_kernel(alpha, rows_out, x_ref, wr_ref, br_ref, w13_ref, w2_ref, g_ref, b_ref, y_ref,
                   gate_ref, acc_ref, xb_ref):
    e = pl.program_id(2)
    f = w13_ref.shape[1] // 2

    @pl.when(e == 0)
    def _route():
        xb_ref[...] = x_ref[0].astype(_BF)
        logits = jnp.dot(wr_ref[...], xb_ref[...], preferred_element_type=_F32) + br_ref[...]
        gate_ref[...] = _route_gates(logits, 0)
        acc_ref[...] = jnp.zeros_like(acc_ref)

    h13 = jnp.dot(w13_ref[0], xb_ref[...], preferred_element_type=_F32)
    ge = gate_ref[pl.ds(e + N_GROUPS, 1), :]
    h = (jax.nn.silu(h13[0:f]) * h13[f:2 * f]) * ge
    acc_ref[...] += jnp.dot(w2_ref[0], h.astype(_BF), preferred_element_type=_F32)

    @pl.when(e == N_EXPERTS - 1)
    def _finish():
        y = _layer_norm_cols(alpha * x_ref[0] + acc_ref[...], g_ref[...], b_ref[...])
        y_ref[0] = y.T if rows_out else y


def _moe_ln_fm(alpha, x_t, wg, bg, we, be, w13_t_bf, w2_t_bf, g, b, tm, rows_out):
    bsz, d, s = x_t.shape
    rr = 2 * BF16_ROWS
    wr = jnp.zeros((rr, d), _F32).at[:N_GROUPS].set(wg.T).at[N_GROUPS:N_GROUPS + N_EXPERTS].set(we.T).astype(_BF)
    br = jnp.zeros((rr, 1), _F32).at[:N_GROUPS, 0].set(bg).at[N_GROUPS:N_GROUPS + N_EXPERTS, 0].set(be)
    f2 = w13_t_bf.shape[1]
    col = pl.BlockSpec((d, 1), lambda b_, i, e: (0, 0))
    out_spec = (pl.BlockSpec((1, tm, d), lambda b_, i, e: (b_, i, 0)) if rows_out
                else pl.BlockSpec((1, d, tm), lambda b_, i, e: (b_, 0, i)))
    out_shape = jax.ShapeDtypeStruct((bsz, s, d) if rows_out else (bsz, d, s), _F32)
    return pl.pallas_call(
        functools.partial(_moe_fm_kernel, alpha, rows_out),
        grid=(bsz, s // tm, N_EXPERTS),
        in_specs=[pl.BlockSpec((1, d, tm), lambda b_, i, e: (b_, 0, i)),
                  pl.BlockSpec((rr, d), lambda b_, i, e: (0, 0)),
                  pl.BlockSpec((rr, 1), lambda b_, i, e: (0, 0)),
                  pl.BlockSpec((1, f2, d), lambda b_, i, e: (e, 0, 0)),
                  pl.BlockSpec((1, d, f2 // 2), lambda b_, i, e: (e, 0, 0)),
                  col, col],
        out_specs=out_spec, out_shape=out_shape,
        scratch_shapes=[pltpu.VMEM((rr, tm), _F32), pltpu.VMEM((d, tm), _F32), pltpu.VMEM((d, tm), _BF)],
        compiler_params=_cparams("parallel", "parallel", "arbitrary"), name="moe_ln_fm",
    )(x_t, wr, br, w13_t_bf, w2_t_bf, g.reshape(d, 1), b.reshape(d, 1))


_VROWS = HEAD_DIM + BF16_ROWS


def _flash_t_kernel(group, kv_shared, qx_heads, kx_global, qi_ref, ki_ref, bi_ref, first_ref, last_ref, *refs):
    refs = list(refs)
    q_ref, k_ref, v_ref, bias_ref = refs[:4]
    pos = 4
    if qx_heads:
        qx_ref, kx_ref = refs[pos:pos + 2]
        pos += 2
        xr = qx_ref.shape[1] // qx_heads
    o_ref, m_ref, acc_ref, s_ref = refs[pos:pos + 4]
    step = pl.program_id(1)
    tq = q_ref.shape[2]
    tk = k_ref.shape[2]

    @pl.when(first_ref[step] == 1)
    def _init():
        m_ref[...] = jnp.full_like(m_ref, NEG_INF)
        acc_ref[...] = jnp.zeros_like(acc_ref)

    for r in range(group):
        rows = slice(r * HEAD_DIM, (r + 1) * HEAD_DIM)
        q_t = q_ref[0, rows, :]
        k_t = k_ref[0, slice(0, HEAD_DIM) if kv_shared else rows, :]
        if qx_heads:
            qr = r if qx_heads > 1 else 0
            q_t = jnp.concatenate([q_t, qx_ref[0, qr * xr:(qr + 1) * xr, :]], axis=0)
            k_t = jnp.concatenate([k_t, kx_ref[...] if kx_global else kx_ref[0, r * xr:(r + 1) * xr, :]], axis=0)
        s_ref[r] = lax.dot_general(k_t, q_t, (((0,), (0,)), ((), ())), preferred_element_type=_F32)

    m_all = m_ref[...]
    acc_all = acc_ref[...]
    ones = jnp.ones((BF16_ROWS, tk), _BF)
    m_out, acc_out = [], []
    for r in range(group):
        kv_rows = slice(0, HEAD_DIM) if kv_shared else slice(r * HEAD_DIM, (r + 1) * HEAD_DIM)
        s = s_ref[r] + bias_ref[0, 0, :, r * tq:(r + 1) * tq]
        m_old = m_all[r:r + 1, :]
        m_new = jnp.maximum(m_old, jnp.max(s, axis=0, keepdims=True))
        a = jnp.exp(m_old - m_new)
        p = jnp.exp(s - m_new).astype(_BF)
        v_t = jnp.concatenate([v_ref[0, kv_rows, :], ones], axis=0)
        acc_out.append(a * acc_all[r * _VROWS:(r + 1) * _VROWS, :] + jnp.dot(v_t, p, preferred_element_type=_F32))
        m_out.append(m_new)
    pad = [jnp.zeros((SUBLANES - group, tq), _F32)] if group < SUBLANES else []
    m_ref[...] = jnp.concatenate(m_out + pad, axis=0)
    acc_ref[...] = jnp.concatenate(acc_out, axis=0)

    @pl.when(last_ref[step] == 1)
    def _done():
        for r in range(group):
            acc = acc_out[r]
            o_ref[0, r * HEAD_DIM:(r + 1) * HEAD_DIM, :] = (
                acc[0:HEAD_DIM] / jnp.maximum(acc[HEAD_DIM:HEAD_DIM + 1], 1e-30))


def _flash_t(zb, q_blk, k_blk, v_blk, units, group, kv_shared, bias, pairs, tq, tk, qx=None, kx=None):
    bsz, _, s = zb.shape
    bh = bsz * units
    hb = bias.shape[0]
    pairs = np.asarray(pairs, np.int32)
    qi, ki, bi = pairs[:, 0], pairs[:, 1], pairs[:, 2]
    first = np.concatenate([[1], (qi[1:] != qi[:-1]).astype(np.int32)]).astype(np.int32)
    last = np.concatenate([(qi[1:] != qi[:-1]).astype(np.int32), [1]]).astype(np.int32)
    gq = group * HEAD_DIM
    kvr = HEAD_DIM if kv_shared else gq
    in_specs = [pl.BlockSpec((1, gq, tq), lambda b, t, qi, ki, bi, f, l: (b // units, q_blk + b % units, qi[t])),
                pl.BlockSpec((1, kvr, tk), lambda b, t, qi, ki, bi, f, l: (b // units, k_blk + b % units, ki[t])),
                pl.BlockSpec((1, kvr, tk), lambda b, t, qi, ki, bi, f, l: (b // units, v_blk + b % units, ki[t])),
                pl.BlockSpec((1, 1, tk, group * tq), lambda b, t, qi, ki, bi, f, l: (b % hb, bi[t], 0, 0))]
    args = [zb, zb, zb, bias]
    qx_heads, kx_global = 0, False
    if qx is not None:
        kx_global = kx.ndim == 2
        xr = kx.shape[0] if kx_global else kx.shape[1] // group
        qx_heads = qx.shape[1] // xr
        assert qx_heads in (1, group) and xr % BF16_ROWS == 0
        in_specs.append(pl.BlockSpec((1, qx_heads * xr, tq), lambda b, t, qi, ki, bi, f, l: (b, 0, qi[t])))
        in_specs.append(pl.BlockSpec((xr, tk), lambda b, t, qi, ki, bi, f, l: (0, ki[t])) if kx_global else
                        pl.BlockSpec((1, group * xr, tk), lambda b, t, qi, ki, bi, f, l: (b, 0, ki[t])))
        args += [qx, kx]
    grid_spec = pltpu.PrefetchScalarGridSpec(
        num_scalar_prefetch=5, grid=(bh, len(qi)), in_specs=in_specs,
        out_specs=pl.BlockSpec((1, gq, tq), lambda b, t, qi, ki, bi, f, l: (b, 0, qi[t])),
        scratch_shapes=[pltpu.VMEM((SUBLANES, tq), _F32), pltpu.VMEM((group * _VROWS, tq), _F32),
                        pltpu.VMEM((group, tk, tq), _F32)],
    )
    return pl.pallas_call(
        functools.partial(_flash_t_kernel, group, kv_shared, qx_heads, kx_global), grid_spec=grid_spec,
        out_shape=jax.ShapeDtypeStruct((bh, gq, s), _F32),
        compiler_params=_cparams("parallel", "arbitrary"), name="flash_t",
    )(jnp.asarray(qi), jnp.asarray(ki), jnp.asarray(bi), jnp.asarray(first), jnp.asarray(last), *args)


def _topk_axis(score, n_sel, keep, axis):
    j = lax.broadcasted_iota(jnp.int32, score.shape, axis)
    big = jnp.int32(1 << 20)
    for _ in range(n_sel):
        mx = jnp.max(score, axis=axis, keepdims=True)
        firsti = jnp.min(jnp.where(score == mx, j, big), axis=axis, keepdims=True)
        hit = j == firsti
        keep = keep | (hit & (mx > 0.5 * NEG_INF))
        score = jnp.where(hit, -3e38, score)
    return keep


def _topk_rows(score, n_sel, keep):
    return _topk_axis(score, n_sel, keep, 0)


def _cmp_select_t_kernel(theta, tq, qpos0, n_sel, tab_ref, q_ref, kc_ref, vc_ref, msel_ref, o_ref, sel_ref, s_ref):
    g = pl.program_id(0) % NSA_KV_HEADS
    i = pl.program_id(1)
    ncp = kc_ref.shape[1]
    nsl = sel_ref.shape[1]
    t0 = qpos0 + i * tq
    n = lax.broadcasted_iota(jnp.int32, (ncp, tq), 0)
    t = t0 + lax.broadcasted_iota(jnp.int32, (ncp, tq), 1)
    ok = t - (n * CMP_STRIDE + (CMP_LEN - 1)) >= 0
    band = tq // CMP_STRIDE + CMP_STRIDE
    assert (CMP_STRIDE + 1) * CMP_STRIDE - (CMP_LEN - 1) >= _FAR_DISTANCE and band <= ncp
    step8 = CMP_STRIDE * SUBLANES
    assert qpos0 % step8 == 0 and tq % step8 == 0 and (ncp - band) % SUBLANES == 0
    w0 = pl.multiple_of(jnp.clip(t0 // CMP_STRIDE - CMP_STRIDE, 0, ncp - band), SUBLANES)
    nw = w0 + lax.broadcasted_iota(jnp.int32, (band, tq), 0)
    tw = t0 + lax.broadcasted_iota(jnp.int32, (band, tq), 1)
    dw = tw - (nw * CMP_STRIDE + (CMP_LEN - 1))
    ind = [dw >= th for th in theta]
    kc = kc_ref[0]
    vc_t = vc_ref[0]
    imp = jnp.zeros((ncp, tq), _F32)
    for r in range(NSA_GROUP):
        base = (g * NSA_GROUP + r) * N_BUCKETS
        far = tab_ref[base + N_BUCKETS - 1]
        corr = jnp.full((band, tq), tab_ref[base] - far, _F32)
        for k in range(1, N_BUCKETS):
            corr = corr + jnp.where(ind[k - 1], tab_ref[base + k] - tab_ref[base + k - 1], 0.0)
        rows = slice(r * HEAD_DIM, (r + 1) * HEAD_DIM)
        s_ref[...] = jnp.dot(kc, q_ref[0, rows, :], preferred_element_type=_F32) + far
        s_ref[pl.ds(w0, band), :] = s_ref[pl.ds(w0, band), :] + corr
        s = jnp.where(ok, s_ref[...], NEG_INF)
        m = jnp.max(s, axis=0, keepdims=True)
        e = jnp.where(ok, jnp.exp(s - m), 0.0)
        p = e / jnp.maximum(jnp.sum(e, axis=0, keepdims=True), 1e-30)
        o_ref[0, rows, :] = jnp.dot(vc_t, p.astype(_BF), preferred_element_type=_F32)
        imp = imp + p
    hi, mid, lo = _split3(imp)
    msel = msel_ref[...]
    p_s = (jnp.dot(msel, hi, preferred_element_type=_F32) + jnp.dot(msel, mid, preferred_element_type=_F32)
           + jnp.dot(msel, lo, preferred_element_type=_F32))
    j = lax.broadcasted_iota(jnp.int32, (nsl, tq), 0)
    qb = (t0 + lax.broadcasted_iota(jnp.int32, (nsl, tq), 1)) >> int(math.log2(SEL_BLOCK))
    valid = j <= qb
    forced = (j == 0) | (j == qb) | (j == qb - 1)
    score = jnp.where(valid, jnp.where(forced, FORCE_SCORE, p_s), NEG_INF)
    chosen = _topk_rows(score, n_sel, jnp.zeros((nsl, tq), jnp.bool_))
    sel_ref[0] = jnp.where(chosen, 0.0, NEG_INF).astype(_BF)


def _cmp_select_t(zb, kc, vc_t, tab, tq, n_blocks, qpos0=0):
    bsz, _, s = zb.shape
    bg = bsz * NSA_KV_HEADS
    ncp = kc.shape[1]
    nsl = _round_up(n_blocks, LANES)
    gq = NSA_GROUP * HEAD_DIM
    jj = np.arange(nsl)[:, None]
    nn = np.arange(ncp)[None, :]
    msel = ((nn >= CMP_PER_SEL * jj - 1) & (nn <= CMP_PER_SEL * jj + CMP_PER_SEL - 1) & (jj < n_blocks))
    msel = jnp.asarray(msel.astype(np.float32), _BF)
    tabf = tab[:, :NSA_HEADS].T.reshape(-1)
    kern = functools.partial(_cmp_select_t_kernel, _bucket_thresholds(), tq, qpos0, min(NSA_TOPN, n_blocks))
    return pl.pallas_call(
        kern,
        grid=(bg, s // tq),
        in_specs=[pl.BlockSpec(memory_space=pltpu.SMEM),
                  pl.BlockSpec((1, gq, tq), lambda b, i: (b // NSA_KV_HEADS, b % NSA_KV_HEADS, i)),
                  pl.BlockSpec((1, ncp, HEAD_DIM), lambda b, i: (b, 0, 0)),
                  pl.BlockSpec((1, HEAD_DIM, ncp), lambda b, i: (b, 0, 0)),
                  pl.BlockSpec((nsl, ncp), lambda b, i: (0, 0))],
        out_specs=[pl.BlockSpec((1, gq, tq), lambda b, i: (b, 0, i)),
                   pl.BlockSpec((1, nsl, tq), lambda b, i: (b, 0, i))],
        out_shape=[jax.ShapeDtypeStruct((bg, gq, s), _F32), jax.ShapeDtypeStruct((bg, nsl, s), _BF)],
        scratch_shapes=[pltpu.VMEM((ncp, tq), _F32)],
        compiler_params=_cparams("parallel", "parallel"), name="nsa_cmp_select_t",
    )(tabf, zb, kc, vc_t, msel)


def _moba_select_t_kernel(tq, n_blocks, kf_ref, avg_ref, q_ref, sel_ref, km_ref):
    i = pl.program_id(1)

    @pl.when(i == 0)
    def _means():
        km_ref[...] = _dot3(kf_ref[0], avg_ref[...])

    gate = lax.dot_general(km_ref[...].astype(_BF), q_ref[0], (((0,), (0,)), ((), ())),
                           preferred_element_type=_F32)
    j = lax.broadcasted_iota(jnp.int32, gate.shape, 0)
    qb = (i * tq + lax.broadcasted_iota(jnp.int32, gate.shape, 1)) >> int(math.log2(MOBA_BLOCK))
    score = jnp.where(j < qb, gate, NEG_INF)
    chosen = _topk_rows(score, min(MOBA_TOPK, n_blocks), j == qb)
    sel_ref[0] = jnp.where(chosen, 0.0, NEG_INF)[0:sel_ref.shape[1]].astype(_BF)


def _moba_select_t(zf, zb, k_blk, tq):
    bsz, _, s = zb.shape
    n_blocks = s // MOBA_BLOCK
    assert n_blocks <= LANES
    rows = _round_up(n_blocks, BF16_ROWS)
    avg = (np.arange(s)[:, None] // MOBA_BLOCK == np.arange(LANES)[None, :]).astype(np.float32) / MOBA_BLOCK
    bh = bsz * MOBA_HEADS
    return pl.pallas_call(
        functools.partial(_moba_select_t_kernel, tq, n_blocks),
        grid=(bh, s // tq),
        in_specs=[pl.BlockSpec((1, HEAD_DIM, s), lambda b, i: (b // MOBA_HEADS, k_blk + b % MOBA_HEADS, 0)),
                  pl.BlockSpec((s, LANES), lambda b, i: (0, 0)),
                  pl.BlockSpec((1, HEAD_DIM, tq), lambda b, i: (b // MOBA_HEADS, b % MOBA_HEADS, i))],
        out_specs=pl.BlockSpec((1, rows, tq), lambda b, i: (b, 0, i)),
        out_shape=jax.ShapeDtypeStruct((bh, rows, s), _BF),
        scratch_shapes=[pltpu.VMEM((HEAD_DIM, LANES), _F32)],
        compiler_params=_cparams("parallel", "arbitrary"), name="moba_select_t",
    )(zf, jnp.asarray(avg, _BF), zb)


def _linear_kernel(x_ref, w_ref, b_ref, o_ref):
    o_ref[...] = jnp.dot(x_ref[...].astype(_BF), w_ref[...], preferred_element_type=_F32) + b_ref[...]


def _linear(x, w_bf, b, tm):
    m, k = x.shape
    n = w_bf.shape[1]
    return pl.pallas_call(
        _linear_kernel,
        grid=(m // tm,),
        in_specs=[pl.BlockSpec((tm, k), lambda i: (i, 0)),
                  pl.BlockSpec((k, n), lambda i: (0, 0)),
                  pl.BlockSpec((1, n), lambda i: (0, 0))],
        out_specs=pl.BlockSpec((tm, n), lambda i: (i, 0)),
        out_shape=jax.ShapeDtypeStruct((m, n), _F32),
        compiler_params=_cparams("parallel"),
        name="linear",
    )(x, w_bf, b.reshape(1, n))


def _layer_norm_rows(y, g, b):
    mu = jnp.mean(y, axis=-1, keepdims=True)
    yc = y - mu
    var = jnp.mean(yc * yc, axis=-1, keepdims=True)
    return yc * lax.rsqrt(var + LN_EPS) * g + b


def _even_out_kernel(alpha, oc_ref, os_ref, ow_ref, gl_ref, of_ref, ex_ref, wa_ref, wf_ref, x_ref,
                     g_ref, b_ref, y_ref):
    wa = NSA_HEADS * HEAD_DIM
    gexp = _dot3(jax.nn.sigmoid(gl_ref[...]), ex_ref[...])
    o_a = (gexp[:, 0:wa] * oc_ref[...] + gexp[:, wa:2 * wa] * os_ref[...] + gexp[:, 2 * wa:3 * wa] * ow_ref[...])
    m = jnp.dot(o_a.astype(_BF), wa_ref[...], preferred_element_type=_F32)
    m += jnp.dot(of_ref[...].astype(_BF), wf_ref[...], preferred_element_type=_F32)
    y_ref[...] = _layer_norm_rows(alpha * x_ref[...] + m, g_ref[...], b_ref[...])


def _even_out(alpha, o_c, o_s, o_w, gl, o_f, w_out_bf, x, g, b, tm):
    n, d = x.shape
    wa = NSA_HEADS * HEAD_DIM
    wf = FOX_HEADS * HEAD_DIM
    ex_np = np.zeros((LANES, 3 * wa), np.float32)
    for j in range(3):
        for h in range(NSA_HEADS):
            ex_np[j * NSA_HEADS + h, j * wa + h * HEAD_DIM:j * wa + (h + 1) * HEAD_DIM] = 1.0
    ex = jnp.asarray(ex_np, _BF)
    gl = _pad_last(gl, LANES)
    row = lambda w: pl.BlockSpec((tm, w), lambda i: (i, 0))
    full = lambda a: pl.BlockSpec(a.shape, lambda i: (0,) * a.ndim)
    args = (o_c, o_s, o_w, gl, o_f, ex, w_out_bf[:wa], w_out_bf[wa:], x, g.reshape(1, d), b.reshape(1, d))
    specs = [row(wa), row(wa), row(wa), row(LANES), row(wf), full(ex), full(args[6]), full(args[7]),
             row(d), full(args[9]), full(args[10])]
    return pl.pallas_call(
        functools.partial(_even_out_kernel, alpha),
        grid=(n // tm,), in_specs=specs, out_specs=row(d),
        out_shape=jax.ShapeDtypeStruct((n, d), _F32),
        compiler_params=_cparams("parallel"), name="even_out_ln",
    )(*args)


def _odd_out_kernel(alpha, o_ref, w_ref, x_ref, g_ref, b_ref, y_ref):
    m = jnp.dot(o_ref[...].astype(_BF), w_ref[...], preferred_element_type=_F32)
    y_ref[...] = _layer_norm_rows(alpha * x_ref[...] + m, g_ref[...], b_ref[...])


def _odd_out(alpha, o, w_out_bf, x, g, b, tm):
    n, d = x.shape
    row = lambda w: pl.BlockSpec((tm, w), lambda i: (i, 0))
    full = lambda shp: pl.BlockSpec(shp, lambda i: (0,) * len(shp))
    return pl.pallas_call(
        functools.partial(_odd_out_kernel, alpha),
        grid=(n // tm,),
        in_specs=[row(o.shape[1]), full(w_out_bf.shape), row(d), full((1, d)), full((1, d))],
        out_specs=row(d), out_shape=jax.ShapeDtypeStruct((n, d), _F32),
        compiler_params=_cparams("parallel"), name="odd_out_ln",
    )(o, w_out_bf, x, g.reshape(1, d), b.reshape(1, d))


def _moe_kernel(alpha, x_ref, wr_ref, br_ref, w1_ref, w3_ref, w2_ref, g_ref, b_ref, y_ref,
                gate_ref, acc_ref):
    e = pl.program_id(1)
    x = x_ref[...]

    @pl.when(e == 0)
    def _route():
        logits = jnp.dot(x.astype(_BF), wr_ref[...], preferred_element_type=_F32) + br_ref[...]
        gate_ref[...] = _route_gates(logits, 1)
        acc_ref[...] = jnp.zeros_like(acc_ref)

    xb = x.astype(_BF)
    h1 = jnp.dot(xb, w1_ref[0], preferred_element_type=_F32)
    h3 = jnp.dot(xb, w3_ref[0], preferred_element_type=_F32)
    gate = gate_ref[...]
    lane = lax.broadcasted_iota(jnp.int32, gate.shape, 1)
    ge = jnp.sum(jnp.where(lane == e + N_GROUPS, gate, 0.0), axis=-1, keepdims=True)
    h = (jax.nn.silu(h1) * h3) * ge
    acc_ref[...] += jnp.dot(h.astype(_BF), w2_ref[0], preferred_element_type=_F32)

    @pl.when(e == N_EXPERTS - 1)
    def _finish():
        y_ref[...] = _layer_norm_rows(alpha * x + acc_ref[...], g_ref[...], b_ref[...])


def _moe_ln(alpha, x, wg, bg, we, be, w1_bf, w3_bf, w2_bf, g, b, tm):
    n, d = x.shape
    f = w1_bf.shape[2]
    wr = jnp.zeros((d, LANES), _F32).at[:, :N_GROUPS].set(wg).at[:, N_GROUPS:N_GROUPS + N_EXPERTS].set(we)
    wr = wr.astype(_BF)
    br = jnp.zeros((1, LANES), _F32).at[0, :N_GROUPS].set(bg).at[0, N_GROUPS:N_GROUPS + N_EXPERTS].set(be)
    return pl.pallas_call(
        functools.partial(_moe_kernel, alpha),
        grid=(n // tm, N_EXPERTS),
        in_specs=[pl.BlockSpec((tm, d), lambda i, e: (i, 0)),
                  pl.BlockSpec((d, LANES), lambda i, e: (0, 0)),
                  pl.BlockSpec((1, LANES), lambda i, e: (0, 0)),
                  pl.BlockSpec((1, d, f), lambda i, e: (e, 0, 0)),
                  pl.BlockSpec((1, d, f), lambda i, e: (e, 0, 0)),
                  pl.BlockSpec((1, f, d), lambda i, e: (e, 0, 0)),
                  pl.BlockSpec((1, d), lambda i, e: (0, 0)),
                  pl.BlockSpec((1, d), lambda i, e: (0, 0))],
        out_specs=pl.BlockSpec((tm, d), lambda i, e: (i, 0)),
        out_shape=jax.ShapeDtypeStruct((n, d), _F32),
        scratch_shapes=[pltpu.VMEM((tm, LANES), _F32), pltpu.VMEM((tm, d), _F32)],
        compiler_params=_cparams("parallel", "arbitrary"), name="moe_ln",
    )(x, wr, br, w1_bf, w3_bf, w2_bf, g.reshape(1, d), b.reshape(1, d))


def _pages_per_step(n_pages, page_bytes=PAGE_BYTES_PER_STEP // PAGES_PER_STEP):
    return next(n for n in (2 * PAGES_PER_STEP, PAGES_PER_STEP, 4, 2, 1)
                if n_pages % n == 0 and (n * page_bytes <= PAGE_BYTES_PER_STEP or n == 1))


def _paged_attn_kernel(n_pg, n_steps, kvh, sel, fox, pt_ref, *refs):
    refs = list(refs)
    qbd_ref = refs[0]
    pages = refs[1:1 + n_pg]
    pos = 1 + n_pg
    bias_ref, bias_new_ref, knew_ref, vnew_ref = refs[pos:pos + 4]
    pos += 4
    if sel:
        selb_ref, e_ref, e_new_ref = refs[pos:pos + 3]
        pos += 3
    if fox:
        cq_ref, ck_ref, ck_new_ref = refs[pos:pos + 3]
        pos += 3
    o_ref, m_ref, l_ref, acc_ref = refs[pos:pos + 4]
    step = pl.program_id(1)
    qbd = qbd_ref[0]
    r, f = qbd.shape
    nt = (((1,), (1,)), ((), ()))

    @pl.when(step == 0)
    def _init():
        m_ref[...] = jnp.full_like(m_ref, NEG_INF)
        l_ref[...] = jnp.zeros_like(l_ref)
        acc_ref[...] = jnp.zeros_like(acc_ref)

    def extra(bias, e, ck):
        add = bias
        if sel:
            add = add + jnp.dot(selb_ref[0], e, preferred_element_type=_F32)
        if fox:
            add = add + cq_ref[0][:, 0:1] - jnp.concatenate([ck] * (r // kvh), axis=0)
        return add

    def absorb(k_list, v_list, add):
        s = jnp.concatenate([jnp.dot(qbd, k, preferred_element_type=_F32) for k in k_list], axis=1) + add
        m_old = m_ref[...]
        m_new = jnp.maximum(m_old, jnp.max(s, axis=1, keepdims=True))
        a = jnp.exp(m_old - m_new)
        p = jnp.exp(s - m_new)
        l_ref[...] = a * l_ref[...] + jnp.sum(p, axis=1, keepdims=True)
        pv = None
        for j, v in enumerate(v_list):
            t = lax.dot_general(p[:, j * PAGE_SIZE:(j + 1) * PAGE_SIZE].astype(_BF), v, nt,
                                preferred_element_type=_F32)
            pv = t if pv is None else pv + t
        acc_ref[...] = a * acc_ref[...] + pv
        m_ref[...] = m_new

    absorb([pg[0, 0].reshape(f, PAGE_SIZE).astype(_BF) for pg in pages],
           [pg[0, 1].reshape(f, PAGE_SIZE).astype(_BF) for pg in pages],
           extra(bias_ref[...], e_ref[...] if sel else None, ck_ref[0] if fox else None))

    @pl.when(step == n_steps - 1)
    def _done():
        absorb([knew_ref[0]], [vnew_ref[0]],
               extra(bias_new_ref[...], e_new_ref[...] if sel else None, ck_new_ref[0] if fox else None))
        o_ref[0] = acc_ref[...] / jnp.maximum(l_ref[...], 1e-30)


def _paged_attn(qbd, pages_t, page_table, bias, knew_t, vnew_t, selb=None, e=None, cq=None, ck=None, ck_new=None):
    bsz, r, f = qbd.shape
    n_pages = page_table.shape[1]
    kvh = pages_t.shape[2]
    length = n_pages * PAGE_SIZE
    n_pg = _pages_per_step(n_pages, 2 * f * PAGE_SIZE * 4)
    n_steps = n_pages // n_pg
    sel = selb is not None
    fox = cq is not None
    w = n_pg * PAGE_SIZE

    def page_map(k):
        return lambda b, p, pt: (pt[b * n_pages + p * n_pg + k], 0, 0, 0, 0)

    per_b = lambda shape: pl.BlockSpec((1,) + shape, lambda b, p, pt: (b, 0, 0))
    whole = lambda shape: pl.BlockSpec(shape, lambda b, p, pt: (0, 0))
    in_specs = [per_b((r, f))]
    in_specs += [pl.BlockSpec((1, 2, kvh, HEAD_DIM, PAGE_SIZE), page_map(k)) for k in range(n_pg)]
    in_specs += [pl.BlockSpec((r, w), lambda b, p, pt: (0, p)), whole((r, PAGE_SIZE)),
                 per_b((f, PAGE_SIZE)), per_b((f, PAGE_SIZE))]
    args = [qbd] + [pages_t] * n_pg + [bias[:, :length], bias[:, length:], knew_t, vnew_t]
    if sel:
        nbl = selb.shape[2]
        in_specs += [per_b((r, nbl)), pl.BlockSpec((nbl, w), lambda b, p, pt: (0, p)), whole((nbl, PAGE_SIZE))]
        args += [selb, e[:, :length], e[:, length:]]
    if fox:
        in_specs += [per_b((r, LANES)), pl.BlockSpec((1, kvh, w), lambda b, p, pt: (b, 0, p)), per_b((kvh, PAGE_SIZE))]
        args += [cq, ck, ck_new]
    grid_spec = pltpu.PrefetchScalarGridSpec(
        num_scalar_prefetch=1, grid=(bsz, n_steps), in_specs=in_specs,
        out_specs=per_b((r, f)),
        scratch_shapes=[pltpu.VMEM((r, 1), _F32), pltpu.VMEM((r, 1), _F32), pltpu.VMEM((r, f), _F32)],
    )
    return pl.pallas_call(
        functools.partial(_paged_attn_kernel, n_pg, n_steps, kvh, sel, fox), grid_spec=grid_spec,
        out_shape=jax.ShapeDtypeStruct((bsz, r, f), _F32),
        compiler_params=_cparams("parallel", "arbitrary"), name="paged_attn",
    )(page_table.reshape(-1), *args)


def _moba_pick_kernel(n_pg, n_steps, n_blocks, n_top, pt_ref, qbd_ref, *refs):
    pages = refs[:n_pg]
    selb_ref, km_ref = refs[n_pg], refs[n_pg + 1]
    step = pl.program_id(1)
    f = km_ref.shape[0]

    @pl.when(step == 0)
    def _init():
        km_ref[...] = jnp.zeros_like(km_ref)

    lane = lax.broadcasted_iota(jnp.int32, km_ref.shape, 1)
    km = km_ref[...]
    for k in range(n_pg):
        col = jnp.sum(pages[k][0, 0].reshape(f, PAGE_SIZE), axis=1, keepdims=True)
        blk = (step * n_pg + k) // (MOBA_BLOCK // PAGE_SIZE)
        km = jnp.where(lane == blk, km + col, km)
    km_ref[...] = km

    @pl.when(step == n_steps - 1)
    def _pick():
        means = (km * (1.0 / MOBA_BLOCK)).astype(_BF)
        gate = jnp.dot(qbd_ref[0], means, preferred_element_type=_F32)
        j = lax.broadcasted_iota(jnp.int32, gate.shape, 1)
        score = jnp.where(j < n_blocks, gate, NEG_INF)
        chosen = _topk_axis(score, n_top, j == n_blocks, 1)
        selb_ref[0] = jnp.where(chosen, 0.0, NEG_INF).astype(_BF)


def _moba_pick(qbd, pages_t, page_table):
    bsz, r, f = qbd.shape
    n_pages = page_table.shape[1]
    kvh = pages_t.shape[2]
    n_blocks = n_pages * PAGE_SIZE // MOBA_BLOCK
    assert n_blocks < LANES and (n_pages * PAGE_SIZE) % MOBA_BLOCK == 0
    n_pg = _pages_per_step(n_pages, f * PAGE_SIZE * 4)
    n_steps = n_pages // n_pg

    def page_map(k):
        return lambda b, p, pt: (pt[b * n_pages + p * n_pg + k], 0, 0, 0, 0)

    grid_spec = pltpu.PrefetchScalarGridSpec(
        num_scalar_prefetch=1, grid=(bsz, n_steps),
        in_specs=[pl.BlockSpec((1, r, f), lambda b, p, pt: (b, 0, 0))]
        + [pl.BlockSpec((1, 1, kvh, HEAD_DIM, PAGE_SIZE), page_map(k)) for k in range(n_pg)],
        out_specs=pl.BlockSpec((1, r, LANES), lambda b, p, pt: (b, 0, 0)),
        scratch_shapes=[pltpu.VMEM((f, LANES), _F32)],
    )
    return pl.pallas_call(
        functools.partial(_moba_pick_kernel, n_pg, n_steps, n_blocks, min(MOBA_TOPK, n_blocks)), grid_spec=grid_spec,
        out_shape=jax.ShapeDtypeStruct((bsz, r, LANES), _BF),
        compiler_params=_cparams("parallel", "arbitrary"), name="moba_pick",
    )(page_table.reshape(-1), qbd, *([pages_t] * n_pg))


def _compress_kernel(x_ref, pe_ref, w1a_ref, w1b_ref, b1_ref, w2_ref, b2_ref, o_ref):
    x = x_ref[0, 0]
    pe = pe_ref[0]
    hf = jnp.dot((x + pe[0:1]).astype(_BF), w1a_ref[0], preferred_element_type=_F32)
    hs = jnp.dot((x + pe[1:2]).astype(_BF), w1b_ref[0], preferred_element_type=_F32)
    hs_next = pltpu.roll(hs, hs.shape[0] - 1, 0)
    h = jax.nn.gelu(hf + hs_next + b1_ref[0], approximate=True)
    o_ref[0, 0] = jnp.dot(h.astype(_BF), w2_ref[0], preferred_element_type=_F32) + b2_ref[0]


def _nsa_compress(x, pe, w1, b1, w2, b2):
    sg, bsz, nch, flat = x.shape
    pe2 = pe.reshape(2, CMP_STRIDE, 2, HEAD_DIM).transpose(2, 0, 1, 3).reshape(2, 2, flat)
    w1r = w1.reshape(2, CMP_STRIDE, 2, HEAD_DIM, CMP_HID).transpose(2, 0, 1, 3, 4).reshape(2, 2, flat, CMP_HID)
    w1r = w1r.astype(_BF)
    smap = lambda s, b: (s // NSA_KV_HEADS, 0, 0)
    return pl.pallas_call(
        _compress_kernel,
        grid=(sg, bsz),
        in_specs=[pl.BlockSpec((1, 1, nch, flat), lambda s, b: (s, b, 0, 0)),
                  pl.BlockSpec((1, 2, flat), smap),
                  pl.BlockSpec((1, flat, CMP_HID), smap),
                  pl.BlockSpec((1, flat, CMP_HID), smap),
                  pl.BlockSpec((1, 1, CMP_HID), smap),
                  pl.BlockSpec((1, CMP_HID, HEAD_DIM), smap),
                  pl.BlockSpec((1, 1, HEAD_DIM), smap)],
        out_specs=pl.BlockSpec((1, 1, nch, HEAD_DIM), lambda s, b: (s, b, 0, 0)),
        out_shape=jax.ShapeDtypeStruct((sg, bsz, nch, HEAD_DIM), _F32),
        compiler_params=_cparams("parallel", "parallel"), name="nsa_compress",
    )(x, pe2, w1r[:, 0], w1r[:, 1], b1.reshape(2, 1, CMP_HID), w2.astype(_BF), b2.reshape(2, 1, HEAD_DIM))


def _compress_rows_kernel(x_ref, pe_ref, w1a_ref, w1b_ref, b1_ref, w2_ref, b2_ref, o_ref):
    nch = o_ref.shape[2]
    hf = jnp.zeros((nch, CMP_HID), _F32)
    hs = jnp.zeros((nch, CMP_HID), _F32)
    for c in range(CMP_STRIDE):
        xc = x_ref[0, 0, pl.ds(c, nch, stride=CMP_STRIDE), :]
        hf += jnp.dot((xc + pe_ref[0, 0, c:c + 1, :]).astype(_BF), w1a_ref[0, c], preferred_element_type=_F32)
        hs += jnp.dot((xc + pe_ref[0, 1, c:c + 1, :]).astype(_BF), w1b_ref[0, c], preferred_element_type=_F32)
    hs_next = pltpu.roll(hs, nch - 1, 0)
    h = jax.nn.gelu(hf + hs_next + b1_ref[0], approximate=True)
    o_ref[0, 0] = jnp.dot(h.astype(_BF), w2_ref[0], preferred_element_type=_F32) + b2_ref[0]


def _nsa_compress_rows(x, pe, w1, b1, w2, b2):
    sg, bsz, length, dh = x.shape
    nch = length // CMP_STRIDE
    pe2 = pe.reshape(2, CMP_STRIDE, 2, dh).transpose(2, 0, 1, 3)
    w1r = w1.reshape(2, CMP_STRIDE, 2, dh, CMP_HID).transpose(2, 0, 1, 3, 4).astype(_BF)
    s3 = lambda s, b: (s // NSA_KV_HEADS, 0, 0)
    s4 = lambda s, b: (s // NSA_KV_HEADS, 0, 0, 0)
    return pl.pallas_call(
        _compress_rows_kernel,
        grid=(sg, bsz),
        in_specs=[pl.BlockSpec((1, 1, length, dh), lambda s, b: (s, b, 0, 0)),
                  pl.BlockSpec((1, 2, CMP_STRIDE, dh), s4),
                  pl.BlockSpec((1, CMP_STRIDE, dh, CMP_HID), s4),
                  pl.BlockSpec((1, CMP_STRIDE, dh, CMP_HID), s4),
                  pl.BlockSpec((1, 1, CMP_HID), s3),
                  pl.BlockSpec((1, CMP_HID, dh), s3),
                  pl.BlockSpec((1, 1, dh), s3)],
        out_specs=pl.BlockSpec((1, 1, nch, dh), lambda s, b: (s, b, 0, 0)),
        out_shape=jax.ShapeDtypeStruct((sg, bsz, nch, dh), _F32),
        compiler_params=_cparams("parallel", "parallel"), name="nsa_compress_rows",
    )(x, pe2, w1r[:, 0], w1r[:, 1], b1.reshape(2, 1, CMP_HID), w2.astype(_BF), b2.reshape(2, 1, dh))


def _chunks_from_fm(kv_t):
    bsz, _, length = kv_t.shape
    nch = length // CMP_STRIDE
    sg = 2 * NSA_KV_HEADS
    x = kv_t.reshape(bsz, sg, HEAD_DIM, nch, CMP_STRIDE).transpose(1, 0, 3, 4, 2)
    return x.reshape(sg, bsz, nch, CMP_STRIDE * HEAD_DIM)


def _logf_cumsum_kernel(n_new, x_ref, u_ref, lf_ref, c_ref, hi_ref, mid_ref, lo_ref):
    length = x_ref.shape[1]
    x = x_ref[...]
    col = lax.broadcasted_iota(jnp.int32, x.shape, 1)
    ls = jnp.minimum(x, 0.0) - jnp.log1p(jnp.exp(-jnp.abs(x)))
    lf = jnp.where(col >= length - n_new, ls, x)
    lf_ref[...] = lf
    u = u_ref[...]
    carry = jnp.zeros((x.shape[0], 1), _F32)
    for k in range(length // LANES):
        blk = _dot3(lf[:, k * LANES:(k + 1) * LANES], u) + carry
        c_ref[:, k * LANES:(k + 1) * LANES] = blk
        hi, mid, lo = _split3(blk)
        hi_ref[:, k * LANES:(k + 1) * LANES] = hi
        mid_ref[:, k * LANES:(k + 1) * LANES] = mid
        lo_ref[:, k * LANES:(k + 1) * LANES] = lo
        carry = blk[:, LANES - 1:LANES]


def _logf_cumsum(x, n_new, rows_per_step):
    rows, length = x.shape
    u = jnp.asarray(np.triu(np.ones((LANES, LANES), np.float32)), _BF)
    spec = pl.BlockSpec((rows_per_step, length), lambda i: (i, 0))
    lf, c, hi, mid, lo = pl.pallas_call(
        functools.partial(_logf_cumsum_kernel, n_new),
        grid=(rows // rows_per_step,),
        in_specs=[spec, pl.BlockSpec((LANES, LANES), lambda i: (0, 0))],
        out_specs=[spec] * 5,
        out_shape=[jax.ShapeDtypeStruct((rows, length), _F32)] * 2 + [jax.ShapeDtypeStruct((rows, length), _BF)] * 3,
        compiler_params=_cparams("parallel"), name="fox_logf_cumsum",
    )(x, u)
    return lf, c, (hi, mid, lo)


def _fox_aug_rows(c3):
    one = jnp.ones_like(c3[0])
    zero = jnp.zeros_like(c3[0])
    pad = [zero] * (BF16_ROWS - 6)
    qaug = jnp.stack(list(c3) + [one, one, one] + pad, axis=1)
    kaug = jnp.stack([one, one, one] + [-c for c in c3] + pad, axis=1)
    return qaug, kaug


def _gather_kernel(n_pg, pt_ref, *refs):
    ins, out = refs[:n_pg], refs[n_pg]
    for k in range(n_pg):
        out[0, :, k * PAGE_SIZE:(k + 1) * PAGE_SIZE] = ins[k][0]


def _gather_fm(pool_t, page_table):
    bsz, n_pages = page_table.shape
    f = pool_t.shape[1]
    n_pg = _pages_per_step(n_pages)

    def in_map(k):
        return lambda b, p, pt: (pt[b * n_pages + p * n_pg + k], 0, 0)

    grid_spec = pltpu.PrefetchScalarGridSpec(
        num_scalar_prefetch=1,
        grid=(bsz, n_pages // n_pg),
        in_specs=[pl.BlockSpec((1, f, PAGE_SIZE), in_map(k)) for k in range(n_pg)],
        out_specs=pl.BlockSpec((1, f, n_pg * PAGE_SIZE), lambda b, p, pt: (b, 0, p)),
    )
    return pl.pallas_call(
        functools.partial(_gather_kernel, n_pg), grid_spec=grid_spec,
        out_shape=jax.ShapeDtypeStruct((bsz, f, n_pages * PAGE_SIZE), pool_t.dtype),
        compiler_params=_cparams("parallel", "arbitrary"), name="page_gather",
    )(page_table.reshape(-1), *([pool_t] * n_pg))


def _gather_rows_kernel(n_pg, n_heads, pt_ref, *refs):
    ins, out = refs[:n_pg], refs[n_pg]
    for k in range(n_pg):
        for h in range(n_heads):
            out[h, 0, k * PAGE_SIZE:(k + 1) * PAGE_SIZE, :] = ins[k][0, h * HEAD_DIM:(h + 1) * HEAD_DIM, :].T


def _gather_rows(pool_t, page_table):
    bsz, n_pages = page_table.shape
    n_heads = pool_t.shape[1] // HEAD_DIM
    n_pg = _pages_per_step(n_pages)

    def in_map(k):
        return lambda b, p, pt: (pt[b * n_pages + p * n_pg + k], 0, 0)

    grid_spec = pltpu.PrefetchScalarGridSpec(
        num_scalar_prefetch=1,
        grid=(bsz, n_pages // n_pg),
        in_specs=[pl.BlockSpec((1, n_heads * HEAD_DIM, PAGE_SIZE), in_map(k)) for k in range(n_pg)],
        out_specs=pl.BlockSpec((n_heads, 1, n_pg * PAGE_SIZE, HEAD_DIM), lambda b, p, pt: (0, b, p, 0)),
    )
    return pl.pallas_call(
        functools.partial(_gather_rows_kernel, n_pg, n_heads), grid_spec=grid_spec,
        out_shape=jax.ShapeDtypeStruct((n_heads, bsz, n_pages * PAGE_SIZE, HEAD_DIM), pool_t.dtype),
        compiler_params=_cparams("parallel", "arbitrary"), name="page_gather_rows",
    )(page_table.reshape(-1), *([pool_t] * n_pg))


def _pages_fm(cache_l):
    return cache_l.transpose(0, 2, 3, 4, 1)


def _kv_group_onehot(heads, group):
    return (np.arange(heads)[:, None] // group == np.arange(heads // group)[None, :]).astype(np.float32)


def _block_diag_queries(q, group):
    b, t, h, dh = q.shape
    oh = jnp.asarray(_kv_group_onehot(h, group))
    x = q[:, :, :, None, :] * oh[None, None, :, :, None]
    return x.reshape(b, t * h, (h // group) * dh).astype(_BF)


def _own_head_columns(o, t, heads, group):
    b = o.shape[0]
    oh = jnp.asarray(_kv_group_onehot(heads, group))
    x = o.reshape(b, t, heads, heads // group, HEAD_DIM) * oh[None, None, :, :, None]
    return jnp.sum(x, axis=3).reshape(b, t, heads * HEAD_DIM)


def _sample_bias(tabh, heads, d0, t, cols, window=None):
    tile = _bias_tile(tabh, d0, t, cols, window)
    tile = jnp.broadcast_to(tile, (heads, t, cols))
    return tile.transpose(1, 0, 2).reshape(t * heads, cols)


def _new_rows_fm(x):
    return _pad_last(x.transpose(0, 2, 1), PAGE_SIZE).astype(_BF)


def _even_sample(z, li, cache_cmp, cache_sel, win_state, cache_fkv, cache_flogf, page_table, tab, cmp_w, tq):
    bs, n_pages = page_table.shape
    past = n_pages * PAGE_SIZE
    ts = z.shape[0] // bs
    q_a, kv_c, kv_s, kv_w, g_bm, q_f, kv_f, f_logit = _even_split(z, bs, ts)
    g = NSA_KV_HEADS
    tabn = tab[:, :NSA_HEADS]
    wide = g * HEAD_DIM
    total = past + PAGE_SIZE
    assert past % CMP_STRIDE == 0 and past % SEL_BLOCK == 0 and ts <= CMP_STRIDE

    pool_c = _pages_fm(cache_cmp[li]).reshape(-1, 2 * wide, PAGE_SIZE)
    cmp_tok = _nsa_compress_rows(_gather_rows(pool_c, page_table), *cmp_w)
    ncp = _round_up(cmp_tok.shape[2], LANES)
    cmp_tok = _pad_axis(cmp_tok, 2, ncp).astype(_BF)
    kc = cmp_tok[:g].transpose(1, 0, 2, 3).reshape(bs * g, ncp, HEAD_DIM)
    vc_t = cmp_tok[g:].transpose(1, 0, 3, 2).reshape(bs * g, HEAD_DIM, ncp)
    assert ts <= tq
    q_t = _pad_last((q_a * SCALE).reshape(bs, ts, NSA_HEADS * HEAD_DIM).transpose(0, 2, 1), tq).astype(_BF)
    o_c, sel_t = _cmp_select_t(q_t, kc, vc_t, tab, tq, total // SEL_BLOCK, past)
    o_c = o_c.reshape(bs, NSA_HEADS * HEAD_DIM, tq).transpose(0, 2, 1)[:, :ts]
    nsl = sel_t.shape[1]

    qbd = _block_diag_queries(q_a * SCALE, NSA_GROUP)
    sel_rows = jnp.repeat(sel_t.reshape(bs, g, nsl, tq)[..., :ts].transpose(0, 3, 1, 2), NSA_GROUP, axis=2)
    sel_rows = sel_rows.reshape(bs, ts * NSA_HEADS, nsl)
    e_sel = jnp.asarray((np.arange(nsl)[:, None] == np.arange(total)[None, :] // SEL_BLOCK).astype(np.float32), _BF)
    flat = lambda kv, s: kv[:, :, s].reshape(bs, ts, -1)
    o_s = _paged_attn(qbd, _pages_fm(cache_sel[li]), page_table, _sample_bias(tabn, NSA_HEADS, past, ts, total),
                      _new_rows_fm(flat(kv_s, 0)), _new_rows_fm(flat(kv_s, 1)), selb=sel_rows, e=e_sel)
    win_buf = win_state[li]
    wb = win_buf.shape[1]
    assert wb % PAGE_SIZE == 0
    wpages = _pages_fm(win_buf).reshape(bs, 2, g, HEAD_DIM, wb // PAGE_SIZE, PAGE_SIZE)
    wpages = wpages.transpose(0, 4, 1, 2, 3, 5).reshape(bs * (wb // PAGE_SIZE), 2, g, HEAD_DIM, PAGE_SIZE)
    wtable = jnp.arange(bs * (wb // PAGE_SIZE), dtype=jnp.int32).reshape(bs, wb // PAGE_SIZE)
    o_w = _paged_attn(qbd, wpages, wtable, _sample_bias(tabn, NSA_HEADS, wb, ts, wb + PAGE_SIZE, NSA_WINDOW),
                      _new_rows_fm(flat(kv_w, 0)), _new_rows_fm(flat(kv_w, 1)))
    o_s = _own_head_columns(o_s, ts, NSA_HEADS, NSA_GROUP)
    o_w = _own_head_columns(o_w, ts, NSA_HEADS, NSA_GROUP)

    past_l = _gather_fm(cache_flogf[li].transpose(0, 2, 1), page_table)
    lf_len = _round_up(past + ts, LANES)
    front = lf_len - past - ts
    fl_all = jnp.concatenate([jnp.zeros((bs, FOX_HEADS, front), _F32), past_l, f_logit.transpose(0, 2, 1)], axis=-1)
    logf_t, c, _ = _logf_cumsum(fl_all.reshape(bs * FOX_HEADS, lf_len), ts, min(bs * FOX_HEADS, 64))
    c = c.reshape(bs, FOX_HEADS, lf_len)
    logf = logf_t[:, lf_len - ts:].reshape(bs, FOX_HEADS, ts).transpose(0, 2, 1)
    c_new = c[:, :, front + past:]
    cq = jnp.broadcast_to(c_new.transpose(0, 2, 1).reshape(bs, ts * FOX_HEADS, 1), (bs, ts * FOX_HEADS, LANES))
    o_f = _paged_attn(_block_diag_queries(q_f * SCALE, 1), _pages_fm(cache_fkv[li]), page_table,
                      _sample_bias(None, FOX_HEADS, past, ts, total),
                      _new_rows_fm(flat(kv_f, 0)), _new_rows_fm(flat(kv_f, 1)),
                      cq=cq, ck=c[:, :, front:front + past], ck_new=_pad_last(c_new, PAGE_SIZE))
    o_f = _own_head_columns(o_f, ts, FOX_HEADS, 1)
    kvw_all = jnp.concatenate([win_buf, kv_w], axis=1)
    n_tok = bs * ts
    outs = (o_c.reshape(n_tok, -1), o_s.reshape(n_tok, -1), o_w.reshape(n_tok, -1), o_f.reshape(n_tok, -1),
            g_bm.reshape(n_tok, -1))
    return outs, (kv_c, kv_s, kvw_all[:, ts:], kv_f, logf)


def _odd_sample(z, li, cache_kv, page_table, tab):
    bs, n_pages = page_table.shape
    past = n_pages * PAGE_SIZE
    ts = z.shape[0] // bs
    hw = MOBA_HEADS * HEAD_DIM
    z = z.reshape(bs, ts, 3 * hw)
    q = z[..., :hw].reshape(bs, ts, MOBA_HEADS, HEAD_DIM)
    total = past + PAGE_SIZE
    assert ts <= MOBA_BLOCK and past % MOBA_BLOCK == 0
    pages = _pages_fm(cache_kv[li])
    qbd = _block_diag_queries(q * SCALE, 1)
    selb = _moba_pick(qbd, pages, page_table)
    e_blk = jnp.asarray((np.arange(LANES)[:, None] == np.arange(total)[None, :] // MOBA_BLOCK).astype(np.float32), _BF)
    o = _paged_attn(qbd, pages, page_table, _sample_bias(tab[:, :MOBA_HEADS], MOBA_HEADS, past, ts, total),
                    _new_rows_fm(z[..., hw:2 * hw]), _new_rows_fm(z[..., 2 * hw:]), selb=selb, e=e_blk)
    o = _own_head_columns(o, ts, MOBA_HEADS, 1)
    return o.reshape(bs * ts, hw), z[..., hw:].reshape(bs, ts, 2, MOBA_HEADS, HEAD_DIM)


def _even_split(z, b, t):
    q_a, kv_c, kv_s, kv_w, g_a, q_f, kv_f, f_logit = jnp.split(z.reshape(b, t, -1), _EVEN_CUTS, axis=-1)
    kvshape = (b, t, 2, NSA_KV_HEADS, HEAD_DIM)
    g_bm = g_a.reshape(b, t, NSA_HEADS, 3).transpose(0, 1, 3, 2).reshape(b, t, 3 * NSA_HEADS)
    return (q_a.reshape(b, t, NSA_HEADS, HEAD_DIM), kv_c.reshape(kvshape), kv_s.reshape(kvshape),
            kv_w.reshape(kvshape), g_bm, q_f.reshape(b, t, FOX_HEADS, HEAD_DIM),
            kv_f.reshape(b, t, 2, FOX_HEADS, HEAD_DIM), f_logit)


def _even_row_perm():
    cuts = (0,) + _EVEN_CUTS + (sum(_EVEN_SIZES),)
    seg = lambda k: np.arange(cuts[k], cuts[k + 1])
    gates = cuts[4] + (np.arange(NSA_HEADS)[None, :] * 3 + np.arange(3)[:, None]).reshape(-1)
    return np.concatenate([seg(0), seg(5), seg(6), seg(1), seg(2), seg(3), gates, seg(7)])


def _kv_leaf(kv_t, heads):
    bsz, _, s = kv_t.shape
    return kv_t.reshape(bsz, 2, heads, HEAD_DIM, s).transpose(0, 4, 1, 2, 3)


def _even_prompt(x_rows, tab, w_in, b_in, cmp_w, w_out, ln_g, ln_b, alpha, tm):
    bsz, s, d = x_rows.shape
    perm = _even_row_perm()
    w_t = w_in.T[perm].astype(_BF)
    scale = np.ones((_EVEN_OUT, 1), np.float32)
    scale[_QA:_QA + NSA_HEADS * HEAD_DIM] = SCALE
    scale[_QF:_QF + FOX_HEADS * HEAD_DIM] = SCALE
    nsa_kv, fox_kv = 2 * NSA_KV_HEADS * HEAD_DIM, 2 * FOX_HEADS * HEAD_DIM
    segments = [(_KVC, nsa_kv), (_KVS, nsa_kv), (_KVW, nsa_kv), (_KVF, fox_kv), (_GA, _EVEN_OUT - _GA)]
    zb, (kvc_t, kvs_t, kvw_t, kvf_t, tail_t) = _proj_fm(x_rows, w_t, b_in[perm].reshape(-1, 1), jnp.asarray(scale),
                                                         tm, True, segments)
    tabn = tab[:, :NSA_HEADS]
    g = NSA_KV_HEADS
    hb = HEAD_DIM

    assert s % SEL_BLOCK == 0
    cmp_tok = _nsa_compress(_chunks_from_fm(kvc_t), *cmp_w)
    ncp = _round_up(cmp_tok.shape[2], LANES)
    cmp_tok = _pad_axis(cmp_tok, 2, ncp).astype(_BF)
    kc = cmp_tok[:g].transpose(1, 0, 2, 3).reshape(bsz * g, ncp, HEAD_DIM)
    vc_t = cmp_tok[g:].transpose(1, 0, 3, 2).reshape(bsz * g, HEAD_DIM, ncp)
    tq, tk = 256, 512
    o_c, sel = _cmp_select_t(zb, kc, vc_t, tab, tq, s // SEL_BLOCK)

    pairs, deltas = _plan_tiles(s // tq, tq, tk, None, True)
    bias = _bias_tiles_t(tabn, deltas, tk, tq, None, NSA_GROUP)
    block_of_key = lambda n, blk: jnp.asarray(
        (np.arange(n)[:, None] == np.arange(s)[None, :] // blk).astype(np.float32), _BF)
    o_s = _flash_t(zb, _QA // (NSA_GROUP * hb), _KVS // hb, _KVS // hb + g, g, NSA_GROUP, True, bias, pairs, tq, tk,
                   qx=sel, kx=block_of_key(sel.shape[1], SEL_BLOCK))
    pairs, deltas = _plan_tiles(s // tq, tq, tk, NSA_WINDOW, True)
    bias = _bias_tiles_t(tabn, deltas, tk, tq, NSA_WINDOW, NSA_GROUP)
    o_w = _flash_t(zb, _QA // (NSA_GROUP * hb), _KVW // hb, _KVW // hb + g, g, NSA_GROUP, True, bias, pairs, tq, tk)

    f_logit_t = tail_t[:, _FL - _GA:_FL - _GA + FOX_HEADS, :]
    logf_t, _, c3 = _logf_cumsum(f_logit_t.reshape(bsz * FOX_HEADS, s), s, bsz * FOX_HEADS)
    qaug, kaug = _fox_aug_rows(c3)
    tqf = tkf = 512
    hps = FOX_HEADS_PER_STEP
    units = FOX_HEADS // hps
    assert _QF % (hps * hb) == 0 and _KVF % (hps * hb) == 0
    pairs, deltas = _plan_tiles(s // tqf, tqf, tkf, None, False)
    bias = _bias_tiles_t(None, deltas, tkf, tqf, None, hps)
    per_unit = lambda a: a.reshape(bsz * units, hps * BF16_ROWS, s)
    o_f = _flash_t(zb, _QF // (hps * hb), _KVF // (hps * hb), _KVF // (hps * hb) + units, units, hps, False,
                   bias, pairs, tqf, tkf, qx=per_unit(qaug), kx=per_unit(kaug))

    wide = lambda o: o.reshape(bsz, -1, s)
    x_t = _even_out_fm(alpha, wide(o_c), wide(o_s), wide(o_w), tail_t, wide(o_f), w_out.T.astype(_BF), x_rows,
                       ln_g, ln_b, tm)
    kv_c = _kv_leaf(kvc_t, g)
    kv_s = _kv_leaf(kvs_t, g)
    kv_w = _kv_leaf(kvw_t, g)
    kv_f = _kv_leaf(kvf_t, FOX_HEADS)
    logf = logf_t.reshape(bsz, FOX_HEADS, s).transpose(0, 2, 1)
    return x_t, (kv_c, kv_s, kv_w[:, max(s - NSA_WINDOW, 0):], kv_f, logf)


def _odd_prompt(x_t, tab, w_in, w_out, ln_g, ln_b, alpha, tm):
    bsz, d, s = x_t.shape
    hw = MOBA_HEADS * HEAD_DIM
    scale = np.ones((3 * hw, 1), np.float32)
    scale[:hw] = SCALE
    zb, (kv_t,) = _proj_fm(x_t, w_in.T.astype(_BF), jnp.zeros((3 * hw, 1), _F32), jnp.asarray(scale), tm, False,
                           [(hw, 2 * hw)])
    assert s % MOBA_BLOCK == 0
    sel = _moba_select_t(kv_t, zb, 0, 1024 if s % 1024 == 0 else 256)
    tq = tk = 512
    hps = MOBA_HEADS_PER_STEP
    units = MOBA_HEADS // hps
    pairs, deltas = _plan_tiles(s // tq, tq, tk, None, True)
    bias = _bias_tiles_t(tab[:, :MOBA_HEADS], deltas, tk, tq, None, hps)
    nbr = sel.shape[1]
    block_of_key = jnp.asarray((np.arange(nbr)[:, None] == np.arange(s)[None, :] // MOBA_BLOCK).astype(np.float32), _BF)
    o = _flash_t(zb, 0, units, 2 * units, units, hps, False, bias, pairs, tq, tk,
                 qx=sel.reshape(bsz * units, hps * nbr, s), kx=block_of_key)
    x_t = _odd_out_fm(alpha, o.reshape(bsz, hw, s), w_out.T.astype(_BF), x_t, ln_g, ln_b, tm)
    return x_t, _kv_leaf(kv_t, MOBA_HEADS)


def _kernel_impl(x_prompt, x_sample, cache_nsa_cmp, cache_nsa_sel, state_nsa_win, cache_fox_kv,
                 cache_fox_logf, cache_moba_kv, page_table, rel_bias, ln_g, ln_b, w_in_even, b_in_even,
                 nsa_cmp_pe, nsa_cmp_w1, nsa_cmp_b1, nsa_cmp_w2, nsa_cmp_b2, w_out_even, w_in_odd,
                 w_out_odd, moe_wg, moe_bg, moe_we, moe_be, moe_w1, moe_w3, moe_w2):
    bp, sp, d = x_prompt.shape
    bs, ts, _ = x_sample.shape
    n_pages = page_table.shape[1]
    past = n_pages * PAGE_SIZE
    depth = ln_g.shape[0]
    alpha = (2 * depth) ** 0.25
    ns_tok = bs * ts
    tm_p = 512
    tm_moe = 1024
    tm_s = ns_tok
    assert sp % 1024 == 0 and depth % 2 == 0
    xp = x_prompt
    xs = x_sample.reshape(ns_tok, d)
    outs = {k: [] for k in ("cmp_p", "cmp_s", "sel_p", "sel_s", "win_p", "win_s", "fkv_p", "fkv_s",
                            "flf_p", "flf_s", "mkv_p", "mkv_s")}
    tq_s = LANES

    for layer in range(depth):
        li = layer // 2
        if layer % 2 == 0:
            assert layer == 0
            cmp_w = (nsa_cmp_pe[li], nsa_cmp_w1[li], nsa_cmp_b1[li], nsa_cmp_w2[li], nsa_cmp_b2[li])
            xp, (kv_c, kv_s, kv_w, kv_f, logf) = _even_prompt(
                xp, rel_bias, w_in_even[li], b_in_even[li], cmp_w, w_out_even[li], ln_g[layer, 0], ln_b[layer, 0],
                alpha, tm_p)
            outs["cmp_p"].append(kv_c)
            outs["sel_p"].append(kv_s)
            outs["win_p"].append(kv_w)
            outs["fkv_p"].append(kv_f)
            outs["flf_p"].append(logf)
            w_in = w_in_even[li].astype(_BF)
            w_out = w_out_even[li].astype(_BF)
            z = _linear(xs, w_in, b_in_even[li], tm_s)
            (o_c, o_s, o_w, o_f, g_bm), (kv_c, kv_s, kv_w, kv_f, logf) = _even_sample(
                z, li, cache_nsa_cmp, cache_nsa_sel, state_nsa_win, cache_fox_kv, cache_fox_logf, page_table,
                rel_bias, cmp_w, tq_s)
            xs = _even_out(alpha, o_c, o_s, o_w, g_bm, o_f, w_out, xs, ln_g[layer, 0], ln_b[layer, 0], tm_s)
            outs["cmp_s"].append(kv_c)
            outs["sel_s"].append(kv_s)
            outs["win_s"].append(kv_w)
            outs["fkv_s"].append(kv_f)
            outs["flf_s"].append(logf)
        else:
            hw = MOBA_HEADS * HEAD_DIM
            xp, kv = _odd_prompt(xp, rel_bias, w_in_odd[li], w_out_odd[li], ln_g[layer, 0], ln_b[layer, 0], alpha, tm_p)
            outs["mkv_p"].append(kv)
            w_in = w_in_odd[li].astype(_BF)
            w_out = w_out_odd[li].astype(_BF)
            z = _linear(xs, w_in, jnp.zeros((3 * hw,), _F32), tm_s)
            o, kv = _odd_sample(z, li, cache_moba_kv, page_table, rel_bias)
            xs = _odd_out(alpha, o, w_out, xs, ln_g[layer, 0], ln_b[layer, 0], tm_s)
            outs["mkv_s"].append(kv)
        w1b, w3b, w2b = moe_w1[layer].astype(_BF), moe_w3[layer].astype(_BF), moe_w2[layer].astype(_BF)
        w13_t = jnp.concatenate([w1b.transpose(0, 2, 1), w3b.transpose(0, 2, 1)], axis=1)
        router = (moe_wg[layer], moe_bg[layer], moe_we[layer], moe_be[layer])
        xp = _moe_ln_fm(alpha, xp, *router, w13_t, w2b.transpose(0, 2, 1), ln_g[layer, 1], ln_b[layer, 1], tm_moe,
                        layer == depth - 1)
        xs = _moe_ln(alpha, xs, *router, w1b, w3b, w2b, ln_g[layer, 1], ln_b[layer, 1], tm_s)

    st = lambda k: jnp.stack(outs[k])
    return (xp, xs.reshape(bs, ts, d), st("cmp_p"), st("cmp_s"), st("sel_p"), st("sel_s"),
            st("win_p"), st("win_s"), st("fkv_p"), st("fkv_s"), st("flf_p"), st("flf_s"), st("mkv_p"), st("mkv_s"))


def kernel(x_prompt, x_sample, cache_nsa_cmp, cache_nsa_sel, state_nsa_win, cache_fox_kv, cache_fox_logf, cache_moba_kv, page_table, rel_bias, ln_g, ln_b, w_in_even, b_in_even, nsa_cmp_pe, nsa_cmp_w1, nsa_cmp_b1, nsa_cmp_w2, nsa_cmp_b2, w_out_even, w_in_odd, w_out_odd, moe_wg, moe_bg, moe_we, moe_be, moe_w1, moe_w3, moe_w2):
    return _kernel_impl(x_prompt, x_sample, cache_nsa_cmp, cache_nsa_sel, state_nsa_win, cache_fox_kv,
                        cache_fox_logf, cache_moba_kv, page_table, rel_bias, ln_g, ln_b, w_in_even, b_in_even,
                        nsa_cmp_pe, nsa_cmp_w1, nsa_cmp_b1, nsa_cmp_w2, nsa_cmp_b2, w_out_even, w_in_odd,
                        w_out_odd, moe_wg, moe_bg, moe_we, moe_be, moe_w1, moe_w3, moe_w2)
```
